```python
import math
import jax, jax.numpy as jnp
from jax import lax
import numpy as np

D_MODEL = 1024
BATCH = 8
SEQ = 4096
DEPTH = 4

HEAD_DIM = 64
N_A_LAYERS = DEPTH // 2
N_B_LAYERS = DEPTH - N_A_LAYERS
MEM_LEN = 256
MEM_HEADS = 4
MEM_W = MEM_HEADS * HEAD_DIM
RWKV_HEADS = (D_MODEL - MEM_W) // HEAD_DIM
RWKV_W = RWKV_HEADS * HEAD_DIM
DECAY_LORA = 64
AAA_LORA = 64
GATE_LORA = 128
RWKV_SHIFT_W = 3 * RWKV_W + DECAY_LORA + AAA_LORA + GATE_LORA
A_IN_W = RWKV_SHIFT_W + MEM_W
DIL_GROUPS = ((128, 1), (512, 4), (2048, 16))
DIL_GROUP_HEADS = 4
DIL_HEADS = len(DIL_GROUPS) * DIL_GROUP_HEADS
DIL_W = DIL_HEADS * HEAD_DIM
DIL_OUT_W = DIL_GROUP_HEADS * HEAD_DIM
BLOCK = 128
NUM_BUCKETS = 32
MAX_DISTANCE = 2048
D_FF = 2816
NORM_EPS = 1e-6
LNX_EPS = 64e-5
NEG_INF = -1e30

kernel_name = 'yoco_rwkv7_dilated_macaron_hybrid'


def rms_norm(x, g):
    xf = x.astype(jnp.float32)
    y = xf * lax.rsqrt(jnp.mean(xf * xf, axis=-1, keepdims=True) + NORM_EPS)
    return (y * g.astype(jnp.float32)).astype(x.dtype)


def split_heads(t):
    return t.reshape(t.shape[:-1] + (t.shape[-1] // HEAD_DIM, HEAD_DIM))


def swiglu_ffn(x, norm_g, w_in, w_out):
    gate, up = jnp.split(rms_norm(x, norm_g) @ w_in, 2, axis=-1)
    return (jax.nn.silu(gate) * up) @ w_out


def token_shift(p, mu):
    prev = jnp.pad(p, ((0, 0), (1, 0), (0, 0)))[:, :-1]
    return p + mu * (prev - p)


def rwkv7_scan(r, w, k, v, kk, a):
    b, _, h, n = r.shape

    def step(state, inp):
        r_t, w_t, k_t, v_t, kk_t, a_t = inp
        sa = jnp.einsum('bhvk,bhk->bhv', state, -kk_t)
        state = (state * w_t[:, :, None, :]
                 + sa[..., None] * (kk_t * a_t)[:, :, None, :]
                 + v_t[..., None] * k_t[:, :, None, :])
        return state, jnp.einsum('bhvk,bhk->bhv', state, r_t)

    xs = tuple(jnp.moveaxis(t.astype(jnp.float32), 1, 0) for t in (r, w, k, v, kk, a))
    _, ys = lax.scan(step, jnp.zeros((b, h, n, n), jnp.float32), xs)
    return jnp.moveaxis(ys, 0, 1)


def rwkv7_mix(p, mu, w0, w_up, a0, a_up, g_up, kk_scale, k_a, r_k, lnx_g, lnx_b):
    out_dtype = p.dtype
    b, s = p.shape[:2]
    p = token_shift(p, mu).astype(jnp.float32)
    cuts = [RWKV_W, 2 * RWKV_W, 3 * RWKV_W, 3 * RWKV_W + DECAY_LORA,
            3 * RWKV_W + DECAY_LORA + AAA_LORA]
    r, k, v, w_lo, a_lo, g_lo = jnp.split(p, cuts, axis=-1)
    w_log = -jax.nn.softplus(-(w0 + jnp.tanh(w_lo) @ w_up)) - 0.5
    decay = jnp.exp(-jnp.exp(w_log))
    a = jax.nn.sigmoid(a0 + a_lo @ a_up)
    g = jax.nn.sigmoid(g_lo) @ g_up
    kk = split_heads(k * kk_scale)
    kk = kk / jnp.maximum(jnp.linalg.norm(kk, axis=-1, keepdims=True), 1e-12)
    k = k * (1.0 + (a - 1.0) * k_a)
    r_h, k_h, v_h, a_h, w_h = (split_heads(t) for t in (r, k, v, a, decay))
    y = rwkv7_scan(r_h, w_h, k_h, v_h, kk, a_h)
    mean = jnp.mean(y, axis=-1, keepdims=True)
    var = jnp.mean(jnp.square(y - mean), axis=-1, keepdims=True)
    y = ((y - mean) * lax.rsqrt(var + LNX_EPS)).reshape(b, s, RWKV_W) * lnx_g + lnx_b
    bonus = jnp.sum(r_h * k_h * r_k, axis=-1, keepdims=True) * v_h
    y = (y + bonus.reshape(b, s, RWKV_W)) * g
    return y.astype(out_dtype)


def memory_attention(q, mem, mem_norm_g, w_kv, q_gain, k_gain):
    k, v = jnp.split(rms_norm(mem, mem_norm_g) @ w_kv, 2, axis=-1)
    k = rms_norm(split_heads(k), k_gain)
    v = split_heads(v)
    q = rms_norm(q, q_gain)
    logits = jnp.einsum('bshd,bmhd->bhsm', q, k).astype(jnp.float32) / math.sqrt(HEAD_DIM)
    probs = jax.nn.softmax(logits, axis=-1)
    return jnp.einsum('bhsm,bmhd->bshd', probs.astype(v.dtype), v)


def padded_len(seq_len, dil):
    unit = dil * BLOCK
    return -(-seq_len // unit) * unit


def strided_blocks(t, dil, s_pad):
    b, s = t.shape[:2]
    t = jnp.pad(t, ((0, 0), (0, s_pad - s)) + ((0, 0),) * (t.ndim - 2))
    t = t.reshape((b, s_pad // dil, dil) + t.shape[2:])
    t = jnp.moveaxis(t, 2, 1)
    return t.reshape((b, dil, s_pad // (dil * BLOCK), BLOCK) + t.shape[3:])


def unstride_blocks(t, seq_len):
    b, dil, nb, blk = t.shape[:4]
    t = t.reshape((b, dil, nb * blk) + t.shape[4:])
    t = jnp.moveaxis(t, 1, 2)
    return t.reshape((b, dil * nb * blk) + t.shape[3:])[:, :seq_len]


def with_prev_block(t):
    prev = jnp.pad(t[:, :, :-1], ((0, 0), (0, 0), (1, 0), (0, 0), (0, 0), (0, 0)))
    return jnp.concatenate([prev, t], axis=3)


def t5_bucket(dist):
    max_exact = NUM_BUCKETS // 2
    d_f = jnp.maximum(dist, 1).astype(jnp.float32)
    large = max_exact + (jnp.log(d_f / max_exact) / math.log(MAX_DISTANCE / max_exact)
                         * (NUM_BUCKETS - max_exact)).astype(jnp.int32)
    large = jnp.minimum(large, NUM_BUCKETS - 1)
    return jnp.where(dist < max_exact, dist, large)


def band_mask_and_bias(window, dil, n_blocks, table):
    span = window // dil
    qi = jnp.arange(BLOCK)[:, None]
    kj = jnp.arange(2 * BLOCK)[None, :]
    dsub = BLOCK + qi - kj
    band = (dsub >= 0) & (dsub <= span)
    first = (jnp.arange(n_blocks)[:, None, None] > 0) | (kj[None] >= BLOCK)
    mask = band[None] & first
    bias = jnp.transpose(table[t5_bucket(jnp.maximum(dsub, 0) * dil)], (2, 0, 1))
    return mask, bias


def shared_kv_blocks(x, kv_norm, kv_w, k_gain):
    k, v = jnp.split(rms_norm(x, kv_norm) @ kv_w, 2, axis=-1)
    k = rms_norm(split_heads(k), k_gain)
    v = split_heads(v)
    seq_len = x.shape[1]
    k_blocks, v_blocks = [], []
    for g, (_, dil) in enumerate(DIL_GROUPS):
        hs = slice(g * DIL_GROUP_HEADS, (g + 1) * DIL_GROUP_HEADS)
        s_pad = padded_len(seq_len, dil)
        k_blocks.append(strided_blocks(k[:, :, hs], dil, s_pad))
        v_blocks.append(strided_blocks(v[:, :, hs], dil, s_pad))
    return k_blocks, v_blocks


def dilated_attention(q, k_blocks, v_blocks, q_gain, rel_bias, seq_len):
    q = rms_norm(q, q_gain)
    scale = 1.0 / math.sqrt(HEAD_DIM)
    outs, lses = [], []
    for g, (window, dil) in enumerate(DIL_GROUPS):
        hs = slice(g * DIL_GROUP_HEADS, (g + 1) * DIL_GROUP_HEADS)
        qb = strided_blocks(q[:, :, hs], dil, padded_len(seq_len, dil))
        kw = with_prev_block(k_blocks[g])
        vw = with_prev_block(v_blocks[g]).astype(jnp.float32)
        mask, bias = band_mask_and_bias(window, dil, qb.shape[2], rel_bias[:, hs])
        logits = jnp.einsum('brnqhd,brnkhd->brnhqk', qb, kw).astype(jnp.float32) * scale
        logits = jnp.where(mask[:, None], logits + bias, NEG_INF)
        m = jnp.max(logits, axis=-1, keepdims=True)
        p = jnp.exp(logits - m)
        l = jnp.sum(p, axis=-1, keepdims=True)
        o = jnp.einsum('brnhqk,brnkhd->brnqhd', p, vw) / jnp.swapaxes(l, 3, 4)
        lse = jnp.swapaxes(m + jnp.log(l), 3, 4)
        outs.append(unstride_blocks(o, seq_len))
        lses.append(unstride_blocks(lse, seq_len))
    weights = jax.nn.softmax(jnp.stack(lses), axis=0)
    return jnp.sum(weights * jnp.stack(outs), axis=0).astype(q.dtype)


def setup_inputs(seed: int = 0) -> dict:
    key = jax.random.key(seed)
    keys = iter(jax.random.split(key, 48))

    def normal(shape, scale):
        return jax.random.normal(next(keys), shape, jnp.float32) * scale

    def gain(shape):
        return 1.0 + normal(shape, 0.02)

    def uniform(shape, lo, hi):
        return jax.random.uniform(next(keys), shape, jnp.float32, lo, hi)

    d, f = D_MODEL, D_FF
    return {
        'x': normal((BATCH, SEQ, d), 1.0),
        'mem': normal((BATCH, MEM_LEN, d), 1.0),
        'ffn_pre_norm': gain((DEPTH, d)),
        'ffn_pre_w_in': normal((DEPTH, d, 2 * f), d ** -0.5),
        'ffn_pre_w_out': normal((DEPTH, f, d), f ** -0.5),
        'mix_norm': gain((DEPTH, d)),
        'ffn_post_norm': gain((DEPTH, d)),
        'ffn_post_w_in': normal((DEPTH, d, 2 * f), d ** -0.5),
        'ffn_post_w_out': normal((DEPTH, f, d), f ** -0.5),
        'mem_norm': gain((DEPTH, d)),
        'mem_w_kv': normal((DEPTH, d, 2 * MEM_W), d ** -0.5),
        'mem_q_norm': gain((DEPTH, HEAD_DIM)),
        'mem_k_norm': gain((DEPTH, HEAD_DIM)),
        'a_w_in': normal((N_A_LAYERS, d, A_IN_W), d ** -0.5),
        'a_shift_mu': uniform((N_A_LAYERS, RWKV_SHIFT_W), 0.0, 1.0),
        'a_w0': uniform((N_A_LAYERS, RWKV_W), -6.0, -1.0),
        'a_w_up': normal((N_A_LAYERS, DECAY_LORA, RWKV_W), 0.5 * DECAY_LORA ** -0.5),
        'a_a0': normal((N_A_LAYERS, RWKV_W), 0.1),
        'a_a_up': normal((N_A_LAYERS, AAA_LORA, RWKV_W), AAA_LORA ** -0.5),
        'a_g_up': normal((N_A_LAYERS, GATE_LORA, RWKV_W), GATE_LORA ** -0.5),
        'a_kk_scale': 0.85 + normal((N_A_LAYERS, RWKV_W), 0.02),
        'a_k_a': gain((N_A_LAYERS, RWKV_W)),
        'a_r_k': normal((N_A_LAYERS, RWKV_HEADS, HEAD_DIM), 0.1),
        'a_lnx_g': gain((N_A_LAYERS, RWKV_W)),
        'a_lnx_b': normal((N_A_LAYERS, RWKV_W), 0.01),
        'a_w_out': normal((N_A_LAYERS, RWKV_W + MEM_W, d), (RWKV_W + MEM_W) ** -0.5),
        'b_w_q': normal((N_B_LAYERS, d, DIL_W + MEM_W), d ** -0.5),
        'b_q_norm': gain((N_B_LAYERS, HEAD_DIM)),
        'b_w_out': normal((N_B_LAYERS, DIL_OUT_W + MEM_W, d), (DIL_OUT_W + MEM_W) ** -0.5),
        'kv_norm': gain((d,)),
        'kv_w': normal((d, 2 * DIL_W), d ** -0.5),
        'kv_k_norm': gain((HEAD_DIM,)),
        'rel_bias': normal((NUM_BUCKETS, DIL_HEADS), 0.2),
    }


def reference(x, mem, ffn_pre_norm, ffn_pre_w_in, ffn_pre_w_out, mix_norm,
              ffn_post_norm, ffn_post_w_in, ffn_post_w_out,
              mem_norm, mem_w_kv, mem_q_norm, mem_k_norm,
              a_w_in, a_shift_mu, a_w0, a_w_up, a_a0, a_a_up, a_g_up,
              a_kk_scale, a_k_a, a_r_k, a_lnx_g, a_lnx_b, a_w_out,
              b_w_q, b_q_norm, b_w_out, kv_norm, kv_w, kv_k_norm, rel_bias):
    b, s = x.shape[:2]
    k_blocks, v_blocks = None, None
    for layer in range(DEPTH):
        x = x + 0.5 * swiglu_ffn(x, ffn_pre_norm[layer], ffn_pre_w_in[layer], ffn_pre_w_out[layer])
        u = rms_norm(x, mix_norm[layer])
        if layer < N_A_LAYERS:
            i = layer
            proj = u @ a_w_in[i]
            y_main = rwkv7_mix(proj[..., :RWKV_SHIFT_W], a_shift_mu[i], a_w0[i], a_w_up[i],
                               a_a0[i], a_a_up[i], a_g_up[i], a_kk_scale[i], a_k_a[i],
                               a_r_k[i], a_lnx_g[i], a_lnx_b[i])
            y_mem = memory_attention(split_heads(proj[..., RWKV_SHIFT_W:]), mem, mem_norm[layer],
                                     mem_w_kv[layer], mem_q_norm[layer], mem_k_norm[layer])
            y = jnp.concatenate([y_main, y_mem.reshape(b, s, MEM_W)], axis=-1) @ a_w_out[i]
        else:
            j = layer - N_A_LAYERS
            q_all = u @ b_w_q[j]
            y_dil = dilated_attention(split_heads(q_all[..., :DIL_W]), k_blocks, v_blocks,
                                      b_q_norm[j], rel_bias, s)
            y_mem = memory_attention(split_heads(q_all[..., DIL_W:]), mem, mem_norm[layer],
                                     mem_w_kv[layer], mem_q_norm[layer], mem_k_norm[layer])
            y = jnp.concatenate([y_dil.reshape(b, s, DIL_OUT_W),
                                 y_mem.reshape(b, s, MEM_W)], axis=-1) @ b_w_out[j]
        x = x + y
        x = x + 0.5 * swiglu_ffn(x, ffn_post_norm[layer], ffn_post_w_in[layer], ffn_post_w_out[layer])
        if layer == N_A_LAYERS - 1:
            k_blocks, v_blocks = shared_kv_blocks(x, kv_norm, kv_w, kv_k_norm)
    return x
```

```python
import functools
import math

import jax
import jax.numpy as jnp
import numpy as np
from jax import lax
from jax.experimental import pallas as pl
from jax.experimental.pallas import tpu as pltpu

F32 = jnp.float32
BF16 = jnp.bfloat16

HEAD_DIM = 64
MEM_HEADS = 4
MEM_W = MEM_HEADS * HEAD_DIM
RWKV_HEADS = 12
RWKV_W = RWKV_HEADS * HEAD_DIM
DECAY_LORA = 64
AAA_LORA = 64
GATE_LORA = 128
RWKV_SHIFT_W = 3 * RWKV_W + DECAY_LORA + AAA_LORA + GATE_LORA
LORA_LO = 3 * RWKV_W
DIL_GROUPS = ((128, 1), (512, 4), (2048, 16))
DIL_GROUP_HEADS = 4
DIL_W = len(DIL_GROUPS) * DIL_GROUP_HEADS * HEAD_DIM
DIL_OUT_W = DIL_GROUP_HEADS * HEAD_DIM
BLOCK = 128
NUM_BUCKETS = 32
MAX_DISTANCE = 2048
NORM_EPS = 1e-6
LNX_EPS = 64e-5
NEG_INF = -1e30

V7X_LANES = 128
V7X_SUBLANES = 8
V7X_VMEM_BYTES = 64 * 1024 * 1024

RWKV_CHUNK = 64
PAIR_W = 2 * HEAD_DIM
SLAB_W = 4 * HEAD_DIM

_NT = (((1,), (1,)), ((), ()))
_TN = (((0,), (0,)), ((), ()))


def _cparams(n_axes, vmem_mib):
    return pltpu.CompilerParams(
        dimension_semantics=("arbitrary",) * n_axes,
        vmem_limit_bytes=min(vmem_mib * 1024 * 1024, V7X_VMEM_BYTES - 4 * 1024 * 1024),
    )


def _const_spec(shape):
    zeros = (0,) * len(shape)
    return pl.BlockSpec(shape, lambda *_: zeros)


def _dot(a, b):
    return jnp.dot(a, b, preferred_element_type=F32)


def _rms(x, g, eps):
    return x * lax.rsqrt(jnp.mean(x * x, axis=-1, keepdims=True) + eps) * g


def _split2(x):
    hi = x.astype(BF16)
    lo = (x - hi.astype(F32)).astype(BF16)
    return hi, lo


def _seg_sum(x, ones_bd):
    outs = []
    for s in range(x.shape[-1] // SLAB_W):
        hi, lo = _split2(x[:, s * SLAB_W:(s + 1) * SLAB_W])
        outs.append(_dot(hi, ones_bd) + _dot(lo, ones_bd))
    return outs[0] if len(outs) == 1 else jnp.concatenate(outs, axis=-1)


def _block_ones(n, blk, lower=False):
    i = np.arange(n)
    m = (i[:, None] // blk) == (i[None, :] // blk)
    if lower:
        m = m & (i[:, None] >= i[None, :])
    return jnp.asarray(m, dtype=BF16)


def _ffn_kernel(x_ref, g_ref, win_ref, wout_ref, o_ref):
    x = x_ref[...]
    d_ff = wout_ref.shape[0]
    xn = _rms(x, g_ref[...], NORM_EPS).astype(BF16)
    h = _dot(xn, win_ref[...])
    gate = h[:, :d_ff]
    up = h[:, d_ff:]
    act = (gate * jax.nn.sigmoid(gate) * up).astype(BF16)
    o_ref[...] = x + 0.5 * _dot(act, wout_ref[...])


def _ffn(x, g, w_in, w_out, tm=512):
    n, d = x.shape
    return pl.pallas_call(
        _ffn_kernel,
        grid=(n // tm,),
        in_specs=[
            pl.BlockSpec((tm, d), lambda i: (i, 0)),
            _const_spec((1, d)),
            pl.BlockSpec(w_in.shape, lambda i: (0, 0), pipeline_mode=pl.Buffered(1)),
            pl.BlockSpec(w_out.shape, lambda i: (0, 0), pipeline_mode=pl.Buffered(1)),
        ],
        out_specs=pl.BlockSpec((tm, d), lambda i: (i, 0)),
        out_shape=jax.ShapeDtypeStruct((n, d), F32),
        compiler_params=_cparams(1, 56),
        name="ffn",
    )(x, g, w_in, w_out)


def _memkv_kernel(mem_ref, g_ref, w_ref, kg_ref, ones_ref, k_ref, v_ref):
    m = _rms(mem_ref[0], g_ref[0], NORM_EPS).astype(BF16)
    kv = _dot(m, w_ref[0])
    k = kv[:, :MEM_W]
    ms = _seg_sum(k * k, ones_ref[...]) * (1.0 / HEAD_DIM)
    k_ref[0, 0] = (k * lax.rsqrt(ms + NORM_EPS) * kg_ref[0]).astype(BF16)
    v_ref[0, 0] = kv[:, MEM_W:].astype(BF16)


def _memkv(mem, mem_norm, w_kv, k_gain, ones_bd):
    b, m, d = mem.shape
    depth = w_kv.shape[0]
    out = jax.ShapeDtypeStruct((depth, b, m, MEM_W), BF16)
    return pl.pallas_call(
        _memkv_kernel,
        grid=(depth, b),
        in_specs=[
            pl.BlockSpec((1, m, d), lambda l, i: (i, 0, 0)),
            pl.BlockSpec((1, 1, d), lambda l, i: (l, 0, 0)),
            pl.BlockSpec((1, d, 2 * MEM_W), lambda l, i: (l, 0, 0)),
            pl.BlockSpec((1, 1, MEM_W), lambda l, i: (l, 0, 0)),
            _const_spec((SLAB_W, SLAB_W)),
        ],
        out_specs=[pl.BlockSpec((1, 1, m, MEM_W), lambda l, i: (l, i, 0, 0))] * 2,
        out_shape=[out, out],
        compiler_params=_cparams(2, 32),
        name="memkv",
    )(mem, mem_norm, w_kv, k_gain, ones_bd)


def _head_norm(q, gain, ones_bd):
    ms = _seg_sum(q * q, ones_bd) * (1.0 / HEAD_DIM)
    return q * lax.rsqrt(ms + NORM_EPS) * gain


def _aproj_kernel(x_ref, g_ref, w_ref, mu_ref, qg_ref, ones_ref, ps_ref, qm_ref, carry_ref, *, tiles_per_seq):
    i = pl.program_id(0)
    tm = x_ref.shape[0]
    u = _rms(x_ref[...], g_ref[...], NORM_EPS).astype(BF16)
    proj = _dot(u, w_ref[...])
    p = proj[:, :RWKV_SHIFT_W]

    @pl.when(i % tiles_per_seq == 0)
    def _():
        carry_ref[...] = jnp.zeros_like(carry_ref)

    row = lax.broadcasted_iota(jnp.int32, p.shape, 0)
    prev = jnp.where(row == 0, carry_ref[V7X_SUBLANES - 1:V7X_SUBLANES, :], pltpu.roll(p, 1, 0))
    ps_ref[...] = p + mu_ref[...] * (prev - p)
    carry_ref[...] = p[tm - V7X_SUBLANES:, :]
    qm_ref[...] = _head_norm(proj[:, RWKV_SHIFT_W:], qg_ref[...], ones_ref[...])


def _aproj(x, g, w_in, mu, q_gain, ones_bd, seq_len, tm=512):
    n, d = x.shape
    return pl.pallas_call(
        functools.partial(_aproj_kernel, tiles_per_seq=seq_len // tm),
        grid=(n // tm,),
        in_specs=[
            pl.BlockSpec((tm, d), lambda i: (i, 0)),
            _const_spec((1, d)),
            pl.BlockSpec(w_in.shape, lambda i: (0, 0), pipeline_mode=pl.Buffered(1)),
            _const_spec((1, RWKV_SHIFT_W)),
            _const_spec((1, MEM_W)),
            _const_spec((SLAB_W, SLAB_W)),
        ],
        out_specs=[pl.BlockSpec((tm, RWKV_SHIFT_W), lambda i: (i, 0)),
                   pl.BlockSpec((tm, MEM_W), lambda i: (i, 0))],
        out_shape=[jax.ShapeDtypeStruct((n, RWKV_SHIFT_W), F32),
                   jax.ShapeDtypeStruct((n, MEM_W), F32)],
        scratch_shapes=[pltpu.VMEM((V7X_SUBLANES, RWKV_SHIFT_W), F32)],
        compiler_params=_cparams(1, 48),
        name="aproj",
    )(x, g, w_in, mu, q_gain, ones_bd)


def _qproj_kernel(x_ref, g_ref, w_ref, qg_ref, ones_ref, slab_ref, *flat_refs, normed_w):
    u = _rms(x_ref[...], g_ref[...], NORM_EPS).astype(BF16)
    proj = _dot(u, w_ref[...])
    normed = _head_norm(proj[:, :normed_w], qg_ref[...], ones_ref[...])
    n_slabs = slab_ref.shape[1]
    for j in range(n_slabs):
        src = normed if (j + 1) * V7X_LANES <= normed_w else proj
        slab_ref[0, j] = src[:, j * V7X_LANES:(j + 1) * V7X_LANES]
    if flat_refs:
        flat_refs[0][...] = normed[:, n_slabs * V7X_LANES:]


def _qproj(x, g, w, q_gain, ones_bd, seq_len, slab_w, tm=512):
    n, d = x.shape
    wo = w.shape[1]
    normed_w = q_gain.shape[-1]
    n_slabs = slab_w // V7X_LANES
    tiles_per_seq = seq_len // tm
    out_specs = [pl.BlockSpec((1, n_slabs, tm, V7X_LANES),
                              lambda i: (i // tiles_per_seq, 0, i % tiles_per_seq, 0))]
    out_shape = [jax.ShapeDtypeStruct((n // seq_len, n_slabs, seq_len, V7X_LANES), F32)]
    if wo > slab_w:
        assert normed_w == wo
        out_specs.append(pl.BlockSpec((tm, wo - slab_w), lambda i: (i, 0)))
        out_shape.append(jax.ShapeDtypeStruct((n, wo - slab_w), F32))
    return pl.pallas_call(
        functools.partial(_qproj_kernel, normed_w=normed_w),
        grid=(n // tm,),
        in_specs=[
            pl.BlockSpec((tm, d), lambda i: (i, 0)),
            _const_spec((1, d)),
            pl.BlockSpec(w.shape, lambda i: (0, 0), pipeline_mode=pl.Buffered(1)),
            _const_spec((1, normed_w)),
            _const_spec((SLAB_W, SLAB_W)),
        ],
        out_specs=out_specs,
        out_shape=out_shape,
        compiler_params=_cparams(1, 48),
        name="qproj",
    )(x, g, w, q_gain, ones_bd)


def _tri_inverse(nmat, ri, ci):
    eye = (ri == ci).astype(F32)
    same = {s: (ri >> s) == (ci >> s) for s in (3, 4, 5)}
    n8 = jnp.where(same[3], nmat, 0.0)
    n8b = n8.astype(BF16)
    n2 = _dot(n8b, n8b)
    n2b = n2.astype(BF16)
    n4 = _dot(n2b, n2b)
    t = _dot((eye + n8).astype(BF16), (eye + n2).astype(BF16))
    t = _dot(t.astype(BF16), (eye + n4).astype(BF16))
    for lo, hi in ((3, 4), (4, 5), (5, None)):
        off = ~same[lo] if hi is None else (same[hi] & ~same[lo])
        noff = jnp.where(off, nmat, 0.0).astype(BF16)
        tb = t.astype(BF16)
        t = t + _dot(tb, _dot(noff, tb).astype(BF16))
    return t


def _rwkv_kernel(ps_ref, vec_ref, wup_ref, aup_ref, gup_ref, tri_ref, blk_ref, ones_ref, y_ref,
                 s_ref, rt_ref, kt_ref, bt_ref, at_ref, kh_ref, bh_ref, v_ref, gam_ref, yacc_ref,
                 bonus_ref, gate_ref):
    tblk = ps_ref.shape[1]
    n_pairs = RWKV_W // PAIR_W

    @pl.when(pl.program_id(1) == 0)
    def _():
        s_ref[...] = jnp.zeros_like(s_ref)

    ps = ps_ref[0]
    r = ps[:, :RWKV_W]
    k = ps[:, RWKV_W:2 * RWKV_W]
    v = ps[:, 2 * RWKV_W:3 * RWKV_W]
    lora_in = ps[:, LORA_LO:LORA_LO + PAIR_W]
    g_lo = ps[:, LORA_LO + PAIR_W:]
    w0, a0, kk_scale, k_a = vec_ref[0:1, :], vec_ref[1:2, :], vec_ref[2:3, :], vec_ref[3:4, :]
    r_k, lnx_g, lnx_b = vec_ref[4:5, :], vec_ref[5:6, :], vec_ref[6:7, :]
    ones_bd = ones_ref[...]

    nz = -(w0 + _dot(jnp.tanh(lora_in).astype(BF16), wup_ref[...]))
    softplus = jnp.maximum(nz, 0.0) + jnp.log(1.0 + jnp.exp(-jnp.abs(nz)))
    lw = -jnp.exp(-softplus - 0.5)
    a = jax.nn.sigmoid(a0 + _dot(lora_in.astype(BF16), aup_ref[...]))
    gate_ref[...] = _dot(jax.nn.sigmoid(g_lo).astype(BF16), gup_ref[...])
    kk = k * kk_scale
    kk = kk / jnp.maximum(jnp.sqrt(_seg_sum(kk * kk, ones_bd)), 1e-12)
    k2 = k * (1.0 + (a - 1.0) * k_a)
    kka = kk * a
    bonus_ref[...] = _seg_sum(r * k2 * r_k, ones_bd) * v

    hi = lw.astype(BF16)
    rem = lw - hi.astype(F32)
    mid = rem.astype(BF16)
    low = (rem - mid.astype(F32)).astype(BF16)
    tri = tri_ref[...]
    blk = blk_ref[...]
    gcum = _dot(tri, hi) + _dot(tri, mid) + _dot(tri, low)
    gtot = _dot(blk, hi) + _dot(blk, mid) + _dot(blk, low)
    e_neg = jnp.exp(-gcum)
    e_last = jnp.exp(gtot - gcum)
    rt_ref[...] = r * jnp.exp(gcum)
    kt_ref[...] = k2 * e_neg
    bt_ref[...] = kka * e_neg
    at_ref[...] = -kk * jnp.exp(gcum - lw)
    kh_ref[...] = k2 * e_last
    bh_ref[...] = kka * e_last
    v_ref[...] = v
    gam_ref[...] = jnp.exp(gtot)

    ri = lax.broadcasted_iota(jnp.int32, (PAIR_W, PAIR_W), 0)
    ci = lax.broadcasted_iota(jnp.int32, (PAIR_W, PAIR_W), 1)
    tpos = ri & (RWKV_CHUNK - 1)
    spos = ci & (RWKV_CHUNK - 1)
    strict = tpos > spos
    incl = tpos >= spos
    head0 = lax.broadcasted_iota(jnp.int32, (RWKV_CHUNK, PAIR_W), 1) < HEAD_DIM

    def embed(x):
        return jnp.concatenate([jnp.where(head0, x, 0.0), jnp.where(head0, 0.0, x)], axis=0)

    def chunk_body(c, carry):
        r0 = pl.multiple_of(c * RWKV_CHUNK, RWKV_CHUNK)
        rows = pl.ds(r0, RWKV_CHUNK)
        for p in range(n_pairs):
            lanes = slice(p * PAIR_W, (p + 1) * PAIR_W)
            at_m = embed(at_ref[rows, lanes]).astype(BF16)
            rt_m = embed(rt_ref[rows, lanes]).astype(BF16)
            bt_m = embed(bt_ref[rows, lanes]).astype(BF16)
            kt_m = embed(kt_ref[rows, lanes]).astype(BF16)
            v_m = embed(v_ref[rows, lanes]).astype(BF16)
            m4 = lax.dot_general(jnp.concatenate([at_m, rt_m], axis=0),
                                 jnp.concatenate([bt_m, kt_m], axis=0), _NT,
                                 preferred_element_type=F32)
            n_ab = jnp.where(strict, m4[:PAIR_W, :PAIR_W], 0.0)
            a_ak = jnp.where(strict, m4[:PAIR_W, PAIR_W:], 0.0).astype(BF16)
            a_rb = jnp.where(incl, m4[PAIR_W:, :PAIR_W], 0.0).astype(BF16)
            a_rk = jnp.where(incl, m4[PAIR_W:, PAIR_W:], 0.0).astype(BF16)
            t_inv = _tri_inverse(n_ab, ri, ci).astype(BF16)
            akv = _dot(a_ak, v_m).astype(BF16)
            wu0 = _dot(t_inv, jnp.concatenate([at_m, akv], axis=1))
            state = s_ref[p]
            state_b = state.astype(BF16)
            u = (lax.dot_general(wu0[:, :PAIR_W].astype(BF16), state_b, _NT, preferred_element_type=F32)
                 + wu0[:, PAIR_W:])
            u_b = u.astype(BF16)
            y_m = (lax.dot_general(rt_m, state_b, _NT, preferred_element_type=F32)
                   + _dot(a_rb, u_b) + _dot(a_rk, v_m))
            yacc_ref[rows, lanes] = y_m[:RWKV_CHUNK] + y_m[RWKV_CHUNK:]
            bk_h = jnp.concatenate([embed(bh_ref[rows, lanes]), embed(kh_ref[rows, lanes])],
                                   axis=0).astype(BF16)
            upd = lax.dot_general(jnp.concatenate([u_b, v_m], axis=0), bk_h, _TN,
                                  preferred_element_type=F32)
            s_ref[p] = state * gam_ref[pl.ds(r0, 1), lanes] + upd
        return carry

    lax.fori_loop(0, tblk // RWKV_CHUNK, chunk_body, 0)

    y = yacc_ref[...]
    mean = _seg_sum(y, ones_bd) * (1.0 / HEAD_DIM)
    dev = y - mean
    var = _seg_sum(dev * dev, ones_bd) * (1.0 / HEAD_DIM)
    yn = dev * lax.rsqrt(var + LNX_EPS) * lnx_g + lnx_b
    y_ref[0] = (yn + bonus_ref[...]) * gate_ref[...]


def _rwkv(ps, vecs, w_up_p, a_up_p, g_up, ones_bd, tblk=256):
    b, s, _ = ps.shape
    tri = _block_ones(tblk, RWKV_CHUNK, lower=True)
    blk = _block_ones(tblk, RWKV_CHUNK)
    act = pltpu.VMEM((tblk, RWKV_W), F32)
    return pl.pallas_call(
        _rwkv_kernel,
        grid=(b, s // tblk),
        in_specs=[
            pl.BlockSpec((1, tblk, RWKV_SHIFT_W), lambda i, t: (i, t, 0)),
            _const_spec(vecs.shape),
            _const_spec(w_up_p.shape),
            _const_spec(a_up_p.shape),
            _const_spec(g_up.shape),
            _const_spec((tblk, tblk)),
            _const_spec((tblk, tblk)),
            _const_spec((SLAB_W, SLAB_W)),
        ],
        out_specs=pl.BlockSpec((1, tblk, RWKV_W), lambda i, t: (i, t, 0)),
        out_shape=jax.ShapeDtypeStruct((b, s, RWKV_W), F32),
        scratch_shapes=[pltpu.VMEM((RWKV_W // PAIR_W, PAIR_W, PAIR_W), F32)] + [act] * 11,
        compiler_params=_cparams(2, 48),
        name="rwkv",
    )(ps, vecs, w_up_p, a_up_p, g_up, tri, blk, ones_bd)


def _mixout_kernel(x_ref, ya_ref, qm_ref, k_ref, v_ref, w_ref, o_ref):
    wa = ya_ref.shape[-1]
    qm = qm_ref[...]
    kmem = k_ref[0]
    vmem = v_ref[0]
    head = lax.broadcasted_iota(jnp.int32, qm.shape, 1) // HEAD_DIM
    y_mem = jnp.zeros(qm.shape, F32)
    for h in range(MEM_HEADS):
        mine = head == h
        logits = lax.dot_general(jnp.where(mine, qm, 0.0).astype(BF16), kmem, _NT,
                                 preferred_element_type=F32)
        p = jnp.exp(logits - jnp.max(logits, axis=-1, keepdims=True))
        inv_l = 1.0 / jnp.sum(p, axis=-1, keepdims=True)
        y_mem = jnp.where(mine, _dot(p.astype(BF16), vmem) * inv_l, y_mem)
    y = _dot(ya_ref[...].astype(BF16), w_ref[:wa, :]) + _dot(y_mem.astype(BF16), w_ref[wa:, :])
    o_ref[...] = x_ref[...] + y


def _mixout(x, ya, qm, k_mem, v_mem, w_out, seq_len, tm=512):
    n, d = x.shape
    wa = ya.shape[-1]
    tiles_per_seq = seq_len // tm
    m = k_mem.shape[1]
    return pl.pallas_call(
        _mixout_kernel,
        grid=(n // tm,),
        in_specs=[
            pl.BlockSpec((tm, d), lambda i: (i, 0)),
            pl.BlockSpec((tm, wa), lambda i: (i, 0)),
            pl.BlockSpec((tm, MEM_W), lambda i: (i, 0)),
            pl.BlockSpec((1, m, MEM_W), lambda i: (i // tiles_per_seq, 0, 0)),
            pl.BlockSpec((1, m, MEM_W), lambda i: (i // tiles_per_seq, 0, 0)),
            _const_spec(w_out.shape),
        ],
        out_specs=pl.BlockSpec((tm, d), lambda i: (i, 0)),
        out_shape=jax.ShapeDtypeStruct((n, d), F32),
        compiler_params=_cparams(1, 40),
        name="mixout",
    )(x, ya, qm, k_mem, v_mem, w_out)


def _t5_bucket(dist):
    max_exact = NUM_BUCKETS // 2
    d_f = jnp.maximum(dist, 1).astype(F32)
    large = max_exact + (jnp.log(d_f / max_exact) / math.log(MAX_DISTANCE / max_exact)
                         * (NUM_BUCKETS - max_exact)).astype(jnp.int32)
    large = jnp.minimum(large, NUM_BUCKETS - 1)
    return jnp.where(dist < max_exact, dist, large)


def _band_buckets():
    qi = jnp.arange(BLOCK)[:, None]
    kj = jnp.arange(2 * BLOCK)[None, :]
    dsub = BLOCK + qi - kj
    out = []
    for window, dil in DIL_GROUPS:
        band = (dsub >= 0) & (dsub <= window // dil)
        out.append(jnp.where(band, _t5_bucket(jnp.maximum(dsub, 0) * dil), -1))
    return jnp.stack(out).astype(jnp.int32)


def _bias_kernel(tab_ref, idx_ref, o_ref):
    head = pl.program_id(0)
    idx = idx_ref[0]
    acc = jnp.full(idx.shape, NEG_INF, F32)
    for bucket in range(NUM_BUCKETS):
        acc = jnp.where(idx == bucket, tab_ref[bucket, head], acc)
    o_ref[0] = acc


def _band_bias(rel_bias):
    n_heads = rel_bias.shape[1]
    return pl.pallas_call(
        _bias_kernel,
        grid=(n_heads,),
        in_specs=[
            pl.BlockSpec(memory_space=pltpu.SMEM),
            pl.BlockSpec((1, BLOCK, 2 * BLOCK), lambda h: (h // DIL_GROUP_HEADS, 0, 0)),
        ],
        out_specs=pl.BlockSpec((1, BLOCK, 2 * BLOCK), lambda h: (h, 0, 0)),
        out_shape=jax.ShapeDtypeStruct((n_heads, BLOCK, 2 * BLOCK), F32),
        compiler_params=_cparams(1, 16),
        name="band_bias",
    )(rel_bias, _band_buckets())


def _dil_group(dil, q_ref, k_ref, v_ref, bias_ref, acc_ref, m_ref, l_ref):
    seq_len = q_ref.shape[2]
    n_blocks = seq_len // (dil * BLOCK)
    shift = n_blocks.bit_length() - 1
    head0 = lax.broadcasted_iota(jnp.int32, (BLOCK, PAIR_W), 1) < HEAD_DIM
    in_prev = lax.broadcasted_iota(jnp.int32, (BLOCK, 2 * BLOCK), 1) < BLOCK

    def rows(sub, blk):
        if dil == 1:
            return pl.ds(pl.multiple_of(BLOCK * blk, BLOCK), BLOCK)
        return pl.ds(sub + dil * BLOCK * blk, BLOCK, stride=dil)

    def tile(t, carry):
        sub = t >> shift
        blk = t & (n_blocks - 1)
        own = rows(sub, blk)
        prev = rows(sub, jnp.maximum(blk - 1, 0))
        no_prev = jnp.logical_and(blk == 0, in_prev)
        for pair in range(DIL_OUT_W // PAIR_W):
            q = q_ref[0, pair, own, :]
            keys = jnp.concatenate([k_ref[0, pair, prev, :], k_ref[0, pair, own, :]], axis=0).astype(BF16)
            vals = jnp.concatenate([v_ref[0, pair, prev, :], v_ref[0, pair, own, :]], axis=0).astype(BF16)
            m_t = l_t = acc_t = None
            for h in range(2):
                mine = head0 if h == 0 else ~head0
                logits = lax.dot_general(jnp.where(mine, q, 0.0).astype(BF16), keys, _NT,
                                         preferred_element_type=F32) + bias_ref[2 * pair + h]
                logits = jnp.where(no_prev, NEG_INF, logits)
                mx = jnp.max(logits, axis=-1, keepdims=True)
                p = jnp.exp(logits - mx)
                ls = jnp.sum(p, axis=-1, keepdims=True)
                pv = _dot(p.astype(BF16), vals)
                if h == 0:
                    m_t, l_t, acc_t = mx, ls, pv
                else:
                    m_t = jnp.where(mine, mx, m_t)
                    l_t = jnp.where(mine, ls, l_t)
                    acc_t = jnp.where(mine, pv, acc_t)
            m_old = m_ref[pair, own, :]
            m_new = jnp.maximum(m_old, m_t)
            e_old = jnp.exp(m_old - m_new)
            e_t = jnp.exp(m_t - m_new)
            m_ref[pair, own, :] = m_new
            l_ref[pair, own, :] = l_ref[pair, own, :] * e_old + l_t * e_t
            acc_ref[pair, own, :] = acc_ref[pair, own, :] * e_old + acc_t * e_t
        return carry

    lax.fori_loop(0, dil * n_blocks, tile, 0)


def _dil_kernel(q_ref, k_ref, v_ref, bias_ref, o_ref, acc_ref, m_ref, l_ref):
    g = pl.program_id(1)

    @pl.when(g == 0)
    def _():
        m_ref[...] = jnp.full(m_ref.shape, NEG_INF, F32)
        l_ref[...] = jnp.zeros_like(l_ref)
        acc_ref[...] = jnp.zeros_like(acc_ref)

    for gi, (_, dil) in enumerate(DIL_GROUPS):
        pl.when(g == gi)(functools.partial(_dil_group, dil, q_ref, k_ref, v_ref, bias_ref, acc_ref, m_ref, l_ref))

    @pl.when(g == len(DIL_GROUPS) - 1)
    def _():
        for pair in range(DIL_OUT_W // PAIR_W):
            o_ref[0, :, pair * PAIR_W:(pair + 1) * PAIR_W] = acc_ref[pair] / l_ref[pair]


def _dilated_attention(q_slabs, kv_slabs, bias):
    b, _, s, _ = q_slabs.shape
    n_groups = len(DIL_GROUPS)
    pairs = DIL_OUT_W // PAIR_W
    blk = (1, pairs, s, PAIR_W)
    return pl.pallas_call(
        _dil_kernel,
        grid=(b, n_groups),
        in_specs=[
            pl.BlockSpec(blk, lambda i, g: (i, g, 0, 0)),
            pl.BlockSpec(blk, lambda i, g: (i, g, 0, 0)),
            pl.BlockSpec(blk, lambda i, g: (i, n_groups + g, 0, 0)),
            pl.BlockSpec((DIL_GROUP_HEADS, BLOCK, 2 * BLOCK), lambda i, g: (g, 0, 0)),
        ],
        out_specs=pl.BlockSpec((1, s, DIL_OUT_W), lambda i, g: (i, 0, 0)),
        out_shape=jax.ShapeDtypeStruct((b, s, DIL_OUT_W), F32),
        scratch_shapes=[pltpu.VMEM((pairs, s, PAIR_W), F32)] * 3,
        compiler_params=_cparams(2, 56),
        name="dilated_attention",
    )(q_slabs, kv_slabs, kv_slabs, bias)


def kernel(x, mem, ffn_pre_norm, ffn_pre_w_in, ffn_pre_w_out, mix_norm, ffn_post_norm, ffn_post_w_in, ffn_post_w_out, mem_norm, mem_w_kv, mem_q_norm, mem_k_norm, a_w_in, a_shift_mu, a_w0, a_w_up, a_a0, a_a_up, a_g_up, a_kk_scale, a_k_a, a_r_k, a_lnx_g, a_lnx_b, a_w_out, b_w_q, b_q_norm, b_w_out, kv_norm, kv_w, kv_k_norm, rel_bias):
    b, s, d = x.shape
    depth = ffn_pre_w_in.shape[0]
    n_a = a_w_in.shape[0]
    n = b * s
    scale = 1.0 / math.sqrt(HEAD_DIM)
    ones_bd = _block_ones(SLAB_W, HEAD_DIM)
    row = lambda p: p.reshape(1, -1)

    k_mem, v_mem = _memkv(mem, mem_norm[:, None, :], mem_w_kv.astype(BF16),
                          jnp.tile(mem_k_norm, (1, MEM_HEADS))[:, None, :], ones_bd)
    mem_q_gain = jnp.tile(mem_q_norm, (1, MEM_HEADS)) * scale

    xf = x.reshape(n, d)
    kv = None
    bias = None
    for layer in range(depth):
        xf = _ffn(xf, row(ffn_pre_norm[layer]), ffn_pre_w_in[layer].astype(BF16),
                  ffn_pre_w_out[layer].astype(BF16))
        if layer < n_a:
            i = layer
            ps, qm = _aproj(xf, row(mix_norm[layer]), a_w_in[i].astype(BF16), row(a_shift_mu[i]),
                            row(mem_q_gain[layer]), ones_bd, s)
            zeros = jnp.zeros((DECAY_LORA, RWKV_W), F32)
            vecs = jnp.stack([a_w0[i], a_a0[i], a_kk_scale[i], a_k_a[i], a_r_k[i].reshape(-1),
                              a_lnx_g[i], a_lnx_b[i], jnp.zeros((RWKV_W,), F32)])
            y_main = _rwkv(ps.reshape(b, s, RWKV_SHIFT_W), vecs,
                           jnp.concatenate([a_w_up[i], zeros]).astype(BF16),
                           jnp.concatenate([zeros, a_a_up[i]]).astype(BF16),
                           a_g_up[i].astype(BF16), ones_bd)
            xf = _mixout(xf, y_main.reshape(n, RWKV_W), qm, k_mem[layer], v_mem[layer],
                         a_w_out[i].astype(BF16), s)
        else:
            j = layer - n_a
            q_gain = jnp.concatenate([jnp.tile(b_q_norm[j], DIL_W // HEAD_DIM) * scale, mem_q_gain[layer]])
            q_dil, qm = _qproj(xf, row(mix_norm[layer]), b_w_q[j].astype(BF16), row(q_gain), ones_bd, s, DIL_W)
            if bias is None:
                bias = _band_bias(rel_bias)
            y_dil = _dilated_attention(q_dil, kv, bias)
            xf = _mixout(xf, y_dil.reshape(n, DIL_OUT_W), qm, k_mem[layer], v_mem[layer],
                         b_w_out[j].astype(BF16), s)
        xf = _ffn(xf, row(ffn_post_norm[layer]), ffn_post_w_in[layer].astype(BF16),
                  ffn_post_w_out[layer].astype(BF16))
        if layer == n_a - 1:
            k_gain = jnp.tile(kv_k_norm, DIL_W // HEAD_DIM)
            kv, = _qproj(xf, row(kv_norm), kv_w.astype(BF16), row(k_gain), ones_bd, s, 2 * DIL_W)
    return xf.reshape(b, s, d)
```

```python
import functools
import math

import jax
import jax.numpy as jnp
import numpy as np
from jax import lax
from jax.experimental import pallas as pl
from jax.experimental.pallas import tpu as pltpu

F32 = jnp.float32
BF16 = jnp.bfloat16

HEAD_DIM = 64
MEM_HEADS = 4
MEM_W = MEM_HEADS * HEAD_DIM
RWKV_HEADS = 12
RWKV_W = RWKV_HEADS * HEAD_DIM
DECAY_LORA = 64
AAA_LORA = 64
GATE_LORA = 128
RWKV_SHIFT_W = 3 * RWKV_W + DECAY_LORA + AAA_LORA + GATE_LORA
LORA_LO = 3 * RWKV_W
DIL_GROUPS = ((128, 1), (512, 4), (2048, 16))
DIL_GROUP_HEADS = 4
DIL_W = len(DIL_GROUPS) * DIL_GROUP_HEADS * HEAD_DIM
DIL_OUT_W = DIL_GROUP_HEADS * HEAD_DIM
BLOCK = 128
NUM_BUCKETS = 32
MAX_DISTANCE = 2048
NORM_EPS = 1e-6
LNX_EPS = 64e-5
NEG_INF = -1e30

V7X_LANES = 128
V7X_SUBLANES = 8
V7X_VMEM_BYTES = 64 * 1024 * 1024

RWKV_CHUNK = 64
PAIR_W = 2 * HEAD_DIM
SLAB_W = 4 * HEAD_DIM

_NT = (((1,), (1,)), ((), ()))
_TN = (((0,), (0,)), ((), ()))


def _cparams(n_axes, vmem_mib):
    return pltpu.CompilerParams(
        dimension_semantics=("arbitrary",) * n_axes,
        vmem_limit_bytes=min(vmem_mib * 1024 * 1024, V7X_VMEM_BYTES - 4 * 1024 * 1024),
    )


def _const_spec(shape):
    zeros = (0,) * len(shape)
    return pl.BlockSpec(shape, lambda *_: zeros)


def _dot(a, b):
    return jnp.dot(a, b, preferred_element_type=F32)


def _rms(x, g, eps):
    return x * lax.rsqrt(jnp.mean(x * x, axis=-1, keepdims=True) + eps) * g


def _split2(x):
    hi = x.astype(BF16)
    lo = (x - hi.astype(F32)).astype(BF16)
    return hi, lo


def _seg_sum(x, ones_bd):
    outs = []
    for s in range(x.shape[-1] // SLAB_W):
        hi, lo = _split2(x[:, s * SLAB_W:(s + 1) * SLAB_W])
        outs.append(_dot(hi, ones_bd) + _dot(lo, ones_bd))
    return outs[0] if len(outs) == 1 else jnp.concatenate(outs, axis=-1)


def _block_ones(n, blk, lower=False):
    i = np.arange(n)
    m = (i[:, None] // blk) == (i[None, :] // blk)
    if lower:
        m = m & (i[:, None] >= i[None, :])
    return jnp.asarray(m, dtype=BF16)


def _ffn_kernel(x_ref, g_ref, win_ref, wout_ref, o_ref):
    x = x_ref[...]
    d_ff = wout_ref.shape[0]
    xn = _rms(x, g_ref[...], NORM_EPS).astype(BF16)
    h = _dot(xn, win_ref[...])
    gate = h[:, :d_ff]
    up = h[:, d_ff:]
    act = (gate * jax.nn.sigmoid(gate) * up).astype(BF16)
    o_ref[...] = x + 0.5 * _dot(act, wout_ref[...])


def _ffn(x, g, w_in, w_out, tm=512):
    n, d = x.shape
    return pl.pallas_call(
        _ffn_kernel,
        grid=(n // tm,),
        in_specs=[
            pl.BlockSpec((tm, d), lambda i: (i, 0)),
            _const_spec((1, d)),
            pl.BlockSpec(w_in.shape, lambda i: (0, 0), pipeline_mode=pl.Buffered(1)),
            pl.BlockSpec(w_out.shape, lambda i: (0, 0), pipeline_mode=pl.Buffered(1)),
        ],
        out_specs=pl.BlockSpec((tm, d), lambda i: (i, 0)),
        out_shape=jax.ShapeDtypeStruct((n, d), F32),
        compiler_params=_cparams(1, 56),
        name="ffn",
    )(x, g, w_in, w_out)


def _memkv_kernel(mem_ref, g_ref, w_ref, kg_ref, ones_ref, k_ref, v_ref):
    m = _rms(mem_ref[0], g_ref[0], NORM_EPS).astype(BF16)
    kv = _dot(m, w_ref[0])
    k = kv[:, :MEM_W]
    ms = _seg_sum(k * k, ones_ref[...]) * (1.0 / HEAD_DIM)
    k_ref[0, 0] = (k * lax.rsqrt(ms + NORM_EPS) * kg_ref[0]).astype(BF16)
    v_ref[0, 0] = kv[:, MEM_W:].astype(BF16)


def _memkv(mem, mem_norm, w_kv, k_gain, ones_bd):
    b, m, d = mem.shape
    depth = w_kv.shape[0]
    out = jax.ShapeDtypeStruct((depth, b, m, MEM_W), BF16)
    return pl.pallas_call(
        _memkv_kernel,
        grid=(depth, b),
        in_specs=[
            pl.BlockSpec((1, m, d), lambda l, i: (i, 0, 0)),
            pl.BlockSpec((1, 1, d), lambda l, i: (l, 0, 0)),
            pl.BlockSpec((1, d, 2 * MEM_W), lambda l, i: (l, 0, 0)),
            pl.BlockSpec((1, 1, MEM_W), lambda l, i: (l, 0, 0)),
            _const_spec((SLAB_W, SLAB_W)),
        ],
        out_specs=[pl.BlockSpec((1, 1, m, MEM_W), lambda l, i: (l, i, 0, 0))] * 2,
        out_shape=[out, out],
        compiler_params=_cparams(2, 32),
        name="memkv",
    )(mem, mem_norm, w_kv, k_gain, ones_bd)


def _head_norm(q, gain, ones_bd):
    ms = _seg_sum(q * q, ones_bd) * (1.0 / HEAD_DIM)
    return q * lax.rsqrt(ms + NORM_EPS) * gain


def _aproj_kernel(x_ref, g_ref, w_ref, mu_ref, qg_ref, ones_ref, ps_ref, qm_ref, carry_ref, *, tiles_per_seq):
    i = pl.program_id(0)
    tm = x_ref.shape[0]
    u = _rms(x_ref[...], g_ref[...], NORM_EPS).astype(BF16)
    proj = _dot(u, w_ref[...])
    p = proj[:, :RWKV_SHIFT_W]

    @pl.when(i % tiles_per_seq == 0)
    def _():
        carry_ref[...] = jnp.zeros_like(carry_ref)

    row = lax.broadcasted_iota(jnp.int32, p.shape, 0)
    prev = jnp.where(row == 0, carry_ref[V7X_SUBLANES - 1:V7X_SUBLANES, :], pltpu.roll(p, 1, 0))
    ps_ref[...] = p + mu_ref[...] * (prev - p)
    carry_ref[...] = p[tm - V7X_SUBLANES:, :]
    qm_ref[...] = _head_norm(proj[:, RWKV_SHIFT_W:], qg_ref[...], ones_ref[...])


def _aproj(x, g, w_in, mu, q_gain, ones_bd, seq_len, tm=512):
    n, d = x.shape
    return pl.pallas_call(
        functools.partial(_aproj_kernel, tiles_per_seq=seq_len // tm),
        grid=(n // tm,),
        in_specs=[
            pl.BlockSpec((tm, d), lambda i: (i, 0)),
            _const_spec((1, d)),
            pl.BlockSpec(w_in.shape, lambda i: (0, 0), pipeline_mode=pl.Buffered(1)),
            _const_spec((1, RWKV_SHIFT_W)),
            _const_spec((1, MEM_W)),
            _const_spec((SLAB_W, SLAB_W)),
        ],
        out_specs=[pl.BlockSpec((tm, RWKV_SHIFT_W), lambda i: (i, 0)),
                   pl.BlockSpec((tm, MEM_W), lambda i: (i, 0))],
        out_shape=[jax.ShapeDtypeStruct((n, RWKV_SHIFT_W), F32),
                   jax.ShapeDtypeStruct((n, MEM_W), F32)],
        scratch_shapes=[pltpu.VMEM((V7X_SUBLANES, RWKV_SHIFT_W), F32)],
        compiler_params=_cparams(1, 48),
        name="aproj",
    )(x, g, w_in, mu, q_gain, ones_bd)


def _qproj_kernel(x_ref, g_ref, w_ref, qg_ref, ones_ref, slab_ref, *flat_refs, normed_w):
    u = _rms(x_ref[...], g_ref[...], NORM_EPS).astype(BF16)
    proj = _dot(u, w_ref[...])
    normed = _head_norm(proj[:, :normed_w], qg_ref[...], ones_ref[...])
    n_slabs = slab_ref.shape[1]
    for j in range(n_slabs):
        src = normed if (j + 1) * V7X_LANES <= normed_w else proj
        slab_ref[0, j] = src[:, j * V7X_LANES:(j + 1) * V7X_LANES]
    if flat_refs:
        flat_refs[0][...] = normed[:, n_slabs * V7X_LANES:]


def _qproj(x, g, w, q_gain, ones_bd, seq_len, slab_w, tm=512):
    n, d = x.shape
    wo = w.shape[1]
    normed_w = q_gain.shape[-1]
    n_slabs = slab_w // V7X_LANES
    tiles_per_seq = seq_len // tm
    out_specs = [pl.BlockSpec((1, n_slabs, tm, V7X_LANES),
                              lambda i: (i // tiles_per_seq, 0, i % tiles_per_seq, 0))]
    out_shape = [jax.ShapeDtypeStruct((n // seq_len, n_slabs, seq_len, V7X_LANES), F32)]
    if wo > slab_w:
        assert normed_w == wo
        out_specs.append(pl.BlockSpec((tm, wo - slab_w), lambda i: (i, 0)))
        out_shape.append(jax.ShapeDtypeStruct((n, wo - slab_w), F32))
    return pl.pallas_call(
        functools.partial(_qproj_kernel, normed_w=normed_w),
        grid=(n // tm,),
        in_specs=[
            pl.BlockSpec((tm, d), lambda i: (i, 0)),
            _const_spec((1, d)),
            pl.BlockSpec(w.shape, lambda i: (0, 0), pipeline_mode=pl.Buffered(1)),
            _const_spec((1, normed_w)),
            _const_spec((SLAB_W, SLAB_W)),
        ],
        out_specs=out_specs,
        out_shape=out_shape,
        compiler_params=_cparams(1, 48),
        name="qproj",
    )(x, g, w, q_gain, ones_bd)


def _tri_inverse(nmats, ri, ci):
    eye = (ri == ci).astype(F32)
    same = {s: (ri >> s) == (ci >> s) for s in (3, 4, 5)}
    n8 = [jnp.where(same[3], n, 0.0) for n in nmats]
    n8b = [n.astype(BF16) for n in n8]
    n2 = [_dot(n, n) for n in n8b]
    n2b = [n.astype(BF16) for n in n2]
    n4 = [_dot(n, n) for n in n2b]
    t = [_dot((eye + a).astype(BF16), (eye + b).astype(BF16)) for a, b in zip(n8, n2)]
    t = [_dot(a.astype(BF16), (eye + b).astype(BF16)) for a, b in zip(t, n4)]
    for lo, hi in ((3, 4), (4, 5), (5, None)):
        off = ~same[lo] if hi is None else (same[hi] & ~same[lo])
        tb = [a.astype(BF16) for a in t]
        z = [_dot(jnp.where(off, n, 0.0).astype(BF16), a).astype(BF16) for n, a in zip(nmats, tb)]
        t = [a + _dot(ab, zz) for a, ab, zz in zip(t, tb, z)]
    return t


def _rwkv_kernel(ps_ref, vec_ref, wup_ref, aup_ref, gup_ref, tri_ref, blk_ref, ones_ref, y_ref,
                 s_ref, rt_ref, kt_ref, bt_ref, at_ref, kh_ref, bh_ref, v_ref, gam_ref, yacc_ref,
                 bonus_ref, gate_ref):
    tblk = ps_ref.shape[1]
    n_pairs = RWKV_W // PAIR_W

    @pl.when(pl.program_id(1) == 0)
    def _():
        s_ref[...] = jnp.zeros_like(s_ref)

    ps = ps_ref[0]
    r = ps[:, :RWKV_W]
    k = ps[:, RWKV_W:2 * RWKV_W]
    v = ps[:, 2 * RWKV_W:3 * RWKV_W]
    lora_in = ps[:, LORA_LO:LORA_LO + PAIR_W]
    g_lo = ps[:, LORA_LO + PAIR_W:]
    w0, a0, kk_scale, k_a = vec_ref[0:1, :], vec_ref[1:2, :], vec_ref[2:3, :], vec_ref[3:4, :]
    r_k, lnx_g, lnx_b = vec_ref[4:5, :], vec_ref[5:6, :], vec_ref[6:7, :]
    ones_bd = ones_ref[...]

    nz = -(w0 + _dot(jnp.tanh(lora_in).astype(BF16), wup_ref[...]))
    softplus = jnp.maximum(nz, 0.0) + jnp.log(1.0 + jnp.exp(-jnp.abs(nz)))
    lw = -jnp.exp(-softplus - 0.5)
    a = jax.nn.sigmoid(a0 + _dot(lora_in.astype(BF16), aup_ref[...]))
    gate_ref[...] = _dot(jax.nn.sigmoid(g_lo).astype(BF16), gup_ref[...])
    kk = k * kk_scale
    kk = kk / jnp.maximum(jnp.sqrt(_seg_sum(kk * kk, ones_bd)), 1e-12)
    k2 = k * (1.0 + (a - 1.0) * k_a)
    kka = kk * a
    bonus_ref[...] = _seg_sum(r * k2 * r_k, ones_bd) * v

    hi = lw.astype(BF16)
    rem = lw - hi.astype(F32)
    mid = rem.astype(BF16)
    low = (rem - mid.astype(F32)).astype(BF16)
    tri = tri_ref[...]
    blk = blk_ref[...]
    gcum = _dot(tri, hi) + _dot(tri, mid) + _dot(tri, low)
    gtot = _dot(blk, hi) + _dot(blk, mid) + _dot(blk, low)
    e_neg = jnp.exp(-gcum)
    e_last = jnp.exp(gtot - gcum)
    rt_ref[...] = r * jnp.exp(gcum)
    kt_ref[...] = k2 * e_neg
    bt_ref[...] = kka * e_neg
    at_ref[...] = -kk * jnp.exp(gcum - lw)
    kh_ref[...] = k2 * e_last
    bh_ref[...] = kka * e_last
    v_ref[...] = v
    gam_ref[...] = jnp.exp(gtot)

    ri = lax.broadcasted_iota(jnp.int32, (PAIR_W, PAIR_W), 0)
    ci = lax.broadcasted_iota(jnp.int32, (PAIR_W, PAIR_W), 1)
    tpos = ri & (RWKV_CHUNK - 1)
    spos = ci & (RWKV_CHUNK - 1)
    strict = tpos > spos
    incl = tpos >= spos
    head0 = lax.broadcasted_iota(jnp.int32, (RWKV_CHUNK, PAIR_W), 1) < HEAD_DIM

    def embed(x):
        return jnp.concatenate([jnp.where(head0, x, 0.0), jnp.where(head0, 0.0, x)], axis=0)

    def chunk_body(c, carry):
        r0 = pl.multiple_of(c * RWKV_CHUNK, RWKV_CHUNK)
        rows = pl.ds(r0, RWKV_CHUNK)
        at_c, rt_c, bt_c, kt_c = at_ref[rows, :], rt_ref[rows, :], bt_ref[rows, :], kt_ref[rows, :]
        v_c, bh_c, kh_c = v_ref[rows, :], bh_ref[rows, :], kh_ref[rows, :]
        gam_c = gam_ref[pl.ds(r0, 1), :]
        states = [s_ref[p] for p in range(n_pairs)]
        pairs = range(n_pairs)
        lanes = [slice(p * PAIR_W, (p + 1) * PAIR_W) for p in pairs]

        def emb(x_c):
            return [embed(x_c[:, ln]).astype(BF16) for ln in lanes]

        at_m, rt_m, bt_m, kt_m, v_m = emb(at_c), emb(rt_c), emb(bt_c), emb(kt_c), emb(v_c)
        m4 = [lax.dot_general(jnp.concatenate([at_m[p], rt_m[p]], axis=0),
                              jnp.concatenate([bt_m[p], kt_m[p]], axis=0), _NT,
                              preferred_element_type=F32) for p in pairs]
        n_ab = [jnp.where(strict, m[:PAIR_W, :PAIR_W], 0.0) for m in m4]
        a_ak = [jnp.where(strict, m[:PAIR_W, PAIR_W:], 0.0).astype(BF16) for m in m4]
        a_rb = [jnp.where(incl, m[PAIR_W:, :PAIR_W], 0.0).astype(BF16) for m in m4]
        a_rk = [jnp.where(incl, m[PAIR_W:, PAIR_W:], 0.0).astype(BF16) for m in m4]
        akv = [_dot(a_ak[p], v_m[p]).astype(BF16) for p in pairs]
        t_inv = [t.astype(BF16) for t in _tri_inverse(n_ab, ri, ci)]
        wu0 = [_dot(t_inv[p], jnp.concatenate([at_m[p], akv[p]], axis=1)) for p in pairs]
        state_b = [s.astype(BF16) for s in states]
        u_b = [(lax.dot_general(wu0[p][:, :PAIR_W].astype(BF16), state_b[p], _NT, preferred_element_type=F32)
                + wu0[p][:, PAIR_W:]).astype(BF16) for p in pairs]
        bk_h = [jnp.concatenate([b, k], axis=0) for b, k in zip(emb(bh_c), emb(kh_c))]
        upd = [lax.dot_general(jnp.concatenate([u_b[p], v_m[p]], axis=0), bk_h[p], _TN,
                               preferred_element_type=F32) for p in pairs]
        y_m = [lax.dot_general(rt_m[p], state_b[p], _NT, preferred_element_type=F32)
               + _dot(a_rb[p], u_b[p]) + _dot(a_rk[p], v_m[p]) for p in pairs]
        yacc_ref[rows, :] = jnp.concatenate([y[:RWKV_CHUNK] + y[RWKV_CHUNK:] for y in y_m], axis=1)
        for p in pairs:
            s_ref[p] = states[p] * gam_c[:, lanes[p]] + upd[p]
        return carry

    lax.fori_loop(0, tblk // RWKV_CHUNK, chunk_body, 0)

    y = yacc_ref[...]
    mean = _seg_sum(y, ones_bd) * (1.0 / HEAD_DIM)
    dev = y - mean
    var = _seg_sum(dev * dev, ones_bd) * (1.0 / HEAD_DIM)
    yn = dev * lax.rsqrt(var + LNX_EPS) * lnx_g + lnx_b
    y_ref[0] = (yn + bonus_ref[...]) * gate_ref[...]


def _rwkv(ps, vecs, w_up_p, a_up_p, g_up, ones_bd, tblk=256):
    b, s, _ = ps.shape
    tri = _block_ones(tblk, RWKV_CHUNK, lower=True)
    blk = _block_ones(tblk, RWKV_CHUNK)
    act = pltpu.VMEM((tblk, RWKV_W), F32)
    return pl.pallas_call(
        _rwkv_kernel,
        grid=(b, s // tblk),
        in_specs=[
            pl.BlockSpec((1, tblk, RWKV_SHIFT_W), lambda i, t: (i, t, 0)),
            _const_spec(vecs.shape),
            _const_spec(w_up_p.shape),
            _const_spec(a_up_p.shape),
            _const_spec(g_up.shape),
            _const_spec((tblk, tblk)),
            _const_spec((tblk, tblk)),
            _const_spec((SLAB_W, SLAB_W)),
        ],
        out_specs=pl.BlockSpec((1, tblk, RWKV_W), lambda i, t: (i, t, 0)),
        out_shape=jax.ShapeDtypeStruct((b, s, RWKV_W), F32),
        scratch_shapes=[pltpu.VMEM((RWKV_W // PAIR_W, PAIR_W, PAIR_W), F32)] + [act] * 11,
        compiler_params=_cparams(2, 48),
        name="rwkv",
    )(ps, vecs, w_up_p, a_up_p, g_up, tri, blk, ones_bd)


def _mixout_kernel(x_ref, ya_ref, qm_ref, k_ref, v_ref, w_ref, o_ref):
    wa = ya_ref.shape[-1]
    qm = qm_ref[...]
    kmem = k_ref[0]
    vmem = v_ref[0]
    head = lax.broadcasted_iota(jnp.int32, qm.shape, 1) // HEAD_DIM
    heads = range(MEM_HEADS)
    logits = [lax.dot_general(jnp.where(head == h, qm, 0.0).astype(BF16), kmem, _NT,
                              preferred_element_type=F32) for h in heads]
    p = [jnp.exp(lg - jnp.max(lg, axis=-1, keepdims=True)) for lg in logits]
    inv_l = [1.0 / jnp.sum(ph, axis=-1, keepdims=True) for ph in p]
    pv = [_dot(ph.astype(BF16), vmem) for ph in p]
    y_mem = pv[0] * inv_l[0]
    for h in heads[1:]:
        y_mem = jnp.where(head == h, pv[h] * inv_l[h], y_mem)
    y = _dot(ya_ref[...].astype(BF16), w_ref[:wa, :]) + _dot(y_mem.astype(BF16), w_ref[wa:, :])
    o_ref[...] = x_ref[...] + y


def _mixout(x, ya, qm, k_mem, v_mem, w_out, seq_len, tm=512):
    n, d = x.shape
    wa = ya.shape[-1]
    tiles_per_seq = seq_len // tm
    m = k_mem.shape[1]
    return pl.pallas_call(
        _mixout_kernel,
        grid=(n // tm,),
        in_specs=[
            pl.BlockSpec((tm, d), lambda i: (i, 0)),
            pl.BlockSpec((tm, wa), lambda i: (i, 0)),
            pl.BlockSpec((tm, MEM_W), lambda i: (i, 0)),
            pl.BlockSpec((1, m, MEM_W), lambda i: (i // tiles_per_seq, 0, 0)),
            pl.BlockSpec((1, m, MEM_W), lambda i: (i // tiles_per_seq, 0, 0)),
            _const_spec(w_out.shape),
        ],
        out_specs=pl.BlockSpec((tm, d), lambda i: (i, 0)),
        out_shape=jax.ShapeDtypeStruct((n, d), F32),
        compiler_params=_cparams(1, 40),
        name="mixout",
    )(x, ya, qm, k_mem, v_mem, w_out)


def _t5_bucket(dist):
    max_exact = NUM_BUCKETS // 2
    d_f = jnp.maximum(dist, 1).astype(F32)
    large = max_exact + (jnp.log(d_f / max_exact) / math.log(MAX_DISTANCE / max_exact)
                         * (NUM_BUCKETS - max_exact)).astype(jnp.int32)
    large = jnp.minimum(large, NUM_BUCKETS - 1)
    return jnp.where(dist < max_exact, dist, large)


def _band_buckets():
    qi = jnp.arange(BLOCK)[:, None]
    kj = jnp.arange(2 * BLOCK)[None, :]
    dsub = BLOCK + qi - kj
    out = []
    for window, dil in DIL_GROUPS:
        band = (dsub >= 0) & (dsub <= window // dil)
        out.append(jnp.where(band, _t5_bucket(jnp.maximum(dsub, 0) * dil), -1))
    return jnp.stack(out).astype(jnp.int32)


def _bias_kernel(tab_ref, idx_ref, o_ref):
    head = pl.program_id(0)
    idx = idx_ref[0]
    acc = jnp.full(idx.shape, NEG_INF, F32)
    for bucket in range(NUM_BUCKETS):
        acc = jnp.where(idx == bucket, tab_ref[bucket, head], acc)
    o_ref[0] = acc


def _band_bias(rel_bias):
    n_heads = rel_bias.shape[1]
    return pl.pallas_call(
        _bias_kernel,
        grid=(n_heads,),
        in_specs=[
            pl.BlockSpec(memory_space=pltpu.SMEM),
            pl.BlockSpec((1, BLOCK, 2 * BLOCK), lambda h: (h // DIL_GROUP_HEADS, 0, 0)),
        ],
        out_specs=pl.BlockSpec((1, BLOCK, 2 * BLOCK), lambda h: (h, 0, 0)),
        out_shape=jax.ShapeDtypeStruct((n_heads, BLOCK, 2 * BLOCK), F32),
        compiler_params=_cparams(1, 16),
        name="band_bias",
    )(rel_bias, _band_buckets())


def _dil_group(dil, q_ref, k_ref, v_ref, bias_ref, acc_ref, m_ref, l_ref):
    seq_len = q_ref.shape[2]
    n_blocks = seq_len // (dil * BLOCK)
    shift = n_blocks.bit_length() - 1
    head0 = lax.broadcasted_iota(jnp.int32, (BLOCK, PAIR_W), 1) < HEAD_DIM
    in_prev = lax.broadcasted_iota(jnp.int32, (BLOCK, 2 * BLOCK), 1) < BLOCK

    def rows(sub, blk):
        if dil == 1:
            return pl.ds(pl.multiple_of(BLOCK * blk, BLOCK), BLOCK)
        return pl.ds(sub + dil * BLOCK * blk, BLOCK, stride=dil)

    def tile(t, carry):
        sub = t >> shift
        blk = t & (n_blocks - 1)
        own = rows(sub, blk)
        prev = rows(sub, jnp.maximum(blk - 1, 0))
        no_prev = jnp.logical_and(blk == 0, in_prev)
        pairs = range(DIL_OUT_W // PAIR_W)
        heads = [(pair, h) for pair in pairs for h in range(2)]
        q = [q_ref[0, pair, own, :] for pair in pairs]
        keys = [jnp.concatenate([k_ref[0, pair, prev, :], k_ref[0, pair, own, :]], axis=0).astype(BF16)
                for pair in pairs]
        vals = [jnp.concatenate([v_ref[0, pair, prev, :], v_ref[0, pair, own, :]], axis=0).astype(BF16)
                for pair in pairs]
        m_old = [m_ref[pair, own, :] for pair in pairs]
        l_old = [l_ref[pair, own, :] for pair in pairs]
        acc_old = [acc_ref[pair, own, :] for pair in pairs]
        logits = [lax.dot_general(jnp.where(head0 if h == 0 else ~head0, q[pair], 0.0).astype(BF16),
                                  keys[pair], _NT, preferred_element_type=F32) for pair, h in heads]
        logits = [jnp.where(no_prev, NEG_INF, lg + bias_ref[2 * pair + h]) for lg, (pair, h) in zip(logits, heads)]
        mx = [jnp.max(lg, axis=-1, keepdims=True) for lg in logits]
        p = [jnp.exp(lg - m) for lg, m in zip(logits, mx)]
        ls = [jnp.sum(ph, axis=-1, keepdims=True) for ph in p]
        pv = [_dot(ph.astype(BF16), vals[pair]) for ph, (pair, h) in zip(p, heads)]
        for pair in pairs:
            m_t = jnp.where(head0, mx[2 * pair], mx[2 * pair + 1])
            l_t = jnp.where(head0, ls[2 * pair], ls[2 * pair + 1])
            acc_t = jnp.where(head0, pv[2 * pair], pv[2 * pair + 1])
            m_new = jnp.maximum(m_old[pair], m_t)
            e_old = jnp.exp(m_old[pair] - m_new)
            e_t = jnp.exp(m_t - m_new)
            m_ref[pair, own, :] = m_new
            l_ref[pair, own, :] = l_old[pair] * e_old + l_t * e_t
            acc_ref[pair, own, :] = acc_old[pair] * e_old + acc_t * e_t
        return carry

    lax.fori_loop(0, dil * n_blocks, tile, 0)


def _dil_kernel(q_ref, k_ref, v_ref, bias_ref, o_ref, acc_ref, m_ref, l_ref):
    g = pl.program_id(1)

    @pl.when(g == 0)
    def _():
        m_ref[...] = jnp.full(m_ref.shape, NEG_INF, F32)
        l_ref[...] = jnp.zeros_like(l_ref)
        acc_ref[...] = jnp.zeros_like(acc_ref)

    for gi, (_, dil) in enumerate(DIL_GROUPS):
        pl.when(g == gi)(functools.partial(_dil_group, dil, q_ref, k_ref, v_ref, bias_ref, acc_ref, m_ref, l_ref))

    @pl.when(g == len(DIL_GROUPS) - 1)
    def _():
        for pair in range(DIL_OUT_W // PAIR_W):
            o_ref[0, :, pair * PAIR_W:(pair + 1) * PAIR_W] = acc_ref[pair] / l_ref[pair]


def _dilated_attention(q_slabs, kv_slabs, bias):
    b, _, s, _ = q_slabs.shape
    n_groups = len(DIL_GROUPS)
    pairs = DIL_OUT_W // PAIR_W
    blk = (1, pairs, s, PAIR_W)
    return pl.pallas_call(
        _dil_kernel,
        grid=(b, n_groups),
        in_specs=[
            pl.BlockSpec(blk, lambda i, g: (i, g, 0, 0)),
            pl.BlockSpec(blk, lambda i, g: (i, g, 0, 0)),
            pl.BlockSpec(blk, lambda i, g: (i, n_groups + g, 0, 0)),
            pl.BlockSpec((DIL_GROUP_HEADS, BLOCK, 2 * BLOCK), lambda i, g: (g, 0, 0)),
        ],
        out_specs=pl.BlockSpec((1, s, DIL_OUT_W), lambda i, g: (i, 0, 0)),
        out_shape=jax.ShapeDtypeStruct((b, s, DIL_OUT_W), F32),
        scratch_shapes=[pltpu.VMEM((pairs, s, PAIR_W), F32)] * 3,
        compiler_params=_cparams(2, 56),
        name="dilated_attention",
    )(q_slabs, kv_slabs, kv_slabs, bias)


def kernel(x, mem, ffn_pre_norm, ffn_pre_w_in, ffn_pre_w_out, mix_norm, ffn_post_norm, ffn_post_w_in, ffn_post_w_out, mem_norm, mem_w_kv, mem_q_norm, mem_k_norm, a_w_in, a_shift_mu, a_w0, a_w_up, a_a0, a_a_up, a_g_up, a_kk_scale, a_k_a, a_r_k, a_lnx_g, a_lnx_b, a_w_out, b_w_q, b_q_norm, b_w_out, kv_norm, kv_w, kv_k_norm, rel_bias):
    b, s, d = x.shape
    depth = ffn_pre_w_in.shape[0]
    n_a = a_w_in.shape[0]
    n = b * s
    scale = 1.0 / math.sqrt(HEAD_DIM)
    ones_bd = _block_ones(SLAB_W, HEAD_DIM)
    row = lambda p: p.reshape(1, -1)

    k_mem, v_mem = _memkv(mem, mem_norm[:, None, :], mem_w_kv.astype(BF16),
                          jnp.tile(mem_k_norm, (1, MEM_HEADS))[:, None, :], ones_bd)
    mem_q_gain = jnp.tile(mem_q_norm, (1, MEM_HEADS)) * scale

    xf = x.reshape(n, d)
    kv = None
    bias = None
    for layer in range(depth):
        xf = _ffn(xf, row(ffn_pre_norm[layer]), ffn_pre_w_in[layer].astype(BF16),
                  ffn_pre_w_out[layer].astype(BF16))
        if layer < n_a:
            i = layer
            ps, qm = _aproj(xf, row(mix_norm[layer]), a_w_in[i].astype(BF16), row(a_shift_mu[i]),
                            row(mem_q_gain[layer]), ones_bd, s)
            zeros = jnp.zeros((DECAY_LORA, RWKV_W), F32)
            vecs = jnp.stack([a_w0[i], a_a0[i], a_kk_scale[i], a_k_a[i], a_r_k[i].reshape(-1),
                              a_lnx_g[i], a_lnx_b[i], jnp.zeros((RWKV_W,), F32)])
            y_main = _rwkv(ps.reshape(b, s, RWKV_SHIFT_W), vecs,
                           jnp.concatenate([a_w_up[i], zeros]).astype(BF16),
                           jnp.concatenate([zeros, a_a_up[i]]).astype(BF16),
                           a_g_up[i].astype(BF16), ones_bd)
            xf = _mixout(xf, y_main.reshape(n, RWKV_W), qm, k_mem[layer], v_mem[layer],
                         a_w_out[i].astype(BF16), s)
        else:
            j = layer - n_a
            q_gain = jnp.concatenate([jnp.tile(b_q_norm[j], DIL_W // HEAD_DIM) * scale, mem_q_gain[layer]])
            q_dil, qm = _qproj(xf, row(mix_norm[layer]), b_w_q[j].astype(BF16), row(q_gain), ones_bd, s, DIL_W)
            if bias is None:
                bias = _band_bias(rel_bias)
            y_dil = _dilated_attention(q_dil, kv, bias)
            xf = _mixout(xf, y_dil.reshape(n, DIL_OUT_W), qm, k_mem[layer], v_mem[layer],
                         b_w_out[j].astype(BF16), s)
        xf = _ffn(xf, row(ffn_post_norm[layer]), ffn_post_w_in[layer].astype(BF16),
                  ffn_post_w_out[layer].astype(BF16))
        if layer == n_a - 1:
            k_gain = jnp.tile(kv_k_norm, DIL_W // HEAD_DIM)
            kv, = _qproj(xf, row(kv_norm), kv_w.astype(BF16), row(k_gain), ones_bd, s, 2 * DIL_W)
    return xf.reshape(b, s, d)
```

```python
import functools
import math

import jax
import jax.numpy as jnp
import numpy as np
from jax import lax
from jax.experimental import pallas as pl
from jax.experimental.pallas import tpu as pltpu

F32 = jnp.float32
BF16 = jnp.bfloat16

HEAD_DIM = 64
MEM_HEADS = 4
MEM_W = MEM_HEADS * HEAD_DIM
RWKV_HEADS = 12
RWKV_W = RWKV_HEADS * HEAD_DIM
DECAY_LORA = 64
AAA_LORA = 64
GATE_LORA = 128
RWKV_SHIFT_W = 3 * RWKV_W + DECAY_LORA + AAA_LORA + GATE_LORA
LORA_LO = 3 * RWKV_W
DIL_GROUPS = ((128, 1), (512, 4), (2048, 16))
DIL_GROUP_HEADS = 4
DIL_W = len(DIL_GROUPS) * DIL_GROUP_HEADS * HEAD_DIM
DIL_OUT_W = DIL_GROUP_HEADS * HEAD_DIM
BLOCK = 128
NUM_BUCKETS = 32
MAX_DISTANCE = 2048
NORM_EPS = 1e-6
LNX_EPS = 64e-5
NEG_INF = -1e30

V7X_LANES = 128
V7X_SUBLANES = 8
V7X_VMEM_BYTES = 64 * 1024 * 1024

RWKV_CHUNK = 64
RWKV_PHASE_A_CHUNKS = 4
RWKV_CUMSUM_ROWS = 256
PAIR_W = 2 * HEAD_DIM
SLAB_W = 4 * HEAD_DIM

_NT = (((1,), (1,)), ((), ()))
_TN = (((0,), (0,)), ((), ()))


def _cparams(n_axes, vmem_mib):
    return pltpu.CompilerParams(
        dimension_semantics=("arbitrary",) * n_axes,
        vmem_limit_bytes=min(vmem_mib * 1024 * 1024, V7X_VMEM_BYTES - 4 * 1024 * 1024),
    )


def _const_spec(shape):
    zeros = (0,) * len(shape)
    return pl.BlockSpec(shape, lambda *_: zeros)


def _dot(a, b):
    return jnp.dot(a, b, preferred_element_type=F32)


def _rms(x, g, eps):
    return x * lax.rsqrt(jnp.mean(x * x, axis=-1, keepdims=True) + eps) * g


def _split2(x):
    hi = x.astype(BF16)
    lo = (x - hi.astype(F32)).astype(BF16)
    return hi, lo


def _seg_sum(x, ones_bd, pieces=2):
    outs = []
    for s in range(x.shape[-1] // SLAB_W):
        slab = x[:, s * SLAB_W:(s + 1) * SLAB_W]
        if pieces == 1:
            outs.append(_dot(slab.astype(BF16), ones_bd))
        else:
            hi, lo = _split2(slab)
            outs.append(_dot(hi, ones_bd) + _dot(lo, ones_bd))
    return outs[0] if len(outs) == 1 else jnp.concatenate(outs, axis=-1)


def _block_ones(n, blk, lower=False):
    i = np.arange(n)
    m = (i[:, None] // blk) == (i[None, :] // blk)
    if lower:
        m = m & (i[:, None] >= i[None, :])
    return jnp.asarray(m, dtype=BF16)


def _ffn_kernel(x_ref, g_ref, win_ref, wout_ref, o_ref):
    x = x_ref[...]
    d_ff = wout_ref.shape[0]
    xn = _rms(x, g_ref[...], NORM_EPS).astype(BF16)
    h = _dot(xn, win_ref[...])
    gate = h[:, :d_ff]
    up = h[:, d_ff:]
    act = (gate * jax.nn.sigmoid(gate) * up).astype(BF16)
    o_ref[...] = x + 0.5 * _dot(act, wout_ref[...])


def _ffn(x, g, w_in, w_out, tm=512):
    n, d = x.shape
    return pl.pallas_call(
        _ffn_kernel,
        grid=(n // tm,),
        in_specs=[
            pl.BlockSpec((tm, d), lambda i: (i, 0)),
            _const_spec((1, d)),
            pl.BlockSpec(w_in.shape, lambda i: (0, 0), pipeline_mode=pl.Buffered(1)),
            pl.BlockSpec(w_out.shape, lambda i: (0, 0), pipeline_mode=pl.Buffered(1)),
        ],
        out_specs=pl.BlockSpec((tm, d), lambda i: (i, 0)),
        out_shape=jax.ShapeDtypeStruct((n, d), F32),
        compiler_params=_cparams(1, 56),
        name="ffn",
    )(x, g, w_in, w_out)


def _memkv_kernel(mem_ref, g_ref, w_ref, kg_ref, ones_ref, k_ref, v_ref):
    m = _rms(mem_ref[0], g_ref[0], NORM_EPS).astype(BF16)
    kv = _dot(m, w_ref[0])
    k = kv[:, :MEM_W]
    ms = _seg_sum(k * k, ones_ref[...]) * (1.0 / HEAD_DIM)
    k_ref[0, 0] = (k * lax.rsqrt(ms + NORM_EPS) * kg_ref[0]).astype(BF16)
    v_ref[0, 0] = kv[:, MEM_W:].astype(BF16)


def _memkv(mem, mem_norm, w_kv, k_gain, ones_bd):
    b, m, d = mem.shape
    depth = w_kv.shape[0]
    out = jax.ShapeDtypeStruct((depth, b, m, MEM_W), BF16)
    return pl.pallas_call(
        _memkv_kernel,
        grid=(depth, b),
        in_specs=[
            pl.BlockSpec((1, m, d), lambda l, i: (i, 0, 0)),
            pl.BlockSpec((1, 1, d), lambda l, i: (l, 0, 0)),
            pl.BlockSpec((1, d, 2 * MEM_W), lambda l, i: (l, 0, 0)),
            pl.BlockSpec((1, 1, MEM_W), lambda l, i: (l, 0, 0)),
            _const_spec((SLAB_W, SLAB_W)),
        ],
        out_specs=[pl.BlockSpec((1, 1, m, MEM_W), lambda l, i: (l, i, 0, 0))] * 2,
        out_shape=[out, out],
        compiler_params=_cparams(2, 32),
        name="memkv",
    )(mem, mem_norm, w_kv, k_gain, ones_bd)


def _head_norm(q, gain, ones_bd):
    ms = _seg_sum(q * q, ones_bd) * (1.0 / HEAD_DIM)
    return q * lax.rsqrt(ms + NORM_EPS) * gain


def _aproj_kernel(x_ref, g_ref, w_ref, mu_ref, qg_ref, ones_ref, ps_ref, qm_ref, carry_ref, *, tiles_per_seq):
    i = pl.program_id(0)
    tm = x_ref.shape[0]
    u = _rms(x_ref[...], g_ref[...], NORM_EPS).astype(BF16)
    proj = _dot(u, w_ref[...])
    p = proj[:, :RWKV_SHIFT_W]

    @pl.when(i % tiles_per_seq == 0)
    def _():
        carry_ref[...] = jnp.zeros_like(carry_ref)

    row = lax.broadcasted_iota(jnp.int32, p.shape, 0)
    prev = jnp.where(row == 0, carry_ref[V7X_SUBLANES - 1:V7X_SUBLANES, :], pltpu.roll(p, 1, 0))
    ps_ref[...] = p + mu_ref[...] * (prev - p)
    carry_ref[...] = p[tm - V7X_SUBLANES:, :]
    qm_ref[...] = _head_norm(proj[:, RWKV_SHIFT_W:], qg_ref[...], ones_ref[...])


def _aproj(x, g, w_in, mu, q_gain, ones_bd, seq_len, tm=512):
    n, d = x.shape
    return pl.pallas_call(
        functools.partial(_aproj_kernel, tiles_per_seq=seq_len // tm),
        grid=(n // tm,),
        in_specs=[
            pl.BlockSpec((tm, d), lambda i: (i, 0)),
            _const_spec((1, d)),
            pl.BlockSpec(w_in.shape, lambda i: (0, 0), pipeline_mode=pl.Buffered(1)),
            _const_spec((1, RWKV_SHIFT_W)),
            _const_spec((1, MEM_W)),
            _const_spec((SLAB_W, SLAB_W)),
        ],
        out_specs=[pl.BlockSpec((tm, RWKV_SHIFT_W), lambda i: (i, 0)),
                   pl.BlockSpec((tm, MEM_W), lambda i: (i, 0))],
        out_shape=[jax.ShapeDtypeStruct((n, RWKV_SHIFT_W), F32),
                   jax.ShapeDtypeStruct((n, MEM_W), F32)],
        scratch_shapes=[pltpu.VMEM((V7X_SUBLANES, RWKV_SHIFT_W), F32)],
        compiler_params=_cparams(1, 48),
        name="aproj",
    )(x, g, w_in, mu, q_gain, ones_bd)


def _qproj_kernel(x_ref, g_ref, w_ref, qg_ref, ones_ref, slab_ref, *flat_refs, normed_w):
    u = _rms(x_ref[...], g_ref[...], NORM_EPS).astype(BF16)
    proj = _dot(u, w_ref[...])
    normed = _head_norm(proj[:, :normed_w], qg_ref[...], ones_ref[...])
    n_slabs = slab_ref.shape[1]
    for j in range(n_slabs):
        src = normed if (j + 1) * V7X_LANES <= normed_w else proj
        slab_ref[0, j] = src[:, j * V7X_LANES:(j + 1) * V7X_LANES]
    if flat_refs:
        flat_refs[0][...] = normed[:, n_slabs * V7X_LANES:]


def _qproj(x, g, w, q_gain, ones_bd, seq_len, slab_w, tm=512):
    n, d = x.shape
    wo = w.shape[1]
    normed_w = q_gain.shape[-1]
    n_slabs = slab_w // V7X_LANES
    tiles_per_seq = seq_len // tm
    out_specs = [pl.BlockSpec((1, n_slabs, tm, V7X_LANES),
                              lambda i: (i // tiles_per_seq, 0, i % tiles_per_seq, 0))]
    out_shape = [jax.ShapeDtypeStruct((n // seq_len, n_slabs, seq_len, V7X_LANES), F32)]
    if wo > slab_w:
        assert normed_w == wo
        out_specs.append(pl.BlockSpec((tm, wo - slab_w), lambda i: (i, 0)))
        out_shape.append(jax.ShapeDtypeStruct((n, wo - slab_w), F32))
    return pl.pallas_call(
        functools.partial(_qproj_kernel, normed_w=normed_w),
        grid=(n // tm,),
        in_specs=[
            pl.BlockSpec((tm, d), lambda i: (i, 0)),
            _const_spec((1, d)),
            pl.BlockSpec(w.shape, lambda i: (0, 0), pipeline_mode=pl.Buffered(1)),
            _const_spec((1, normed_w)),
            _const_spec((SLAB_W, SLAB_W)),
        ],
        out_specs=out_specs,
        out_shape=out_shape,
        compiler_params=_cparams(1, 48),
        name="qproj",
    )(x, g, w, q_gain, ones_bd)


def _embed(x):
    head0 = lax.broadcasted_iota(jnp.int32, x.shape, 1) < HEAD_DIM
    zero = jnp.zeros_like(x)
    return jnp.concatenate([jnp.where(head0, x, zero), jnp.where(head0, zero, x)], axis=0)


def _tri_inverse(nmats, tpos, spos):
    eye = (tpos == spos).astype(F32)
    same = {s: (tpos >> s) == (spos >> s) for s in (3, 4, 5)}

    def mm(lhs, rhs):
        return [_dot(a, _embed(b)) for a, b in zip(lhs, rhs)]

    def bf(xs):
        return [x.astype(BF16) for x in xs]

    n8 = [jnp.where(same[3], n, 0.0) for n in nmats]
    n8b = bf(n8)
    n2 = mm(n8b, n8b)
    n2b = bf(n2)
    n4 = mm(n2b, n2b)
    t = mm(bf([eye + a for a in n8]), bf([eye + a for a in n2]))
    t = mm(bf(t), bf([eye + a for a in n4]))
    for lo, hi in ((3, 4), (4, 5), (5, None)):
        off = ~same[lo] if hi is None else (same[hi] & ~same[lo])
        tb = bf(t)
        z = mm(bf([jnp.where(off, n, 0.0) for n in nmats]), tb)
        t = [a + d for a, d in zip(t, mm(tb, bf(z)))]
    return t


def _rwkv_kernel(ps_ref, vec_ref, wup_ref, aup_ref, gup_ref, tri_ref, ones_ref, y_ref,
                 s_ref, rt_ref, kt_ref, bt_ref, at_ref, kh_ref, bh_ref, v_ref, gam_ref, yacc_ref,
                 bonus_ref, gate_ref, p_s, qt_s, rp_s, y0_s):
    tblk = ps_ref.shape[1]
    n_pairs = RWKV_W // PAIR_W

    @pl.when(pl.program_id(1) == 0)
    def _():
        s_ref[...] = jnp.zeros_like(s_ref)

    ps = ps_ref[0]
    r = ps[:, :RWKV_W]
    k = ps[:, RWKV_W:2 * RWKV_W]
    v = ps[:, 2 * RWKV_W:3 * RWKV_W]
    lora_in = ps[:, LORA_LO:LORA_LO + PAIR_W]
    g_lo = ps[:, LORA_LO + PAIR_W:]
    w0, a0, kk_scale, k_a = vec_ref[0:1, :], vec_ref[1:2, :], vec_ref[2:3, :], vec_ref[3:4, :]
    r_k, lnx_g, lnx_b = vec_ref[4:5, :], vec_ref[5:6, :], vec_ref[6:7, :]
    ones_bd = ones_ref[...]

    nz = -(w0 + _dot(jnp.tanh(lora_in).astype(BF16), wup_ref[...]))
    softplus = jnp.maximum(nz, 0.0) + jnp.log(1.0 + jnp.exp(-jnp.abs(nz)))
    lw = -jnp.exp(-softplus - 0.5)
    a = jax.nn.sigmoid(a0 + _dot(lora_in.astype(BF16), aup_ref[...]))
    gate_ref[...] = _dot(jax.nn.sigmoid(g_lo).astype(BF16), gup_ref[...])
    kk = k * kk_scale
    kk = kk / jnp.maximum(jnp.sqrt(_seg_sum(kk * kk, ones_bd, pieces=1)), 1e-12)
    k2 = k * (1.0 + (a - 1.0) * k_a)
    kka = kk * a
    bonus_ref[...] = _seg_sum(r * k2 * r_k, ones_bd) * v

    hi, lo = _split2(lw)
    tri = tri_ref[...]
    gcum = jnp.concatenate(
        [_dot(tri, hi[s0:s0 + RWKV_CUMSUM_ROWS]) + _dot(tri, lo[s0:s0 + RWKV_CUMSUM_ROWS])
         for s0 in range(0, tblk, RWKV_CUMSUM_ROWS)], axis=0)
    n_chunks = tblk // RWKV_CHUNK
    gam_rows = [jnp.exp(gcum[(c + 1) * RWKV_CHUNK - 1:(c + 1) * RWKV_CHUNK, :]) for c in range(n_chunks)]
    gam = jnp.concatenate([jnp.broadcast_to(g, (RWKV_CHUNK, RWKV_W)) for g in gam_rows], axis=0)
    for c in range(n_chunks):
        gam_ref[c:c + 1, :] = gam_rows[c]
    e_neg = jnp.exp(-gcum)
    e_last = gam * e_neg
    rt_ref[...] = (r * jnp.exp(gcum)).astype(BF16)
    kt_ref[...] = (k2 * e_neg).astype(BF16)
    bt_ref[...] = (kka * e_neg).astype(BF16)
    at_ref[...] = (-kk * jnp.exp(gcum - lw)).astype(BF16)
    kh_ref[...] = (k2 * e_last).astype(BF16)
    bh_ref[...] = (kka * e_last).astype(BF16)
    v_ref[...] = v.astype(BF16)

    tpos = lax.broadcasted_iota(jnp.int32, (RWKV_CHUNK, PAIR_W), 0)
    spos = lax.broadcasted_iota(jnp.int32, (RWKV_CHUNK, PAIR_W), 1) & (HEAD_DIM - 1)
    strict = tpos > spos
    incl = tpos >= spos
    head0 = lax.broadcasted_iota(jnp.int32, (RWKV_CHUNK, PAIR_W), 1) < HEAD_DIM
    same_head = ((lax.broadcasted_iota(jnp.int32, (PAIR_W, PAIR_W), 0) < HEAD_DIM)
                 == (lax.broadcasted_iota(jnp.int32, (PAIR_W, PAIR_W), 1) < HEAD_DIM))

    pairs = range(n_pairs)
    lanes = [slice(p * PAIR_W, (p + 1) * PAIR_W) for p in pairs]
    n_chunks = tblk // RWKV_CHUNK

    def phase_a(i, carry):
        slots, at_v, rt_v, bt_v, kt_v, v_v, bh_v, kh_v = [], [], [], [], [], [], [], []
        for cc in range(RWKV_PHASE_A_CHUNKS):
            c = i * RWKV_PHASE_A_CHUNKS + cc
            rows = pl.ds(pl.multiple_of(c * RWKV_CHUNK, RWKV_CHUNK), RWKV_CHUNK)
            loaded = [ref[rows, :] for ref in (at_ref, rt_ref, bt_ref, kt_ref, v_ref, bh_ref, kh_ref)]
            for p in pairs:
                slots.append(c * n_pairs + p)
                for dst, x_c in zip((at_v, rt_v, bt_v, kt_v, v_v, bh_v, kh_v), loaded):
                    dst.append(x_c[:, lanes[p]])
        chains = range(len(slots))
        m4 = [lax.dot_general(jnp.concatenate([at_v[j], rt_v[j]], axis=0),
                              jnp.concatenate([_embed(bt_v[j]), _embed(kt_v[j])], axis=0), _NT,
                              preferred_element_type=F32) for j in chains]
        n_ab = [jnp.where(strict, m[:RWKV_CHUNK, :PAIR_W], 0.0) for m in m4]
        a_ak = [jnp.where(strict, m[:RWKV_CHUNK, PAIR_W:], 0.0).astype(BF16) for m in m4]
        a_rb = [jnp.where(incl, m[RWKV_CHUNK:, :PAIR_W], 0.0).astype(BF16) for m in m4]
        a_rk = [jnp.where(incl, m[RWKV_CHUNK:, PAIR_W:], 0.0).astype(BF16) for m in m4]
        v_m = [_embed(x) for x in v_v]
        akv = [_dot(a_ak[j], v_m[j]).astype(BF16) for j in chains]
        t_inv = [t.astype(BF16) for t in _tri_inverse(n_ab, tpos, spos)]
        wu0 = [_dot(t_inv[j], jnp.concatenate([_embed(at_v[j]), _embed(akv[j])], axis=1))
               for j in chains]
        w_b = [x[:, :PAIR_W].astype(BF16) for x in wu0]
        u0_b = [x[:, PAIR_W:].astype(BF16) for x in wu0]
        p_m = [jnp.where(same_head, lax.dot_general(bh_v[j], w_b[j], _TN, preferred_element_type=F32), 0.0)
               for j in chains]
        q_full = [lax.dot_general(jnp.concatenate([u0_b[j], v_v[j]], axis=0),
                                  jnp.concatenate([bh_v[j], kh_v[j]], axis=0), _TN,
                                  preferred_element_type=F32) for j in chains]
        q_t = [jnp.where(head0, q[:RWKV_CHUNK], q[RWKV_CHUNK:]) for q in q_full]
        r_p = [rt_v[j].astype(F32) + _dot(a_rb[j], _embed(w_b[j])) for j in chains]
        y_0 = [_dot(jnp.concatenate([a_rb[j], a_rk[j]], axis=1),
                    jnp.concatenate([_embed(u0_b[j]), v_m[j]], axis=0)) for j in chains]
        for j in chains:
            p_s[slots[j]] = p_m[j].astype(BF16)
            qt_s[slots[j]] = q_t[j]
            rp_s[slots[j]] = r_p[j].astype(BF16)
            y0_s[slots[j]] = y_0[j]
        return carry

    lax.fori_loop(0, n_chunks // RWKV_PHASE_A_CHUNKS, phase_a, 0)

    states = [s_ref[p] for p in pairs]
    for c in range(n_chunks):
        r0 = c * RWKV_CHUNK
        gam_c = gam_ref[c:c + 1, :]
        slots = [c * n_pairs + p for p in pairs]
        state_b = [s.astype(BF16) for s in states]
        upd = [lax.dot_general(state_b[p], p_s[slots[p]], _NT, preferred_element_type=F32) for p in pairs]
        y_c = [lax.dot_general(rp_s[slots[p]], _embed(state_b[p]), _NT, preferred_element_type=F32)
               + y0_s[slots[p]] for p in pairs]
        states = [states[p] * gam_c[:, lanes[p]] + upd[p] + qt_s[slots[p]] for p in pairs]
        yacc_ref[r0:r0 + RWKV_CHUNK, :] = jnp.concatenate(y_c, axis=1)
    for p in pairs:
        s_ref[p] = states[p]

    y = yacc_ref[...]
    mean = _seg_sum(y, ones_bd) * (1.0 / HEAD_DIM)
    dev = y - mean
    var = _seg_sum(dev * dev, ones_bd, pieces=1) * (1.0 / HEAD_DIM)
    yn = dev * lax.rsqrt(var + LNX_EPS) * lnx_g + lnx_b
    y_ref[0] = (yn + bonus_ref[...]) * gate_ref[...]


def _rwkv(ps, vecs, w_up_p, a_up_p, g_up, ones_bd, tblk=256):
    b, s, _ = ps.shape
    tri = _block_ones(RWKV_CUMSUM_ROWS, RWKV_CHUNK, lower=True)
    act = pltpu.VMEM((tblk, RWKV_W), F32)
    act_b = pltpu.VMEM((tblk, RWKV_W), BF16)
    n_pairs = RWKV_W // PAIR_W
    n_chunks = tblk // RWKV_CHUNK
    n_slots = n_chunks * n_pairs
    gam_rows = -(-n_chunks // V7X_SUBLANES) * V7X_SUBLANES
    scratch = [pltpu.VMEM((n_pairs, HEAD_DIM, PAIR_W), F32)]
    scratch += [act_b] * 7
    scratch += [pltpu.VMEM((gam_rows, RWKV_W), F32)] + [act] * 3
    scratch += [
        pltpu.VMEM((n_slots, PAIR_W, PAIR_W), BF16),
        pltpu.VMEM((n_slots, RWKV_CHUNK, PAIR_W), F32),
        pltpu.VMEM((n_slots, RWKV_CHUNK, PAIR_W), BF16),
        pltpu.VMEM((n_slots, RWKV_CHUNK, PAIR_W), F32),
    ]
    return pl.pallas_call(
        _rwkv_kernel,
        grid=(b, s // tblk),
        in_specs=[
            pl.BlockSpec((1, tblk, RWKV_SHIFT_W), lambda i, t: (i, t, 0)),
            _const_spec(vecs.shape),
            _const_spec(w_up_p.shape),
            _const_spec(a_up_p.shape),
            _const_spec(g_up.shape),
            _const_spec(tri.shape),
            _const_spec((SLAB_W, SLAB_W)),
        ],
        out_specs=pl.BlockSpec((1, tblk, RWKV_W), lambda i, t: (i, t, 0)),
        out_shape=jax.ShapeDtypeStruct((b, s, RWKV_W), F32),
        scratch_shapes=scratch,
        compiler_params=_cparams(2, 48),
        name="rwkv",
    )(ps, vecs, w_up_p, a_up_p, g_up, tri, ones_bd)


def _mixout_kernel(x_ref, ya_ref, qm_ref, k_ref, v_ref, w_ref, o_ref):
    wa = ya_ref.shape[-1]
    qm = qm_ref[...]
    kmem = k_ref[0]
    vmem = v_ref[0]
    head = lax.broadcasted_iota(jnp.int32, qm.shape, 1) // HEAD_DIM
    heads = range(MEM_HEADS)
    logits = [lax.dot_general(jnp.where(head == h, qm, 0.0).astype(BF16), kmem, _NT,
                              preferred_element_type=F32) for h in heads]
    p = [jnp.exp(lg - jnp.max(lg, axis=-1, keepdims=True)) for lg in logits]
    inv_l = [1.0 / jnp.sum(ph, axis=-1, keepdims=True) for ph in p]
    pv = [_dot(ph.astype(BF16), vmem) for ph in p]
    y_mem = pv[0] * inv_l[0]
    for h in heads[1:]:
        y_mem = jnp.where(head == h, pv[h] * inv_l[h], y_mem)
    y = _dot(ya_ref[...].astype(BF16), w_ref[:wa, :]) + _dot(y_mem.astype(BF16), w_ref[wa:, :])
    o_ref[...] = x_ref[...] + y


def _mixout(x, ya, qm, k_mem, v_mem, w_out, seq_len, tm=512):
    n, d = x.shape
    wa = ya.shape[-1]
    tiles_per_seq = seq_len // tm
    m = k_mem.shape[1]
    return pl.pallas_call(
        _mixout_kernel,
        grid=(n // tm,),
        in_specs=[
            pl.BlockSpec((tm, d), lambda i: (i, 0)),
            pl.BlockSpec((tm, wa), lambda i: (i, 0)),
            pl.BlockSpec((tm, MEM_W), lambda i: (i, 0)),
            pl.BlockSpec((1, m, MEM_W), lambda i: (i // tiles_per_seq, 0, 0)),
            pl.BlockSpec((1, m, MEM_W), lambda i: (i // tiles_per_seq, 0, 0)),
            _const_spec(w_out.shape),
        ],
        out_specs=pl.BlockSpec((tm, d), lambda i: (i, 0)),
        out_shape=jax.ShapeDtypeStruct((n, d), F32),
        compiler_params=_cparams(1, 40),
        name="mixout",
    )(x, ya, qm, k_mem, v_mem, w_out)


def _t5_bucket(dist):
    max_exact = NUM_BUCKETS // 2
    d_f = jnp.maximum(dist, 1).astype(F32)
    large = max_exact + (jnp.log(d_f / max_exact) / math.log(MAX_DISTANCE / max_exact)
                         * (NUM_BUCKETS - max_exact)).astype(jnp.int32)
    large = jnp.minimum(large, NUM_BUCKETS - 1)
    return jnp.where(dist < max_exact, dist, large)


def _band_buckets():
    qi = jnp.arange(BLOCK)[:, None]
    kj = jnp.arange(2 * BLOCK)[None, :]
    dsub = BLOCK + qi - kj
    out = []
    for window, dil in DIL_GROUPS:
        band = (dsub >= 0) & (dsub <= window // dil)
        out.append(jnp.where(band, _t5_bucket(jnp.maximum(dsub, 0) * dil), -1))
    return jnp.stack(out).astype(jnp.int32)


def _bias_kernel(tab_ref, idx_ref, o_ref):
    head = pl.program_id(0)
    idx = idx_ref[0]
    acc = jnp.full(idx.shape, NEG_INF, F32)
    for bucket in range(NUM_BUCKETS):
        acc = jnp.where(idx == bucket, tab_ref[bucket, head], acc)
    o_ref[0] = acc


def _band_bias(rel_bias):
    n_heads = rel_bias.shape[1]
    return pl.pallas_call(
        _bias_kernel,
        grid=(n_heads,),
        in_specs=[
            pl.BlockSpec(memory_space=pltpu.SMEM),
            pl.BlockSpec((1, BLOCK, 2 * BLOCK), lambda h: (h // DIL_GROUP_HEADS, 0, 0)),
        ],
        out_specs=pl.BlockSpec((1, BLOCK, 2 * BLOCK), lambda h: (h, 0, 0)),
        out_shape=jax.ShapeDtypeStruct((n_heads, BLOCK, 2 * BLOCK), F32),
        compiler_params=_cparams(1, 16),
        name="band_bias",
    )(rel_bias, _band_buckets())


def _dil_group(dil, q_ref, k_ref, v_ref, bias_ref, acc_ref, m_ref, l_ref):
    seq_len = q_ref.shape[2]
    n_blocks = seq_len // (dil * BLOCK)
    shift = n_blocks.bit_length() - 1
    head0 = lax.broadcasted_iota(jnp.int32, (BLOCK, PAIR_W), 1) < HEAD_DIM
    in_prev = lax.broadcasted_iota(jnp.int32, (BLOCK, 2 * BLOCK), 1) < BLOCK

    def rows(sub, blk):
        if dil == 1:
            return pl.ds(pl.multiple_of(BLOCK * blk, BLOCK), BLOCK)
        return pl.ds(sub + dil * BLOCK * blk, BLOCK, stride=dil)

    def tile(t, carry):
        sub = t >> shift
        blk = t & (n_blocks - 1)
        own = rows(sub, blk)
        prev = rows(sub, jnp.maximum(blk - 1, 0))
        no_prev = jnp.logical_and(blk == 0, in_prev)
        pairs = range(DIL_OUT_W // PAIR_W)
        heads = [(pair, h) for pair in pairs for h in range(2)]
        q = [q_ref[0, pair, own, :] for pair in pairs]
        keys = [jnp.concatenate([k_ref[0, pair, prev, :], k_ref[0, pair, own, :]], axis=0).astype(BF16)
                for pair in pairs]
        vals = [jnp.concatenate([v_ref[0, pair, prev, :], v_ref[0, pair, own, :]], axis=0).astype(BF16)
                for pair in pairs]
        m_old = [m_ref[pair, own, :] for pair in pairs]
        l_old = [l_ref[pair, own, :] for pair in pairs]
        acc_old = [acc_ref[pair, own, :] for pair in pairs]
        logits = [lax.dot_general(jnp.where(head0 if h == 0 else ~head0, q[pair], 0.0).astype(BF16),
                                  keys[pair], _NT, preferred_element_type=F32) for pair, h in heads]
        logits = [jnp.where(no_prev, NEG_INF, lg + bias_ref[2 * pair + h]) for lg, (pair, h) in zip(logits, heads)]
        mx = [jnp.max(lg, axis=-1, keepdims=True) for lg in logits]
        p = [jnp.exp(lg - m) for lg, m in zip(logits, mx)]
        ls = [jnp.sum(ph, axis=-1, keepdims=True) for ph in p]
        pv = [_dot(ph.astype(BF16), vals[pair]) for ph, (pair, h) in zip(p, heads)]
        for pair in pairs:
            m_t = jnp.where(head0, mx[2 * pair], mx[2 * pair + 1])
            l_t = jnp.where(head0, ls[2 * pair], ls[2 * pair + 1])
            acc_t = jnp.where(head0, pv[2 * pair], pv[2 * pair + 1])
            m_new = jnp.maximum(m_old[pair], m_t)
            e_old = jnp.exp(m_old[pair] - m_new)
            e_t = jnp.exp(m_t - m_new)
            m_ref[pair, own, :] = m_new
            l_ref[pair, own, :] = l_old[pair] * e_old + l_t * e_t
            acc_ref[pair, own, :] = acc_old[pair] * e_old + acc_t * e_t
        return carry

    lax.fori_loop(0, dil * n_blocks, tile, 0)


def _dil_kernel(q_ref, k_ref, v_ref, bias_ref, o_ref, acc_ref, m_ref, l_ref):
    g = pl.program_id(1)

    @pl.when(g == 0)
    def _():
        m_ref[...] = jnp.full(m_ref.shape, NEG_INF, F32)
        l_ref[...] = jnp.zeros_like(l_ref)
        acc_ref[...] = jnp.zeros_like(acc_ref)

    for gi, (_, dil) in enumerate(DIL_GROUPS):
        pl.when(g == gi)(functools.partial(_dil_group, dil, q_ref, k_ref, v_ref, bias_ref, acc_ref, m_ref, l_ref))

    @pl.when(g == len(DIL_GROUPS) - 1)
    def _():
        for pair in range(DIL_OUT_W // PAIR_W):
            o_ref[0, :, pair * PAIR_W:(pair + 1) * PAIR_W] = acc_ref[pair] / l_ref[pair]


def _dilated_attention(q_slabs, kv_slabs, bias):
    b, _, s, _ = q_slabs.shape
    n_groups = len(DIL_GROUPS)
    pairs = DIL_OUT_W // PAIR_W
    blk = (1, pairs, s, PAIR_W)
    return pl.pallas_call(
        _dil_kernel,
        grid=(b, n_groups),
        in_specs=[
            pl.BlockSpec(blk, lambda i, g: (i, g, 0, 0)),
            pl.BlockSpec(blk, lambda i, g: (i, g, 0, 0)),
            pl.BlockSpec(blk, lambda i, g: (i, n_groups + g, 0, 0)),
            pl.BlockSpec((DIL_GROUP_HEADS, BLOCK, 2 * BLOCK), lambda i, g: (g, 0, 0)),
        ],
        out_specs=pl.BlockSpec((1, s, DIL_OUT_W), lambda i, g: (i, 0, 0)),
        out_shape=jax.ShapeDtypeStruct((b, s, DIL_OUT_W), F32),
        scratch_shapes=[pltpu.VMEM((pairs, s, PAIR_W), F32)] * 3,
        compiler_params=_cparams(2, 56),
        name="dilated_attention",
    )(q_slabs, kv_slabs, kv_slabs, bias)


def kernel(x, mem, ffn_pre_norm, ffn_pre_w_in, ffn_pre_w_out, mix_norm, ffn_post_norm, ffn_post_w_in, ffn_post_w_out, mem_norm, mem_w_kv, mem_q_norm, mem_k_norm, a_w_in, a_shift_mu, a_w0, a_w_up, a_a0, a_a_up, a_g_up, a_kk_scale, a_k_a, a_r_k, a_lnx_g, a_lnx_b, a_w_out, b_w_q, b_q_norm, b_w_out, kv_norm, kv_w, kv_k_norm, rel_bias):
    b, s, d = x.shape
    depth = ffn_pre_w_in.shape[0]
    n_a = a_w_in.shape[0]
    n = b * s
    scale = 1.0 / math.sqrt(HEAD_DIM)
    ones_bd = _block_ones(SLAB_W, HEAD_DIM)
    row = lambda p: p.reshape(1, -1)

    k_mem, v_mem = _memkv(mem, mem_norm[:, None, :], mem_w_kv.astype(BF16),
                          jnp.tile(mem_k_norm, (1, MEM_HEADS))[:, None, :], ones_bd)
    mem_q_gain = jnp.tile(mem_q_norm, (1, MEM_HEADS)) * scale

    xf = x.reshape(n, d)
    kv = None
    bias = None
    for layer in range(depth):
        xf = _ffn(xf, row(ffn_pre_norm[layer]), ffn_pre_w_in[layer].astype(BF16),
                  ffn_pre_w_out[layer].astype(BF16))
        if layer < n_a:
            i = layer
            ps, qm = _aproj(xf, row(mix_norm[layer]), a_w_in[i].astype(BF16), row(a_shift_mu[i]),
                            row(mem_q_gain[layer]), ones_bd, s)
            zeros = jnp.zeros((DECAY_LORA, RWKV_W), F32)
            vecs = jnp.stack([a_w0[i], a_a0[i], a_kk_scale[i], a_k_a[i], a_r_k[i].reshape(-1),
                              a_lnx_g[i], a_lnx_b[i], jnp.zeros((RWKV_W,), F32)])
            y_main = _rwkv(ps.reshape(b, s, RWKV_SHIFT_W), vecs,
                           jnp.concatenate([a_w_up[i], zeros]).astype(BF16),
                           jnp.concatenate([zeros, a_a_up[i]]).astype(BF16),
                           a_g_up[i].astype(BF16), ones_bd)
            xf = _mixout(xf, y_main.reshape(n, RWKV_W), qm, k_mem[layer], v_mem[layer],
                         a_w_out[i].astype(BF16), s)
        else:
            j = layer - n_a
            q_gain = jnp.concatenate([jnp.tile(b_q_norm[j], DIL_W // HEAD_DIM) * scale, mem_q_gain[layer]])
            q_dil, qm = _qproj(xf, row(mix_norm[layer]), b_w_q[j].astype(BF16), row(q_gain), ones_bd, s, DIL_W)
            if bias is None:
                bias = _band_bias(rel_bias)
            y_dil = _dilated_attention(q_dil, kv, bias)
            xf = _mixout(xf, y_dil.reshape(n, DIL_OUT_W), qm, k_mem[layer], v_mem[layer],
                         b_w_out[j].astype(BF16), s)
        xf = _ffn(xf, row(ffn_post_norm[layer]), ffn_post_w_in[layer].astype(BF16),
                  ffn_post_w_out[layer].astype(BF16))
        if layer == n_a - 1:
            k_gain = jnp.tile(kv_k_norm, DIL_W // HEAD_DIM)
            kv, = _qproj(xf, row(kv_norm), kv_w.astype(BF16), row(k_gain), ones_bd, s, 2 * DIL_W)
    return xf.reshape(b, s, d)
```

```python
import functools
import math

import jax
import jax.numpy as jnp
import numpy as np
from jax import lax
from jax.experimental import pallas as pl
from jax.experimental.pallas import tpu as pltpu

F32 = jnp.float32
BF16 = jnp.bfloat16

HEAD_DIM = 64
MEM_HEADS = 4
MEM_W = MEM_HEADS * HEAD_DIM
RWKV_HEADS = 12
RWKV_W = RWKV_HEADS * HEAD_DIM
DECAY_LORA = 64
AAA_LORA = 64
GATE_LORA = 128
RWKV_SHIFT_W = 3 * RWKV_W + DECAY_LORA + AAA_LORA + GATE_LORA
LORA_LO = 3 * RWKV_W
DIL_GROUPS = ((128, 1), (512, 4), (2048, 16))
DIL_GROUP_HEADS = 4
DIL_W = len(DIL_GROUPS) * DIL_GROUP_HEADS * HEAD_DIM
DIL_OUT_W = DIL_GROUP_HEADS * HEAD_DIM
BLOCK = 128
NUM_BUCKETS = 32
MAX_DISTANCE = 2048
NORM_EPS = 1e-6
LNX_EPS = 64e-5
NEG_INF = -1e30

V7X_LANES = 128
V7X_SUBLANES = 8
V7X_VMEM_BYTES = 64 * 1024 * 1024

RWKV_CHUNK = 64
RWKV_PHASE_A_CHUNKS = 4
RWKV_CUMSUM_ROWS = 256
DIL_TILES_PER_ITER = 2
PROJ_SUB_ROWS = 128
PAIR_W = 2 * HEAD_DIM
SLAB_W = 4 * HEAD_DIM

_NT = (((1,), (1,)), ((), ()))
_TN = (((0,), (0,)), ((), ()))


def _cparams(n_axes, vmem_mib):
    return pltpu.CompilerParams(
        dimension_semantics=("arbitrary",) * n_axes,
        vmem_limit_bytes=min(vmem_mib * 1024 * 1024, V7X_VMEM_BYTES - 4 * 1024 * 1024),
    )


def _const_spec(shape):
    zeros = (0,) * len(shape)
    return pl.BlockSpec(shape, lambda *_: zeros)


def _dot(a, b):
    return jnp.dot(a, b, preferred_element_type=F32)


def _rms(x, g, eps):
    return x * lax.rsqrt(jnp.mean(x * x, axis=-1, keepdims=True) + eps) * g


def _split2(x):
    hi = x.astype(BF16)
    lo = (x - hi.astype(F32)).astype(BF16)
    return hi, lo


def _seg_sum(x, ones_bd, pieces=2):
    outs = []
    for s in range(x.shape[-1] // SLAB_W):
        slab = x[:, s * SLAB_W:(s + 1) * SLAB_W]
        if pieces == 1:
            outs.append(_dot(slab.astype(BF16), ones_bd))
        else:
            hi, lo = _split2(slab)
            outs.append(_dot(hi, ones_bd) + _dot(lo, ones_bd))
    return outs[0] if len(outs) == 1 else jnp.concatenate(outs, axis=-1)


def _block_ones(n, blk, lower=False):
    i = np.arange(n)
    m = (i[:, None] // blk) == (i[None, :] // blk)
    if lower:
        m = m & (i[:, None] >= i[None, :])
    return jnp.asarray(m, dtype=BF16)


def _ffn_kernel(x_ref, g_ref, win_ref, wout_ref, o_ref):
    x = x_ref[...]
    d_ff = wout_ref.shape[0]
    xn = _rms(x, g_ref[...], NORM_EPS).astype(BF16)
    h = _dot(xn, win_ref[...])
    gate = h[:, :d_ff]
    up = h[:, d_ff:]
    act = (gate * jax.nn.sigmoid(gate) * up).astype(BF16)
    o_ref[...] = x + 0.5 * _dot(act, wout_ref[...])


def _ffn(x, g, w_in, w_out, tm=512):
    n, d = x.shape
    return pl.pallas_call(
        _ffn_kernel,
        grid=(n // tm,),
        in_specs=[
            pl.BlockSpec((tm, d), lambda i: (i, 0)),
            _const_spec((1, d)),
            pl.BlockSpec(w_in.shape, lambda i: (0, 0), pipeline_mode=pl.Buffered(1)),
            pl.BlockSpec(w_out.shape, lambda i: (0, 0), pipeline_mode=pl.Buffered(1)),
        ],
        out_specs=pl.BlockSpec((tm, d), lambda i: (i, 0)),
        out_shape=jax.ShapeDtypeStruct((n, d), F32),
        compiler_params=_cparams(1, 56),
        name="ffn",
    )(x, g, w_in, w_out)


def _memkv_kernel(mem_ref, g_ref, w_ref, kg_ref, ones_ref, k_ref, v_ref):
    m = _rms(mem_ref[0], g_ref[0], NORM_EPS).astype(BF16)
    kv = _dot(m, w_ref[0])
    k = kv[:, :MEM_W]
    ms = _seg_sum(k * k, ones_ref[...]) * (1.0 / HEAD_DIM)
    k_ref[0, 0] = (k * lax.rsqrt(ms + NORM_EPS) * kg_ref[0]).astype(BF16)
    v_ref[0, 0] = kv[:, MEM_W:].astype(BF16)


def _memkv(mem, mem_norm, w_kv, k_gain, ones_bd):
    b, m, d = mem.shape
    depth = w_kv.shape[0]
    out = jax.ShapeDtypeStruct((depth, b, m, MEM_W), BF16)
    return pl.pallas_call(
        _memkv_kernel,
        grid=(depth, b),
        in_specs=[
            pl.BlockSpec((1, m, d), lambda l, i: (i, 0, 0)),
            pl.BlockSpec((1, 1, d), lambda l, i: (l, 0, 0)),
            pl.BlockSpec((1, d, 2 * MEM_W), lambda l, i: (l, 0, 0)),
            pl.BlockSpec((1, 1, MEM_W), lambda l, i: (l, 0, 0)),
            _const_spec((SLAB_W, SLAB_W)),
        ],
        out_specs=[pl.BlockSpec((1, 1, m, MEM_W), lambda l, i: (l, i, 0, 0))] * 2,
        out_shape=[out, out],
        compiler_params=_cparams(2, 32),
        name="memkv",
    )(mem, mem_norm, w_kv, k_gain, ones_bd)


def _head_norm(q, gain, ones_bd):
    ms = _seg_sum(q * q, ones_bd) * (1.0 / HEAD_DIM)
    return q * lax.rsqrt(ms + NORM_EPS) * gain


def _aproj_kernel(x_ref, g_ref, w_ref, mu_ref, qg_ref, ones_ref, ps_ref, qm_ref, carry_ref, *, tiles_per_seq):
    i = pl.program_id(0)
    tm = x_ref.shape[0]
    u = _rms(x_ref[...], g_ref[...], NORM_EPS).astype(BF16)

    @pl.when(i % tiles_per_seq == 0)
    def _():
        carry_ref[...] = jnp.zeros_like(carry_ref)

    starts = range(0, tm, PROJ_SUB_ROWS)
    projs = [_dot(u[r0:r0 + PROJ_SUB_ROWS], w_ref[...]) for r0 in starts]
    row = lax.broadcasted_iota(jnp.int32, (PROJ_SUB_ROWS, RWKV_SHIFT_W), 0)
    last = carry_ref[V7X_SUBLANES - 1:V7X_SUBLANES, :]
    for r0, proj in zip(starts, projs):
        p = proj[:, :RWKV_SHIFT_W]
        prev = jnp.where(row == 0, last, pltpu.roll(p, 1, 0))
        ps_ref[r0:r0 + PROJ_SUB_ROWS, :] = p + mu_ref[...] * (prev - p)
        last = p[PROJ_SUB_ROWS - 1:, :]
        qm_ref[r0:r0 + PROJ_SUB_ROWS, :] = _head_norm(proj[:, RWKV_SHIFT_W:], qg_ref[...], ones_ref[...])
    carry_ref[...] = projs[-1][PROJ_SUB_ROWS - V7X_SUBLANES:, :RWKV_SHIFT_W]


def _aproj(x, g, w_in, mu, q_gain, ones_bd, seq_len, tm=512):
    n, d = x.shape
    return pl.pallas_call(
        functools.partial(_aproj_kernel, tiles_per_seq=seq_len // tm),
        grid=(n // tm,),
        in_specs=[
            pl.BlockSpec((tm, d), lambda i: (i, 0)),
            _const_spec((1, d)),
            pl.BlockSpec(w_in.shape, lambda i: (0, 0), pipeline_mode=pl.Buffered(1)),
            _const_spec((1, RWKV_SHIFT_W)),
            _const_spec((1, MEM_W)),
            _const_spec((SLAB_W, SLAB_W)),
        ],
        out_specs=[pl.BlockSpec((tm, RWKV_SHIFT_W), lambda i: (i, 0)),
                   pl.BlockSpec((tm, MEM_W), lambda i: (i, 0))],
        out_shape=[jax.ShapeDtypeStruct((n, RWKV_SHIFT_W), F32),
                   jax.ShapeDtypeStruct((n, MEM_W), F32)],
        scratch_shapes=[pltpu.VMEM((V7X_SUBLANES, RWKV_SHIFT_W), F32)],
        compiler_params=_cparams(1, 48),
        name="aproj",
    )(x, g, w_in, mu, q_gain, ones_bd)


def _qproj_kernel(x_ref, g_ref, w_ref, qg_ref, ones_ref, slab_ref, *flat_refs, normed_w):
    u = _rms(x_ref[...], g_ref[...], NORM_EPS).astype(BF16)
    proj = _dot(u, w_ref[...])
    normed = _head_norm(proj[:, :normed_w], qg_ref[...], ones_ref[...])
    n_slabs = slab_ref.shape[1]
    for j in range(n_slabs):
        src = normed if (j + 1) * V7X_LANES <= normed_w else proj
        slab_ref[0, j] = src[:, j * V7X_LANES:(j + 1) * V7X_LANES]
    if flat_refs:
        flat_refs[0][...] = normed[:, n_slabs * V7X_LANES:]


def _qproj(x, g, w, q_gain, ones_bd, seq_len, slab_w, tm=512):
    n, d = x.shape
    wo = w.shape[1]
    normed_w = q_gain.shape[-1]
    n_slabs = slab_w // V7X_LANES
    tiles_per_seq = seq_len // tm
    out_specs = [pl.BlockSpec((1, n_slabs, tm, V7X_LANES),
                              lambda i: (i // tiles_per_seq, 0, i % tiles_per_seq, 0))]
    out_shape = [jax.ShapeDtypeStruct((n // seq_len, n_slabs, seq_len, V7X_LANES), F32)]
    if wo > slab_w:
        assert normed_w == wo
        out_specs.append(pl.BlockSpec((tm, wo - slab_w), lambda i: (i, 0)))
        out_shape.append(jax.ShapeDtypeStruct((n, wo - slab_w), F32))
    return pl.pallas_call(
        functools.partial(_qproj_kernel, normed_w=normed_w),
        grid=(n // tm,),
        in_specs=[
            pl.BlockSpec((tm, d), lambda i: (i, 0)),
            _const_spec((1, d)),
            pl.BlockSpec(w.shape, lambda i: (0, 0), pipeline_mode=pl.Buffered(1)),
            _const_spec((1, normed_w)),
            _const_spec((SLAB_W, SLAB_W)),
        ],
        out_specs=out_specs,
        out_shape=out_shape,
        compiler_params=_cparams(1, 48),
        name="qproj",
    )(x, g, w, q_gain, ones_bd)


def _embed(x):
    head0 = lax.broadcasted_iota(jnp.int32, x.shape, 1) < HEAD_DIM
    zero = jnp.zeros_like(x)
    return jnp.concatenate([jnp.where(head0, x, zero), jnp.where(head0, zero, x)], axis=0)


def _tri_inverse(nmats, tpos, spos):
    eye = (tpos == spos).astype(F32)
    same = {s: (tpos >> s) == (spos >> s) for s in (3, 4, 5)}

    def mm(lhs, rhs):
        return [_dot(a, _embed(b)) for a, b in zip(lhs, rhs)]

    def bf(xs):
        return [x.astype(BF16) for x in xs]

    n8 = [jnp.where(same[3], n, 0.0) for n in nmats]
    n8b = bf(n8)
    n2 = mm(n8b, n8b)
    n2b = bf(n2)
    n4 = mm(n2b, n2b)
    t = mm(bf([eye + a for a in n8]), bf([eye + a for a in n2]))
    t = mm(bf(t), bf([eye + a for a in n4]))
    for lo, hi in ((3, 4), (4, 5), (5, None)):
        off = ~same[lo] if hi is None else (same[hi] & ~same[lo])
        tb = bf(t)
        z = mm(bf([jnp.where(off, n, 0.0) for n in nmats]), tb)
        t = [a + d for a, d in zip(t, mm(tb, bf(z)))]
    return t


def _rwkv_kernel(ps_ref, vec_ref, wup_ref, aup_ref, gup_ref, tri_ref, ones_ref, y_ref,
                 s_ref, rt_ref, kt_ref, bt_ref, at_ref, kh_ref, bh_ref, v_ref, gam_ref, yacc_ref,
                 bonus_ref, gate_ref, p_s, qt_s, rp_s, y0_s):
    tblk = ps_ref.shape[1]
    n_pairs = RWKV_W // PAIR_W

    @pl.when(pl.program_id(1) == 0)
    def _():
        s_ref[...] = jnp.zeros_like(s_ref)

    ps = ps_ref[0]
    r = ps[:, :RWKV_W]
    k = ps[:, RWKV_W:2 * RWKV_W]
    v = ps[:, 2 * RWKV_W:3 * RWKV_W]
    lora_in = ps[:, LORA_LO:LORA_LO + PAIR_W]
    g_lo = ps[:, LORA_LO + PAIR_W:]
    w0, a0, kk_scale, k_a = vec_ref[0:1, :], vec_ref[1:2, :], vec_ref[2:3, :], vec_ref[3:4, :]
    r_k, lnx_g, lnx_b = vec_ref[4:5, :], vec_ref[5:6, :], vec_ref[6:7, :]
    ones_bd = ones_ref[...]

    nz = -(w0 + _dot(jnp.tanh(lora_in).astype(BF16), wup_ref[...]))
    softplus = jnp.maximum(nz, 0.0) + jnp.log(1.0 + jnp.exp(-jnp.abs(nz)))
    lw = -jnp.exp(-softplus - 0.5)
    a = jax.nn.sigmoid(a0 + _dot(lora_in.astype(BF16), aup_ref[...]))
    gate_ref[...] = _dot(jax.nn.sigmoid(g_lo).astype(BF16), gup_ref[...])
    kk = k * kk_scale
    kk = kk / jnp.maximum(jnp.sqrt(_seg_sum(kk * kk, ones_bd, pieces=1)), 1e-12)
    k2 = k * (1.0 + (a - 1.0) * k_a)
    kka = kk * a
    bonus_ref[...] = _seg_sum(r * k2 * r_k, ones_bd) * v

    hi, lo = _split2(lw)
    tri = tri_ref[...]
    gcum = jnp.concatenate(
        [_dot(tri, hi[s0:s0 + RWKV_CUMSUM_ROWS]) + _dot(tri, lo[s0:s0 + RWKV_CUMSUM_ROWS])
         for s0 in range(0, tblk, RWKV_CUMSUM_ROWS)], axis=0)
    n_chunks = tblk // RWKV_CHUNK
    gam_rows = [jnp.exp(gcum[(c + 1) * RWKV_CHUNK - 1:(c + 1) * RWKV_CHUNK, :]) for c in range(n_chunks)]
    gam = jnp.concatenate([jnp.broadcast_to(g, (RWKV_CHUNK, RWKV_W)) for g in gam_rows], axis=0)
    for c in range(n_chunks):
        gam_ref[c:c + 1, :] = gam_rows[c]
    e_neg = jnp.exp(-gcum)
    e_last = gam * e_neg
    rt_ref[...] = (r * jnp.exp(gcum)).astype(BF16)
    kt_ref[...] = (k2 * e_neg).astype(BF16)
    bt_ref[...] = (kka * e_neg).astype(BF16)
    at_ref[...] = (-kk * jnp.exp(gcum - lw)).astype(BF16)
    kh_ref[...] = (k2 * e_last).astype(BF16)
    bh_ref[...] = (kka * e_last).astype(BF16)
    v_ref[...] = v.astype(BF16)

    tpos = lax.broadcasted_iota(jnp.int32, (RWKV_CHUNK, PAIR_W), 0)
    spos = lax.broadcasted_iota(jnp.int32, (RWKV_CHUNK, PAIR_W), 1) & (HEAD_DIM - 1)
    strict = tpos > spos
    incl = tpos >= spos
    head0 = lax.broadcasted_iota(jnp.int32, (RWKV_CHUNK, PAIR_W), 1) < HEAD_DIM
    same_head = ((lax.broadcasted_iota(jnp.int32, (PAIR_W, PAIR_W), 0) < HEAD_DIM)
                 == (lax.broadcasted_iota(jnp.int32, (PAIR_W, PAIR_W), 1) < HEAD_DIM))

    pairs = range(n_pairs)
    lanes = [slice(p * PAIR_W, (p + 1) * PAIR_W) for p in pairs]
    n_chunks = tblk // RWKV_CHUNK

    def phase_a(i, carry):
        slots, at_v, rt_v, bt_v, kt_v, v_v, bh_v, kh_v = [], [], [], [], [], [], [], []
        for cc in range(RWKV_PHASE_A_CHUNKS):
            c = i * RWKV_PHASE_A_CHUNKS + cc
            rows = pl.ds(pl.multiple_of(c * RWKV_CHUNK, RWKV_CHUNK), RWKV_CHUNK)
            loaded = [ref[rows, :] for ref in (at_ref, rt_ref, bt_ref, kt_ref, v_ref, bh_ref, kh_ref)]
            for p in pairs:
                slots.append(c * n_pairs + p)
                for dst, x_c in zip((at_v, rt_v, bt_v, kt_v, v_v, bh_v, kh_v), loaded):
                    dst.append(x_c[:, lanes[p]])
        chains = range(len(slots))
        m4 = [lax.dot_general(jnp.concatenate([at_v[j], rt_v[j]], axis=0),
                              jnp.concatenate([_embed(bt_v[j]), _embed(kt_v[j])], axis=0), _NT,
                              preferred_element_type=F32) for j in chains]
        n_ab = [jnp.where(strict, m[:RWKV_CHUNK, :PAIR_W], 0.0) for m in m4]
        a_ak = [jnp.where(strict, m[:RWKV_CHUNK, PAIR_W:], 0.0).astype(BF16) for m in m4]
        a_rb = [jnp.where(incl, m[RWKV_CHUNK:, :PAIR_W], 0.0).astype(BF16) for m in m4]
        a_rk = [jnp.where(incl, m[RWKV_CHUNK:, PAIR_W:], 0.0).astype(BF16) for m in m4]
        v_m = [_embed(x) for x in v_v]
        akv = [_dot(a_ak[j], v_m[j]).astype(BF16) for j in chains]
        t_inv = [t.astype(BF16) for t in _tri_inverse(n_ab, tpos, spos)]
        wu0 = [_dot(t_inv[j], jnp.concatenate([_embed(at_v[j]), _embed(akv[j])], axis=1))
               for j in chains]
        w_b = [x[:, :PAIR_W].astype(BF16) for x in wu0]
        u0_b = [x[:, PAIR_W:].astype(BF16) for x in wu0]
        p_m = [jnp.where(same_head, lax.dot_general(bh_v[j], w_b[j], _TN, preferred_element_type=F32), 0.0)
               for j in chains]
        q_full = [lax.dot_general(jnp.concatenate([u0_b[j], v_v[j]], axis=0),
                                  jnp.concatenate([bh_v[j], kh_v[j]], axis=0), _TN,
                                  preferred_element_type=F32) for j in chains]
        q_t = [jnp.where(head0, q[:RWKV_CHUNK], q[RWKV_CHUNK:]) for q in q_full]
        r_p = [rt_v[j].astype(F32) + _dot(a_rb[j], _embed(w_b[j])) for j in chains]
        y_0 = [_dot(jnp.concatenate([a_rb[j], a_rk[j]], axis=1),
                    jnp.concatenate([_embed(u0_b[j]), v_m[j]], axis=0)) for j in chains]
        for j in chains:
            p_s[slots[j]] = p_m[j].astype(BF16)
            qt_s[slots[j]] = q_t[j]
            rp_s[slots[j]] = r_p[j].astype(BF16)
            y0_s[slots[j]] = y_0[j]
        return carry

    lax.fori_loop(0, n_chunks // RWKV_PHASE_A_CHUNKS, phase_a, 0)

    states = [s_ref[p] for p in pairs]
    for c in range(n_chunks):
        r0 = c * RWKV_CHUNK
        gam_c = gam_ref[c:c + 1, :]
        slots = [c * n_pairs + p for p in pairs]
        state_b = [s.astype(BF16) for s in states]
        upd = [lax.dot_general(state_b[p], p_s[slots[p]], _NT, preferred_element_type=F32) for p in pairs]
        y_c = [lax.dot_general(rp_s[slots[p]], _embed(state_b[p]), _NT, preferred_element_type=F32)
               + y0_s[slots[p]] for p in pairs]
        states = [states[p] * gam_c[:, lanes[p]] + upd[p] + qt_s[slots[p]] for p in pairs]
        yacc_ref[r0:r0 + RWKV_CHUNK, :] = jnp.concatenate(y_c, axis=1)
    for p in pairs:
        s_ref[p] = states[p]

    y = yacc_ref[...]
    mean = _seg_sum(y, ones_bd) * (1.0 / HEAD_DIM)
    dev = y - mean
    var = _seg_sum(dev * dev, ones_bd, pieces=1) * (1.0 / HEAD_DIM)
    yn = dev * lax.rsqrt(var + LNX_EPS) * lnx_g + lnx_b
    y_ref[0] = (yn + bonus_ref[...]) * gate_ref[...]


def _rwkv(ps, vecs, w_up_p, a_up_p, g_up, ones_bd, tblk=256):
    b, s, _ = ps.shape
    tri = _block_ones(RWKV_CUMSUM_ROWS, RWKV_CHUNK, lower=True)
    act = pltpu.VMEM((tblk, RWKV_W), F32)
    act_b = pltpu.VMEM((tblk, RWKV_W), BF16)
    n_pairs = RWKV_W // PAIR_W
    n_chunks = tblk // RWKV_CHUNK
    n_slots = n_chunks * n_pairs
    gam_rows = -(-n_chunks // V7X_SUBLANES) * V7X_SUBLANES
    scratch = [pltpu.VMEM((n_pairs, HEAD_DIM, PAIR_W), F32)]
    scratch += [act_b] * 7
    scratch += [pltpu.VMEM((gam_rows, RWKV_W), F32)] + [act] * 3
    scratch += [
        pltpu.VMEM((n_slots, PAIR_W, PAIR_W), BF16),
        pltpu.VMEM((n_slots, RWKV_CHUNK, PAIR_W), F32),
        pltpu.VMEM((n_slots, RWKV_CHUNK, PAIR_W), BF16),
        pltpu.VMEM((n_slots, RWKV_CHUNK, PAIR_W), F32),
    ]
    return pl.pallas_call(
        _rwkv_kernel,
        grid=(b, s // tblk),
        in_specs=[
            pl.BlockSpec((1, tblk, RWKV_SHIFT_W), lambda i, t: (i, t, 0)),
            _const_spec(vecs.shape),
            _const_spec(w_up_p.shape),
            _const_spec(a_up_p.shape),
            _const_spec(g_up.shape),
            _const_spec(tri.shape),
            _const_spec((SLAB_W, SLAB_W)),
        ],
        out_specs=pl.BlockSpec((1, tblk, RWKV_W), lambda i, t: (i, t, 0)),
        out_shape=jax.ShapeDtypeStruct((b, s, RWKV_W), F32),
        scratch_shapes=scratch,
        compiler_params=_cparams(2, 48),
        name="rwkv",
    )(ps, vecs, w_up_p, a_up_p, g_up, tri, ones_bd)


def _mixout_kernel(x_ref, ya_ref, qm_ref, k_ref, v_ref, w_ref, o_ref):
    wa = ya_ref.shape[-1]
    qm = qm_ref[...]
    kmem = k_ref[0]
    vmem = v_ref[0]
    head = lax.broadcasted_iota(jnp.int32, qm.shape, 1) // HEAD_DIM
    heads = range(MEM_HEADS)
    logits = [lax.dot_general(jnp.where(head == h, qm, 0.0).astype(BF16), kmem, _NT,
                              preferred_element_type=F32) for h in heads]
    p = [jnp.exp(lg - jnp.max(lg, axis=-1, keepdims=True)) for lg in logits]
    inv_l = [1.0 / jnp.sum(ph, axis=-1, keepdims=True) for ph in p]
    pv = [_dot(ph.astype(BF16), vmem) for ph in p]
    y_mem = pv[0] * inv_l[0]
    for h in heads[1:]:
        y_mem = jnp.where(head == h, pv[h] * inv_l[h], y_mem)
    y = _dot(ya_ref[...].astype(BF16), w_ref[:wa, :]) + _dot(y_mem.astype(BF16), w_ref[wa:, :])
    o_ref[...] = x_ref[...] + y


def _mixout(x, ya, qm, k_mem, v_mem, w_out, seq_len, tm=512):
    n, d = x.shape
    wa = ya.shape[-1]
    tiles_per_seq = seq_len // tm
    m = k_mem.shape[1]
    return pl.pallas_call(
        _mixout_kernel,
        grid=(n // tm,),
        in_specs=[
            pl.BlockSpec((tm, d), lambda i: (i, 0)),
            pl.BlockSpec((tm, wa), lambda i: (i, 0)),
            pl.BlockSpec((tm, MEM_W), lambda i: (i, 0)),
            pl.BlockSpec((1, m, MEM_W), lambda i: (i // tiles_per_seq, 0, 0)),
            pl.BlockSpec((1, m, MEM_W), lambda i: (i // tiles_per_seq, 0, 0)),
            _const_spec(w_out.shape),
        ],
        out_specs=pl.BlockSpec((tm, d), lambda i: (i, 0)),
        out_shape=jax.ShapeDtypeStruct((n, d), F32),
        compiler_params=_cparams(1, 40),
        name="mixout",
    )(x, ya, qm, k_mem, v_mem, w_out)


def _t5_bucket(dist):
    max_exact = NUM_BUCKETS // 2
    d_f = jnp.maximum(dist, 1).astype(F32)
    large = max_exact + (jnp.log(d_f / max_exact) / math.log(MAX_DISTANCE / max_exact)
                         * (NUM_BUCKETS - max_exact)).astype(jnp.int32)
    large = jnp.minimum(large, NUM_BUCKETS - 1)
    return jnp.where(dist < max_exact, dist, large)


def _band_buckets():
    qi = jnp.arange(BLOCK)[:, None]
    kj = jnp.arange(2 * BLOCK)[None, :]
    dsub = BLOCK + qi - kj
    out = []
    for window, dil in DIL_GROUPS:
        band = (dsub >= 0) & (dsub <= window // dil)
        out.append(jnp.where(band, _t5_bucket(jnp.maximum(dsub, 0) * dil), -1))
    return jnp.stack(out).astype(jnp.int32)


def _bias_kernel(tab_ref, idx_ref, o_ref):
    head = pl.program_id(0)
    idx = idx_ref[0]
    acc = jnp.full(idx.shape, NEG_INF, F32)
    for bucket in range(NUM_BUCKETS):
        acc = jnp.where(idx == bucket, tab_ref[bucket, head], acc)
    o_ref[0] = acc


def _band_bias(rel_bias):
    n_heads = rel_bias.shape[1]
    return pl.pallas_call(
        _bias_kernel,
        grid=(n_heads,),
        in_specs=[
            pl.BlockSpec(memory_space=pltpu.SMEM),
            pl.BlockSpec((1, BLOCK, 2 * BLOCK), lambda h: (h // DIL_GROUP_HEADS, 0, 0)),
        ],
        out_specs=pl.BlockSpec((1, BLOCK, 2 * BLOCK), lambda h: (h, 0, 0)),
        out_shape=jax.ShapeDtypeStruct((n_heads, BLOCK, 2 * BLOCK), F32),
        compiler_params=_cparams(1, 16),
        name="band_bias",
    )(rel_bias, _band_buckets())


def _dil_group(dil, q_ref, k_ref, v_ref, bias_ref, acc_ref, m_ref, l_ref):
    seq_len = q_ref.shape[2]
    n_blocks = seq_len // (dil * BLOCK)
    shift = n_blocks.bit_length() - 1
    head0 = lax.broadcasted_iota(jnp.int32, (BLOCK, PAIR_W), 1) < HEAD_DIM
    in_prev = lax.broadcasted_iota(jnp.int32, (BLOCK, 2 * BLOCK), 1) < BLOCK

    def rows(sub, blk):
        if dil == 1:
            return pl.ds(pl.multiple_of(BLOCK * blk, BLOCK), BLOCK)
        return pl.ds(sub + dil * BLOCK * blk, BLOCK, stride=dil)

    def tiles(i, carry):
        slabs = []
        for j in range(DIL_TILES_PER_ITER):
            t = i * DIL_TILES_PER_ITER + j
            sub = t >> shift
            blk = t & (n_blocks - 1)
            own = rows(sub, blk)
            prev = rows(sub, jnp.maximum(blk - 1, 0))
            no_prev = jnp.logical_and(blk == 0, in_prev)
            slabs += [(pair, own, prev, no_prev) for pair in range(DIL_OUT_W // PAIR_W)]
        q = [q_ref[0, pair, own, :] for pair, own, _, _ in slabs]
        keys = [jnp.concatenate([k_ref[0, pair, prev, :], k_ref[0, pair, own, :]], axis=0).astype(BF16)
                for pair, own, prev, _ in slabs]
        vals = [jnp.concatenate([v_ref[0, pair, prev, :], v_ref[0, pair, own, :]], axis=0).astype(BF16)
                for pair, own, prev, _ in slabs]
        m_old = [m_ref[pair, own, :] for pair, own, _, _ in slabs]
        l_old = [l_ref[pair, own, :] for pair, own, _, _ in slabs]
        acc_old = [acc_ref[pair, own, :] for pair, own, _, _ in slabs]
        heads = [(s, h) for s in range(len(slabs)) for h in range(2)]
        logits = [lax.dot_general(jnp.where(head0 if h == 0 else ~head0, q[s], 0.0).astype(BF16),
                                  keys[s], _NT, preferred_element_type=F32) for s, h in heads]
        logits = [jnp.where(slabs[s][3], NEG_INF, lg + bias_ref[2 * slabs[s][0] + h])
                  for lg, (s, h) in zip(logits, heads)]
        mx = [jnp.max(lg, axis=-1, keepdims=True) for lg in logits]
        p = [jnp.exp(lg - m) for lg, m in zip(logits, mx)]
        ls = [jnp.sum(ph, axis=-1, keepdims=True) for ph in p]
        pv = [_dot(ph.astype(BF16), vals[s]) for ph, (s, h) in zip(p, heads)]
        for s, (pair, own, _, _) in enumerate(slabs):
            m_t = jnp.where(head0, mx[2 * s], mx[2 * s + 1])
            l_t = jnp.where(head0, ls[2 * s], ls[2 * s + 1])
            acc_t = jnp.where(head0, pv[2 * s], pv[2 * s + 1])
            m_new = jnp.maximum(m_old[s], m_t)
            e_old = jnp.exp(m_old[s] - m_new)
            e_t = jnp.exp(m_t - m_new)
            m_ref[pair, own, :] = m_new
            l_ref[pair, own, :] = l_old[s] * e_old + l_t * e_t
            acc_ref[pair, own, :] = acc_old[s] * e_old + acc_t * e_t
        return carry

    lax.fori_loop(0, dil * n_blocks // DIL_TILES_PER_ITER, tiles, 0)


def _dil_kernel(q_ref, k_ref, v_ref, bias_ref, o_ref, acc_ref, m_ref, l_ref):
    g = pl.program_id(1)

    @pl.when(g == 0)
    def _():
        m_ref[...] = jnp.full(m_ref.shape, NEG_INF, F32)
        l_ref[...] = jnp.zeros_like(l_ref)
        acc_ref[...] = jnp.zeros_like(acc_ref)

    for gi, (_, dil) in enumerate(DIL_GROUPS):
        pl.when(g == gi)(functools.partial(_dil_group, dil, q_ref, k_ref, v_ref, bias_ref, acc_ref, m_ref, l_ref))

    @pl.when(g == len(DIL_GROUPS) - 1)
    def _():
        for pair in range(DIL_OUT_W // PAIR_W):
            o_ref[0, :, pair * PAIR_W:(pair + 1) * PAIR_W] = acc_ref[pair] / l_ref[pair]


def _dilated_attention(q_slabs, kv_slabs, bias):
    b, _, s, _ = q_slabs.shape
    n_groups = len(DIL_GROUPS)
    pairs = DIL_OUT_W // PAIR_W
    blk = (1, pairs, s, PAIR_W)
    return pl.pallas_call(
        _dil_kernel,
        grid=(b, n_groups),
        in_specs=[
            pl.BlockSpec(blk, lambda i, g: (i, g, 0, 0)),
            pl.BlockSpec(blk, lambda i, g: (i, g, 0, 0)),
            pl.BlockSpec(blk, lambda i, g: (i, n_groups + g, 0, 0)),
            pl.BlockSpec((DIL_GROUP_HEADS, BLOCK, 2 * BLOCK), lambda i, g: (g, 0, 0)),
        ],
        out_specs=pl.BlockSpec((1, s, DIL_OUT_W), lambda i, g: (i, 0, 0)),
        out_shape=jax.ShapeDtypeStruct((b, s, DIL_OUT_W), F32),
        scratch_shapes=[pltpu.VMEM((pairs, s, PAIR_W), F32)] * 3,
        compiler_params=_cparams(2, 56),
        name="dilated_attention",
    )(q_slabs, kv_slabs, kv_slabs, bias)


def kernel(x, mem, ffn_pre_norm, ffn_pre_w_in, ffn_pre_w_out, mix_norm, ffn_post_norm, ffn_post_w_in, ffn_post_w_out, mem_norm, mem_w_kv, mem_q_norm, mem_k_norm, a_w_in, a_shift_mu, a_w0, a_w_up, a_a0, a_a_up, a_g_up, a_kk_scale, a_k_a, a_r_k, a_lnx_g, a_lnx_b, a_w_out, b_w_q, b_q_norm, b_w_out, kv_norm, kv_w, kv_k_norm, rel_bias):
    b, s, d = x.shape
    depth = ffn_pre_w_in.shape[0]
    n_a = a_w_in.shape[0]
    n = b * s
    scale = 1.0 / math.sqrt(HEAD_DIM)
    ones_bd = _block_ones(SLAB_W, HEAD_DIM)
    row = lambda p: p.reshape(1, -1)

    k_mem, v_mem = _memkv(mem, mem_norm[:, None, :], mem_w_kv.astype(BF16),
                          jnp.tile(mem_k_norm, (1, MEM_HEADS))[:, None, :], ones_bd)
    mem_q_gain = jnp.tile(mem_q_norm, (1, MEM_HEADS)) * scale

    xf = x.reshape(n, d)
    kv = None
    bias = None
    for layer in range(depth):
        xf = _ffn(xf, row(ffn_pre_norm[layer]), ffn_pre_w_in[layer].astype(BF16),
                  ffn_pre_w_out[layer].astype(BF16))
        if layer < n_a:
            i = layer
            ps, qm = _aproj(xf, row(mix_norm[layer]), a_w_in[i].astype(BF16), row(a_shift_mu[i]),
                            row(mem_q_gain[layer]), ones_bd, s)
            zeros = jnp.zeros((DECAY_LORA, RWKV_W), F32)
            vecs = jnp.stack([a_w0[i], a_a0[i], a_kk_scale[i], a_k_a[i], a_r_k[i].reshape(-1),
                              a_lnx_g[i], a_lnx_b[i], jnp.zeros((RWKV_W,), F32)])
            y_main = _rwkv(ps.reshape(b, s, RWKV_SHIFT_W), vecs,
                           jnp.concatenate([a_w_up[i], zeros]).astype(BF16),
                           jnp.concatenate([zeros, a_a_up[i]]).astype(BF16),
                           a_g_up[i].astype(BF16), ones_bd)
            xf = _mixout(xf, y_main.reshape(n, RWKV_W), qm, k_mem[layer], v_mem[layer],
                         a_w_out[i].astype(BF16), s)
        else:
            j = layer - n_a
            q_gain = jnp.concatenate([jnp.tile(b_q_norm[j], DIL_W // HEAD_DIM) * scale, mem_q_gain[layer]])
            q_dil, qm = _qproj(xf, row(mix_norm[layer]), b_w_q[j].astype(BF16), row(q_gain), ones_bd, s, DIL_W)
            if bias is None:
                bias = _band_bias(rel_bias)
            y_dil = _dilated_attention(q_dil, kv, bias)
            xf = _mixout(xf, y_dil.reshape(n, DIL_OUT_W), qm, k_mem[layer], v_mem[layer],
                         b_w_out[j].astype(BF16), s)
        xf = _ffn(xf, row(ffn_post_norm[layer]), ffn_post_w_in[layer].astype(BF16),
                  ffn_post_w_out[layer].astype(BF16))
        if layer == n_a - 1:
            k_gain = jnp.tile(kv_k_norm, DIL_W // HEAD_DIM)
            kv, = _qproj(xf, row(kv_norm), kv_w.astype(BF16), row(k_gain), ones_bd, s, 2 * DIL_W)
    return xf.reshape(b, s, d)
```

```python
import functools
import math

import jax
import jax.numpy as jnp
import numpy as np
from jax import lax
from jax.experimental import pallas as pl
from jax.experimental.pallas import tpu as pltpu

F32 = jnp.float32
BF16 = jnp.bfloat16

HEAD_DIM = 64
MEM_HEADS = 4
MEM_W = MEM_HEADS * HEAD_DIM
RWKV_HEADS = 12
RWKV_W = RWKV_HEADS * HEAD_DIM
DECAY_LORA = 64
AAA_LORA = 64
GATE_LORA = 128
RWKV_SHIFT_W = 3 * RWKV_W + DECAY_LORA + AAA_LORA + GATE_LORA
LORA_LO = 3 * RWKV_W
DIL_GROUPS = ((128, 1), (512, 4), (2048, 16))
DIL_GROUP_HEADS = 4
DIL_W = len(DIL_GROUPS) * DIL_GROUP_HEADS * HEAD_DIM
DIL_OUT_W = DIL_GROUP_HEADS * HEAD_DIM
BLOCK = 128
NUM_BUCKETS = 32
MAX_DISTANCE = 2048
NORM_EPS = 1e-6
LNX_EPS = 64e-5
NEG_INF = -1e30

V7X_LANES = 128
V7X_SUBLANES = 8
V7X_VMEM_BYTES = 64 * 1024 * 1024

RWKV_CHUNK = 64
RWKV_PHASE_A_CHUNKS = 4
RWKV_CUMSUM_ROWS = 256
DIL_TILES_PER_ITER = 2
DIL_ROW_PHASES = 4
PROJ_SUB_ROWS = 128
PAIR_W = 2 * HEAD_DIM
SLAB_W = 4 * HEAD_DIM

_NT = (((1,), (1,)), ((), ()))
_TN = (((0,), (0,)), ((), ()))


def _cparams(n_axes, vmem_mib):
    return pltpu.CompilerParams(
        dimension_semantics=("arbitrary",) * n_axes,
        vmem_limit_bytes=min(vmem_mib * 1024 * 1024, V7X_VMEM_BYTES - 4 * 1024 * 1024),
    )


def _const_spec(shape):
    zeros = (0,) * len(shape)
    return pl.BlockSpec(shape, lambda *_: zeros)


def _dot(a, b):
    return jnp.dot(a, b, preferred_element_type=F32)


def _rms(x, g, eps):
    return x * lax.rsqrt(jnp.mean(x * x, axis=-1, keepdims=True) + eps) * g


def _split2(x):
    hi = x.astype(BF16)
    lo = (x - hi.astype(F32)).astype(BF16)
    return hi, lo


def _seg_sum(x, ones_bd, pieces=2):
    outs = []
    for s in range(x.shape[-1] // SLAB_W):
        slab = x[:, s * SLAB_W:(s + 1) * SLAB_W]
        if pieces == 1:
            outs.append(_dot(slab.astype(BF16), ones_bd))
        else:
            hi, lo = _split2(slab)
            outs.append(_dot(hi, ones_bd) + _dot(lo, ones_bd))
    return outs[0] if len(outs) == 1 else jnp.concatenate(outs, axis=-1)


def _block_ones(n, blk, lower=False):
    i = np.arange(n)
    m = (i[:, None] // blk) == (i[None, :] // blk)
    if lower:
        m = m & (i[:, None] >= i[None, :])
    return jnp.asarray(m, dtype=BF16)


def _ffn_kernel(x_ref, g_ref, win_ref, wout_ref, o_ref):
    x = x_ref[...]
    d_ff = wout_ref.shape[0]
    xn = _rms(x, g_ref[...], NORM_EPS).astype(BF16)
    h = _dot(xn, win_ref[...])
    gate = h[:, :d_ff]
    up = h[:, d_ff:]
    act = (gate * jax.nn.sigmoid(gate) * up).astype(BF16)
    o_ref[...] = x + 0.5 * _dot(act, wout_ref[...])


def _ffn(x, g, w_in, w_out, tm=512):
    n, d = x.shape
    return pl.pallas_call(
        _ffn_kernel,
        grid=(n // tm,),
        in_specs=[
            pl.BlockSpec((tm, d), lambda i: (i, 0)),
            _const_spec((1, d)),
            pl.BlockSpec(w_in.shape, lambda i: (0, 0), pipeline_mode=pl.Buffered(1)),
            pl.BlockSpec(w_out.shape, lambda i: (0, 0), pipeline_mode=pl.Buffered(1)),
        ],
        out_specs=pl.BlockSpec((tm, d), lambda i: (i, 0)),
        out_shape=jax.ShapeDtypeStruct((n, d), F32),
        compiler_params=_cparams(1, 56),
        name="ffn",
    )(x, g, w_in, w_out)


def _memkv_kernel(mem_ref, g_ref, w_ref, kg_ref, ones_ref, k_ref, v_ref):
    m = _rms(mem_ref[0], g_ref[0], NORM_EPS).astype(BF16)
    kv = _dot(m, w_ref[0])
    k = kv[:, :MEM_W]
    ms = _seg_sum(k * k, ones_ref[...]) * (1.0 / HEAD_DIM)
    k_ref[0, 0] = (k * lax.rsqrt(ms + NORM_EPS) * kg_ref[0]).astype(BF16)
    v_ref[0, 0] = kv[:, MEM_W:].astype(BF16)


def _memkv(mem, mem_norm, w_kv, k_gain, ones_bd):
    b, m, d = mem.shape
    depth = w_kv.shape[0]
    out = jax.ShapeDtypeStruct((depth, b, m, MEM_W), BF16)
    return pl.pallas_call(
        _memkv_kernel,
        grid=(depth, b),
        in_specs=[
            pl.BlockSpec((1, m, d), lambda l, i: (i, 0, 0)),
            pl.BlockSpec((1, 1, d), lambda l, i: (l, 0, 0)),
            pl.BlockSpec((1, d, 2 * MEM_W), lambda l, i: (l, 0, 0)),
            pl.BlockSpec((1, 1, MEM_W), lambda l, i: (l, 0, 0)),
            _const_spec((SLAB_W, SLAB_W)),
        ],
        out_specs=[pl.BlockSpec((1, 1, m, MEM_W), lambda l, i: (l, i, 0, 0))] * 2,
        out_shape=[out, out],
        compiler_params=_cparams(2, 32),
        name="memkv",
    )(mem, mem_norm, w_kv, k_gain, ones_bd)


def _head_norm(q, gain, ones_bd):
    ms = _seg_sum(q * q, ones_bd) * (1.0 / HEAD_DIM)
    return q * lax.rsqrt(ms + NORM_EPS) * gain


def _aproj_kernel(x_ref, g_ref, w_ref, mu_ref, qg_ref, ones_ref, ps_ref, qm_ref, carry_ref, *, tiles_per_seq):
    i = pl.program_id(0)
    tm = x_ref.shape[0]
    u = _rms(x_ref[...], g_ref[...], NORM_EPS).astype(BF16)

    @pl.when(i % tiles_per_seq == 0)
    def _():
        carry_ref[...] = jnp.zeros_like(carry_ref)

    starts = range(0, tm, PROJ_SUB_ROWS)
    projs = [_dot(u[r0:r0 + PROJ_SUB_ROWS], w_ref[...]) for r0 in starts]
    row = lax.broadcasted_iota(jnp.int32, (PROJ_SUB_ROWS, RWKV_SHIFT_W), 0)
    last = carry_ref[V7X_SUBLANES - 1:V7X_SUBLANES, :]
    for r0, proj in zip(starts, projs):
        p = proj[:, :RWKV_SHIFT_W]
        prev = jnp.where(row == 0, last, pltpu.roll(p, 1, 0))
        ps_ref[r0:r0 + PROJ_SUB_ROWS, :] = p + mu_ref[...] * (prev - p)
        last = p[PROJ_SUB_ROWS - 1:, :]
        qm_ref[r0:r0 + PROJ_SUB_ROWS, :] = _head_norm(proj[:, RWKV_SHIFT_W:], qg_ref[...], ones_ref[...])
    carry_ref[...] = projs[-1][PROJ_SUB_ROWS - V7X_SUBLANES:, :RWKV_SHIFT_W]


def _aproj(x, g, w_in, mu, q_gain, ones_bd, seq_len, tm=512):
    n, d = x.shape
    return pl.pallas_call(
        functools.partial(_aproj_kernel, tiles_per_seq=seq_len // tm),
        grid=(n // tm,),
        in_specs=[
            pl.BlockSpec((tm, d), lambda i: (i, 0)),
            _const_spec((1, d)),
            pl.BlockSpec(w_in.shape, lambda i: (0, 0), pipeline_mode=pl.Buffered(1)),
            _const_spec((1, RWKV_SHIFT_W)),
            _const_spec((1, MEM_W)),
            _const_spec((SLAB_W, SLAB_W)),
        ],
        out_specs=[pl.BlockSpec((tm, RWKV_SHIFT_W), lambda i: (i, 0)),
                   pl.BlockSpec((tm, MEM_W), lambda i: (i, 0))],
        out_shape=[jax.ShapeDtypeStruct((n, RWKV_SHIFT_W), F32),
                   jax.ShapeDtypeStruct((n, MEM_W), F32)],
        scratch_shapes=[pltpu.VMEM((V7X_SUBLANES, RWKV_SHIFT_W), F32)],
        compiler_params=_cparams(1, 48),
        name="aproj",
    )(x, g, w_in, mu, q_gain, ones_bd)


def _qproj_kernel(x_ref, g_ref, w_ref, qg_ref, ones_ref, slab_ref, *rest, normed_w):
    stage_ref = rest[-1]
    flat_refs = rest[:-1]
    tm = x_ref.shape[0]
    u = _rms(x_ref[...], g_ref[...], NORM_EPS).astype(BF16)
    proj = _dot(u, w_ref[...])
    normed = _head_norm(proj[:, :normed_w], qg_ref[...], ones_ref[...])
    n_slabs = slab_ref.shape[1]
    for j in range(n_slabs):
        src = normed if (j + 1) * V7X_LANES <= normed_w else proj
        stage_ref[j] = src[:, j * V7X_LANES:(j + 1) * V7X_LANES]
    for j in range(n_slabs):
        for c in range(DIL_ROW_PHASES):
            slab_ref[0, j, c] = stage_ref[j, pl.ds(c, tm // DIL_ROW_PHASES, stride=DIL_ROW_PHASES), :]
    if flat_refs:
        flat_refs[0][...] = normed[:, n_slabs * V7X_LANES:]


def _qproj(x, g, w, q_gain, ones_bd, seq_len, slab_w, tm=512):
    n, d = x.shape
    wo = w.shape[1]
    normed_w = q_gain.shape[-1]
    n_slabs = slab_w // V7X_LANES
    tiles_per_seq = seq_len // tm
    phase_rows = seq_len // DIL_ROW_PHASES
    out_specs = [pl.BlockSpec((1, n_slabs, DIL_ROW_PHASES, tm // DIL_ROW_PHASES, V7X_LANES),
                              lambda i: (i // tiles_per_seq, 0, 0, i % tiles_per_seq, 0))]
    out_shape = [jax.ShapeDtypeStruct((n // seq_len, n_slabs, DIL_ROW_PHASES, phase_rows, V7X_LANES), F32)]
    if wo > slab_w:
        assert normed_w == wo
        out_specs.append(pl.BlockSpec((tm, wo - slab_w), lambda i: (i, 0)))
        out_shape.append(jax.ShapeDtypeStruct((n, wo - slab_w), F32))
    outs = list(pl.pallas_call(
        functools.partial(_qproj_kernel, normed_w=normed_w),
        grid=(n // tm,),
        in_specs=[
            pl.BlockSpec((tm, d), lambda i: (i, 0)),
            _const_spec((1, d)),
            pl.BlockSpec(w.shape, lambda i: (0, 0), pipeline_mode=pl.Buffered(1)),
            _const_spec((1, normed_w)),
            _const_spec((SLAB_W, SLAB_W)),
        ],
        out_specs=out_specs,
        out_shape=out_shape,
        scratch_shapes=[pltpu.VMEM((n_slabs, tm, V7X_LANES), F32)],
        compiler_params=_cparams(1, 48),
        name="qproj",
    )(x, g, w, q_gain, ones_bd))
    outs[0] = outs[0].reshape(n // seq_len, n_slabs, seq_len, V7X_LANES)
    return outs


def _embed(x):
    head0 = lax.broadcasted_iota(jnp.int32, x.shape, 1) < HEAD_DIM
    zero = jnp.zeros_like(x)
    return jnp.concatenate([jnp.where(head0, x, zero), jnp.where(head0, zero, x)], axis=0)


def _tri_inverse(nmats, tpos, spos):
    eye = (tpos == spos).astype(F32)
    same = {s: (tpos >> s) == (spos >> s) for s in (3, 4, 5)}

    def mm(lhs, rhs):
        return [_dot(a, _embed(b)) for a, b in zip(lhs, rhs)]

    def bf(xs):
        return [x.astype(BF16) for x in xs]

    n8 = [jnp.where(same[3], n, 0.0) for n in nmats]
    n8b = bf(n8)
    n2 = mm(n8b, n8b)
    n2b = bf(n2)
    n4 = mm(n2b, n2b)
    t = mm(bf([eye + a for a in n8]), bf([eye + a for a in n2]))
    t = mm(bf(t), bf([eye + a for a in n4]))
    for lo, hi in ((3, 4), (4, 5), (5, None)):
        off = ~same[lo] if hi is None else (same[hi] & ~same[lo])
        tb = bf(t)
        z = mm(bf([jnp.where(off, n, 0.0) for n in nmats]), tb)
        t = [a + d for a, d in zip(t, mm(tb, bf(z)))]
    return t


def _rwkv_kernel(ps_ref, vec_ref, wup_ref, aup_ref, gup_ref, tri_ref, ones_ref, y_ref,
                 s_ref, rt_ref, kt_ref, bt_ref, at_ref, kh_ref, bh_ref, v_ref, gam_ref, yacc_ref,
                 bonus_ref, gate_ref, p_s, qt_s, rp_s, y0_s):
    tblk = ps_ref.shape[1]
    n_pairs = RWKV_W // PAIR_W

    @pl.when(pl.program_id(1) == 0)
    def _():
        s_ref[...] = jnp.zeros_like(s_ref)

    ps = ps_ref[0]
    r = ps[:, :RWKV_W]
    k = ps[:, RWKV_W:2 * RWKV_W]
    v = ps[:, 2 * RWKV_W:3 * RWKV_W]
    lora_in = ps[:, LORA_LO:LORA_LO + PAIR_W]
    g_lo = ps[:, LORA_LO + PAIR_W:]
    w0, a0, kk_scale, k_a = vec_ref[0:1, :], vec_ref[1:2, :], vec_ref[2:3, :], vec_ref[3:4, :]
    r_k, lnx_g, lnx_b = vec_ref[4:5, :], vec_ref[5:6, :], vec_ref[6:7, :]
    ones_bd = ones_ref[...]

    nz = -(w0 + _dot(jnp.tanh(lora_in).astype(BF16), wup_ref[...]))
    softplus = jnp.maximum(nz, 0.0) + jnp.log(1.0 + jnp.exp(-jnp.abs(nz)))
    lw = -jnp.exp(-softplus - 0.5)
    a = jax.nn.sigmoid(a0 + _dot(lora_in.astype(BF16), aup_ref[...]))
    gate_ref[...] = _dot(jax.nn.sigmoid(g_lo).astype(BF16), gup_ref[...])
    kk = k * kk_scale
    kk = kk / jnp.maximum(jnp.sqrt(_seg_sum(kk * kk, ones_bd, pieces=1)), 1e-12)
    k2 = k * (1.0 + (a - 1.0) * k_a)
    kka = kk * a
    bonus_ref[...] = _seg_sum(r * k2 * r_k, ones_bd) * v

    hi, lo = _split2(lw)
    tri = tri_ref[...]
    gcum = jnp.concatenate(
        [_dot(tri, hi[s0:s0 + RWKV_CUMSUM_ROWS]) + _dot(tri, lo[s0:s0 + RWKV_CUMSUM_ROWS])
         for s0 in range(0, tblk, RWKV_CUMSUM_ROWS)], axis=0)
    n_chunks = tblk // RWKV_CHUNK
    gam_rows = [jnp.exp(gcum[(c + 1) * RWKV_CHUNK - 1:(c + 1) * RWKV_CHUNK, :]) for c in range(n_chunks)]
    gam = jnp.concatenate([jnp.broadcast_to(g, (RWKV_CHUNK, RWKV_W)) for g in gam_rows], axis=0)
    for c in range(n_chunks):
        gam_ref[c:c + 1, :] = gam_rows[c]
    e_neg = jnp.exp(-gcum)
    e_last = gam * e_neg
    rt_ref[...] = (r * jnp.exp(gcum)).astype(BF16)
    kt_ref[...] = (k2 * e_neg).astype(BF16)
    bt_ref[...] = (kka * e_neg).astype(BF16)
    at_ref[...] = (-kk * jnp.exp(gcum - lw)).astype(BF16)
    kh_ref[...] = (k2 * e_last).astype(BF16)
    bh_ref[...] = (kka * e_last).astype(BF16)
    v_ref[...] = v.astype(BF16)

    tpos = lax.broadcasted_iota(jnp.int32, (RWKV_CHUNK, PAIR_W), 0)
    spos = lax.broadcasted_iota(jnp.int32, (RWKV_CHUNK, PAIR_W), 1) & (HEAD_DIM - 1)
    strict = tpos > spos
    incl = tpos >= spos
    head0 = lax.broadcasted_iota(jnp.int32, (RWKV_CHUNK, PAIR_W), 1) < HEAD_DIM
    same_head = ((lax.broadcasted_iota(jnp.int32, (PAIR_W, PAIR_W), 0) < HEAD_DIM)
                 == (lax.broadcasted_iota(jnp.int32, (PAIR_W, PAIR_W), 1) < HEAD_DIM))

    pairs = range(n_pairs)
    lanes = [slice(p * PAIR_W, (p + 1) * PAIR_W) for p in pairs]

    def phase_a(i, carry):
        slots, at_v, rt_v, bt_v, kt_v, v_v, bh_v, kh_v = [], [], [], [], [], [], [], []
        for cc in range(RWKV_PHASE_A_CHUNKS):
            c = i * RWKV_PHASE_A_CHUNKS + cc
            rows = pl.ds(pl.multiple_of(c * RWKV_CHUNK, RWKV_CHUNK), RWKV_CHUNK)
            loaded = [ref[rows, :] for ref in (at_ref, rt_ref, bt_ref, kt_ref, v_ref, bh_ref, kh_ref)]
            for p in pairs:
                slots.append(c * n_pairs + p)
                for dst, x_c in zip((at_v, rt_v, bt_v, kt_v, v_v, bh_v, kh_v), loaded):
                    dst.append(x_c[:, lanes[p]])
        chains = range(len(slots))
        m4 = [lax.dot_general(jnp.concatenate([at_v[j], rt_v[j]], axis=0),
                              jnp.concatenate([_embed(bt_v[j]), _embed(kt_v[j])], axis=0), _NT,
                              preferred_element_type=F32) for j in chains]
        n_ab = [jnp.where(strict, m[:RWKV_CHUNK, :PAIR_W], 0.0) for m in m4]
        a_ak = [jnp.where(strict, m[:RWKV_CHUNK, PAIR_W:], 0.0).astype(BF16) for m in m4]
        a_rb = [jnp.where(incl, m[RWKV_CHUNK:, :PAIR_W], 0.0).astype(BF16) for m in m4]
        a_rk = [jnp.where(incl, m[RWKV_CHUNK:, PAIR_W:], 0.0).astype(BF16) for m in m4]
        v_m = [_embed(x) for x in v_v]
        akv = [_dot(a_ak[j], v_m[j]).astype(BF16) for j in chains]
        t_inv = [t.astype(BF16) for t in _tri_inverse(n_ab, tpos, spos)]
        wu0 = [_dot(t_inv[j], jnp.concatenate([_embed(at_v[j]), _embed(akv[j])], axis=1))
               for j in chains]
        w_b = [x[:, :PAIR_W].astype(BF16) for x in wu0]
        u0_b = [x[:, PAIR_W:].astype(BF16) for x in wu0]
        p_m = [jnp.where(same_head, lax.dot_general(bh_v[j], w_b[j], _TN, preferred_element_type=F32), 0.0)
               for j in chains]
        q_full = [lax.dot_general(jnp.concatenate([u0_b[j], v_v[j]], axis=0),
                                  jnp.concatenate([bh_v[j], kh_v[j]], axis=0), _TN,
                                  preferred_element_type=F32) for j in chains]
        q_t = [jnp.where(head0, q[:RWKV_CHUNK], q[RWKV_CHUNK:]) for q in q_full]
        r_p = [rt_v[j].astype(F32) + _dot(a_rb[j], _embed(w_b[j])) for j in chains]
        y_0 = [_dot(jnp.concatenate([a_rb[j], a_rk[j]], axis=1),
                    jnp.concatenate([_embed(u0_b[j]), v_m[j]], axis=0)) for j in chains]
        for j in chains:
            p_s[slots[j]] = p_m[j].astype(BF16)
            qt_s[slots[j]] = q_t[j]
            rp_s[slots[j]] = r_p[j].astype(BF16)
            y0_s[slots[j]] = y_0[j]
        return carry

    lax.fori_loop(0, n_chunks // RWKV_PHASE_A_CHUNKS, phase_a, 0)

    states = [s_ref[p] for p in pairs]
    for c in range(n_chunks):
        r0 = c * RWKV_CHUNK
        gam_c = gam_ref[c:c + 1, :]
        slots = [c * n_pairs + p for p in pairs]
        state_b = [s.astype(BF16) for s in states]
        upd = [lax.dot_general(state_b[p], p_s[slots[p]], _NT, preferred_element_type=F32) for p in pairs]
        y_c = [lax.dot_general(rp_s[slots[p]], _embed(state_b[p]), _NT, preferred_element_type=F32)
               + y0_s[slots[p]] for p in pairs]
        states = [states[p] * gam_c[:, lanes[p]] + upd[p] + qt_s[slots[p]] for p in pairs]
        yacc_ref[r0:r0 + RWKV_CHUNK, :] = jnp.concatenate(y_c, axis=1)
    for p in pairs:
        s_ref[p] = states[p]

    y = yacc_ref[...]
    mean = _seg_sum(y, ones_bd) * (1.0 / HEAD_DIM)
    dev = y - mean
    var = _seg_sum(dev * dev, ones_bd, pieces=1) * (1.0 / HEAD_DIM)
    yn = dev * lax.rsqrt(var + LNX_EPS) * lnx_g + lnx_b
    y_ref[0] = (yn + bonus_ref[...]) * gate_ref[...]


def _rwkv(ps, vecs, w_up_p, a_up_p, g_up, ones_bd, tblk=256):
    b, s, _ = ps.shape
    tri = _block_ones(RWKV_CUMSUM_ROWS, RWKV_CHUNK, lower=True)
    act = pltpu.VMEM((tblk, RWKV_W), F32)
    act_b = pltpu.VMEM((tblk, RWKV_W), BF16)
    n_pairs = RWKV_W // PAIR_W
    n_chunks = tblk // RWKV_CHUNK
    n_slots = n_chunks * n_pairs
    gam_rows = -(-n_chunks // V7X_SUBLANES) * V7X_SUBLANES
    scratch = [pltpu.VMEM((n_pairs, HEAD_DIM, PAIR_W), F32)]
    scratch += [act_b] * 7
    scratch += [pltpu.VMEM((gam_rows, RWKV_W), F32)] + [act] * 3
    scratch += [
        pltpu.VMEM((n_slots, PAIR_W, PAIR_W), BF16),
        pltpu.VMEM((n_slots, RWKV_CHUNK, PAIR_W), F32),
        pltpu.VMEM((n_slots, RWKV_CHUNK, PAIR_W), BF16),
        pltpu.VMEM((n_slots, RWKV_CHUNK, PAIR_W), F32),
    ]
    return pl.pallas_call(
        _rwkv_kernel,
        grid=(b, s // tblk),
        in_specs=[
            pl.BlockSpec((1, tblk, RWKV_SHIFT_W), lambda i, t: (i, t, 0)),
            _const_spec(vecs.shape),
            _const_spec(w_up_p.shape),
            _const_spec(a_up_p.shape),
            _const_spec(g_up.shape),
            _const_spec(tri.shape),
            _const_spec((SLAB_W, SLAB_W)),
        ],
        out_specs=pl.BlockSpec((1, tblk, RWKV_W), lambda i, t: (i, t, 0)),
        out_shape=jax.ShapeDtypeStruct((b, s, RWKV_W), F32),
        scratch_shapes=scratch,
        compiler_params=_cparams(2, 48),
        name="rwkv",
    )(ps, vecs, w_up_p, a_up_p, g_up, tri, ones_bd)


def _mixout_kernel(x_ref, ya_ref, qm_ref, k_ref, v_ref, w_ref, o_ref):
    if len(ya_ref.shape) == 4:
        ya = jnp.concatenate([ya_ref[0, j] for j in range(ya_ref.shape[1])], axis=1)
    else:
        ya = ya_ref[...]
    wa = ya.shape[-1]
    qm = qm_ref[...]
    kmem = k_ref[0]
    vmem = v_ref[0]
    head = lax.broadcasted_iota(jnp.int32, qm.shape, 1) // HEAD_DIM
    heads = range(MEM_HEADS)
    logits = [lax.dot_general(jnp.where(head == h, qm, 0.0).astype(BF16), kmem, _NT,
                              preferred_element_type=F32) for h in heads]
    p = [jnp.exp(lg - jnp.max(lg, axis=-1, keepdims=True)) for lg in logits]
    inv_l = [1.0 / jnp.sum(ph, axis=-1, keepdims=True) for ph in p]
    pv = [_dot(ph.astype(BF16), vmem) for ph in p]
    y_mem = pv[0] * inv_l[0]
    for h in heads[1:]:
        y_mem = jnp.where(head == h, pv[h] * inv_l[h], y_mem)
    y = _dot(ya.astype(BF16), w_ref[:wa, :]) + _dot(y_mem.astype(BF16), w_ref[wa:, :])
    o_ref[...] = x_ref[...] + y


def _mixout(x, ya, qm, k_mem, v_mem, w_out, seq_len, tm=512):
    n, d = x.shape
    tiles_per_seq = seq_len // tm
    m = k_mem.shape[1]
    if ya.ndim == 4:
        ya_spec = pl.BlockSpec((1, ya.shape[1], tm, V7X_LANES),
                               lambda i: (i // tiles_per_seq, 0, i % tiles_per_seq, 0))
    else:
        ya_spec = pl.BlockSpec((tm, ya.shape[-1]), lambda i: (i, 0))
    return pl.pallas_call(
        _mixout_kernel,
        grid=(n // tm,),
        in_specs=[
            pl.BlockSpec((tm, d), lambda i: (i, 0)),
            ya_spec,
            pl.BlockSpec((tm, MEM_W), lambda i: (i, 0)),
            pl.BlockSpec((1, m, MEM_W), lambda i: (i // tiles_per_seq, 0, 0)),
            pl.BlockSpec((1, m, MEM_W), lambda i: (i // tiles_per_seq, 0, 0)),
            _const_spec(w_out.shape),
        ],
        out_specs=pl.BlockSpec((tm, d), lambda i: (i, 0)),
        out_shape=jax.ShapeDtypeStruct((n, d), F32),
        compiler_params=_cparams(1, 40),
        name="mixout",
    )(x, ya, qm, k_mem, v_mem, w_out)


def _t5_bucket(dist):
    max_exact = NUM_BUCKETS // 2
    d_f = jnp.maximum(dist, 1).astype(F32)
    large = max_exact + (jnp.log(d_f / max_exact) / math.log(MAX_DISTANCE / max_exact)
                         * (NUM_BUCKETS - max_exact)).astype(jnp.int32)
    large = jnp.minimum(large, NUM_BUCKETS - 1)
    return jnp.where(dist < max_exact, dist, large)


def _band_buckets():
    run = BLOCK // DIL_ROW_PHASES
    out = []
    for window, dil in DIL_GROUPS:
        rows_q = np.arange(BLOCK)
        rows_k = np.arange(2 * BLOCK)
        if dil < DIL_ROW_PHASES:
            u_q = DIL_ROW_PHASES * (rows_q % run) + rows_q // run
            u_k = ((rows_k % (2 * run)) // run) * BLOCK + DIL_ROW_PHASES * (rows_k % run) + rows_k // (2 * run)
        else:
            u_q, u_k = rows_q, rows_k
        dsub = jnp.asarray(BLOCK + u_q[:, None] - u_k[None, :])
        band = (dsub >= 0) & (dsub <= window // dil)
        out.append(jnp.where(band, _t5_bucket(jnp.maximum(dsub, 0) * dil), -1))
    return jnp.stack(out).astype(jnp.int32)


def _bias_kernel(tab_ref, idx_ref, o_ref):
    head = pl.program_id(0)
    idx = idx_ref[0]
    acc = jnp.full(idx.shape, NEG_INF, F32)
    for bucket in range(NUM_BUCKETS):
        acc = jnp.where(idx == bucket, tab_ref[bucket, head], acc)
    o_ref[0] = acc


def _band_bias(rel_bias):
    n_heads = rel_bias.shape[1]
    return pl.pallas_call(
        _bias_kernel,
        grid=(n_heads,),
        in_specs=[
            pl.BlockSpec(memory_space=pltpu.SMEM),
            pl.BlockSpec((1, BLOCK, 2 * BLOCK), lambda h: (h // DIL_GROUP_HEADS, 0, 0)),
        ],
        out_specs=pl.BlockSpec((1, BLOCK, 2 * BLOCK), lambda h: (h, 0, 0)),
        out_shape=jax.ShapeDtypeStruct((n_heads, BLOCK, 2 * BLOCK), F32),
        compiler_params=_cparams(1, 16),
        name="band_bias",
    )(rel_bias, _band_buckets())


def _dil_group(dil, q_ref, k_ref, v_ref, bias_ref, acc_ref, m_ref, l_ref):
    seq_len = q_ref.shape[2]
    phase_rows = seq_len // DIL_ROW_PHASES
    run = BLOCK // DIL_ROW_PHASES
    head0 = lax.broadcasted_iota(jnp.int32, (BLOCK, PAIR_W), 1) < HEAD_DIM
    kcol = lax.broadcasted_iota(jnp.int32, (BLOCK, 2 * BLOCK), 1)
    in_prev = (kcol % (2 * run) < run) if dil < DIL_ROW_PHASES else (kcol < BLOCK)

    def windows(t):
        if dil < DIL_ROW_PHASES:
            blk = t
            back = jnp.maximum(blk - 1, 0)
            own = [pl.ds(pl.multiple_of(c * phase_rows + run * blk, run), run) for c in range(DIL_ROW_PHASES)]
            prev = [pl.ds(pl.multiple_of(c * phase_rows + run * back, run), run) for c in range(DIL_ROW_PHASES)]
            return own, prev, blk == 0
        step = dil // DIL_ROW_PHASES
        n_blocks = phase_rows // (step * BLOCK)
        sub = t >> (n_blocks.bit_length() - 1)
        blk = t & (n_blocks - 1)
        base = (sub & (DIL_ROW_PHASES - 1)) * phase_rows + (sub >> (DIL_ROW_PHASES.bit_length() - 1))
        back = jnp.maximum(blk - 1, 0)
        if step == 1:
            own = [pl.ds(pl.multiple_of(base + BLOCK * blk, BLOCK), BLOCK)]
            prev = [pl.ds(pl.multiple_of(base + BLOCK * back, BLOCK), BLOCK)]
        else:
            own = [pl.ds(base + step * BLOCK * blk, BLOCK, stride=step)]
            prev = [pl.ds(base + step * BLOCK * back, BLOCK, stride=step)]
        return own, prev, blk == 0

    def gather(get, wins):
        parts = [get(w) for w in wins]
        return parts[0] if len(parts) == 1 else jnp.concatenate(parts, axis=0)

    def scatter(put, wins, value):
        rows = BLOCK // len(wins)
        for i, w in enumerate(wins):
            put(w, value[i * rows:(i + 1) * rows])

    def tiles(i, carry):
        slabs = []
        for j in range(DIL_TILES_PER_ITER):
            own, prev, first = windows(i * DIL_TILES_PER_ITER + j)
            key_wins = [w for pw, ow in zip(prev, own) for w in (pw, ow)]
            no_prev = jnp.logical_and(first, in_prev)
            slabs += [(pair, own, key_wins, no_prev) for pair in range(DIL_OUT_W // PAIR_W)]
        q = [gather(lambda w: q_ref[0, pair, w, :], own) for pair, own, _, _ in slabs]
        keys = [gather(lambda w: k_ref[0, pair, w, :], kw).astype(BF16) for pair, _, kw, _ in slabs]
        vals = [gather(lambda w: v_ref[0, pair, w, :], kw).astype(BF16) for pair, _, kw, _ in slabs]
        m_old = [gather(lambda w: m_ref[pair, w, :], own) for pair, own, _, _ in slabs]
        l_old = [gather(lambda w: l_ref[pair, w, :], own) for pair, own, _, _ in slabs]
        acc_old = [gather(lambda w: acc_ref[pair, w, :], own) for pair, own, _, _ in slabs]
        heads = [(s, h) for s in range(len(slabs)) for h in range(2)]
        logits = [lax.dot_general(jnp.where(head0 if h == 0 else ~head0, q[s], 0.0).astype(BF16),
                                  keys[s], _NT, preferred_element_type=F32) for s, h in heads]
        logits = [jnp.where(slabs[s][3], NEG_INF, lg + bias_ref[2 * slabs[s][0] + h])
                  for lg, (s, h) in zip(logits, heads)]
        mx = [jnp.max(lg, axis=-1, keepdims=True) for lg in logits]
        p = [jnp.exp(lg - m) for lg, m in zip(logits, mx)]
        ls = [jnp.sum(ph, axis=-1, keepdims=True) for ph in p]
        pv = [_dot(ph.astype(BF16), vals[s]) for ph, (s, h) in zip(p, heads)]
        for s, (pair, own, _, _) in enumerate(slabs):
            m_t = jnp.where(head0, mx[2 * s], mx[2 * s + 1])
            l_t = jnp.where(head0, ls[2 * s], ls[2 * s + 1])
            acc_t = jnp.where(head0, pv[2 * s], pv[2 * s + 1])
            m_new = jnp.maximum(m_old[s], m_t)
            e_old = jnp.exp(m_old[s] - m_new)
            e_t = jnp.exp(m_t - m_new)

            def put(ref, value, pair=pair, own=own):
                def store(w, rows):
                    ref[pair, w, :] = rows
                scatter(store, own, value)

            put(m_ref, m_new)
            put(l_ref, l_old[s] * e_old + l_t * e_t)
            put(acc_ref, acc_old[s] * e_old + acc_t * e_t)
        return carry

    lax.fori_loop(0, seq_len // BLOCK // DIL_TILES_PER_ITER, tiles, 0)


def _dil_kernel(q_ref, k_ref, v_ref, bias_ref, o_ref, acc_ref, m_ref, l_ref):
    g = pl.program_id(1)

    @pl.when(g == 0)
    def _():
        m_ref[...] = jnp.full(m_ref.shape, NEG_INF, F32)
        l_ref[...] = jnp.zeros_like(l_ref)
        acc_ref[...] = jnp.zeros_like(acc_ref)

    for gi, (_, dil) in enumerate(DIL_GROUPS):
        pl.when(g == gi)(functools.partial(_dil_group, dil, q_ref, k_ref, v_ref, bias_ref, acc_ref, m_ref, l_ref))

    @pl.when(g == len(DIL_GROUPS) - 1)
    def _():
        phase_rows = o_ref.shape[2] // DIL_ROW_PHASES
        for pair in range(DIL_OUT_W // PAIR_W):
            for c in range(DIL_ROW_PHASES):
                rows = slice(c * phase_rows, (c + 1) * phase_rows)
                o_ref[0, pair, pl.ds(c, phase_rows, stride=DIL_ROW_PHASES), :] = (
                    acc_ref[pair, rows, :] / l_ref[pair, rows, :])


def _dilated_attention(q_slabs, kv_slabs, bias):
    b, _, s, _ = q_slabs.shape
    n_groups = len(DIL_GROUPS)
    pairs = DIL_OUT_W // PAIR_W
    blk = (1, pairs, s, PAIR_W)
    return pl.pallas_call(
        _dil_kernel,
        grid=(b, n_groups),
        in_specs=[
            pl.BlockSpec(blk, lambda i, g: (i, g, 0, 0)),
            pl.BlockSpec(blk, lambda i, g: (i, g, 0, 0)),
            pl.BlockSpec(blk, lambda i, g: (i, n_groups + g, 0, 0)),
            pl.BlockSpec((DIL_GROUP_HEADS, BLOCK, 2 * BLOCK), lambda i, g: (g, 0, 0)),
        ],
        out_specs=pl.BlockSpec(blk, lambda i, g: (i, 0, 0, 0)),
        out_shape=jax.ShapeDtypeStruct((b, pairs, s, PAIR_W), F32),
        scratch_shapes=[pltpu.VMEM((pairs, s, PAIR_W), F32)] * 3,
        compiler_params=_cparams(2, 56),
        name="dilated_attention",
    )(q_slabs, kv_slabs, kv_slabs, bias)


def kernel(x, mem, ffn_pre_norm, ffn_pre_w_in, ffn_pre_w_out, mix_norm, ffn_post_norm, ffn_post_w_in, ffn_post_w_out, mem_norm, mem_w_kv, mem_q_norm, mem_k_norm, a_w_in, a_shift_mu, a_w0, a_w_up, a_a0, a_a_up, a_g_up, a_kk_scale, a_k_a, a_r_k, a_lnx_g, a_lnx_b, a_w_out, b_w_q, b_q_norm, b_w_out, kv_norm, kv_w, kv_k_norm, rel_bias):
    b, s, d = x.shape
    depth = ffn_pre_w_in.shape[0]
    n_a = a_w_in.shape[0]
    n = b * s
    scale = 1.0 / math.sqrt(HEAD_DIM)
    ones_bd = _block_ones(SLAB_W, HEAD_DIM)
    row = lambda p: p.reshape(1, -1)

    k_mem, v_mem = _memkv(mem, mem_norm[:, None, :], mem_w_kv.astype(BF16),
                          jnp.tile(mem_k_norm, (1, MEM_HEADS))[:, None, :], ones_bd)
    mem_q_gain = jnp.tile(mem_q_norm, (1, MEM_HEADS)) * scale

    xf = x.reshape(n, d)
    kv = None
    bias = None
    for layer in range(depth):
        xf = _ffn(xf, row(ffn_pre_norm[layer]), ffn_pre_w_in[layer].astype(BF16),
                  ffn_pre_w_out[layer].astype(BF16))
        if layer < n_a:
            i = layer
            ps, qm = _aproj(xf, row(mix_norm[layer]), a_w_in[i].astype(BF16), row(a_shift_mu[i]),
                            row(mem_q_gain[layer]), ones_bd, s)
            zeros = jnp.zeros((DECAY_LORA, RWKV_W), F32)
            vecs = jnp.stack([a_w0[i], a_a0[i], a_kk_scale[i], a_k_a[i], a_r_k[i].reshape(-1),
                              a_lnx_g[i], a_lnx_b[i], jnp.zeros((RWKV_W,), F32)])
            y_main = _rwkv(ps.reshape(b, s, RWKV_SHIFT_W), vecs,
                           jnp.concatenate([a_w_up[i], zeros]).astype(BF16),
                           jnp.concatenate([zeros, a_a_up[i]]).astype(BF16),
                           a_g_up[i].astype(BF16), ones_bd)
            xf = _mixout(xf, y_main.reshape(n, RWKV_W), qm, k_mem[layer], v_mem[layer],
                         a_w_out[i].astype(BF16), s)
        else:
            j = layer - n_a
            q_gain = jnp.concatenate([jnp.tile(b_q_norm[j], DIL_W // HEAD_DIM) * scale, mem_q_gain[layer]])
            q_dil, qm = _qproj(xf, row(mix_norm[layer]), b_w_q[j].astype(BF16), row(q_gain), ones_bd, s, DIL_W)
            if bias is None:
                bias = _band_bias(rel_bias)
            y_dil = _dilated_attention(q_dil, kv, bias)
            xf = _mixout(xf, y_dil, qm, k_mem[layer], v_mem[layer],
                         b_w_out[j].astype(BF16), s)
        xf = _ffn(xf, row(ffn_post_norm[layer]), ffn_post_w_in[layer].astype(BF16),
                  ffn_post_w_out[layer].astype(BF16))
        if layer == n_a - 1:
            k_gain = jnp.tile(kv_k_norm, DIL_W // HEAD_DIM)
            kv, = _qproj(xf, row(kv_norm), kv_w.astype(BF16), row(k_gain), ones_bd, s, 2 * DIL_W)
    return xf.reshape(b, s, d)
```

```python
import functools
import math

import jax
import jax.numpy as jnp
import numpy as np
from jax import lax
from jax.experimental import pallas as pl
from jax.experimental.pallas import tpu as pltpu

F32 = jnp.float32
BF16 = jnp.bfloat16

HEAD_DIM = 64
MEM_HEADS = 4
MEM_W = MEM_HEADS * HEAD_DIM
RWKV_HEADS = 12
RWKV_W = RWKV_HEADS * HEAD_DIM
DECAY_LORA = 64
AAA_LORA = 64
GATE_LORA = 128
RWKV_SHIFT_W = 3 * RWKV_W + DECAY_LORA + AAA_LORA + GATE_LORA
LORA_LO = 3 * RWKV_W
DIL_GROUPS = ((128, 1), (512, 4), (2048, 16))
DIL_GROUP_HEADS = 4
DIL_W = len(DIL_GROUPS) * DIL_GROUP_HEADS * HEAD_DIM
DIL_OUT_W = DIL_GROUP_HEADS * HEAD_DIM
BLOCK = 128
NUM_BUCKETS = 32
MAX_DISTANCE = 2048
NORM_EPS = 1e-6
LNX_EPS = 64e-5
NEG_INF = -1e30

V7X_LANES = 128
V7X_SUBLANES = 8
V7X_VMEM_BYTES = 64 * 1024 * 1024

RWKV_CHUNK = 64
RWKV_PHASE_A_CHUNKS = 4
RWKV_CUMSUM_ROWS = 256
DIL_TILES_PER_ITER = 2
DIL_ROW_PHASES = 4
PROJ_SUB_ROWS = 128
PAIR_W = 2 * HEAD_DIM
SLAB_W = 4 * HEAD_DIM

_NT = (((1,), (1,)), ((), ()))
_TN = (((0,), (0,)), ((), ()))


def _cparams(n_axes, vmem_mib):
    return pltpu.CompilerParams(
        dimension_semantics=("arbitrary",) * n_axes,
        vmem_limit_bytes=min(vmem_mib * 1024 * 1024, V7X_VMEM_BYTES - 4 * 1024 * 1024),
    )


def _const_spec(shape):
    zeros = (0,) * len(shape)
    return pl.BlockSpec(shape, lambda *_: zeros)


def _dot(a, b):
    return jnp.dot(a, b, preferred_element_type=F32)


def _rms(x, g, eps):
    return x * lax.rsqrt(jnp.mean(x * x, axis=-1, keepdims=True) + eps) * g


def _split2(x):
    hi = x.astype(BF16)
    lo = (x - hi.astype(F32)).astype(BF16)
    return hi, lo


def _seg_sum(x, ones_bd, pieces=2):
    outs = []
    for s in range(x.shape[-1] // SLAB_W):
        slab = x[:, s * SLAB_W:(s + 1) * SLAB_W]
        if pieces == 1:
            outs.append(_dot(slab.astype(BF16), ones_bd))
        else:
            hi, lo = _split2(slab)
            outs.append(_dot(hi, ones_bd) + _dot(lo, ones_bd))
    return outs[0] if len(outs) == 1 else jnp.concatenate(outs, axis=-1)


def _block_ones(n, blk, lower=False):
    i = np.arange(n)
    m = (i[:, None] // blk) == (i[None, :] // blk)
    if lower:
        m = m & (i[:, None] >= i[None, :])
    return jnp.asarray(m, dtype=BF16)


def _ffn_kernel(x_ref, g_ref, win_ref, wout_ref, o_ref):
    x = x_ref[...]
    d_ff = wout_ref.shape[1]
    xn = _rms(x, g_ref[...], NORM_EPS).astype(BF16)
    h = _dot(xn, win_ref[0])
    gate = h[:, :d_ff]
    up = h[:, d_ff:]
    act = (gate * jax.nn.sigmoid(gate) * up).astype(BF16)
    o_ref[...] = x + 0.5 * _dot(act, wout_ref[0])


def _ffn(x, g, w_in, w_out, layer, tm=512):
    n, d = x.shape
    return pl.pallas_call(
        _ffn_kernel,
        grid=(n // tm,),
        in_specs=[
            pl.BlockSpec((tm, d), lambda i: (i, 0)),
            _const_spec((1, d)),
            pl.BlockSpec((1,) + w_in.shape[1:], lambda i: (layer, 0, 0), pipeline_mode=pl.Buffered(1)),
            pl.BlockSpec((1,) + w_out.shape[1:], lambda i: (layer, 0, 0), pipeline_mode=pl.Buffered(1)),
        ],
        out_specs=pl.BlockSpec((tm, d), lambda i: (i, 0)),
        out_shape=jax.ShapeDtypeStruct((n, d), F32),
        compiler_params=_cparams(1, 56),
        name="ffn",
    )(x, g, w_in, w_out)


def _memkv_kernel(mem_ref, g_ref, w_ref, kg_ref, ones_ref, k_ref, v_ref):
    m = _rms(mem_ref[0], g_ref[0], NORM_EPS).astype(BF16)
    kv = _dot(m, w_ref[0])
    k = kv[:, :MEM_W]
    ms = _seg_sum(k * k, ones_ref[...]) * (1.0 / HEAD_DIM)
    k_ref[0, 0] = (k * lax.rsqrt(ms + NORM_EPS) * kg_ref[0]).astype(BF16)
    v_ref[0, 0] = kv[:, MEM_W:].astype(BF16)


def _memkv(mem, mem_norm, w_kv, k_gain, ones_bd):
    b, m, d = mem.shape
    depth = w_kv.shape[0]
    out = jax.ShapeDtypeStruct((depth, b, m, MEM_W), BF16)
    return pl.pallas_call(
        _memkv_kernel,
        grid=(depth, b),
        in_specs=[
            pl.BlockSpec((1, m, d), lambda l, i: (i, 0, 0)),
            pl.BlockSpec((1, 1, d), lambda l, i: (l, 0, 0)),
            pl.BlockSpec((1, d, 2 * MEM_W), lambda l, i: (l, 0, 0)),
            pl.BlockSpec((1, 1, MEM_W), lambda l, i: (l, 0, 0)),
            _const_spec((SLAB_W, SLAB_W)),
        ],
        out_specs=[pl.BlockSpec((1, 1, m, MEM_W), lambda l, i: (l, i, 0, 0))] * 2,
        out_shape=[out, out],
        compiler_params=_cparams(2, 32),
        name="memkv",
    )(mem, mem_norm, w_kv, k_gain, ones_bd)


def _head_norm(q, gain, ones_bd):
    ms = _seg_sum(q * q, ones_bd, pieces=1) * (1.0 / HEAD_DIM)
    return q * lax.rsqrt(ms + NORM_EPS) * gain


def _aproj_kernel(x_ref, g_ref, w_ref, mu_ref, qg_ref, ones_ref, ps_ref, qm_ref, carry_ref, *, tiles_per_seq):
    i = pl.program_id(0)
    tm = x_ref.shape[0]
    u = _rms(x_ref[...], g_ref[...], NORM_EPS).astype(BF16)

    @pl.when(i % tiles_per_seq == 0)
    def _():
        carry_ref[...] = jnp.zeros_like(carry_ref)

    starts = range(0, tm, PROJ_SUB_ROWS)
    projs = [_dot(u[r0:r0 + PROJ_SUB_ROWS], w_ref[0]) for r0 in starts]
    row = lax.broadcasted_iota(jnp.int32, (PROJ_SUB_ROWS, RWKV_SHIFT_W), 0)
    last = carry_ref[V7X_SUBLANES - 1:V7X_SUBLANES, :]
    for r0, proj in zip(starts, projs):
        p = proj[:, :RWKV_SHIFT_W]
        prev = jnp.where(row == 0, last, pltpu.roll(p, 1, 0))
        ps_ref[r0:r0 + PROJ_SUB_ROWS, :] = p + mu_ref[...] * (prev - p)
        last = p[PROJ_SUB_ROWS - 1:, :]
        qm_ref[r0:r0 + PROJ_SUB_ROWS, :] = _head_norm(proj[:, RWKV_SHIFT_W:], qg_ref[...], ones_ref[...])
    carry_ref[...] = projs[-1][PROJ_SUB_ROWS - V7X_SUBLANES:, :RWKV_SHIFT_W]


def _aproj(x, g, w_in, layer, mu, q_gain, ones_bd, seq_len, tm=512):
    n, d = x.shape
    return pl.pallas_call(
        functools.partial(_aproj_kernel, tiles_per_seq=seq_len // tm),
        grid=(n // tm,),
        in_specs=[
            pl.BlockSpec((tm, d), lambda i: (i, 0)),
            _const_spec((1, d)),
            pl.BlockSpec((1,) + w_in.shape[1:], lambda i: (layer, 0, 0), pipeline_mode=pl.Buffered(1)),
            _const_spec((1, RWKV_SHIFT_W)),
            _const_spec((1, MEM_W)),
            _const_spec((SLAB_W, SLAB_W)),
        ],
        out_specs=[pl.BlockSpec((tm, RWKV_SHIFT_W), lambda i: (i, 0)),
                   pl.BlockSpec((tm, MEM_W), lambda i: (i, 0))],
        out_shape=[jax.ShapeDtypeStruct((n, RWKV_SHIFT_W), F32),
                   jax.ShapeDtypeStruct((n, MEM_W), F32)],
        scratch_shapes=[pltpu.VMEM((V7X_SUBLANES, RWKV_SHIFT_W), F32)],
        compiler_params=_cparams(1, 48),
        name="aproj",
    )(x, g, w_in, mu, q_gain, ones_bd)


def _qproj_kernel(x_ref, g_ref, w_ref, qg_ref, ones_ref, slab_ref, *rest, normed_w):
    stage_ref = rest[-1]
    flat_refs = rest[:-1]
    tm = x_ref.shape[0]
    u = _rms(x_ref[...], g_ref[...], NORM_EPS).astype(BF16)
    proj = _dot(u, w_ref[...])
    normed = _head_norm(proj[:, :normed_w], qg_ref[...], ones_ref[...])
    n_slabs = slab_ref.shape[1]
    for j in range(n_slabs):
        src = normed if (j + 1) * V7X_LANES <= normed_w else proj
        stage_ref[j] = src[:, j * V7X_LANES:(j + 1) * V7X_LANES]
    for j in range(n_slabs):
        for c in range(DIL_ROW_PHASES):
            slab_ref[0, j, c] = stage_ref[j, pl.ds(c, tm // DIL_ROW_PHASES, stride=DIL_ROW_PHASES), :]
    if flat_refs:
        flat_refs[0][...] = normed[:, n_slabs * V7X_LANES:]


def _qproj(x, g, w, q_gain, ones_bd, seq_len, slab_w, tm=512):
    n, d = x.shape
    wo = w.shape[1]
    normed_w = q_gain.shape[-1]
    n_slabs = slab_w // V7X_LANES
    tiles_per_seq = seq_len // tm
    phase_rows = seq_len // DIL_ROW_PHASES
    out_specs = [pl.BlockSpec((1, n_slabs, DIL_ROW_PHASES, tm // DIL_ROW_PHASES, V7X_LANES),
                              lambda i: (i // tiles_per_seq, 0, 0, i % tiles_per_seq, 0))]
    out_shape = [jax.ShapeDtypeStruct((n // seq_len, n_slabs, DIL_ROW_PHASES, phase_rows, V7X_LANES), F32)]
    if wo > slab_w:
        assert normed_w == wo
        out_specs.append(pl.BlockSpec((tm, wo - slab_w), lambda i: (i, 0)))
        out_shape.append(jax.ShapeDtypeStruct((n, wo - slab_w), F32))
    outs = list(pl.pallas_call(
        functools.partial(_qproj_kernel, normed_w=normed_w),
        grid=(n // tm,),
        in_specs=[
            pl.BlockSpec((tm, d), lambda i: (i, 0)),
            _const_spec((1, d)),
            pl.BlockSpec(w.shape, lambda i: (0, 0), pipeline_mode=pl.Buffered(1)),
            _const_spec((1, normed_w)),
            _const_spec((SLAB_W, SLAB_W)),
        ],
        out_specs=out_specs,
        out_shape=out_shape,
        scratch_shapes=[pltpu.VMEM((n_slabs, tm, V7X_LANES), F32)],
        compiler_params=_cparams(1, 48),
        name="qproj",
    )(x, g, w, q_gain, ones_bd))
    outs[0] = outs[0].reshape(n // seq_len, n_slabs, seq_len, V7X_LANES)
    return outs


def _embed(x):
    head0 = lax.broadcasted_iota(jnp.int32, x.shape, 1) < HEAD_DIM
    zero = jnp.zeros_like(x)
    return jnp.concatenate([jnp.where(head0, x, zero), jnp.where(head0, zero, x)], axis=0)


def _tri_inverse(nmats, tpos, spos):
    eye = (tpos == spos).astype(F32)
    same = {s: (tpos >> s) == (spos >> s) for s in (3, 4, 5)}

    def mm(lhs, rhs):
        return [_dot(a, _embed(b)) for a, b in zip(lhs, rhs)]

    def bf(xs):
        return [x.astype(BF16) for x in xs]

    n8 = [jnp.where(same[3], n, 0.0) for n in nmats]
    n8b = bf(n8)
    n2 = mm(n8b, n8b)
    n2b = bf(n2)
    n4 = mm(n2b, n2b)
    t = mm(bf([eye + a for a in n8]), bf([eye + a for a in n2]))
    t = mm(bf(t), bf([eye + a for a in n4]))
    for lo, hi in ((3, 4), (4, 5), (5, None)):
        off = ~same[lo] if hi is None else (same[hi] & ~same[lo])
        tb = bf(t)
        z = mm(bf([jnp.where(off, n, 0.0) for n in nmats]), tb)
        t = [a + d for a, d in zip(t, mm(tb, bf(z)))]
    return t


def _rwkv_kernel(ps_ref, vec_ref, wup_ref, aup_ref, gup_ref, tri_ref, ones_ref, y_ref,
                 s_ref, rt_ref, kt_ref, bt_ref, at_ref, kh_ref, bh_ref, v_ref, gam_ref, yacc_ref,
                 bonus_ref, gate_ref, p_s, qt_s, rp_s, y0_s):
    tblk = ps_ref.shape[1]
    n_pairs = RWKV_W // PAIR_W

    @pl.when(pl.program_id(1) == 0)
    def _():
        s_ref[...] = jnp.zeros_like(s_ref)

    ps = ps_ref[0]
    r = ps[:, :RWKV_W]
    k = ps[:, RWKV_W:2 * RWKV_W]
    v = ps[:, 2 * RWKV_W:3 * RWKV_W]
    lora_in = ps[:, LORA_LO:LORA_LO + PAIR_W]
    g_lo = ps[:, LORA_LO + PAIR_W:]
    w0, a0, kk_scale, k_a = vec_ref[0:1, :], vec_ref[1:2, :], vec_ref[2:3, :], vec_ref[3:4, :]
    r_k, lnx_g, lnx_b = vec_ref[4:5, :], vec_ref[5:6, :], vec_ref[6:7, :]
    ones_bd = ones_ref[...]

    nz = -(w0 + _dot(jnp.tanh(lora_in).astype(BF16), wup_ref[...]))
    softplus = jnp.maximum(nz, 0.0) + jnp.log(1.0 + jnp.exp(-jnp.abs(nz)))
    lw = -jnp.exp(-softplus - 0.5)
    a = jax.nn.sigmoid(a0 + _dot(lora_in.astype(BF16), aup_ref[...]))
    gate_ref[...] = _dot(jax.nn.sigmoid(g_lo).astype(BF16), gup_ref[...])
    kk = k * kk_scale
    kk = kk / jnp.maximum(jnp.sqrt(_seg_sum(kk * kk, ones_bd, pieces=1)), 1e-12)
    k2 = k * (1.0 + (a - 1.0) * k_a)
    kka = kk * a
    bonus_ref[...] = _seg_sum(r * k2 * r_k, ones_bd) * v

    hi, lo = _split2(lw)
    tri = tri_ref[...]
    gcum = jnp.concatenate(
        [_dot(tri, hi[s0:s0 + RWKV_CUMSUM_ROWS]) + _dot(tri, lo[s0:s0 + RWKV_CUMSUM_ROWS])
         for s0 in range(0, tblk, RWKV_CUMSUM_ROWS)], axis=0)
    n_chunks = tblk // RWKV_CHUNK
    gam_rows = [jnp.exp(gcum[(c + 1) * RWKV_CHUNK - 1:(c + 1) * RWKV_CHUNK, :]) for c in range(n_chunks)]
    gam = jnp.concatenate([jnp.broadcast_to(g, (RWKV_CHUNK, RWKV_W)) for g in gam_rows], axis=0)
    for c in range(n_chunks):
        gam_ref[c:c + 1, :] = gam_rows[c]
    e_neg = jnp.exp(-gcum)
    e_last = gam * e_neg
    rt_ref[...] = (r * jnp.exp(gcum)).astype(BF16)
    kt_ref[...] = (k2 * e_neg).astype(BF16)
    bt_ref[...] = (kka * e_neg).astype(BF16)
    at_ref[...] = (-kk * jnp.exp(gcum - lw)).astype(BF16)
    kh_ref[...] = (k2 * e_last).astype(BF16)
    bh_ref[...] = (kka * e_last).astype(BF16)
    v_ref[...] = v.astype(BF16)

    tpos = lax.broadcasted_iota(jnp.int32, (RWKV_CHUNK, PAIR_W), 0)
    spos = lax.broadcasted_iota(jnp.int32, (RWKV_CHUNK, PAIR_W), 1) & (HEAD_DIM - 1)
    strict = tpos > spos
    incl = tpos >= spos
    head0 = lax.broadcasted_iota(jnp.int32, (RWKV_CHUNK, PAIR_W), 1) < HEAD_DIM
    same_head = ((lax.broadcasted_iota(jnp.int32, (PAIR_W, PAIR_W), 0) < HEAD_DIM)
                 == (lax.broadcasted_iota(jnp.int32, (PAIR_W, PAIR_W), 1) < HEAD_DIM))

    pairs = range(n_pairs)
    lanes = [slice(p * PAIR_W, (p + 1) * PAIR_W) for p in pairs]

    def phase_a(i, carry):
        slots, at_v, rt_v, bt_v, kt_v, v_v, bh_v, kh_v = [], [], [], [], [], [], [], []
        for cc in range(RWKV_PHASE_A_CHUNKS):
            c = i * RWKV_PHASE_A_CHUNKS + cc
            rows = pl.ds(pl.multiple_of(c * RWKV_CHUNK, RWKV_CHUNK), RWKV_CHUNK)
            loaded = [ref[rows, :] for ref in (at_ref, rt_ref, bt_ref, kt_ref, v_ref, bh_ref, kh_ref)]
            for p in pairs:
                slots.append(c * n_pairs + p)
                for dst, x_c in zip((at_v, rt_v, bt_v, kt_v, v_v, bh_v, kh_v), loaded):
                    dst.append(x_c[:, lanes[p]])
        chains = range(len(slots))
        m4 = [lax.dot_general(jnp.concatenate([at_v[j], rt_v[j]], axis=0),
                              jnp.concatenate([_embed(bt_v[j]), _embed(kt_v[j])], axis=0), _NT,
                              preferred_element_type=F32) for j in chains]
        n_ab = [jnp.where(strict, m[:RWKV_CHUNK, :PAIR_W], 0.0) for m in m4]
        a_ak = [jnp.where(strict, m[:RWKV_CHUNK, PAIR_W:], 0.0).astype(BF16) for m in m4]
        a_rb = [jnp.where(incl, m[RWKV_CHUNK:, :PAIR_W], 0.0).astype(BF16) for m in m4]
        a_rk = [jnp.where(incl, m[RWKV_CHUNK:, PAIR_W:], 0.0).astype(BF16) for m in m4]
        v_m = [_embed(x) for x in v_v]
        akv = [_dot(a_ak[j], v_m[j]).astype(BF16) for j in chains]
        t_inv = [t.astype(BF16) for t in _tri_inverse(n_ab, tpos, spos)]
        wu0 = [_dot(t_inv[j], jnp.concatenate([_embed(at_v[j]), _embed(akv[j])], axis=1))
               for j in chains]
        w_b = [x[:, :PAIR_W].astype(BF16) for x in wu0]
        u0_b = [x[:, PAIR_W:].astype(BF16) for x in wu0]
        p_m = [jnp.where(same_head, lax.dot_general(bh_v[j], w_b[j], _TN, preferred_element_type=F32), 0.0)
               for j in chains]
        q_full = [lax.dot_general(jnp.concatenate([u0_b[j], v_v[j]], axis=0),
                                  jnp.concatenate([bh_v[j], kh_v[j]], axis=0), _TN,
                                  preferred_element_type=F32) for j in chains]
        q_t = [jnp.where(head0, q[:RWKV_CHUNK], q[RWKV_CHUNK:]) for q in q_full]
        r_p = [rt_v[j].astype(F32) + _dot(a_rb[j], _embed(w_b[j])) for j in chains]
        y_0 = [_dot(jnp.concatenate([a_rb[j], a_rk[j]], axis=1),
                    jnp.concatenate([_embed(u0_b[j]), v_m[j]], axis=0)) for j in chains]
        for j in chains:
            p_s[slots[j]] = p_m[j].astype(BF16)
            qt_s[slots[j]] = q_t[j]
            rp_s[slots[j]] = r_p[j].astype(BF16)
            y0_s[slots[j]] = y_0[j]
        return carry

    lax.fori_loop(0, n_chunks // RWKV_PHASE_A_CHUNKS, phase_a, 0)

    states = [s_ref[p] for p in pairs]
    for c in range(n_chunks):
        r0 = c * RWKV_CHUNK
        gam_c = gam_ref[c:c + 1, :]
        slots = [c * n_pairs + p for p in pairs]
        state_b = [s.astype(BF16) for s in states]
        upd = [lax.dot_general(state_b[p], p_s[slots[p]], _NT, preferred_element_type=F32) for p in pairs]
        y_c = [lax.dot_general(rp_s[slots[p]], _embed(state_b[p]), _NT, preferred_element_type=F32)
               + y0_s[slots[p]] for p in pairs]
        states = [states[p] * gam_c[:, lanes[p]] + upd[p] + qt_s[slots[p]] for p in pairs]
        yacc_ref[r0:r0 + RWKV_CHUNK, :] = jnp.concatenate(y_c, axis=1)
    for p in pairs:
        s_ref[p] = states[p]

    y = yacc_ref[...]
    mean = _seg_sum(y, ones_bd) * (1.0 / HEAD_DIM)
    dev = y - mean
    var = _seg_sum(dev * dev, ones_bd, pieces=1) * (1.0 / HEAD_DIM)
    yn = dev * lax.rsqrt(var + LNX_EPS) * lnx_g + lnx_b
    y_ref[0] = (yn + bonus_ref[...]) * gate_ref[...]


def _rwkv(ps, vecs, w_up_p, a_up_p, g_up, ones_bd, tblk=256):
    b, s, _ = ps.shape
    tri = _block_ones(RWKV_CUMSUM_ROWS, RWKV_CHUNK, lower=True)
    act = pltpu.VMEM((tblk, RWKV_W), F32)
    act_b = pltpu.VMEM((tblk, RWKV_W), BF16)
    n_pairs = RWKV_W // PAIR_W
    n_chunks = tblk // RWKV_CHUNK
    n_slots = n_chunks * n_pairs
    gam_rows = -(-n_chunks // V7X_SUBLANES) * V7X_SUBLANES
    scratch = [pltpu.VMEM((n_pairs, HEAD_DIM, PAIR_W), F32)]
    scratch += [act_b] * 7
    scratch += [pltpu.VMEM((gam_rows, RWKV_W), F32)] + [act] * 3
    scratch += [
        pltpu.VMEM((n_slots, PAIR_W, PAIR_W), BF16),
        pltpu.VMEM((n_slots, RWKV_CHUNK, PAIR_W), F32),
        pltpu.VMEM((n_slots, RWKV_CHUNK, PAIR_W), BF16),
        pltpu.VMEM((n_slots, RWKV_CHUNK, PAIR_W), F32),
    ]
    return pl.pallas_call(
        _rwkv_kernel,
        grid=(b, s // tblk),
        in_specs=[
            pl.BlockSpec((1, tblk, RWKV_SHIFT_W), lambda i, t: (i, t, 0)),
            _const_spec(vecs.shape),
            _const_spec(w_up_p.shape),
            _const_spec(a_up_p.shape),
            _const_spec(g_up.shape),
            _const_spec(tri.shape),
            _const_spec((SLAB_W, SLAB_W)),
        ],
        out_specs=pl.BlockSpec((1, tblk, RWKV_W), lambda i, t: (i, t, 0)),
        out_shape=jax.ShapeDtypeStruct((b, s, RWKV_W), F32),
        scratch_shapes=scratch,
        compiler_params=_cparams(2, 48),
        name="rwkv",
    )(ps, vecs, w_up_p, a_up_p, g_up, tri, ones_bd)


def _mixout_kernel(x_ref, ya_ref, qm_ref, k_ref, v_ref, w_ref, o_ref):
    if len(ya_ref.shape) == 4:
        ya = jnp.concatenate([ya_ref[0, j] for j in range(ya_ref.shape[1])], axis=1)
    else:
        ya = ya_ref[...]
    wa = ya.shape[-1]
    qm = qm_ref[...]
    kmem = k_ref[0]
    vmem = v_ref[0]
    head = lax.broadcasted_iota(jnp.int32, qm.shape, 1) // HEAD_DIM
    heads = range(MEM_HEADS)
    logits = [lax.dot_general(jnp.where(head == h, qm, 0.0).astype(BF16), kmem, _NT,
                              preferred_element_type=F32) for h in heads]
    p = [jnp.exp(lg - jnp.max(lg, axis=-1, keepdims=True)) for lg in logits]
    inv_l = [1.0 / jnp.sum(ph, axis=-1, keepdims=True) for ph in p]
    pv = [_dot(ph.astype(BF16), vmem) for ph in p]
    y_mem = pv[0] * inv_l[0]
    for h in heads[1:]:
        y_mem = jnp.where(head == h, pv[h] * inv_l[h], y_mem)
    y = _dot(ya.astype(BF16), w_ref[:wa, :]) + _dot(y_mem.astype(BF16), w_ref[wa:, :])
    o_ref[...] = x_ref[...] + y


def _mixout(x, ya, qm, k_mem, v_mem, w_out, seq_len, tm=512):
    n, d = x.shape
    tiles_per_seq = seq_len // tm
    m = k_mem.shape[1]
    if ya.ndim == 4:
        ya_spec = pl.BlockSpec((1, ya.shape[1], tm, V7X_LANES),
                               lambda i: (i // tiles_per_seq, 0, i % tiles_per_seq, 0))
    else:
        ya_spec = pl.BlockSpec((tm, ya.shape[-1]), lambda i: (i, 0))
    return pl.pallas_call(
        _mixout_kernel,
        grid=(n // tm,),
        in_specs=[
            pl.BlockSpec((tm, d), lambda i: (i, 0)),
            ya_spec,
            pl.BlockSpec((tm, MEM_W), lambda i: (i, 0)),
            pl.BlockSpec((1, m, MEM_W), lambda i: (i // tiles_per_seq, 0, 0)),
            pl.BlockSpec((1, m, MEM_W), lambda i: (i // tiles_per_seq, 0, 0)),
            _const_spec(w_out.shape),
        ],
        out_specs=pl.BlockSpec((tm, d), lambda i: (i, 0)),
        out_shape=jax.ShapeDtypeStruct((n, d), F32),
        compiler_params=_cparams(1, 40),
        name="mixout",
    )(x, ya, qm, k_mem, v_mem, w_out)


def _t5_bucket(dist):
    max_exact = NUM_BUCKETS // 2
    d_f = jnp.maximum(dist, 1).astype(F32)
    large = max_exact + (jnp.log(d_f / max_exact) / math.log(MAX_DISTANCE / max_exact)
                         * (NUM_BUCKETS - max_exact)).astype(jnp.int32)
    large = jnp.minimum(large, NUM_BUCKETS - 1)
    return jnp.where(dist < max_exact, dist, large)


def _band_buckets():
    run = BLOCK // DIL_ROW_PHASES
    out = []
    for window, dil in DIL_GROUPS:
        rows_q = np.arange(BLOCK)
        rows_k = np.arange(2 * BLOCK)
        if dil < DIL_ROW_PHASES:
            u_q = DIL_ROW_PHASES * (rows_q % run) + rows_q // run
            u_k = ((rows_k % (2 * run)) // run) * BLOCK + DIL_ROW_PHASES * (rows_k % run) + rows_k // (2 * run)
        else:
            u_q, u_k = rows_q, rows_k
        dsub = jnp.asarray(BLOCK + u_q[:, None] - u_k[None, :])
        band = (dsub >= 0) & (dsub <= window // dil)
        out.append(jnp.where(band, _t5_bucket(jnp.maximum(dsub, 0) * dil), -1))
    return jnp.stack(out).astype(jnp.int32)


def _bias_kernel(tab_ref, idx_ref, o_ref):
    head = pl.program_id(0)
    idx = idx_ref[0]
    acc = jnp.full(idx.shape, NEG_INF, F32)
    for bucket in range(NUM_BUCKETS):
        acc = jnp.where(idx == bucket, tab_ref[bucket, head], acc)
    o_ref[0] = acc


def _band_bias(rel_bias):
    n_heads = rel_bias.shape[1]
    return pl.pallas_call(
        _bias_kernel,
        grid=(n_heads,),
        in_specs=[
            pl.BlockSpec(memory_space=pltpu.SMEM),
            pl.BlockSpec((1, BLOCK, 2 * BLOCK), lambda h: (h // DIL_GROUP_HEADS, 0, 0)),
        ],
        out_specs=pl.BlockSpec((1, BLOCK, 2 * BLOCK), lambda h: (h, 0, 0)),
        out_shape=jax.ShapeDtypeStruct((n_heads, BLOCK, 2 * BLOCK), F32),
        compiler_params=_cparams(1, 16),
        name="band_bias",
    )(rel_bias, _band_buckets())


def _dil_group(dil, q_ref, k_ref, v_ref, bias_ref, acc_ref, m_ref, l_ref):
    seq_len = q_ref.shape[2]
    phase_rows = seq_len // DIL_ROW_PHASES
    run = BLOCK // DIL_ROW_PHASES
    head0 = lax.broadcasted_iota(jnp.int32, (BLOCK, PAIR_W), 1) < HEAD_DIM
    kcol = lax.broadcasted_iota(jnp.int32, (BLOCK, 2 * BLOCK), 1)
    in_prev = (kcol % (2 * run) < run) if dil < DIL_ROW_PHASES else (kcol < BLOCK)

    def windows(t):
        if dil < DIL_ROW_PHASES:
            blk = t
            back = jnp.maximum(blk - 1, 0)
            own = [pl.ds(pl.multiple_of(c * phase_rows + run * blk, run), run) for c in range(DIL_ROW_PHASES)]
            prev = [pl.ds(pl.multiple_of(c * phase_rows + run * back, run), run) for c in range(DIL_ROW_PHASES)]
            return own, prev, blk == 0
        step = dil // DIL_ROW_PHASES
        n_blocks = phase_rows // (step * BLOCK)
        sub = t >> (n_blocks.bit_length() - 1)
        blk = t & (n_blocks - 1)
        base = (sub & (DIL_ROW_PHASES - 1)) * phase_rows + (sub >> (DIL_ROW_PHASES.bit_length() - 1))
        back = jnp.maximum(blk - 1, 0)
        if step == 1:
            own = [pl.ds(pl.multiple_of(base + BLOCK * blk, BLOCK), BLOCK)]
            prev = [pl.ds(pl.multiple_of(base + BLOCK * back, BLOCK), BLOCK)]
        else:
            own = [pl.ds(base + step * BLOCK * blk, BLOCK, stride=step)]
            prev = [pl.ds(base + step * BLOCK * back, BLOCK, stride=step)]
        return own, prev, blk == 0

    def gather(get, wins):
        parts = [get(w) for w in wins]
        return parts[0] if len(parts) == 1 else jnp.concatenate(parts, axis=0)

    def scatter(put, wins, value):
        rows = BLOCK // len(wins)
        for i, w in enumerate(wins):
            put(w, value[i * rows:(i + 1) * rows])

    def tiles(i, carry):
        slabs = []
        for j in range(DIL_TILES_PER_ITER):
            own, prev, first = windows(i * DIL_TILES_PER_ITER + j)
            key_wins = [w for pw, ow in zip(prev, own) for w in (pw, ow)]
            no_prev = jnp.logical_and(first, in_prev)
            slabs += [(pair, own, key_wins, no_prev) for pair in range(DIL_OUT_W // PAIR_W)]
        q = [gather(lambda w: q_ref[0, pair, w, :], own) for pair, own, _, _ in slabs]
        keys = [gather(lambda w: k_ref[0, pair, w, :], kw).astype(BF16) for pair, _, kw, _ in slabs]
        vals = [gather(lambda w: v_ref[0, pair, w, :], kw).astype(BF16) for pair, _, kw, _ in slabs]
        m_old = [gather(lambda w: m_ref[pair, w, :], own) for pair, own, _, _ in slabs]
        l_old = [gather(lambda w: l_ref[pair, w, :], own) for pair, own, _, _ in slabs]
        acc_old = [gather(lambda w: acc_ref[pair, w, :], own) for pair, own, _, _ in slabs]
        heads = [(s, h) for s in range(len(slabs)) for h in range(2)]
        logits = [lax.dot_general(jnp.where(head0 if h == 0 else ~head0, q[s], 0.0).astype(BF16),
                                  keys[s], _NT, preferred_element_type=F32) for s, h in heads]
        logits = [jnp.where(slabs[s][3], NEG_INF, lg + bias_ref[2 * slabs[s][0] + h])
                  for lg, (s, h) in zip(logits, heads)]
        mx = [jnp.max(lg, axis=-1, keepdims=True) for lg in logits]
        p = [jnp.exp(lg - m) for lg, m in zip(logits, mx)]
        ls = [jnp.sum(ph, axis=-1, keepdims=True) for ph in p]
        pv = [_dot(ph.astype(BF16), vals[s]) for ph, (s, h) in zip(p, heads)]
        for s, (pair, own, _, _) in enumerate(slabs):
            m_t = jnp.where(head0, mx[2 * s], mx[2 * s + 1])
            l_t = jnp.where(head0, ls[2 * s], ls[2 * s + 1])
            acc_t = jnp.where(head0, pv[2 * s], pv[2 * s + 1])
            m_new = jnp.maximum(m_old[s], m_t)
            e_old = jnp.exp(m_old[s] - m_new)
            e_t = jnp.exp(m_t - m_new)

            def put(ref, value, pair=pair, own=own):
                def store(w, rows):
                    ref[pair, w, :] = rows
                scatter(store, own, value)

            put(m_ref, m_new)
            put(l_ref, l_old[s] * e_old + l_t * e_t)
            put(acc_ref, acc_old[s] * e_old + acc_t * e_t)
        return carry

    lax.fori_loop(0, seq_len // BLOCK // DIL_TILES_PER_ITER, tiles, 0)


def _dil_kernel(q_ref, k_ref, v_ref, bias_ref, o_ref, acc_ref, m_ref, l_ref):
    g = pl.program_id(1)

    @pl.when(g == 0)
    def _():
        m_ref[...] = jnp.full(m_ref.shape, NEG_INF, F32)
        l_ref[...] = jnp.zeros_like(l_ref)
        acc_ref[...] = jnp.zeros_like(acc_ref)

    for gi, (_, dil) in enumerate(DIL_GROUPS):
        pl.when(g == gi)(functools.partial(_dil_group, dil, q_ref, k_ref, v_ref, bias_ref, acc_ref, m_ref, l_ref))

    @pl.when(g == len(DIL_GROUPS) - 1)
    def _():
        phase_rows = o_ref.shape[2] // DIL_ROW_PHASES
        for pair in range(DIL_OUT_W // PAIR_W):
            for c in range(DIL_ROW_PHASES):
                rows = slice(c * phase_rows, (c + 1) * phase_rows)
                o_ref[0, pair, pl.ds(c, phase_rows, stride=DIL_ROW_PHASES), :] = (
                    acc_ref[pair, rows, :] / l_ref[pair, rows, :])


def _dilated_attention(q_slabs, kv_slabs, bias):
    b, _, s, _ = q_slabs.shape
    n_groups = len(DIL_GROUPS)
    pairs = DIL_OUT_W // PAIR_W
    blk = (1, pairs, s, PAIR_W)
    return pl.pallas_call(
        _dil_kernel,
        grid=(b, n_groups),
        in_specs=[
            pl.BlockSpec(blk, lambda i, g: (i, g, 0, 0)),
            pl.BlockSpec(blk, lambda i, g: (i, g, 0, 0)),
            pl.BlockSpec(blk, lambda i, g: (i, n_groups + g, 0, 0)),
            pl.BlockSpec((DIL_GROUP_HEADS, BLOCK, 2 * BLOCK), lambda i, g: (g, 0, 0)),
        ],
        out_specs=pl.BlockSpec(blk, lambda i, g: (i, 0, 0, 0)),
        out_shape=jax.ShapeDtypeStruct((b, pairs, s, PAIR_W), F32),
        scratch_shapes=[pltpu.VMEM((pairs, s, PAIR_W), F32)] * 3,
        compiler_params=_cparams(2, 56),
        name="dilated_attention",
    )(q_slabs, kv_slabs, kv_slabs, bias)


def kernel(x, mem, ffn_pre_norm, ffn_pre_w_in, ffn_pre_w_out, mix_norm, ffn_post_norm, ffn_post_w_in, ffn_post_w_out, mem_norm, mem_w_kv, mem_q_norm, mem_k_norm, a_w_in, a_shift_mu, a_w0, a_w_up, a_a0, a_a_up, a_g_up, a_kk_scale, a_k_a, a_r_k, a_lnx_g, a_lnx_b, a_w_out, b_w_q, b_q_norm, b_w_out, kv_norm, kv_w, kv_k_norm, rel_bias):
    b, s, d = x.shape
    depth = ffn_pre_w_in.shape[0]
    n_a = a_w_in.shape[0]
    n = b * s
    scale = 1.0 / math.sqrt(HEAD_DIM)
    ones_bd = _block_ones(SLAB_W, HEAD_DIM)
    row = lambda p: p.reshape(1, -1)

    k_mem, v_mem = _memkv(mem, mem_norm[:, None, :], mem_w_kv.astype(BF16),
                          jnp.tile(mem_k_norm, (1, MEM_HEADS))[:, None, :], ones_bd)
    mem_q_gain = jnp.tile(mem_q_norm, (1, MEM_HEADS)) * scale

    pre_w_in, pre_w_out = ffn_pre_w_in.astype(BF16), ffn_pre_w_out.astype(BF16)
    post_w_in, post_w_out = ffn_post_w_in.astype(BF16), ffn_post_w_out.astype(BF16)
    a_w_in_b = a_w_in.astype(BF16)
    xf = x.reshape(n, d)
    kv = None
    bias = None
    for layer in range(depth):
        xf = _ffn(xf, row(ffn_pre_norm[layer]), pre_w_in, pre_w_out, layer)
        if layer < n_a:
            i = layer
            ps, qm = _aproj(xf, row(mix_norm[layer]), a_w_in_b, i, row(a_shift_mu[i]),
                            row(mem_q_gain[layer]), ones_bd, s)
            zeros = jnp.zeros((DECAY_LORA, RWKV_W), F32)
            vecs = jnp.stack([a_w0[i], a_a0[i], a_kk_scale[i], a_k_a[i], a_r_k[i].reshape(-1),
                              a_lnx_g[i], a_lnx_b[i], jnp.zeros((RWKV_W,), F32)])
            y_main = _rwkv(ps.reshape(b, s, RWKV_SHIFT_W), vecs,
                           jnp.concatenate([a_w_up[i], zeros]).astype(BF16),
                           jnp.concatenate([zeros, a_a_up[i]]).astype(BF16),
                           a_g_up[i].astype(BF16), ones_bd)
            xf = _mixout(xf, y_main.reshape(n, RWKV_W), qm, k_mem[layer], v_mem[layer],
                         a_w_out[i].astype(BF16), s)
        else:
            j = layer - n_a
            q_gain = jnp.concatenate([jnp.tile(b_q_norm[j], DIL_W // HEAD_DIM) * scale, mem_q_gain[layer]])
            q_dil, qm = _qproj(xf, row(mix_norm[layer]), b_w_q[j].astype(BF16), row(q_gain), ones_bd, s, DIL_W)
            if bias is None:
                bias = _band_bias(rel_bias)
            y_dil = _dilated_attention(q_dil, kv, bias)
            xf = _mixout(xf, y_dil, qm, k_mem[layer], v_mem[layer],
                         b_w_out[j].astype(BF16), s)
        xf = _ffn(xf, row(ffn_post_norm[layer]), post_w_in, post_w_out, layer)
        if layer == n_a - 1:
            k_gain = jnp.tile(kv_k_norm, DIL_W // HEAD_DIM)
            kv, = _qproj(xf, row(kv_norm), kv_w.astype(BF16), row(k_gain), ones_bd, s, 2 * DIL_W)
    return xf.reshape(b, s, d)
```

```python
import functools
import math

import jax
import jax.numpy as jnp
import numpy as np
from jax import lax
from jax.experimental import pallas as pl
from jax.experimental.pallas import tpu as pltpu

F32 = jnp.float32
BF16 = jnp.bfloat16

HEAD_DIM = 64
MEM_HEADS = 4
MEM_W = MEM_HEADS * HEAD_DIM
RWKV_HEADS = 12
RWKV_W = RWKV_HEADS * HEAD_DIM
DECAY_LORA = 64
AAA_LORA = 64
GATE_LORA = 128
RWKV_SHIFT_W = 3 * RWKV_W + DECAY_LORA + AAA_LORA + GATE_LORA
LORA_LO = 3 * RWKV_W
DIL_GROUPS = ((128, 1), (512, 4), (2048, 16))
DIL_GROUP_HEADS = 4
DIL_W = len(DIL_GROUPS) * DIL_GROUP_HEADS * HEAD_DIM
DIL_OUT_W = DIL_GROUP_HEADS * HEAD_DIM
BLOCK = 128
NUM_BUCKETS = 32
MAX_DISTANCE = 2048
NORM_EPS = 1e-6
LNX_EPS = 64e-5
NEG_INF = -1e30

V7X_LANES = 128
V7X_SUBLANES = 8
V7X_VMEM_BYTES = 64 * 1024 * 1024

RWKV_CHUNK = 64
RWKV_SEQS_PER_STEP = 1
RWKV_PHASE_A_CHUNKS = 4
RWKV_CUMSUM_ROWS = 256
DIL_TILES_PER_ITER = 2
DIL_ROW_PHASES = 4
PROJ_SUB_ROWS = 128
PAIR_W = 2 * HEAD_DIM
SLAB_W = 4 * HEAD_DIM

_NT = (((1,), (1,)), ((), ()))
_TN = (((0,), (0,)), ((), ()))


def _cparams(n_axes, vmem_mib):
    return pltpu.CompilerParams(
        dimension_semantics=("arbitrary",) * n_axes,
        vmem_limit_bytes=min(vmem_mib * 1024 * 1024, V7X_VMEM_BYTES - 4 * 1024 * 1024),
    )


def _const_spec(shape):
    zeros = (0,) * len(shape)
    return pl.BlockSpec(shape, lambda *_: zeros)


def _dot(a, b):
    return jnp.dot(a, b, preferred_element_type=F32)


def _rms(x, g, eps):
    return x * lax.rsqrt(jnp.mean(x * x, axis=-1, keepdims=True) + eps) * g


def _split2(x):
    hi = x.astype(BF16)
    lo = (x - hi.astype(F32)).astype(BF16)
    return hi, lo


def _seg_sum(x, ones_bd, pieces=2):
    outs = []
    for s in range(x.shape[-1] // SLAB_W):
        slab = x[:, s * SLAB_W:(s + 1) * SLAB_W]
        if pieces == 1:
            outs.append(_dot(slab.astype(BF16), ones_bd))
        else:
            hi, lo = _split2(slab)
            outs.append(_dot(hi, ones_bd) + _dot(lo, ones_bd))
    return outs[0] if len(outs) == 1 else jnp.concatenate(outs, axis=-1)


def _block_ones(n, blk, lower=False):
    i = np.arange(n)
    m = (i[:, None] // blk) == (i[None, :] // blk)
    if lower:
        m = m & (i[:, None] >= i[None, :])
    return jnp.asarray(m, dtype=BF16)


def _ffn_kernel(x_ref, g_ref, win_ref, wout_ref, o_ref):
    x = x_ref[...]
    d_ff = wout_ref.shape[1]
    xn = _rms(x, g_ref[...], NORM_EPS).astype(BF16)
    h = _dot(xn, win_ref[0])
    gate = h[:, :d_ff]
    up = h[:, d_ff:]
    act = (gate * jax.nn.sigmoid(gate) * up).astype(BF16)
    o_ref[...] = x + 0.5 * _dot(act, wout_ref[0])


def _ffn(x, g, w_in, w_out, layer, tm=512):
    n, d = x.shape
    return pl.pallas_call(
        _ffn_kernel,
        grid=(n // tm,),
        in_specs=[
            pl.BlockSpec((tm, d), lambda i: (i, 0)),
            _const_spec((1, d)),
            pl.BlockSpec((1,) + w_in.shape[1:], lambda i: (layer, 0, 0), pipeline_mode=pl.Buffered(1)),
            pl.BlockSpec((1,) + w_out.shape[1:], lambda i: (layer, 0, 0), pipeline_mode=pl.Buffered(1)),
        ],
        out_specs=pl.BlockSpec((tm, d), lambda i: (i, 0)),
        out_shape=jax.ShapeDtypeStruct((n, d), F32),
        compiler_params=_cparams(1, 56),
        name="ffn",
    )(x, g, w_in, w_out)


def _memkv_kernel(mem_ref, g_ref, w_ref, kg_ref, ones_ref, k_ref, v_ref):
    m = _rms(mem_ref[0], g_ref[0], NORM_EPS).astype(BF16)
    kv = _dot(m, w_ref[0])
    k = kv[:, :MEM_W]
    ms = _seg_sum(k * k, ones_ref[...]) * (1.0 / HEAD_DIM)
    k_ref[0, 0] = (k * lax.rsqrt(ms + NORM_EPS) * kg_ref[0]).astype(BF16)
    v_ref[0, 0] = kv[:, MEM_W:].astype(BF16)


def _memkv(mem, mem_norm, w_kv, k_gain, ones_bd):
    b, m, d = mem.shape
    depth = w_kv.shape[0]
    out = jax.ShapeDtypeStruct((depth, b, m, MEM_W), BF16)
    return pl.pallas_call(
        _memkv_kernel,
        grid=(depth, b),
        in_specs=[
            pl.BlockSpec((1, m, d), lambda l, i: (i, 0, 0)),
            pl.BlockSpec((1, 1, d), lambda l, i: (l, 0, 0)),
            pl.BlockSpec((1, d, 2 * MEM_W), lambda l, i: (l, 0, 0)),
            pl.BlockSpec((1, 1, MEM_W), lambda l, i: (l, 0, 0)),
            _const_spec((SLAB_W, SLAB_W)),
        ],
        out_specs=[pl.BlockSpec((1, 1, m, MEM_W), lambda l, i: (l, i, 0, 0))] * 2,
        out_shape=[out, out],
        compiler_params=_cparams(2, 32),
        name="memkv",
    )(mem, mem_norm, w_kv, k_gain, ones_bd)


def _head_norm(q, gain, ones_bd):
    ms = _seg_sum(q * q, ones_bd, pieces=1) * (1.0 / HEAD_DIM)
    return q * lax.rsqrt(ms + NORM_EPS) * gain


def _aproj_kernel(x_ref, g_ref, w_ref, mu_ref, qg_ref, ones_ref, ps_ref, qm_ref, carry_ref, *, tiles_per_seq):
    i = pl.program_id(0)
    tm = x_ref.shape[0]
    u = _rms(x_ref[...], g_ref[...], NORM_EPS).astype(BF16)

    @pl.when(i % tiles_per_seq == 0)
    def _():
        carry_ref[...] = jnp.zeros_like(carry_ref)

    starts = range(0, tm, PROJ_SUB_ROWS)
    projs = [_dot(u[r0:r0 + PROJ_SUB_ROWS], w_ref[0]) for r0 in starts]
    row = lax.broadcasted_iota(jnp.int32, (PROJ_SUB_ROWS, RWKV_SHIFT_W), 0)
    last = carry_ref[V7X_SUBLANES - 1:V7X_SUBLANES, :]
    for r0, proj in zip(starts, projs):
        p = proj[:, :RWKV_SHIFT_W]
        prev = jnp.where(row == 0, last, pltpu.roll(p, 1, 0))
        ps_ref[r0:r0 + PROJ_SUB_ROWS, :] = p + mu_ref[...] * (prev - p)
        last = p[PROJ_SUB_ROWS - 1:, :]
        qm_ref[r0:r0 + PROJ_SUB_ROWS, :] = _head_norm(proj[:, RWKV_SHIFT_W:], qg_ref[...],
                                                      ones_ref[...]).astype(qm_ref.dtype)
    carry_ref[...] = projs[-1][PROJ_SUB_ROWS - V7X_SUBLANES:, :RWKV_SHIFT_W]


def _aproj(x, g, w_in, layer, mu, q_gain, ones_bd, seq_len, tm=512):
    n, d = x.shape
    return pl.pallas_call(
        functools.partial(_aproj_kernel, tiles_per_seq=seq_len // tm),
        grid=(n // tm,),
        in_specs=[
            pl.BlockSpec((tm, d), lambda i: (i, 0)),
            _const_spec((1, d)),
            pl.BlockSpec((1,) + w_in.shape[1:], lambda i: (layer, 0, 0), pipeline_mode=pl.Buffered(1)),
            _const_spec((1, RWKV_SHIFT_W)),
            _const_spec((1, MEM_W)),
            _const_spec((SLAB_W, SLAB_W)),
        ],
        out_specs=[pl.BlockSpec((tm, RWKV_SHIFT_W), lambda i: (i, 0)),
                   pl.BlockSpec((tm, MEM_W), lambda i: (i, 0))],
        out_shape=[jax.ShapeDtypeStruct((n, RWKV_SHIFT_W), F32),
                   jax.ShapeDtypeStruct((n, MEM_W), BF16)],
        scratch_shapes=[pltpu.VMEM((V7X_SUBLANES, RWKV_SHIFT_W), F32)],
        compiler_params=_cparams(1, 48),
        name="aproj",
    )(x, g, w_in, mu, q_gain, ones_bd)


def _qproj_kernel(x_ref, g_ref, w_ref, qg_ref, ones_ref, slab_ref, *rest, normed_w):
    stage_ref = rest[-1]
    flat_refs = rest[:-1]
    tm = x_ref.shape[0]
    u = _rms(x_ref[...], g_ref[...], NORM_EPS).astype(BF16)
    proj = _dot(u, w_ref[...])
    normed = _head_norm(proj[:, :normed_w], qg_ref[...], ones_ref[...])
    n_slabs = slab_ref.shape[1]
    for j in range(n_slabs):
        src = normed if (j + 1) * V7X_LANES <= normed_w else proj
        stage_ref[j] = src[:, j * V7X_LANES:(j + 1) * V7X_LANES]
    for j in range(n_slabs):
        for c in range(DIL_ROW_PHASES):
            slab_ref[0, j, c] = stage_ref[j, pl.ds(c, tm // DIL_ROW_PHASES, stride=DIL_ROW_PHASES), :]
    if flat_refs:
        flat_refs[0][...] = normed[:, n_slabs * V7X_LANES:].astype(flat_refs[0].dtype)


def _qproj(x, g, w, q_gain, ones_bd, seq_len, slab_w, tm=512):
    n, d = x.shape
    wo = w.shape[1]
    normed_w = q_gain.shape[-1]
    n_slabs = slab_w // V7X_LANES
    tiles_per_seq = seq_len // tm
    phase_rows = seq_len // DIL_ROW_PHASES
    out_specs = [pl.BlockSpec((1, n_slabs, DIL_ROW_PHASES, tm // DIL_ROW_PHASES, V7X_LANES),
                              lambda i: (i // tiles_per_seq, 0, 0, i % tiles_per_seq, 0))]
    out_shape = [jax.ShapeDtypeStruct((n // seq_len, n_slabs, DIL_ROW_PHASES, phase_rows, V7X_LANES), F32)]
    if wo > slab_w:
        assert normed_w == wo
        out_specs.append(pl.BlockSpec((tm, wo - slab_w), lambda i: (i, 0)))
        out_shape.append(jax.ShapeDtypeStruct((n, wo - slab_w), BF16))
    outs = list(pl.pallas_call(
        functools.partial(_qproj_kernel, normed_w=normed_w),
        grid=(n // tm,),
        in_specs=[
            pl.BlockSpec((tm, d), lambda i: (i, 0)),
            _const_spec((1, d)),
            pl.BlockSpec(w.shape, lambda i: (0, 0), pipeline_mode=pl.Buffered(1)),
            _const_spec((1, normed_w)),
            _const_spec((SLAB_W, SLAB_W)),
        ],
        out_specs=out_specs,
        out_shape=out_shape,
        scratch_shapes=[pltpu.VMEM((n_slabs, tm, V7X_LANES), F32)],
        compiler_params=_cparams(1, 48),
        name="qproj",
    )(x, g, w, q_gain, ones_bd))
    outs[0] = outs[0].reshape(n // seq_len, n_slabs, seq_len, V7X_LANES)
    return outs


def _embed(x):
    head0 = lax.broadcasted_iota(jnp.int32, x.shape, 1) < HEAD_DIM
    zero = jnp.zeros_like(x)
    return jnp.concatenate([jnp.where(head0, x, zero), jnp.where(head0, zero, x)], axis=0)


def _tri_inverse(nmats, tpos, spos):
    eye = (tpos == spos).astype(F32)
    same = {s: (tpos >> s) == (spos >> s) for s in (3, 4, 5)}

    def mm(lhs, rhs):
        return [_dot(a, _embed(b)) for a, b in zip(lhs, rhs)]

    def bf(xs):
        return [x.astype(BF16) for x in xs]

    n8 = [jnp.where(same[3], n, 0.0) for n in nmats]
    n8b = bf(n8)
    n2 = mm(n8b, n8b)
    n2b = bf(n2)
    n4 = mm(n2b, n2b)
    t = mm(bf([eye + a for a in n8]), bf([eye + a for a in n2]))
    t = mm(bf(t), bf([eye + a for a in n4]))
    for lo, hi in ((3, 4), (4, 5), (5, None)):
        off = ~same[lo] if hi is None else (same[hi] & ~same[lo])
        tb = bf(t)
        z = mm(bf([jnp.where(off, n, 0.0) for n in nmats]), tb)
        t = [a + d for a, d in zip(t, mm(tb, bf(z)))]
    return t


def _rwkv_kernel(ps_ref, vec_ref, wup_ref, aup_ref, gup_ref, tri_ref, ones_ref, y_ref,
                 s_ref, rt_ref, kt_ref, bt_ref, at_ref, kh_ref, bh_ref, v_ref, gam_ref, yacc_ref,
                 bonus_ref, gate_ref, p_s, qt_s, rp_s, y0_s):
    n_seqs, seq_rows = ps_ref.shape[1], ps_ref.shape[2]
    tblk = n_seqs * seq_rows
    n_pairs = RWKV_W // PAIR_W

    @pl.when(pl.program_id(1) == 0)
    def _():
        s_ref[...] = jnp.zeros_like(s_ref)

    ps = jnp.concatenate([ps_ref[0, q] for q in range(n_seqs)], axis=0)
    r = ps[:, :RWKV_W]
    k = ps[:, RWKV_W:2 * RWKV_W]
    v = ps[:, 2 * RWKV_W:3 * RWKV_W]
    lora_in = ps[:, LORA_LO:LORA_LO + PAIR_W]
    g_lo = ps[:, LORA_LO + PAIR_W:]
    w0, a0, kk_scale, k_a = vec_ref[0:1, :], vec_ref[1:2, :], vec_ref[2:3, :], vec_ref[3:4, :]
    r_k, lnx_g, lnx_b = vec_ref[4:5, :], vec_ref[5:6, :], vec_ref[6:7, :]
    ones_bd = ones_ref[...]

    nz = -(w0 + _dot(jnp.tanh(lora_in).astype(BF16), wup_ref[...]))
    softplus = jnp.maximum(nz, 0.0) + jnp.log(1.0 + jnp.exp(-jnp.abs(nz)))
    lw = -jnp.exp(-softplus - 0.5)
    a = jax.nn.sigmoid(a0 + _dot(lora_in.astype(BF16), aup_ref[...]))
    gate_ref[...] = _dot(jax.nn.sigmoid(g_lo).astype(BF16), gup_ref[...])
    kk = k * kk_scale
    kk = kk / jnp.maximum(jnp.sqrt(_seg_sum(kk * kk, ones_bd, pieces=1)), 1e-12)
    k2 = k * (1.0 + (a - 1.0) * k_a)
    kka = kk * a
    bonus_ref[...] = _seg_sum(r * k2 * r_k, ones_bd) * v

    hi, lo = _split2(lw)
    tri = tri_ref[...]
    gcum = jnp.concatenate(
        [_dot(tri, hi[s0:s0 + RWKV_CUMSUM_ROWS]) + _dot(tri, lo[s0:s0 + RWKV_CUMSUM_ROWS])
         for s0 in range(0, tblk, RWKV_CUMSUM_ROWS)], axis=0)
    n_chunks = tblk // RWKV_CHUNK
    gam_rows = [jnp.exp(gcum[(c + 1) * RWKV_CHUNK - 1:(c + 1) * RWKV_CHUNK, :]) for c in range(n_chunks)]
    gam = jnp.concatenate([jnp.broadcast_to(g, (RWKV_CHUNK, RWKV_W)) for g in gam_rows], axis=0)
    for c in range(n_chunks):
        gam_ref[c:c + 1, :] = gam_rows[c]
    e_neg = jnp.exp(-gcum)
    e_last = gam * e_neg
    rt_ref[...] = (r * jnp.exp(gcum)).astype(BF16)
    kt_ref[...] = (k2 * e_neg).astype(BF16)
    bt_ref[...] = (kka * e_neg).astype(BF16)
    at_ref[...] = (-kk * jnp.exp(gcum - lw)).astype(BF16)
    kh_ref[...] = (k2 * e_last).astype(BF16)
    bh_ref[...] = (kka * e_last).astype(BF16)
    v_ref[...] = v.astype(BF16)

    tpos = lax.broadcasted_iota(jnp.int32, (RWKV_CHUNK, PAIR_W), 0)
    spos = lax.broadcasted_iota(jnp.int32, (RWKV_CHUNK, PAIR_W), 1) & (HEAD_DIM - 1)
    strict = tpos > spos
    incl = tpos >= spos
    head0 = lax.broadcasted_iota(jnp.int32, (RWKV_CHUNK, PAIR_W), 1) < HEAD_DIM
    same_head = ((lax.broadcasted_iota(jnp.int32, (PAIR_W, PAIR_W), 0) < HEAD_DIM)
                 == (lax.broadcasted_iota(jnp.int32, (PAIR_W, PAIR_W), 1) < HEAD_DIM))

    pairs = range(n_pairs)
    lanes = [slice(p * PAIR_W, (p + 1) * PAIR_W) for p in pairs]

    def phase_a(i, carry):
        slots, at_v, rt_v, bt_v, kt_v, v_v, bh_v, kh_v = [], [], [], [], [], [], [], []
        for cc in range(RWKV_PHASE_A_CHUNKS):
            c = i * RWKV_PHASE_A_CHUNKS + cc
            rows = pl.ds(pl.multiple_of(c * RWKV_CHUNK, RWKV_CHUNK), RWKV_CHUNK)
            loaded = [ref[rows, :] for ref in (at_ref, rt_ref, bt_ref, kt_ref, v_ref, bh_ref, kh_ref)]
            for p in pairs:
                slots.append(c * n_pairs + p)
                for dst, x_c in zip((at_v, rt_v, bt_v, kt_v, v_v, bh_v, kh_v), loaded):
                    dst.append(x_c[:, lanes[p]])
        chains = range(len(slots))
        m4 = [lax.dot_general(jnp.concatenate([at_v[j], rt_v[j]], axis=0),
                              jnp.concatenate([_embed(bt_v[j]), _embed(kt_v[j])], axis=0), _NT,
                              preferred_element_type=F32) for j in chains]
        n_ab = [jnp.where(strict, m[:RWKV_CHUNK, :PAIR_W], 0.0) for m in m4]
        a_ak = [jnp.where(strict, m[:RWKV_CHUNK, PAIR_W:], 0.0).astype(BF16) for m in m4]
        a_rb = [jnp.where(incl, m[RWKV_CHUNK:, :PAIR_W], 0.0).astype(BF16) for m in m4]
        a_rk = [jnp.where(incl, m[RWKV_CHUNK:, PAIR_W:], 0.0).astype(BF16) for m in m4]
        v_m = [_embed(x) for x in v_v]
        akv = [_dot(a_ak[j], v_m[j]).astype(BF16) for j in chains]
        t_inv = [t.astype(BF16) for t in _tri_inverse(n_ab, tpos, spos)]
        wu0 = [_dot(t_inv[j], jnp.concatenate([_embed(at_v[j]), _embed(akv[j])], axis=1))
               for j in chains]
        w_b = [x[:, :PAIR_W].astype(BF16) for x in wu0]
        u0_b = [x[:, PAIR_W:].astype(BF16) for x in wu0]
        p_m = [jnp.where(same_head, lax.dot_general(bh_v[j], w_b[j], _TN, preferred_element_type=F32), 0.0)
               for j in chains]
        q_full = [lax.dot_general(jnp.concatenate([u0_b[j], v_v[j]], axis=0),
                                  jnp.concatenate([bh_v[j], kh_v[j]], axis=0), _TN,
                                  preferred_element_type=F32) for j in chains]
        q_t = [jnp.where(head0, q[:RWKV_CHUNK], q[RWKV_CHUNK:]) for q in q_full]
        r_p = [rt_v[j].astype(F32) + _dot(a_rb[j], _embed(w_b[j])) for j in chains]
        y_0 = [_dot(jnp.concatenate([a_rb[j], a_rk[j]], axis=1),
                    jnp.concatenate([_embed(u0_b[j]), v_m[j]], axis=0)) for j in chains]
        for j in chains:
            p_s[slots[j]] = p_m[j].astype(BF16)
            qt_s[slots[j]] = q_t[j]
            rp_s[slots[j]] = r_p[j].astype(BF16)
            y0_s[slots[j]] = y_0[j]
        return carry

    lax.fori_loop(0, n_chunks // RWKV_PHASE_A_CHUNKS, phase_a, 0)

    seq_chunks = seq_rows // RWKV_CHUNK
    heads = [(q, p) for q in range(n_seqs) for p in pairs]
    states = [s_ref[q * n_pairs + p] for q, p in heads]
    for c in range(seq_chunks):
        chunk = [q * seq_chunks + c for q, _ in heads]
        slots = [ch * n_pairs + p for ch, (_, p) in zip(chunk, heads)]
        gam_c = [gam_ref[ch:ch + 1, lanes[p]] for ch, (_, p) in zip(chunk, heads)]
        state_b = [s.astype(BF16) for s in states]
        upd = [lax.dot_general(sb, p_s[j], _NT, preferred_element_type=F32) for sb, j in zip(state_b, slots)]
        y_c = [lax.dot_general(rp_s[j], _embed(sb), _NT, preferred_element_type=F32) + y0_s[j]
               for sb, j in zip(state_b, slots)]
        states = [s * g + u + qt_s[j] for s, g, u, j in zip(states, gam_c, upd, slots)]
        for q in range(n_seqs):
            r0 = (q * seq_chunks + c) * RWKV_CHUNK
            yacc_ref[r0:r0 + RWKV_CHUNK, :] = jnp.concatenate(y_c[q * n_pairs:(q + 1) * n_pairs], axis=1)
    for i, s in enumerate(states):
        s_ref[i] = s

    y = yacc_ref[...]
    mean = _seg_sum(y, ones_bd) * (1.0 / HEAD_DIM)
    dev = y - mean
    var = _seg_sum(dev * dev, ones_bd, pieces=1) * (1.0 / HEAD_DIM)
    yn = dev * lax.rsqrt(var + LNX_EPS) * lnx_g + lnx_b
    out = ((yn + bonus_ref[...]) * gate_ref[...]).astype(y_ref.dtype)
    for q in range(n_seqs):
        y_ref[0, q] = out[q * seq_rows:(q + 1) * seq_rows]


def _rwkv(ps, vecs, w_up_p, a_up_p, g_up, ones_bd, seq_rows=256):
    b, s, _ = ps.shape
    n_seqs = RWKV_SEQS_PER_STEP
    tblk = n_seqs * seq_rows
    tri = _block_ones(RWKV_CUMSUM_ROWS, RWKV_CHUNK, lower=True)
    act = pltpu.VMEM((tblk, RWKV_W), F32)
    act_b = pltpu.VMEM((tblk, RWKV_W), BF16)
    n_pairs = RWKV_W // PAIR_W
    n_chunks = tblk // RWKV_CHUNK
    n_slots = n_chunks * n_pairs
    gam_rows = -(-n_chunks // V7X_SUBLANES) * V7X_SUBLANES
    scratch = [pltpu.VMEM((n_seqs * n_pairs, HEAD_DIM, PAIR_W), F32)]
    scratch += [act_b] * 7
    scratch += [pltpu.VMEM((gam_rows, RWKV_W), F32)] + [act] * 3
    scratch += [
        pltpu.VMEM((n_slots, PAIR_W, PAIR_W), BF16),
        pltpu.VMEM((n_slots, RWKV_CHUNK, PAIR_W), F32),
        pltpu.VMEM((n_slots, RWKV_CHUNK, PAIR_W), BF16),
        pltpu.VMEM((n_slots, RWKV_CHUNK, PAIR_W), F32),
    ]
    y = pl.pallas_call(
        _rwkv_kernel,
        grid=(b // n_seqs, s // seq_rows),
        in_specs=[
            pl.BlockSpec((1, n_seqs, seq_rows, RWKV_SHIFT_W), lambda i, t: (i, 0, t, 0)),
            _const_spec(vecs.shape),
            _const_spec(w_up_p.shape),
            _const_spec(a_up_p.shape),
            _const_spec(g_up.shape),
            _const_spec(tri.shape),
            _const_spec((SLAB_W, SLAB_W)),
        ],
        out_specs=pl.BlockSpec((1, n_seqs, seq_rows, RWKV_W), lambda i, t: (i, 0, t, 0)),
        out_shape=jax.ShapeDtypeStruct((b // n_seqs, n_seqs, s, RWKV_W), BF16),
        scratch_shapes=scratch,
        compiler_params=_cparams(2, 48),
        name="rwkv",
    )(ps.reshape(b // n_seqs, n_seqs, s, RWKV_SHIFT_W), vecs, w_up_p, a_up_p, g_up, tri, ones_bd)
    return y.reshape(b, s, RWKV_W)


def _mixout_kernel(x_ref, ya_ref, qm_ref, k_ref, v_ref, w_ref, o_ref):
    if len(ya_ref.shape) == 4:
        ya = jnp.concatenate([ya_ref[0, j] for j in range(ya_ref.shape[1])], axis=1)
    else:
        ya = ya_ref[...]
    wa = ya.shape[-1]
    qm = qm_ref[...]
    kmem = k_ref[0]
    vmem = v_ref[0]
    head = lax.broadcasted_iota(jnp.int32, qm.shape, 1) // HEAD_DIM
    heads = range(MEM_HEADS)
    logits = [lax.dot_general(jnp.where(head == h, qm, jnp.zeros_like(qm)), kmem, _NT,
                              preferred_element_type=F32) for h in heads]
    p = [jnp.exp(lg - jnp.max(lg, axis=-1, keepdims=True)) for lg in logits]
    inv_l = [1.0 / jnp.sum(ph, axis=-1, keepdims=True) for ph in p]
    pv = [_dot(ph.astype(BF16), vmem) for ph in p]
    y_mem = pv[0] * inv_l[0]
    for h in heads[1:]:
        y_mem = jnp.where(head == h, pv[h] * inv_l[h], y_mem)
    y = _dot(ya.astype(BF16), w_ref[:wa, :]) + _dot(y_mem.astype(BF16), w_ref[wa:, :])
    o_ref[...] = x_ref[...] + y


def _mixout(x, ya, qm, k_mem, v_mem, w_out, seq_len, tm=512):
    n, d = x.shape
    tiles_per_seq = seq_len // tm
    m = k_mem.shape[1]
    if ya.ndim == 4:
        ya_spec = pl.BlockSpec((1, ya.shape[1], tm, V7X_LANES),
                               lambda i: (i // tiles_per_seq, 0, i % tiles_per_seq, 0))
    else:
        ya_spec = pl.BlockSpec((tm, ya.shape[-1]), lambda i: (i, 0))
    return pl.pallas_call(
        _mixout_kernel,
        grid=(n // tm,),
        in_specs=[
            pl.BlockSpec((tm, d), lambda i: (i, 0)),
            ya_spec,
            pl.BlockSpec((tm, MEM_W), lambda i: (i, 0)),
            pl.BlockSpec((1, m, MEM_W), lambda i: (i // tiles_per_seq, 0, 0)),
            pl.BlockSpec((1, m, MEM_W), lambda i: (i // tiles_per_seq, 0, 0)),
            _const_spec(w_out.shape),
        ],
        out_specs=pl.BlockSpec((tm, d), lambda i: (i, 0)),
        out_shape=jax.ShapeDtypeStruct((n, d), F32),
        compiler_params=_cparams(1, 40),
        name="mixout",
    )(x, ya, qm, k_mem, v_mem, w_out)


def _t5_bucket(dist):
    max_exact = NUM_BUCKETS // 2
    d_f = jnp.maximum(dist, 1).astype(F32)
    large = max_exact + (jnp.log(d_f / max_exact) / math.log(MAX_DISTANCE / max_exact)
                         * (NUM_BUCKETS - max_exact)).astype(jnp.int32)
    large = jnp.minimum(large, NUM_BUCKETS - 1)
    return jnp.where(dist < max_exact, dist, large)


def _band_buckets():
    run = BLOCK // DIL_ROW_PHASES
    out = []
    for window, dil in DIL_GROUPS:
        rows_q = np.arange(BLOCK)
        rows_k = np.arange(2 * BLOCK)
        if dil < DIL_ROW_PHASES:
            u_q = DIL_ROW_PHASES * (rows_q % run) + rows_q // run
            u_k = ((rows_k % (2 * run)) // run) * BLOCK + DIL_ROW_PHASES * (rows_k % run) + rows_k // (2 * run)
        else:
            u_q, u_k = rows_q, rows_k
        dsub = jnp.asarray(BLOCK + u_q[:, None] - u_k[None, :])
        band = (dsub >= 0) & (dsub <= window // dil)
        idx = jnp.where(band, _t5_bucket(jnp.maximum(dsub, 0) * dil), -1)
        out.append(jnp.stack([idx, jnp.where(jnp.asarray(u_k < BLOCK)[None, :], -1, idx)]))
    return jnp.stack(out, axis=1).astype(jnp.int32)


def _bias_kernel(tab_ref, idx_ref, o_ref):
    head = pl.program_id(1)
    idx = idx_ref[0, 0]
    acc = jnp.full(idx.shape, NEG_INF, F32)
    for bucket in range(NUM_BUCKETS):
        acc = jnp.where(idx == bucket, tab_ref[bucket, head], acc)
    o_ref[0, 0] = acc


def _band_bias(rel_bias):
    n_heads = rel_bias.shape[1]
    return pl.pallas_call(
        _bias_kernel,
        grid=(2, n_heads),
        in_specs=[
            pl.BlockSpec(memory_space=pltpu.SMEM),
            pl.BlockSpec((1, 1, BLOCK, 2 * BLOCK), lambda v, h: (v, h // DIL_GROUP_HEADS, 0, 0)),
        ],
        out_specs=pl.BlockSpec((1, 1, BLOCK, 2 * BLOCK), lambda v, h: (v, h, 0, 0)),
        out_shape=jax.ShapeDtypeStruct((2, n_heads, BLOCK, 2 * BLOCK), F32),
        compiler_params=_cparams(2, 16),
        name="band_bias",
    )(rel_bias, _band_buckets())


def _dil_group(dil, q_ref, k_ref, v_ref, bias_ref, acc_ref, m_ref, l_ref):
    seq_len = q_ref.shape[2]
    phase_rows = seq_len // DIL_ROW_PHASES
    run = BLOCK // DIL_ROW_PHASES
    head0 = lax.broadcasted_iota(jnp.int32, (BLOCK, PAIR_W), 1) < HEAD_DIM

    def windows(t):
        if dil < DIL_ROW_PHASES:
            blk = t
            back = jnp.maximum(blk - 1, 0)
            own = [pl.ds(pl.multiple_of(c * phase_rows + run * blk, run), run) for c in range(DIL_ROW_PHASES)]
            prev = [pl.ds(pl.multiple_of(c * phase_rows + run * back, run), run) for c in range(DIL_ROW_PHASES)]
            return own, prev, blk == 0
        step = dil // DIL_ROW_PHASES
        n_blocks = phase_rows // (step * BLOCK)
        sub = t >> (n_blocks.bit_length() - 1)
        blk = t & (n_blocks - 1)
        base = (sub & (DIL_ROW_PHASES - 1)) * phase_rows + (sub >> (DIL_ROW_PHASES.bit_length() - 1))
        back = jnp.maximum(blk - 1, 0)
        if step == 1:
            own = [pl.ds(pl.multiple_of(base + BLOCK * blk, BLOCK), BLOCK)]
            prev = [pl.ds(pl.multiple_of(base + BLOCK * back, BLOCK), BLOCK)]
        else:
            own = [pl.ds(base + step * BLOCK * blk, BLOCK, stride=step)]
            prev = [pl.ds(base + step * BLOCK * back, BLOCK, stride=step)]
        return own, prev, blk == 0

    def gather(get, wins):
        parts = [get(w) for w in wins]
        return parts[0] if len(parts) == 1 else jnp.concatenate(parts, axis=0)

    def scatter(put, wins, value):
        rows = BLOCK // len(wins)
        for i, w in enumerate(wins):
            put(w, value[i * rows:(i + 1) * rows])

    def tiles(i, carry):
        slabs = []
        for j in range(DIL_TILES_PER_ITER):
            own, prev, first = windows(i * DIL_TILES_PER_ITER + j)
            key_wins = [w for pw, ow in zip(prev, own) for w in (pw, ow)]
            variant = first.astype(jnp.int32)
            slabs += [(pair, own, key_wins, variant) for pair in range(DIL_OUT_W // PAIR_W)]
        q = [gather(lambda w: q_ref[0, pair, w, :], own) for pair, own, _, _ in slabs]
        keys = [gather(lambda w: k_ref[0, pair, w, :], kw).astype(BF16) for pair, _, kw, _ in slabs]
        vals = [gather(lambda w: v_ref[0, pair, w, :], kw).astype(BF16) for pair, _, kw, _ in slabs]
        m_old = [gather(lambda w: m_ref[pair, w, :], own) for pair, own, _, _ in slabs]
        l_old = [gather(lambda w: l_ref[pair, w, :], own) for pair, own, _, _ in slabs]
        acc_old = [gather(lambda w: acc_ref[pair, w, :], own) for pair, own, _, _ in slabs]
        heads = [(s, h) for s in range(len(slabs)) for h in range(2)]
        logits = [lax.dot_general(jnp.where(head0 if h == 0 else ~head0, q[s], 0.0).astype(BF16),
                                  keys[s], _NT, preferred_element_type=F32) for s, h in heads]
        logits = [lg + bias_ref[slabs[s][3], 2 * slabs[s][0] + h] for lg, (s, h) in zip(logits, heads)]
        mx = [jnp.max(lg, axis=-1, keepdims=True) for lg in logits]
        p = [jnp.exp(lg - m) for lg, m in zip(logits, mx)]
        ls = [jnp.sum(ph, axis=-1, keepdims=True) for ph in p]
        pv = [_dot(ph.astype(BF16), vals[s]) for ph, (s, h) in zip(p, heads)]
        for s, (pair, own, _, _) in enumerate(slabs):
            m_t = jnp.where(head0, mx[2 * s], mx[2 * s + 1])
            l_t = jnp.where(head0, ls[2 * s], ls[2 * s + 1])
            acc_t = jnp.where(head0, pv[2 * s], pv[2 * s + 1])
            m_new = jnp.maximum(m_old[s], m_t)
            e_old = jnp.exp(m_old[s] - m_new)
            e_t = jnp.exp(m_t - m_new)

            def put(ref, value, pair=pair, own=own):
                def store(w, rows):
                    ref[pair, w, :] = rows
                scatter(store, own, value)

            put(m_ref, m_new)
            put(l_ref, l_old[s] * e_old + l_t * e_t)
            put(acc_ref, acc_old[s] * e_old + acc_t * e_t)
        return carry

    lax.fori_loop(0, seq_len // BLOCK // DIL_TILES_PER_ITER, tiles, 0)


def _dil_kernel(q_ref, k_ref, v_ref, bias_ref, o_ref, acc_ref, m_ref, l_ref):
    g = pl.program_id(1)

    @pl.when(g == 0)
    def _():
        m_ref[...] = jnp.full(m_ref.shape, NEG_INF, F32)
        l_ref[...] = jnp.zeros_like(l_ref)
        acc_ref[...] = jnp.zeros_like(acc_ref)

    for gi, (_, dil) in enumerate(DIL_GROUPS):
        pl.when(g == gi)(functools.partial(_dil_group, dil, q_ref, k_ref, v_ref, bias_ref, acc_ref, m_ref, l_ref))

    @pl.when(g == len(DIL_GROUPS) - 1)
    def _():
        phase_rows = o_ref.shape[2] // DIL_ROW_PHASES
        for pair in range(DIL_OUT_W // PAIR_W):
            for c in range(DIL_ROW_PHASES):
                rows = slice(c * phase_rows, (c + 1) * phase_rows)
                o_ref[0, pair, pl.ds(c, phase_rows, stride=DIL_ROW_PHASES), :] = (
                    acc_ref[pair, rows, :] / l_ref[pair, rows, :])


def _dilated_attention(q_slabs, kv_slabs, bias):
    b, _, s, _ = q_slabs.shape
    n_groups = len(DIL_GROUPS)
    pairs = DIL_OUT_W // PAIR_W
    blk = (1, pairs, s, PAIR_W)
    return pl.pallas_call(
        _dil_kernel,
        grid=(b, n_groups),
        in_specs=[
            pl.BlockSpec(blk, lambda i, g: (i, g, 0, 0)),
            pl.BlockSpec(blk, lambda i, g: (i, g, 0, 0)),
            pl.BlockSpec(blk, lambda i, g: (i, n_groups + g, 0, 0)),
            pl.BlockSpec((2, DIL_GROUP_HEADS, BLOCK, 2 * BLOCK), lambda i, g: (0, g, 0, 0)),
        ],
        out_specs=pl.BlockSpec(blk, lambda i, g: (i, 0, 0, 0)),
        out_shape=jax.ShapeDtypeStruct((b, pairs, s, PAIR_W), F32),
        scratch_shapes=[pltpu.VMEM((pairs, s, PAIR_W), F32)] * 3,
        compiler_params=_cparams(2, 56),
        name="dilated_attention",
    )(q_slabs, kv_slabs, kv_slabs, bias)


def kernel(x, mem, ffn_pre_norm, ffn_pre_w_in, ffn_pre_w_out, mix_norm, ffn_post_norm, ffn_post_w_in, ffn_post_w_out, mem_norm, mem_w_kv, mem_q_norm, mem_k_norm, a_w_in, a_shift_mu, a_w0, a_w_up, a_a0, a_a_up, a_g_up, a_kk_scale, a_k_a, a_r_k, a_lnx_g, a_lnx_b, a_w_out, b_w_q, b_q_norm, b_w_out, kv_norm, kv_w, kv_k_norm, rel_bias):
    b, s, d = x.shape
    depth = ffn_pre_w_in.shape[0]
    n_a = a_w_in.shape[0]
    n = b * s
    scale = 1.0 / math.sqrt(HEAD_DIM)
    ones_bd = _block_ones(SLAB_W, HEAD_DIM)
    row = lambda p: p.reshape(1, -1)

    k_mem, v_mem = _memkv(mem, mem_norm[:, None, :], mem_w_kv.astype(BF16),
                          jnp.tile(mem_k_norm, (1, MEM_HEADS))[:, None, :], ones_bd)
    mem_q_gain = jnp.tile(mem_q_norm, (1, MEM_HEADS)) * scale

    pre_w_in, pre_w_out = ffn_pre_w_in.astype(BF16), ffn_pre_w_out.astype(BF16)
    post_w_in, post_w_out = ffn_post_w_in.astype(BF16), ffn_post_w_out.astype(BF16)
    a_w_in_b = a_w_in.astype(BF16)
    xf = x.reshape(n, d)
    kv = None
    bias = None
    for layer in range(depth):
        xf = _ffn(xf, row(ffn_pre_norm[layer]), pre_w_in, pre_w_out, layer)
        if layer < n_a:
            i = layer
            ps, qm = _aproj(xf, row(mix_norm[layer]), a_w_in_b, i, row(a_shift_mu[i]),
                            row(mem_q_gain[layer]), ones_bd, s)
            zeros = jnp.zeros((DECAY_LORA, RWKV_W), F32)
            vecs = jnp.stack([a_w0[i], a_a0[i], a_kk_scale[i], a_k_a[i], a_r_k[i].reshape(-1),
                              a_lnx_g[i], a_lnx_b[i], jnp.zeros((RWKV_W,), F32)])
            y_main = _rwkv(ps.reshape(b, s, RWKV_SHIFT_W), vecs,
                           jnp.concatenate([a_w_up[i], zeros]).astype(BF16),
                           jnp.concatenate([zeros, a_a_up[i]]).astype(BF16),
                           a_g_up[i].astype(BF16), ones_bd)
            xf = _mixout(xf, y_main.reshape(n, RWKV_W), qm, k_mem[layer], v_mem[layer],
                         a_w_out[i].astype(BF16), s)
        else:
            j = layer - n_a
            q_gain = jnp.concatenate([jnp.tile(b_q_norm[j], DIL_W // HEAD_DIM) * scale, mem_q_gain[layer]])
            q_dil, qm = _qproj(xf, row(mix_norm[layer]), b_w_q[j].astype(BF16), row(q_gain), ones_bd, s, DIL_W)
            if bias is None:
                bias = _band_bias(rel_bias)
            y_dil = _dilated_attention(q_dil, kv, bias)
            xf = _mixout(xf, y_dil, qm, k_mem[layer], v_mem[layer],
                         b_w_out[j].astype(BF16), s)
        xf = _ffn(xf, row(ffn_post_norm[layer]), post_w_in, post_w_out, layer)
        if layer == n_a - 1:
            k_gain = jnp.tile(kv_k_norm, DIL_W // HEAD_DIM)
            kv, = _qproj(xf, row(kv_norm), kv_w.astype(BF16), row(k_gain), ones_bd, s, 2 * DIL_W)
    return xf.reshape(b, s, d)
```

```python
import functools
import math

import jax
import jax.numpy as jnp
import numpy as np
from jax import lax
from jax.experimental import pallas as pl
from jax.experimental.pallas import tpu as pltpu

F32 = jnp.float32
BF16 = jnp.bfloat16

HEAD_DIM = 64
MEM_HEADS = 4
MEM_W = MEM_HEADS * HEAD_DIM
RWKV_HEADS = 12
RWKV_W = RWKV_HEADS * HEAD_DIM
DECAY_LORA = 64
AAA_LORA = 64
GATE_LORA = 128
RWKV_SHIFT_W = 3 * RWKV_W + DECAY_LORA + AAA_LORA + GATE_LORA
LORA_LO = 3 * RWKV_W
DIL_GROUPS = ((128, 1), (512, 4), (2048, 16))
DIL_GROUP_HEADS = 4
DIL_W = len(DIL_GROUPS) * DIL_GROUP_HEADS * HEAD_DIM
DIL_OUT_W = DIL_GROUP_HEADS * HEAD_DIM
BLOCK = 128
NUM_BUCKETS = 32
MAX_DISTANCE = 2048
NORM_EPS = 1e-6
LNX_EPS = 64e-5
NEG_INF = -1e30

V7X_LANES = 128
V7X_SUBLANES = 8
V7X_VMEM_BYTES = 64 * 1024 * 1024

RWKV_CHUNK = 64
RWKV_SEQS_PER_STEP = 1
RWKV_PHASE_A_CHUNKS = 8
FFN_ROWS = 512
APROJ_ROWS = 512
QPROJ_ROWS = 1024
MIXOUT_ROWS = 1024
RWKV_ROWS = 512
RWKV_CUMSUM_ROWS = 256
DIL_TILES_PER_ITER = 2
DIL_ROW_PHASES = 4
PROJ_SUB_ROWS = 128
PAIR_W = 2 * HEAD_DIM
SLAB_W = 4 * HEAD_DIM

_NT = (((1,), (1,)), ((), ()))
_TN = (((0,), (0,)), ((), ()))


def _cparams(n_axes, vmem_mib):
    return pltpu.CompilerParams(
        dimension_semantics=("arbitrary",) * n_axes,
        vmem_limit_bytes=min(vmem_mib * 1024 * 1024, V7X_VMEM_BYTES - 4 * 1024 * 1024),
    )


def _const_spec(shape):
    zeros = (0,) * len(shape)
    return pl.BlockSpec(shape, lambda *_: zeros)


def _dot(a, b):
    return jnp.dot(a, b, preferred_element_type=F32)


def _rms(x, g, eps):
    return x * lax.rsqrt(jnp.mean(x * x, axis=-1, keepdims=True) + eps) * g


def _split2(x):
    hi = x.astype(BF16)
    lo = (x - hi.astype(F32)).astype(BF16)
    return hi, lo


def _seg_sum(x, ones_bd, pieces=2):
    outs = []
    for s in range(x.shape[-1] // SLAB_W):
        slab = x[:, s * SLAB_W:(s + 1) * SLAB_W]
        if pieces == 1:
            outs.append(_dot(slab.astype(BF16), ones_bd))
        else:
            hi, lo = _split2(slab)
            outs.append(_dot(hi, ones_bd) + _dot(lo, ones_bd))
    return outs[0] if len(outs) == 1 else jnp.concatenate(outs, axis=-1)


def _block_ones(n, blk, lower=False):
    i = np.arange(n)
    m = (i[:, None] // blk) == (i[None, :] // blk)
    if lower:
        m = m & (i[:, None] >= i[None, :])
    return jnp.asarray(m, dtype=BF16)


def _ffn_kernel(x_ref, g_ref, win_ref, wout_ref, o_ref):
    x = x_ref[...]
    d_ff = wout_ref.shape[1]
    xn = _rms(x, g_ref[...], NORM_EPS).astype(BF16)
    h = _dot(xn, win_ref[0])
    gate = h[:, :d_ff]
    up = h[:, d_ff:]
    act = (gate * jax.nn.sigmoid(gate) * up).astype(BF16)
    o_ref[...] = x + 0.5 * _dot(act, wout_ref[0])


def _ffn(x, g, w_in, w_out, layer, tm=FFN_ROWS):
    n, d = x.shape
    return pl.pallas_call(
        _ffn_kernel,
        grid=(n // tm,),
        in_specs=[
            pl.BlockSpec((tm, d), lambda i: (i, 0)),
            _const_spec((1, d)),
            pl.BlockSpec((1,) + w_in.shape[1:], lambda i: (layer, 0, 0), pipeline_mode=pl.Buffered(1)),
            pl.BlockSpec((1,) + w_out.shape[1:], lambda i: (layer, 0, 0), pipeline_mode=pl.Buffered(1)),
        ],
        out_specs=pl.BlockSpec((tm, d), lambda i: (i, 0)),
        out_shape=jax.ShapeDtypeStruct((n, d), F32),
        compiler_params=_cparams(1, 56),
        name="ffn",
    )(x, g, w_in, w_out)


def _memkv_kernel(mem_ref, g_ref, w_ref, kg_ref, ones_ref, k_ref, v_ref):
    m = _rms(mem_ref[0], g_ref[0], NORM_EPS).astype(BF16)
    kv = _dot(m, w_ref[0])
    k = kv[:, :MEM_W]
    ms = _seg_sum(k * k, ones_ref[...]) * (1.0 / HEAD_DIM)
    k_ref[0, 0] = (k * lax.rsqrt(ms + NORM_EPS) * kg_ref[0]).astype(BF16)
    v_ref[0, 0] = kv[:, MEM_W:].astype(BF16)


def _memkv(mem, mem_norm, w_kv, k_gain, ones_bd):
    b, m, d = mem.shape
    depth = w_kv.shape[0]
    out = jax.ShapeDtypeStruct((depth, b, m, MEM_W), BF16)
    return pl.pallas_call(
        _memkv_kernel,
        grid=(depth, b),
        in_specs=[
            pl.BlockSpec((1, m, d), lambda l, i: (i, 0, 0)),
            pl.BlockSpec((1, 1, d), lambda l, i: (l, 0, 0)),
            pl.BlockSpec((1, d, 2 * MEM_W), lambda l, i: (l, 0, 0)),
            pl.BlockSpec((1, 1, MEM_W), lambda l, i: (l, 0, 0)),
            _const_spec((SLAB_W, SLAB_W)),
        ],
        out_specs=[pl.BlockSpec((1, 1, m, MEM_W), lambda l, i: (l, i, 0, 0))] * 2,
        out_shape=[out, out],
        compiler_params=_cparams(2, 32),
        name="memkv",
    )(mem, mem_norm, w_kv, k_gain, ones_bd)


def _head_norm(q, gain, ones_bd):
    ms = _seg_sum(q * q, ones_bd, pieces=1) * (1.0 / HEAD_DIM)
    return q * lax.rsqrt(ms + NORM_EPS) * gain


def _aproj_kernel(x_ref, g_ref, w_ref, mu_ref, qg_ref, ones_ref, ps_ref, qm_ref, carry_ref, *, tiles_per_seq):
    i = pl.program_id(0)
    tm = x_ref.shape[0]
    u = _rms(x_ref[...], g_ref[...], NORM_EPS).astype(BF16)

    @pl.when(i % tiles_per_seq == 0)
    def _():
        carry_ref[...] = jnp.zeros_like(carry_ref)

    starts = range(0, tm, PROJ_SUB_ROWS)
    projs = [_dot(u[r0:r0 + PROJ_SUB_ROWS], w_ref[0]) for r0 in starts]
    row = lax.broadcasted_iota(jnp.int32, (PROJ_SUB_ROWS, RWKV_SHIFT_W), 0)
    last = carry_ref[V7X_SUBLANES - 1:V7X_SUBLANES, :]
    for r0, proj in zip(starts, projs):
        p = proj[:, :RWKV_SHIFT_W]
        prev = jnp.where(row == 0, last, pltpu.roll(p, 1, 0))
        ps_ref[r0:r0 + PROJ_SUB_ROWS, :] = p + mu_ref[...] * (prev - p)
        last = p[PROJ_SUB_ROWS - 1:, :]
        qm_ref[r0:r0 + PROJ_SUB_ROWS, :] = _head_norm(proj[:, RWKV_SHIFT_W:], qg_ref[...],
                                                      ones_ref[...]).astype(qm_ref.dtype)
    carry_ref[...] = projs[-1][PROJ_SUB_ROWS - V7X_SUBLANES:, :RWKV_SHIFT_W]


def _aproj(x, g, w_in, layer, mu, q_gain, ones_bd, seq_len, tm=APROJ_ROWS):
    n, d = x.shape
    return pl.pallas_call(
        functools.partial(_aproj_kernel, tiles_per_seq=seq_len // tm),
        grid=(n // tm,),
        in_specs=[
            pl.BlockSpec((tm, d), lambda i: (i, 0)),
            _const_spec((1, d)),
            pl.BlockSpec((1,) + w_in.shape[1:], lambda i: (layer, 0, 0), pipeline_mode=pl.Buffered(1)),
            _const_spec((1, RWKV_SHIFT_W)),
            _const_spec((1, MEM_W)),
            _const_spec((SLAB_W, SLAB_W)),
        ],
        out_specs=[pl.BlockSpec((tm, RWKV_SHIFT_W), lambda i: (i, 0)),
                   pl.BlockSpec((tm, MEM_W), lambda i: (i, 0))],
        out_shape=[jax.ShapeDtypeStruct((n, RWKV_SHIFT_W), F32),
                   jax.ShapeDtypeStruct((n, MEM_W), BF16)],
        scratch_shapes=[pltpu.VMEM((V7X_SUBLANES, RWKV_SHIFT_W), F32)],
        compiler_params=_cparams(1, 48),
        name="aproj",
    )(x, g, w_in, mu, q_gain, ones_bd)


def _qproj_kernel(x_ref, g_ref, w_ref, qg_ref, ones_ref, slab_ref, *rest, normed_w):
    stage_ref = rest[-1]
    flat_refs = rest[:-1]
    tm = x_ref.shape[0]
    u = _rms(x_ref[...], g_ref[...], NORM_EPS).astype(BF16)
    proj = _dot(u, w_ref[...])
    normed = _head_norm(proj[:, :normed_w], qg_ref[...], ones_ref[...])
    n_slabs = slab_ref.shape[1]
    for j in range(n_slabs):
        src = normed if (j + 1) * V7X_LANES <= normed_w else proj
        stage_ref[j] = src[:, j * V7X_LANES:(j + 1) * V7X_LANES]
    for j in range(n_slabs):
        for c in range(DIL_ROW_PHASES):
            slab_ref[0, j, c] = stage_ref[j, pl.ds(c, tm // DIL_ROW_PHASES, stride=DIL_ROW_PHASES), :]
    if flat_refs:
        flat_refs[0][...] = normed[:, n_slabs * V7X_LANES:].astype(flat_refs[0].dtype)


def _qproj(x, g, w, q_gain, ones_bd, seq_len, slab_w, tm=QPROJ_ROWS):
    n, d = x.shape
    wo = w.shape[1]
    normed_w = q_gain.shape[-1]
    n_slabs = slab_w // V7X_LANES
    tiles_per_seq = seq_len // tm
    phase_rows = seq_len // DIL_ROW_PHASES
    out_specs = [pl.BlockSpec((1, n_slabs, DIL_ROW_PHASES, tm // DIL_ROW_PHASES, V7X_LANES),
                              lambda i: (i // tiles_per_seq, 0, 0, i % tiles_per_seq, 0))]
    out_shape = [jax.ShapeDtypeStruct((n // seq_len, n_slabs, DIL_ROW_PHASES, phase_rows, V7X_LANES), F32)]
    if wo > slab_w:
        assert normed_w == wo
        out_specs.append(pl.BlockSpec((tm, wo - slab_w), lambda i: (i, 0)))
        out_shape.append(jax.ShapeDtypeStruct((n, wo - slab_w), BF16))
    outs = list(pl.pallas_call(
        functools.partial(_qproj_kernel, normed_w=normed_w),
        grid=(n // tm,),
        in_specs=[
            pl.BlockSpec((tm, d), lambda i: (i, 0)),
            _const_spec((1, d)),
            pl.BlockSpec(w.shape, lambda i: (0, 0), pipeline_mode=pl.Buffered(1)),
            _const_spec((1, normed_w)),
            _const_spec((SLAB_W, SLAB_W)),
        ],
        out_specs=out_specs,
        out_shape=out_shape,
        scratch_shapes=[pltpu.VMEM((n_slabs, tm, V7X_LANES), F32)],
        compiler_params=_cparams(1, 48),
        name="qproj",
    )(x, g, w, q_gain, ones_bd))
    outs[0] = outs[0].reshape(n // seq_len, n_slabs, seq_len, V7X_LANES)
    return outs


def _embed(x):
    head0 = lax.broadcasted_iota(jnp.int32, x.shape, 1) < HEAD_DIM
    zero = jnp.zeros_like(x)
    return jnp.concatenate([jnp.where(head0, x, zero), jnp.where(head0, zero, x)], axis=0)


def _tri_inverse(nmats, tpos, spos):
    eye = (tpos == spos).astype(F32)
    same = {s: (tpos >> s) == (spos >> s) for s in (3, 4, 5)}

    def mm(lhs, rhs):
        return [_dot(a, _embed(b)) for a, b in zip(lhs, rhs)]

    def bf(xs):
        return [x.astype(BF16) for x in xs]

    n8 = [jnp.where(same[3], n, 0.0) for n in nmats]
    n8b = bf(n8)
    n2 = mm(n8b, n8b)
    n2b = bf(n2)
    n4 = mm(n2b, n2b)
    t = mm(bf([eye + a for a in n8]), bf([eye + a for a in n2]))
    t = mm(bf(t), bf([eye + a for a in n4]))
    for lo, hi in ((3, 4), (4, 5), (5, None)):
        off = ~same[lo] if hi is None else (same[hi] & ~same[lo])
        tb = bf(t)
        z = mm(bf([jnp.where(off, n, 0.0) for n in nmats]), tb)
        t = [a + d for a, d in zip(t, mm(tb, bf(z)))]
    return t


def _rwkv_kernel(ps_ref, vec_ref, wup_ref, aup_ref, gup_ref, tri_ref, ones_ref, y_ref,
                 s_ref, rt_ref, kt_ref, bt_ref, at_ref, kh_ref, bh_ref, v_ref, gam_ref, yacc_ref,
                 bonus_ref, gate_ref, p_s, qt_s, rp_s, y0_s):
    n_seqs, seq_rows = ps_ref.shape[1], ps_ref.shape[2]
    tblk = n_seqs * seq_rows
    n_pairs = RWKV_W // PAIR_W

    @pl.when(pl.program_id(1) == 0)
    def _():
        s_ref[...] = jnp.zeros_like(s_ref)

    ps = jnp.concatenate([ps_ref[0, q] for q in range(n_seqs)], axis=0)
    r = ps[:, :RWKV_W]
    k = ps[:, RWKV_W:2 * RWKV_W]
    v = ps[:, 2 * RWKV_W:3 * RWKV_W]
    lora_in = ps[:, LORA_LO:LORA_LO + PAIR_W]
    g_lo = ps[:, LORA_LO + PAIR_W:]
    w0, a0, kk_scale, k_a = vec_ref[0:1, :], vec_ref[1:2, :], vec_ref[2:3, :], vec_ref[3:4, :]
    r_k, lnx_g, lnx_b = vec_ref[4:5, :], vec_ref[5:6, :], vec_ref[6:7, :]
    ones_bd = ones_ref[...]

    nz = -(w0 + _dot(jnp.tanh(lora_in).astype(BF16), wup_ref[...]))
    softplus = jnp.maximum(nz, 0.0) + jnp.log(1.0 + jnp.exp(-jnp.abs(nz)))
    lw = -jnp.exp(-softplus - 0.5)
    a = jax.nn.sigmoid(a0 + _dot(lora_in.astype(BF16), aup_ref[...]))
    gate_ref[...] = _dot(jax.nn.sigmoid(g_lo).astype(BF16), gup_ref[...])
    kk = k * kk_scale
    kk = kk / jnp.maximum(jnp.sqrt(_seg_sum(kk * kk, ones_bd, pieces=1)), 1e-12)
    k2 = k * (1.0 + (a - 1.0) * k_a)
    kka = kk * a
    bonus_ref[...] = _seg_sum(r * k2 * r_k, ones_bd) * v

    hi, lo = _split2(lw)
    tri = tri_ref[...]
    gcum = jnp.concatenate(
        [_dot(tri, hi[s0:s0 + RWKV_CUMSUM_ROWS]) + _dot(tri, lo[s0:s0 + RWKV_CUMSUM_ROWS])
         for s0 in range(0, tblk, RWKV_CUMSUM_ROWS)], axis=0)
    n_chunks = tblk // RWKV_CHUNK
    gam_rows = [jnp.exp(gcum[(c + 1) * RWKV_CHUNK - 1:(c + 1) * RWKV_CHUNK, :]) for c in range(n_chunks)]
    gam = jnp.concatenate([jnp.broadcast_to(g, (RWKV_CHUNK, RWKV_W)) for g in gam_rows], axis=0)
    for c in range(n_chunks):
        gam_ref[c:c + 1, :] = gam_rows[c]
    e_neg = jnp.exp(-gcum)
    e_last = gam * e_neg
    rt_ref[...] = (r * jnp.exp(gcum)).astype(BF16)
    kt_ref[...] = (k2 * e_neg).astype(BF16)
    bt_ref[...] = (kka * e_neg).astype(BF16)
    at_ref[...] = (-kk * jnp.exp(gcum - lw)).astype(BF16)
    kh_ref[...] = (k2 * e_last).astype(BF16)
    bh_ref[...] = (kka * e_last).astype(BF16)
    v_ref[...] = v.astype(BF16)

    tpos = lax.broadcasted_iota(jnp.int32, (RWKV_CHUNK, PAIR_W), 0)
    spos = lax.broadcasted_iota(jnp.int32, (RWKV_CHUNK, PAIR_W), 1) & (HEAD_DIM - 1)
    strict = tpos > spos
    incl = tpos >= spos
    head0 = lax.broadcasted_iota(jnp.int32, (RWKV_CHUNK, PAIR_W), 1) < HEAD_DIM
    same_head = ((lax.broadcasted_iota(jnp.int32, (PAIR_W, PAIR_W), 0) < HEAD_DIM)
                 == (lax.broadcasted_iota(jnp.int32, (PAIR_W, PAIR_W), 1) < HEAD_DIM))

    pairs = range(n_pairs)
    lanes = [slice(p * PAIR_W, (p + 1) * PAIR_W) for p in pairs]

    def phase_a(i, carry):
        slots, at_v, rt_v, bt_v, kt_v, v_v, bh_v, kh_v = [], [], [], [], [], [], [], []
        for cc in range(RWKV_PHASE_A_CHUNKS):
            c = i * RWKV_PHASE_A_CHUNKS + cc
            rows = pl.ds(pl.multiple_of(c * RWKV_CHUNK, RWKV_CHUNK), RWKV_CHUNK)
            loaded = [ref[rows, :] for ref in (at_ref, rt_ref, bt_ref, kt_ref, v_ref, bh_ref, kh_ref)]
            for p in pairs:
                slots.append(c * n_pairs + p)
                for dst, x_c in zip((at_v, rt_v, bt_v, kt_v, v_v, bh_v, kh_v), loaded):
                    dst.append(x_c[:, lanes[p]])
        chains = range(len(slots))
        m4 = [lax.dot_general(jnp.concatenate([at_v[j], rt_v[j]], axis=0),
                              jnp.concatenate([_embed(bt_v[j]), _embed(kt_v[j])], axis=0), _NT,
                              preferred_element_type=F32) for j in chains]
        n_ab = [jnp.where(strict, m[:RWKV_CHUNK, :PAIR_W], 0.0) for m in m4]
        a_ak = [jnp.where(strict, m[:RWKV_CHUNK, PAIR_W:], 0.0).astype(BF16) for m in m4]
        a_rb = [jnp.where(incl, m[RWKV_CHUNK:, :PAIR_W], 0.0).astype(BF16) for m in m4]
        a_rk = [jnp.where(incl, m[RWKV_CHUNK:, PAIR_W:], 0.0).astype(BF16) for m in m4]
        v_m = [_embed(x) for x in v_v]
        akv = [_dot(a_ak[j], v_m[j]).astype(BF16) for j in chains]
        t_inv = [t.astype(BF16) for t in _tri_inverse(n_ab, tpos, spos)]
        wu0 = [_dot(t_inv[j], jnp.concatenate([_embed(at_v[j]), _embed(akv[j])], axis=1))
               for j in chains]
        w_b = [x[:, :PAIR_W].astype(BF16) for x in wu0]
        u0_b = [x[:, PAIR_W:].astype(BF16) for x in wu0]
        p_m = [jnp.where(same_head, lax.dot_general(bh_v[j], w_b[j], _TN, preferred_element_type=F32), 0.0)
               for j in chains]
        q_full = [lax.dot_general(jnp.concatenate([u0_b[j], v_v[j]], axis=0),
                                  jnp.concatenate([bh_v[j], kh_v[j]], axis=0), _TN,
                                  preferred_element_type=F32) for j in chains]
        q_t = [jnp.where(head0, q[:RWKV_CHUNK], q[RWKV_CHUNK:]) for q in q_full]
        r_p = [rt_v[j].astype(F32) + _dot(a_rb[j], _embed(w_b[j])) for j in chains]
        y_0 = [_dot(jnp.concatenate([a_rb[j], a_rk[j]], axis=1),
                    jnp.concatenate([_embed(u0_b[j]), v_m[j]], axis=0)) for j in chains]
        for j in chains:
            p_s[slots[j]] = p_m[j].astype(BF16)
            qt_s[slots[j]] = q_t[j]
            rp_s[slots[j]] = r_p[j].astype(BF16)
            y0_s[slots[j]] = y_0[j]
        return carry

    lax.fori_loop(0, n_chunks // RWKV_PHASE_A_CHUNKS, phase_a, 0)

    seq_chunks = seq_rows // RWKV_CHUNK
    heads = [(q, p) for q in range(n_seqs) for p in pairs]
    states = [s_ref[q * n_pairs + p] for q, p in heads]
    for c in range(seq_chunks):
        chunk = [q * seq_chunks + c for q, _ in heads]
        slots = [ch * n_pairs + p for ch, (_, p) in zip(chunk, heads)]
        gam_c = [gam_ref[ch:ch + 1, lanes[p]] for ch, (_, p) in zip(chunk, heads)]
        state_b = [s.astype(BF16) for s in states]
        upd = [lax.dot_general(sb, p_s[j], _NT, preferred_element_type=F32) for sb, j in zip(state_b, slots)]
        y_c = [lax.dot_general(rp_s[j], _embed(sb), _NT, preferred_element_type=F32) + y0_s[j]
               for sb, j in zip(state_b, slots)]
        states = [s * g + u + qt_s[j] for s, g, u, j in zip(states, gam_c, upd, slots)]
        for q in range(n_seqs):
            r0 = (q * seq_chunks + c) * RWKV_CHUNK
            yacc_ref[r0:r0 + RWKV_CHUNK, :] = jnp.concatenate(y_c[q * n_pairs:(q + 1) * n_pairs], axis=1)
    for i, s in enumerate(states):
        s_ref[i] = s

    y = yacc_ref[...]
    mean = _seg_sum(y, ones_bd) * (1.0 / HEAD_DIM)
    dev = y - mean
    var = _seg_sum(dev * dev, ones_bd, pieces=1) * (1.0 / HEAD_DIM)
    yn = dev * lax.rsqrt(var + LNX_EPS) * lnx_g + lnx_b
    out = ((yn + bonus_ref[...]) * gate_ref[...]).astype(y_ref.dtype)
    for q in range(n_seqs):
        y_ref[0, q] = out[q * seq_rows:(q + 1) * seq_rows]


def _rwkv(ps, vecs, w_up_p, a_up_p, g_up, ones_bd, seq_rows=RWKV_ROWS):
    b, s, _ = ps.shape
    n_seqs = RWKV_SEQS_PER_STEP
    tblk = n_seqs * seq_rows
    tri = _block_ones(RWKV_CUMSUM_ROWS, RWKV_CHUNK, lower=True)
    act = pltpu.VMEM((tblk, RWKV_W), F32)
    act_b = pltpu.VMEM((tblk, RWKV_W), BF16)
    n_pairs = RWKV_W // PAIR_W
    n_chunks = tblk // RWKV_CHUNK
    n_slots = n_chunks * n_pairs
    gam_rows = -(-n_chunks // V7X_SUBLANES) * V7X_SUBLANES
    scratch = [pltpu.VMEM((n_seqs * n_pairs, HEAD_DIM, PAIR_W), F32)]
    scratch += [act_b] * 7
    scratch += [pltpu.VMEM((gam_rows, RWKV_W), F32)] + [act] * 3
    scratch += [
        pltpu.VMEM((n_slots, PAIR_W, PAIR_W), BF16),
        pltpu.VMEM((n_slots, RWKV_CHUNK, PAIR_W), F32),
        pltpu.VMEM((n_slots, RWKV_CHUNK, PAIR_W), BF16),
        pltpu.VMEM((n_slots, RWKV_CHUNK, PAIR_W), F32),
    ]
    y = pl.pallas_call(
        _rwkv_kernel,
        grid=(b // n_seqs, s // seq_rows),
        in_specs=[
            pl.BlockSpec((1, n_seqs, seq_rows, RWKV_SHIFT_W), lambda i, t: (i, 0, t, 0)),
            _const_spec(vecs.shape),
            _const_spec(w_up_p.shape),
            _const_spec(a_up_p.shape),
            _const_spec(g_up.shape),
            _const_spec(tri.shape),
            _const_spec((SLAB_W, SLAB_W)),
        ],
        out_specs=pl.BlockSpec((1, n_seqs, seq_rows, RWKV_W), lambda i, t: (i, 0, t, 0)),
        out_shape=jax.ShapeDtypeStruct((b // n_seqs, n_seqs, s, RWKV_W), BF16),
        scratch_shapes=scratch,
        compiler_params=_cparams(2, 48),
        name="rwkv",
    )(ps.reshape(b // n_seqs, n_seqs, s, RWKV_SHIFT_W), vecs, w_up_p, a_up_p, g_up, tri, ones_bd)
    return y.reshape(b, s, RWKV_W)


def _mixout_kernel(x_ref, ya_ref, qm_ref, k_ref, v_ref, w_ref, o_ref):
    if len(ya_ref.shape) == 4:
        ya = jnp.concatenate([ya_ref[0, j] for j in range(ya_ref.shape[1])], axis=1)
    else:
        ya = ya_ref[...]
    wa = ya.shape[-1]
    qm = qm_ref[...]
    kmem = k_ref[0]
    vmem = v_ref[0]
    head = lax.broadcasted_iota(jnp.int32, qm.shape, 1) // HEAD_DIM
    heads = range(MEM_HEADS)
    logits = [lax.dot_general(jnp.where(head == h, qm, jnp.zeros_like(qm)), kmem, _NT,
                              preferred_element_type=F32) for h in heads]
    p = [jnp.exp(lg - jnp.max(lg, axis=-1, keepdims=True)) for lg in logits]
    inv_l = [1.0 / jnp.sum(ph, axis=-1, keepdims=True) for ph in p]
    pv = [_dot(ph.astype(BF16), vmem) for ph in p]
    y_mem = pv[0] * inv_l[0]
    for h in heads[1:]:
        y_mem = jnp.where(head == h, pv[h] * inv_l[h], y_mem)
    y = _dot(ya.astype(BF16), w_ref[:wa, :]) + _dot(y_mem.astype(BF16), w_ref[wa:, :])
    o_ref[...] = x_ref[...] + y


def _mixout(x, ya, qm, k_mem, v_mem, w_out, seq_len, tm=MIXOUT_ROWS):
    n, d = x.shape
    tiles_per_seq = seq_len // tm
    m = k_mem.shape[1]
    if ya.ndim == 4:
        ya_spec = pl.BlockSpec((1, ya.shape[1], tm, V7X_LANES),
                               lambda i: (i // tiles_per_seq, 0, i % tiles_per_seq, 0))
    else:
        ya_spec = pl.BlockSpec((tm, ya.shape[-1]), lambda i: (i, 0))
    return pl.pallas_call(
        _mixout_kernel,
        grid=(n // tm,),
        in_specs=[
            pl.BlockSpec((tm, d), lambda i: (i, 0)),
            ya_spec,
            pl.BlockSpec((tm, MEM_W), lambda i: (i, 0)),
            pl.BlockSpec((1, m, MEM_W), lambda i: (i // tiles_per_seq, 0, 0)),
            pl.BlockSpec((1, m, MEM_W), lambda i: (i // tiles_per_seq, 0, 0)),
            _const_spec(w_out.shape),
        ],
        out_specs=pl.BlockSpec((tm, d), lambda i: (i, 0)),
        out_shape=jax.ShapeDtypeStruct((n, d), F32),
        compiler_params=_cparams(1, 40),
        name="mixout",
    )(x, ya, qm, k_mem, v_mem, w_out)


def _t5_bucket(dist):
    max_exact = NUM_BUCKETS // 2
    d_f = jnp.maximum(dist, 1).astype(F32)
    large = max_exact + (jnp.log(d_f / max_exact) / math.log(MAX_DISTANCE / max_exact)
                         * (NUM_BUCKETS - max_exact)).astype(jnp.int32)
    large = jnp.minimum(large, NUM_BUCKETS - 1)
    return jnp.where(dist < max_exact, dist, large)


def _band_buckets():
    run = BLOCK // DIL_ROW_PHASES
    out = []
    for window, dil in DIL_GROUPS:
        rows_q = np.arange(BLOCK)
        rows_k = np.arange(2 * BLOCK)
        if dil < DIL_ROW_PHASES:
            u_q = DIL_ROW_PHASES * (rows_q % run) + rows_q // run
            u_k = ((rows_k % (2 * run)) // run) * BLOCK + DIL_ROW_PHASES * (rows_k % run) + rows_k // (2 * run)
        else:
            u_q, u_k = rows_q, rows_k
        dsub = jnp.asarray(BLOCK + u_q[:, None] - u_k[None, :])
        band = (dsub >= 0) & (dsub <= window // dil)
        idx = jnp.where(band, _t5_bucket(jnp.maximum(dsub, 0) * dil), -1)
        out.append(jnp.stack([idx, jnp.where(jnp.asarray(u_k < BLOCK)[None, :], -1, idx)]))
    return jnp.stack(out, axis=1).astype(jnp.int32)


def _bias_kernel(tab_ref, idx_ref, o_ref):
    head = pl.program_id(1)
    idx = idx_ref[0, 0]
    acc = jnp.full(idx.shape, NEG_INF, F32)
    for bucket in range(NUM_BUCKETS):
        acc = jnp.where(idx == bucket, tab_ref[bucket, head], acc)
    o_ref[0, 0] = acc


def _band_bias(rel_bias):
    n_heads = rel_bias.shape[1]
    return pl.pallas_call(
        _bias_kernel,
        grid=(2, n_heads),
        in_specs=[
            pl.BlockSpec(memory_space=pltpu.SMEM),
            pl.BlockSpec((1, 1, BLOCK, 2 * BLOCK), lambda v, h: (v, h // DIL_GROUP_HEADS, 0, 0)),
        ],
        out_specs=pl.BlockSpec((1, 1, BLOCK, 2 * BLOCK), lambda v, h: (v, h, 0, 0)),
        out_shape=jax.ShapeDtypeStruct((2, n_heads, BLOCK, 2 * BLOCK), F32),
        compiler_params=_cparams(2, 16),
        name="band_bias",
    )(rel_bias, _band_buckets())


def _dil_group(dil, q_ref, k_ref, v_ref, bias_ref, acc_ref, m_ref, l_ref):
    seq_len = q_ref.shape[2]
    phase_rows = seq_len // DIL_ROW_PHASES
    run = BLOCK // DIL_ROW_PHASES
    head0 = lax.broadcasted_iota(jnp.int32, (BLOCK, PAIR_W), 1) < HEAD_DIM

    def windows(t):
        if dil < DIL_ROW_PHASES:
            blk = t
            back = jnp.maximum(blk - 1, 0)
            own = [pl.ds(pl.multiple_of(c * phase_rows + run * blk, run), run) for c in range(DIL_ROW_PHASES)]
            prev = [pl.ds(pl.multiple_of(c * phase_rows + run * back, run), run) for c in range(DIL_ROW_PHASES)]
            return own, prev, blk == 0
        step = dil // DIL_ROW_PHASES
        n_blocks = phase_rows // (step * BLOCK)
        sub = t >> (n_blocks.bit_length() - 1)
        blk = t & (n_blocks - 1)
        base = (sub & (DIL_ROW_PHASES - 1)) * phase_rows + (sub >> (DIL_ROW_PHASES.bit_length() - 1))
        back = jnp.maximum(blk - 1, 0)
        if step == 1:
            own = [pl.ds(pl.multiple_of(base + BLOCK * blk, BLOCK), BLOCK)]
            prev = [pl.ds(pl.multiple_of(base + BLOCK * back, BLOCK), BLOCK)]
        else:
            own = [pl.ds(base + step * BLOCK * blk, BLOCK, stride=step)]
            prev = [pl.ds(base + step * BLOCK * back, BLOCK, stride=step)]
        return own, prev, blk == 0

    def gather(get, wins):
        parts = [get(w) for w in wins]
        return parts[0] if len(parts) == 1 else jnp.concatenate(parts, axis=0)

    def scatter(put, wins, value):
        rows = BLOCK // len(wins)
        for i, w in enumerate(wins):
            put(w, value[i * rows:(i + 1) * rows])

    def tiles(i, carry):
        slabs = []
        for j in range(DIL_TILES_PER_ITER):
            own, prev, first = windows(i * DIL_TILES_PER_ITER + j)
            key_wins = [w for pw, ow in zip(prev, own) for w in (pw, ow)]
            variant = first.astype(jnp.int32)
            slabs += [(pair, own, key_wins, variant) for pair in range(DIL_OUT_W // PAIR_W)]
        q = [gather(lambda w: q_ref[0, pair, w, :], own) for pair, own, _, _ in slabs]
        keys = [gather(lambda w: k_ref[0, pair, w, :], kw).astype(BF16) for pair, _, kw, _ in slabs]
        vals = [gather(lambda w: v_ref[0, pair, w, :], kw).astype(BF16) for pair, _, kw, _ in slabs]
        m_old = [gather(lambda w: m_ref[pair, w, :], own) for pair, own, _, _ in slabs]
        l_old = [gather(lambda w: l_ref[pair, w, :], own) for pair, own, _, _ in slabs]
        acc_old = [gather(lambda w: acc_ref[pair, w, :], own) for pair, own, _, _ in slabs]
        heads = [(s, h) for s in range(len(slabs)) for h in range(2)]
        logits = [lax.dot_general(jnp.where(head0 if h == 0 else ~head0, q[s], 0.0).astype(BF16),
                                  keys[s], _NT, preferred_element_type=F32) for s, h in heads]
        logits = [lg + bias_ref[slabs[s][3], 2 * slabs[s][0] + h] for lg, (s, h) in zip(logits, heads)]
        mx = [jnp.max(lg, axis=-1, keepdims=True) for lg in logits]
        p = [jnp.exp(lg - m) for lg, m in zip(logits, mx)]
        ls = [jnp.sum(ph, axis=-1, keepdims=True) for ph in p]
        pv = [_dot(ph.astype(BF16), vals[s]) for ph, (s, h) in zip(p, heads)]
        for s, (pair, own, _, _) in enumerate(slabs):
            m_t = jnp.where(head0, mx[2 * s], mx[2 * s + 1])
            l_t = jnp.where(head0, ls[2 * s], ls[2 * s + 1])
            acc_t = jnp.where(head0, pv[2 * s], pv[2 * s + 1])
            m_new = jnp.maximum(m_old[s], m_t)
            e_old = jnp.exp(m_old[s] - m_new)
            e_t = jnp.exp(m_t - m_new)

            def put(ref, value, pair=pair, own=own):
                def store(w, rows):
                    ref[pair, w, :] = rows
                scatter(store, own, value)

            put(m_ref, m_new)
            put(l_ref, l_old[s] * e_old + l_t * e_t)
            put(acc_ref, acc_old[s] * e_old + acc_t * e_t)
        return carry

    lax.fori_loop(0, seq_len // BLOCK // DIL_TILES_PER_ITER, tiles, 0)


def _dil_kernel(q_ref, k_ref, v_ref, bias_ref, o_ref, acc_ref, m_ref, l_ref):
    g = pl.program_id(1)

    @pl.when(g == 0)
    def _():
        m_ref[...] = jnp.full(m_ref.shape, NEG_INF, F32)
        l_ref[...] = jnp.zeros_like(l_ref)
        acc_ref[...] = jnp.zeros_like(acc_ref)

    for gi, (_, dil) in enumerate(DIL_GROUPS):
        pl.when(g == gi)(functools.partial(_dil_group, dil, q_ref, k_ref, v_ref, bias_ref, acc_ref, m_ref, l_ref))

    @pl.when(g == len(DIL_GROUPS) - 1)
    def _():
        phase_rows = o_ref.shape[2] // DIL_ROW_PHASES
        for pair in range(DIL_OUT_W // PAIR_W):
            for c in range(DIL_ROW_PHASES):
                rows = slice(c * phase_rows, (c + 1) * phase_rows)
                o_ref[0, pair, pl.ds(c, phase_rows, stride=DIL_ROW_PHASES), :] = (
                    acc_ref[pair, rows, :] / l_ref[pair, rows, :])


def _dilated_attention(q_slabs, kv_slabs, bias):
    b, _, s, _ = q_slabs.shape
    n_groups = len(DIL_GROUPS)
    pairs = DIL_OUT_W // PAIR_W
    blk = (1, pairs, s, PAIR_W)
    return pl.pallas_call(
        _dil_kernel,
        grid=(b, n_groups),
        in_specs=[
            pl.BlockSpec(blk, lambda i, g: (i, g, 0, 0)),
            pl.BlockSpec(blk, lambda i, g: (i, g, 0, 0)),
            pl.BlockSpec(blk, lambda i, g: (i, n_groups + g, 0, 0)),
            pl.BlockSpec((2, DIL_GROUP_HEADS, BLOCK, 2 * BLOCK), lambda i, g: (0, g, 0, 0)),
        ],
        out_specs=pl.BlockSpec(blk, lambda i, g: (i, 0, 0, 0)),
        out_shape=jax.ShapeDtypeStruct((b, pairs, s, PAIR_W), F32),
        scratch_shapes=[pltpu.VMEM((pairs, s, PAIR_W), F32)] * 3,
        compiler_params=_cparams(2, 56),
        name="dilated_attention",
    )(q_slabs, kv_slabs, kv_slabs, bias)


def kernel(x, mem, ffn_pre_norm, ffn_pre_w_in, ffn_pre_w_out, mix_norm, ffn_post_norm, ffn_post_w_in, ffn_post_w_out, mem_norm, mem_w_kv, mem_q_norm, mem_k_norm, a_w_in, a_shift_mu, a_w0, a_w_up, a_a0, a_a_up, a_g_up, a_kk_scale, a_k_a, a_r_k, a_lnx_g, a_lnx_b, a_w_out, b_w_q, b_q_norm, b_w_out, kv_norm, kv_w, kv_k_norm, rel_bias):
    b, s, d = x.shape
    depth = ffn_pre_w_in.shape[0]
    n_a = a_w_in.shape[0]
    n = b * s
    scale = 1.0 / math.sqrt(HEAD_DIM)
    ones_bd = _block_ones(SLAB_W, HEAD_DIM)
    row = lambda p: p.reshape(1, -1)

    k_mem, v_mem = _memkv(mem, mem_norm[:, None, :], mem_w_kv.astype(BF16),
                          jnp.tile(mem_k_norm, (1, MEM_HEADS))[:, None, :], ones_bd)
    mem_q_gain = jnp.tile(mem_q_norm, (1, MEM_HEADS)) * scale

    pre_w_in, pre_w_out = ffn_pre_w_in.astype(BF16), ffn_pre_w_out.astype(BF16)
    post_w_in, post_w_out = ffn_post_w_in.astype(BF16), ffn_post_w_out.astype(BF16)
    a_w_in_b = a_w_in.astype(BF16)
    xf = x.reshape(n, d)
    kv = None
    bias = None
    for layer in range(depth):
        xf = _ffn(xf, row(ffn_pre_norm[layer]), pre_w_in, pre_w_out, layer)
        if layer < n_a:
            i = layer
            ps, qm = _aproj(xf, row(mix_norm[layer]), a_w_in_b, i, row(a_shift_mu[i]),
                            row(mem_q_gain[layer]), ones_bd, s)
            zeros = jnp.zeros((DECAY_LORA, RWKV_W), F32)
            vecs = jnp.stack([a_w0[i], a_a0[i], a_kk_scale[i], a_k_a[i], a_r_k[i].reshape(-1),
                              a_lnx_g[i], a_lnx_b[i], jnp.zeros((RWKV_W,), F32)])
            y_main = _rwkv(ps.reshape(b, s, RWKV_SHIFT_W), vecs,
                           jnp.concatenate([a_w_up[i], zeros]).astype(BF16),
                           jnp.concatenate([zeros, a_a_up[i]]).astype(BF16),
                           a_g_up[i].astype(BF16), ones_bd)
            xf = _mixout(xf, y_main.reshape(n, RWKV_W), qm, k_mem[layer], v_mem[layer],
                         a_w_out[i].astype(BF16), s)
        else:
            j = layer - n_a
            q_gain = jnp.concatenate([jnp.tile(b_q_norm[j], DIL_W // HEAD_DIM) * scale, mem_q_gain[layer]])
            q_dil, qm = _qproj(xf, row(mix_norm[layer]), b_w_q[j].astype(BF16), row(q_gain), ones_bd, s, DIL_W)
            if bias is None:
                bias = _band_bias(rel_bias)
            y_dil = _dilated_attention(q_dil, kv, bias)
            xf = _mixout(xf, y_dil, qm, k_mem[layer], v_mem[layer],
                         b_w_out[j].astype(BF16), s)
        xf = _ffn(xf, row(ffn_post_norm[layer]), post_w_in, post_w_out, layer)
        if layer == n_a - 1:
            k_gain = jnp.tile(kv_k_norm, DIL_W // HEAD_DIM)
            kv, = _qproj(xf, row(kv_norm), kv_w.astype(BF16), row(k_gain), ones_bd, s, 2 * DIL_W)
    return xf.reshape(b, s, d)
```

```python
import functools
import math

import jax
import jax.numpy as jnp
import numpy as np
from jax import lax
from jax.experimental import pallas as pl
from jax.experimental.pallas import tpu as pltpu

F32 = jnp.float32
BF16 = jnp.bfloat16

HEAD_DIM = 64
MEM_HEADS = 4
MEM_W = MEM_HEADS * HEAD_DIM
RWKV_HEADS = 12
RWKV_W = RWKV_HEADS * HEAD_DIM
DECAY_LORA = 64
AAA_LORA = 64
GATE_LORA = 128
RWKV_SHIFT_W = 3 * RWKV_W + DECAY_LORA + AAA_LORA + GATE_LORA
LORA_LO = 3 * RWKV_W
DIL_GROUPS = ((128, 1), (512, 4), (2048, 16))
DIL_GROUP_HEADS = 4
DIL_W = len(DIL_GROUPS) * DIL_GROUP_HEADS * HEAD_DIM
DIL_OUT_W = DIL_GROUP_HEADS * HEAD_DIM
BLOCK = 128
NUM_BUCKETS = 32
MAX_DISTANCE = 2048
NORM_EPS = 1e-6
LNX_EPS = 64e-5
NEG_INF = -1e30

V7X_LANES = 128
V7X_SUBLANES = 8
V7X_VMEM_BYTES = 64 * 1024 * 1024

RWKV_CHUNK = 64
RWKV_SEQS_PER_STEP = 1
RWKV_PHASE_A_CHUNKS = 8
FFN_ROWS = 1024
FFN_CHUNK = 256
APROJ_ROWS = 512
QPROJ_ROWS = 1024
MIXOUT_ROWS = 1024
RWKV_ROWS = 512
RWKV_CUMSUM_ROWS = 256
DIL_TILES_PER_ITER = 2
DIL_ROW_PHASES = 4
PROJ_SUB_ROWS = 128
PAIR_W = 2 * HEAD_DIM
SLAB_W = 4 * HEAD_DIM

_NT = (((1,), (1,)), ((), ()))
_TN = (((0,), (0,)), ((), ()))


def _cparams(n_axes, vmem_mib):
    return pltpu.CompilerParams(
        dimension_semantics=("arbitrary",) * n_axes,
        vmem_limit_bytes=min(vmem_mib * 1024 * 1024, V7X_VMEM_BYTES - 4 * 1024 * 1024),
    )


def _const_spec(shape):
    zeros = (0,) * len(shape)
    return pl.BlockSpec(shape, lambda *_: zeros)


def _dot(a, b):
    return jnp.dot(a, b, preferred_element_type=F32)


def _rms(x, g, eps):
    return x * lax.rsqrt(jnp.mean(x * x, axis=-1, keepdims=True) + eps) * g


def _split2(x):
    hi = x.astype(BF16)
    lo = (x - hi.astype(F32)).astype(BF16)
    return hi, lo


def _seg_sum(x, ones_bd, pieces=2):
    outs = []
    for s in range(x.shape[-1] // SLAB_W):
        slab = x[:, s * SLAB_W:(s + 1) * SLAB_W]
        if pieces == 1:
            outs.append(_dot(slab.astype(BF16), ones_bd))
        else:
            hi, lo = _split2(slab)
            outs.append(_dot(hi, ones_bd) + _dot(lo, ones_bd))
    return outs[0] if len(outs) == 1 else jnp.concatenate(outs, axis=-1)


def _block_ones(n, blk, lower=False):
    i = np.arange(n)
    m = (i[:, None] // blk) == (i[None, :] // blk)
    if lower:
        m = m & (i[:, None] >= i[None, :])
    return jnp.asarray(m, dtype=BF16)


def _ffn_kernel(x_ref, g_ref, win_ref, wout_ref, o_ref):
    x = x_ref[...]
    d_ff = wout_ref.shape[1]
    xn = _rms(x, g_ref[...], NORM_EPS).astype(BF16)
    n_chunks = d_ff // FFN_CHUNK

    def first_stage(c):
        lo = c * FFN_CHUNK
        return (_dot(xn, win_ref[0, :, lo:lo + FFN_CHUNK]),
                _dot(xn, win_ref[0, :, d_ff + lo:d_ff + lo + FFN_CHUNK]))

    y = None
    pending = first_stage(0)
    for c in range(n_chunks):
        gate, up = pending
        if c + 1 < n_chunks:
            pending = first_stage(c + 1)
        act = (gate * jax.nn.sigmoid(gate) * up).astype(BF16)
        part = _dot(act, wout_ref[0, c * FFN_CHUNK:(c + 1) * FFN_CHUNK, :])
        y = part if y is None else y + part
    o_ref[...] = x + 0.5 * y


def _ffn(x, g, w_in, w_out, layer, tm=FFN_ROWS):
    n, d = x.shape
    assert w_out.shape[1] % FFN_CHUNK == 0 and n % tm == 0
    return pl.pallas_call(
        _ffn_kernel,
        grid=(n // tm,),
        in_specs=[
            pl.BlockSpec((tm, d), lambda i: (i, 0)),
            _const_spec((1, d)),
            pl.BlockSpec((1,) + w_in.shape[1:], lambda i: (layer, 0, 0), pipeline_mode=pl.Buffered(1)),
            pl.BlockSpec((1,) + w_out.shape[1:], lambda i: (layer, 0, 0), pipeline_mode=pl.Buffered(1)),
        ],
        out_specs=pl.BlockSpec((tm, d), lambda i: (i, 0)),
        out_shape=jax.ShapeDtypeStruct((n, d), F32),
        compiler_params=_cparams(1, 56),
        name="ffn",
    )(x, g, w_in, w_out)


def _memkv_kernel(mem_ref, g_ref, w_ref, kg_ref, ones_ref, k_ref, v_ref):
    m = _rms(mem_ref[0], g_ref[0], NORM_EPS).astype(BF16)
    kv = _dot(m, w_ref[0])
    k = kv[:, :MEM_W]
    ms = _seg_sum(k * k, ones_ref[...]) * (1.0 / HEAD_DIM)
    k_ref[0, 0] = (k * lax.rsqrt(ms + NORM_EPS) * kg_ref[0]).astype(BF16)
    v_ref[0, 0] = kv[:, MEM_W:].astype(BF16)


def _memkv(mem, mem_norm, w_kv, k_gain, ones_bd):
    b, m, d = mem.shape
    depth = w_kv.shape[0]
    out = jax.ShapeDtypeStruct((depth, b, m, MEM_W), BF16)
    return pl.pallas_call(
        _memkv_kernel,
        grid=(depth, b),
        in_specs=[
            pl.BlockSpec((1, m, d), lambda l, i: (i, 0, 0)),
            pl.BlockSpec((1, 1, d), lambda l, i: (l, 0, 0)),
            pl.BlockSpec((1, d, 2 * MEM_W), lambda l, i: (l, 0, 0)),
            pl.BlockSpec((1, 1, MEM_W), lambda l, i: (l, 0, 0)),
            _const_spec((SLAB_W, SLAB_W)),
        ],
        out_specs=[pl.BlockSpec((1, 1, m, MEM_W), lambda l, i: (l, i, 0, 0))] * 2,
        out_shape=[out, out],
        compiler_params=_cparams(2, 32),
        name="memkv",
    )(mem, mem_norm, w_kv, k_gain, ones_bd)


def _head_norm(q, gain, ones_bd):
    ms = _seg_sum(q * q, ones_bd, pieces=1) * (1.0 / HEAD_DIM)
    return q * lax.rsqrt(ms + NORM_EPS) * gain


def _aproj_kernel(x_ref, g_ref, w_ref, mu_ref, qg_ref, ones_ref, ps_ref, qm_ref, carry_ref, *, tiles_per_seq):
    i = pl.program_id(0)
    tm = x_ref.shape[0]
    u = _rms(x_ref[...], g_ref[...], NORM_EPS).astype(BF16)

    @pl.when(i % tiles_per_seq == 0)
    def _():
        carry_ref[...] = jnp.zeros_like(carry_ref)

    starts = range(0, tm, PROJ_SUB_ROWS)
    projs = [_dot(u[r0:r0 + PROJ_SUB_ROWS], w_ref[0]) for r0 in starts]
    row = lax.broadcasted_iota(jnp.int32, (PROJ_SUB_ROWS, RWKV_SHIFT_W), 0)
    last = carry_ref[V7X_SUBLANES - 1:V7X_SUBLANES, :]
    for r0, proj in zip(starts, projs):
        p = proj[:, :RWKV_SHIFT_W]
        prev = jnp.where(row == 0, last, pltpu.roll(p, 1, 0))
        ps_ref[r0:r0 + PROJ_SUB_ROWS, :] = p + mu_ref[...] * (prev - p)
        last = p[PROJ_SUB_ROWS - 1:, :]
        qm_ref[r0:r0 + PROJ_SUB_ROWS, :] = _head_norm(proj[:, RWKV_SHIFT_W:], qg_ref[...],
                                                      ones_ref[...]).astype(qm_ref.dtype)
    carry_ref[...] = projs[-1][PROJ_SUB_ROWS - V7X_SUBLANES:, :RWKV_SHIFT_W]


def _aproj(x, g, w_in, layer, mu, q_gain, ones_bd, seq_len, tm=APROJ_ROWS):
    n, d = x.shape
    return pl.pallas_call(
        functools.partial(_aproj_kernel, tiles_per_seq=seq_len // tm),
        grid=(n // tm,),
        in_specs=[
            pl.BlockSpec((tm, d), lambda i: (i, 0)),
            _const_spec((1, d)),
            pl.BlockSpec((1,) + w_in.shape[1:], lambda i: (layer, 0, 0), pipeline_mode=pl.Buffered(1)),
            _const_spec((1, RWKV_SHIFT_W)),
            _const_spec((1, MEM_W)),
            _const_spec((SLAB_W, SLAB_W)),
        ],
        out_specs=[pl.BlockSpec((tm, RWKV_SHIFT_W), lambda i: (i, 0)),
                   pl.BlockSpec((tm, MEM_W), lambda i: (i, 0))],
        out_shape=[jax.ShapeDtypeStruct((n, RWKV_SHIFT_W), F32),
                   jax.ShapeDtypeStruct((n, MEM_W), BF16)],
        scratch_shapes=[pltpu.VMEM((V7X_SUBLANES, RWKV_SHIFT_W), F32)],
        compiler_params=_cparams(1, 48),
        name="aproj",
    )(x, g, w_in, mu, q_gain, ones_bd)


def _qproj_kernel(x_ref, g_ref, w_ref, qg_ref, ones_ref, slab_ref, *rest, normed_w):
    stage_ref = rest[-1]
    flat_refs = rest[:-1]
    tm = x_ref.shape[0]
    u = _rms(x_ref[...], g_ref[...], NORM_EPS).astype(BF16)
    proj = _dot(u, w_ref[...])
    normed = _head_norm(proj[:, :normed_w], qg_ref[...], ones_ref[...])
    n_slabs = slab_ref.shape[1]
    for j in range(n_slabs):
        src = normed if (j + 1) * V7X_LANES <= normed_w else proj
        stage_ref[j] = src[:, j * V7X_LANES:(j + 1) * V7X_LANES]
    for j in range(n_slabs):
        for c in range(DIL_ROW_PHASES):
            slab_ref[0, j, c] = stage_ref[j, pl.ds(c, tm // DIL_ROW_PHASES, stride=DIL_ROW_PHASES), :]
    if flat_refs:
        flat_refs[0][...] = normed[:, n_slabs * V7X_LANES:].astype(flat_refs[0].dtype)


def _qproj(x, g, w, q_gain, ones_bd, seq_len, slab_w, tm=QPROJ_ROWS):
    n, d = x.shape
    wo = w.shape[1]
    normed_w = q_gain.shape[-1]
    n_slabs = slab_w // V7X_LANES
    tiles_per_seq = seq_len // tm
    phase_rows = seq_len // DIL_ROW_PHASES
    out_specs = [pl.BlockSpec((1, n_slabs, DIL_ROW_PHASES, tm // DIL_ROW_PHASES, V7X_LANES),
                              lambda i: (i // tiles_per_seq, 0, 0, i % tiles_per_seq, 0))]
    out_shape = [jax.ShapeDtypeStruct((n // seq_len, n_slabs, DIL_ROW_PHASES, phase_rows, V7X_LANES), F32)]
    if wo > slab_w:
        assert normed_w == wo
        out_specs.append(pl.BlockSpec((tm, wo - slab_w), lambda i: (i, 0)))
        out_shape.append(jax.ShapeDtypeStruct((n, wo - slab_w), BF16))
    outs = list(pl.pallas_call(
        functools.partial(_qproj_kernel, normed_w=normed_w),
        grid=(n // tm,),
        in_specs=[
            pl.BlockSpec((tm, d), lambda i: (i, 0)),
            _const_spec((1, d)),
            pl.BlockSpec(w.shape, lambda i: (0, 0), pipeline_mode=pl.Buffered(1)),
            _const_spec((1, normed_w)),
            _const_spec((SLAB_W, SLAB_W)),
        ],
        out_specs=out_specs,
        out_shape=out_shape,
        scratch_shapes=[pltpu.VMEM((n_slabs, tm, V7X_LANES), F32)],
        compiler_params=_cparams(1, 48),
        name="qproj",
    )(x, g, w, q_gain, ones_bd))
    outs[0] = outs[0].reshape(n // seq_len, n_slabs, seq_len, V7X_LANES)
    return outs


def _embed(x):
    head0 = lax.broadcasted_iota(jnp.int32, x.shape, 1) < HEAD_DIM
    zero = jnp.zeros_like(x)
    return jnp.concatenate([jnp.where(head0, x, zero), jnp.where(head0, zero, x)], axis=0)


def _tri_inverse(nmats, tpos, spos):
    eye = (tpos == spos).astype(F32)
    same = {s: (tpos >> s) == (spos >> s) for s in (3, 4, 5)}

    def mm(lhs, rhs):
        return [_dot(a, _embed(b)) for a, b in zip(lhs, rhs)]

    def bf(xs):
        return [x.astype(BF16) for x in xs]

    n8 = [jnp.where(same[3], n, 0.0) for n in nmats]
    n8b = bf(n8)
    n2 = mm(n8b, n8b)
    n2b = bf(n2)
    n4 = mm(n2b, n2b)
    t = mm(bf([eye + a for a in n8]), bf([eye + a for a in n2]))
    t = mm(bf(t), bf([eye + a for a in n4]))
    for lo, hi in ((3, 4), (4, 5), (5, None)):
        off = ~same[lo] if hi is None else (same[hi] & ~same[lo])
        tb = bf(t)
        z = mm(bf([jnp.where(off, n, 0.0) for n in nmats]), tb)
        t = [a + d for a, d in zip(t, mm(tb, bf(z)))]
    return t


def _rwkv_kernel(ps_ref, vec_ref, wup_ref, aup_ref, gup_ref, tri_ref, ones_ref, y_ref,
                 s_ref, rt_ref, kt_ref, bt_ref, at_ref, kh_ref, bh_ref, v_ref, gam_ref, yacc_ref,
                 bonus_ref, gate_ref, p_s, qt_s, rp_s, y0_s):
    n_seqs, seq_rows = ps_ref.shape[1], ps_ref.shape[2]
    tblk = n_seqs * seq_rows
    n_pairs = RWKV_W // PAIR_W

    @pl.when(pl.program_id(1) == 0)
    def _():
        s_ref[...] = jnp.zeros_like(s_ref)

    ps = jnp.concatenate([ps_ref[0, q] for q in range(n_seqs)], axis=0)
    r = ps[:, :RWKV_W]
    k = ps[:, RWKV_W:2 * RWKV_W]
    v = ps[:, 2 * RWKV_W:3 * RWKV_W]
    lora_in = ps[:, LORA_LO:LORA_LO + PAIR_W]
    g_lo = ps[:, LORA_LO + PAIR_W:]
    w0, a0, kk_scale, k_a = vec_ref[0:1, :], vec_ref[1:2, :], vec_ref[2:3, :], vec_ref[3:4, :]
    r_k, lnx_g, lnx_b = vec_ref[4:5, :], vec_ref[5:6, :], vec_ref[6:7, :]
    ones_bd = ones_ref[...]

    nz = -(w0 + _dot(jnp.tanh(lora_in).astype(BF16), wup_ref[...]))
    softplus = jnp.maximum(nz, 0.0) + jnp.log(1.0 + jnp.exp(-jnp.abs(nz)))
    lw = -jnp.exp(-softplus - 0.5)
    a = jax.nn.sigmoid(a0 + _dot(lora_in.astype(BF16), aup_ref[...]))
    gate_ref[...] = _dot(jax.nn.sigmoid(g_lo).astype(BF16), gup_ref[...])
    kk = k * kk_scale
    kk = kk / jnp.maximum(jnp.sqrt(_seg_sum(kk * kk, ones_bd, pieces=1)), 1e-12)
    k2 = k * (1.0 + (a - 1.0) * k_a)
    kka = kk * a
    bonus_ref[...] = _seg_sum(r * k2 * r_k, ones_bd) * v

    hi, lo = _split2(lw)
    tri = tri_ref[...]
    gcum = jnp.concatenate(
        [_dot(tri, hi[s0:s0 + RWKV_CUMSUM_ROWS]) + _dot(tri, lo[s0:s0 + RWKV_CUMSUM_ROWS])
         for s0 in range(0, tblk, RWKV_CUMSUM_ROWS)], axis=0)
    n_chunks = tblk // RWKV_CHUNK
    gam_rows = [jnp.exp(gcum[(c + 1) * RWKV_CHUNK - 1:(c + 1) * RWKV_CHUNK, :]) for c in range(n_chunks)]
    gam = jnp.concatenate([jnp.broadcast_to(g, (RWKV_CHUNK, RWKV_W)) for g in gam_rows], axis=0)
    for c in range(n_chunks):
        gam_ref[c:c + 1, :] = gam_rows[c]
    e_neg = jnp.exp(-gcum)
    e_last = gam * e_neg
    rt_ref[...] = (r * jnp.exp(gcum)).astype(BF16)
    kt_ref[...] = (k2 * e_neg).astype(BF16)
    bt_ref[...] = (kka * e_neg).astype(BF16)
    at_ref[...] = (-kk * jnp.exp(gcum - lw)).astype(BF16)
    kh_ref[...] = (k2 * e_last).astype(BF16)
    bh_ref[...] = (kka * e_last).astype(BF16)
    v_ref[...] = v.astype(BF16)

    tpos = lax.broadcasted_iota(jnp.int32, (RWKV_CHUNK, PAIR_W), 0)
    spos = lax.broadcasted_iota(jnp.int32, (RWKV_CHUNK, PAIR_W), 1) & (HEAD_DIM - 1)
    strict = tpos > spos
    incl = tpos >= spos
    head0 = lax.broadcasted_iota(jnp.int32, (RWKV_CHUNK, PAIR_W), 1) < HEAD_DIM
    same_head = ((lax.broadcasted_iota(jnp.int32, (PAIR_W, PAIR_W), 0) < HEAD_DIM)
                 == (lax.broadcasted_iota(jnp.int32, (PAIR_W, PAIR_W), 1) < HEAD_DIM))

    pairs = range(n_pairs)
    lanes = [slice(p * PAIR_W, (p + 1) * PAIR_W) for p in pairs]

    def phase_a(i, carry):
        slots, at_v, rt_v, bt_v, kt_v, v_v, bh_v, kh_v = [], [], [], [], [], [], [], []
        for cc in range(RWKV_PHASE_A_CHUNKS):
            c = i * RWKV_PHASE_A_CHUNKS + cc
            rows = pl.ds(pl.multiple_of(c * RWKV_CHUNK, RWKV_CHUNK), RWKV_CHUNK)
            loaded = [ref[rows, :] for ref in (at_ref, rt_ref, bt_ref, kt_ref, v_ref, bh_ref, kh_ref)]
            for p in pairs:
                slots.append(c * n_pairs + p)
                for dst, x_c in zip((at_v, rt_v, bt_v, kt_v, v_v, bh_v, kh_v), loaded):
                    dst.append(x_c[:, lanes[p]])
        chains = range(len(slots))
        m4 = [lax.dot_general(jnp.concatenate([at_v[j], rt_v[j]], axis=0),
                              jnp.concatenate([_embed(bt_v[j]), _embed(kt_v[j])], axis=0), _NT,
                              preferred_element_type=F32) for j in chains]
        n_ab = [jnp.where(strict, m[:RWKV_CHUNK, :PAIR_W], 0.0) for m in m4]
        a_ak = [jnp.where(strict, m[:RWKV_CHUNK, PAIR_W:], 0.0).astype(BF16) for m in m4]
        a_rb = [jnp.where(incl, m[RWKV_CHUNK:, :PAIR_W], 0.0).astype(BF16) for m in m4]
        a_rk = [jnp.where(incl, m[RWKV_CHUNK:, PAIR_W:], 0.0).astype(BF16) for m in m4]
        v_m = [_embed(x) for x in v_v]
        akv = [_dot(a_ak[j], v_m[j]).astype(BF16) for j in chains]
        t_inv = [t.astype(BF16) for t in _tri_inverse(n_ab, tpos, spos)]
        wu0 = [_dot(t_inv[j], jnp.concatenate([_embed(at_v[j]), _embed(akv[j])], axis=1))
               for j in chains]
        w_b = [x[:, :PAIR_W].astype(BF16) for x in wu0]
        u0_b = [x[:, PAIR_W:].astype(BF16) for x in wu0]
        p_m = [jnp.where(same_head, lax.dot_general(bh_v[j], w_b[j], _TN, preferred_element_type=F32), 0.0)
               for j in chains]
        q_full = [lax.dot_general(jnp.concatenate([u0_b[j], v_v[j]], axis=0),
                                  jnp.concatenate([bh_v[j], kh_v[j]], axis=0), _TN,
                                  preferred_element_type=F32) for j in chains]
        q_t = [jnp.where(head0, q[:RWKV_CHUNK], q[RWKV_CHUNK:]) for q in q_full]
        r_p = [rt_v[j].astype(F32) + _dot(a_rb[j], _embed(w_b[j])) for j in chains]
        y_0 = [_dot(jnp.concatenate([a_rb[j], a_rk[j]], axis=1),
                    jnp.concatenate([_embed(u0_b[j]), v_m[j]], axis=0)) for j in chains]
        for j in chains:
            p_s[slots[j]] = p_m[j].astype(BF16)
            qt_s[slots[j]] = q_t[j]
            rp_s[slots[j]] = r_p[j].astype(BF16)
            y0_s[slots[j]] = y_0[j]
        return carry

    lax.fori_loop(0, n_chunks // RWKV_PHASE_A_CHUNKS, phase_a, 0)

    seq_chunks = seq_rows // RWKV_CHUNK
    heads = [(q, p) for q in range(n_seqs) for p in pairs]
    states = [s_ref[q * n_pairs + p] for q, p in heads]
    for c in range(seq_chunks):
        chunk = [q * seq_chunks + c for q, _ in heads]
        slots = [ch * n_pairs + p for ch, (_, p) in zip(chunk, heads)]
        gam_c = [gam_ref[ch:ch + 1, lanes[p]] for ch, (_, p) in zip(chunk, heads)]
        state_b = [s.astype(BF16) for s in states]
        upd = [lax.dot_general(sb, p_s[j], _NT, preferred_element_type=F32) for sb, j in zip(state_b, slots)]
        y_c = [lax.dot_general(rp_s[j], _embed(sb), _NT, preferred_element_type=F32) + y0_s[j]
               for sb, j in zip(state_b, slots)]
        states = [s * g + u + qt_s[j] for s, g, u, j in zip(states, gam_c, upd, slots)]
        for q in range(n_seqs):
            r0 = (q * seq_chunks + c) * RWKV_CHUNK
            yacc_ref[r0:r0 + RWKV_CHUNK, :] = jnp.concatenate(y_c[q * n_pairs:(q + 1) * n_pairs], axis=1)
    for i, s in enumerate(states):
        s_ref[i] = s

    y = yacc_ref[...]
    mean = _seg_sum(y, ones_bd) * (1.0 / HEAD_DIM)
    dev = y - mean
    var = _seg_sum(dev * dev, ones_bd, pieces=1) * (1.0 / HEAD_DIM)
    yn = dev * lax.rsqrt(var + LNX_EPS) * lnx_g + lnx_b
    out = ((yn + bonus_ref[...]) * gate_ref[...]).astype(y_ref.dtype)
    for q in range(n_seqs):
        y_ref[0, q] = out[q * seq_rows:(q + 1) * seq_rows]


def _rwkv(ps, vecs, w_up_p, a_up_p, g_up, ones_bd, seq_rows=RWKV_ROWS):
    b, s, _ = ps.shape
    n_seqs = RWKV_SEQS_PER_STEP
    tblk = n_seqs * seq_rows
    tri = _block_ones(RWKV_CUMSUM_ROWS, RWKV_CHUNK, lower=True)
    act = pltpu.VMEM((tblk, RWKV_W), F32)
    act_b = pltpu.VMEM((tblk, RWKV_W), BF16)
    n_pairs = RWKV_W // PAIR_W
    n_chunks = tblk // RWKV_CHUNK
    n_slots = n_chunks * n_pairs
    gam_rows = -(-n_chunks // V7X_SUBLANES) * V7X_SUBLANES
    scratch = [pltpu.VMEM((n_seqs * n_pairs, HEAD_DIM, PAIR_W), F32)]
    scratch += [act_b] * 7
    scratch += [pltpu.VMEM((gam_rows, RWKV_W), F32)] + [act] * 3
    scratch += [
        pltpu.VMEM((n_slots, PAIR_W, PAIR_W), BF16),
        pltpu.VMEM((n_slots, RWKV_CHUNK, PAIR_W), F32),
        pltpu.VMEM((n_slots, RWKV_CHUNK, PAIR_W), BF16),
        pltpu.VMEM((n_slots, RWKV_CHUNK, PAIR_W), F32),
    ]
    y = pl.pallas_call(
        _rwkv_kernel,
        grid=(b // n_seqs, s // seq_rows),
        in_specs=[
            pl.BlockSpec((1, n_seqs, seq_rows, RWKV_SHIFT_W), lambda i, t: (i, 0, t, 0)),
            _const_spec(vecs.shape),
            _const_spec(w_up_p.shape),
            _const_spec(a_up_p.shape),
            _const_spec(g_up.shape),
            _const_spec(tri.shape),
            _const_spec((SLAB_W, SLAB_W)),
        ],
        out_specs=pl.BlockSpec((1, n_seqs, seq_rows, RWKV_W), lambda i, t: (i, 0, t, 0)),
        out_shape=jax.ShapeDtypeStruct((b // n_seqs, n_seqs, s, RWKV_W), BF16),
        scratch_shapes=scratch,
        compiler_params=_cparams(2, 48),
        name="rwkv",
    )(ps.reshape(b // n_seqs, n_seqs, s, RWKV_SHIFT_W), vecs, w_up_p, a_up_p, g_up, tri, ones_bd)
    return y.reshape(b, s, RWKV_W)


def _mixout_kernel(x_ref, ya_ref, qm_ref, k_ref, v_ref, w_ref, o_ref):
    if len(ya_ref.shape) == 4:
        ya = jnp.concatenate([ya_ref[0, j] for j in range(ya_ref.shape[1])], axis=1)
    else:
        ya = ya_ref[...]
    wa = ya.shape[-1]
    qm = qm_ref[...]
    kmem = k_ref[0]
    vmem = v_ref[0]
    head = lax.broadcasted_iota(jnp.int32, qm.shape, 1) // HEAD_DIM
    heads = range(MEM_HEADS)
    logits = [lax.dot_general(jnp.where(head == h, qm, jnp.zeros_like(qm)), kmem, _NT,
                              preferred_element_type=F32) for h in heads]
    p = [jnp.exp(lg - jnp.max(lg, axis=-1, keepdims=True)) for lg in logits]
    inv_l = [1.0 / jnp.sum(ph, axis=-1, keepdims=True) for ph in p]
    pv = [_dot(ph.astype(BF16), vmem) for ph in p]
    y_mem = pv[0] * inv_l[0]
    for h in heads[1:]:
        y_mem = jnp.where(head == h, pv[h] * inv_l[h], y_mem)
    y = _dot(ya.astype(BF16), w_ref[:wa, :]) + _dot(y_mem.astype(BF16), w_ref[wa:, :])
    o_ref[...] = x_ref[...] + y


def _mixout(x, ya, qm, k_mem, v_mem, w_out, seq_len, tm=MIXOUT_ROWS):
    n, d = x.shape
    tiles_per_seq = seq_len // tm
    m = k_mem.shape[1]
    if ya.ndim == 4:
        ya_spec = pl.BlockSpec((1, ya.shape[1], tm, V7X_LANES),
                               lambda i: (i // tiles_per_seq, 0, i % tiles_per_seq, 0))
    else:
        ya_spec = pl.BlockSpec((tm, ya.shape[-1]), lambda i: (i, 0))
    return pl.pallas_call(
        _mixout_kernel,
        grid=(n // tm,),
        in_specs=[
            pl.BlockSpec((tm, d), lambda i: (i, 0)),
            ya_spec,
            pl.BlockSpec((tm, MEM_W), lambda i: (i, 0)),
            pl.BlockSpec((1, m, MEM_W), lambda i: (i // tiles_per_seq, 0, 0)),
            pl.BlockSpec((1, m, MEM_W), lambda i: (i // tiles_per_seq, 0, 0)),
            _const_spec(w_out.shape),
        ],
        out_specs=pl.BlockSpec((tm, d), lambda i: (i, 0)),
        out_shape=jax.ShapeDtypeStruct((n, d), F32),
        compiler_params=_cparams(1, 40),
        name="mixout",
    )(x, ya, qm, k_mem, v_mem, w_out)


def _t5_bucket(dist):
    max_exact = NUM_BUCKETS // 2
    d_f = jnp.maximum(dist, 1).astype(F32)
    large = max_exact + (jnp.log(d_f / max_exact) / math.log(MAX_DISTANCE / max_exact)
                         * (NUM_BUCKETS - max_exact)).astype(jnp.int32)
    large = jnp.minimum(large, NUM_BUCKETS - 1)
    return jnp.where(dist < max_exact, dist, large)


def _band_buckets():
    run = BLOCK // DIL_ROW_PHASES
    out = []
    for window, dil in DIL_GROUPS:
        rows_q = np.arange(BLOCK)
        rows_k = np.arange(2 * BLOCK)
        if dil < DIL_ROW_PHASES:
            u_q = DIL_ROW_PHASES * (rows_q % run) + rows_q // run
            u_k = ((rows_k % (2 * run)) // run) * BLOCK + DIL_ROW_PHASES * (rows_k % run) + rows_k // (2 * run)
        else:
            u_q, u_k = rows_q, rows_k
        dsub = jnp.asarray(BLOCK + u_q[:, None] - u_k[None, :])
        band = (dsub >= 0) & (dsub <= window // dil)
        idx = jnp.where(band, _t5_bucket(jnp.maximum(dsub, 0) * dil), -1)
        out.append(jnp.stack([idx, jnp.where(jnp.asarray(u_k < BLOCK)[None, :], -1, idx)]))
    return jnp.stack(out, axis=1).astype(jnp.int32)


def _bias_kernel(tab_ref, idx_ref, o_ref):
    head = pl.program_id(1)
    idx = idx_ref[0, 0]
    acc = jnp.full(idx.shape, NEG_INF, F32)
    for bucket in range(NUM_BUCKETS):
        acc = jnp.where(idx == bucket, tab_ref[bucket, head], acc)
    o_ref[0, 0] = acc


def _band_bias(rel_bias):
    n_heads = rel_bias.shape[1]
    return pl.pallas_call(
        _bias_kernel,
        grid=(2, n_heads),
        in_specs=[
            pl.BlockSpec(memory_space=pltpu.SMEM),
            pl.BlockSpec((1, 1, BLOCK, 2 * BLOCK), lambda v, h: (v, h // DIL_GROUP_HEADS, 0, 0)),
        ],
        out_specs=pl.BlockSpec((1, 1, BLOCK, 2 * BLOCK), lambda v, h: (v, h, 0, 0)),
        out_shape=jax.ShapeDtypeStruct((2, n_heads, BLOCK, 2 * BLOCK), F32),
        compiler_params=_cparams(2, 16),
        name="band_bias",
    )(rel_bias, _band_buckets())


def _dil_group(dil, q_ref, k_ref, v_ref, bias_ref, acc_ref, m_ref, l_ref):
    seq_len = q_ref.shape[2]
    phase_rows = seq_len // DIL_ROW_PHASES
    run = BLOCK // DIL_ROW_PHASES
    head0 = lax.broadcasted_iota(jnp.int32, (BLOCK, PAIR_W), 1) < HEAD_DIM

    def windows(t):
        if dil < DIL_ROW_PHASES:
            blk = t
            back = jnp.maximum(blk - 1, 0)
            own = [pl.ds(pl.multiple_of(c * phase_rows + run * blk, run), run) for c in range(DIL_ROW_PHASES)]
            prev = [pl.ds(pl.multiple_of(c * phase_rows + run * back, run), run) for c in range(DIL_ROW_PHASES)]
            return own, prev, blk == 0
        step = dil // DIL_ROW_PHASES
        n_blocks = phase_rows // (step * BLOCK)
        sub = t >> (n_blocks.bit_length() - 1)
        blk = t & (n_blocks - 1)
        base = (sub & (DIL_ROW_PHASES - 1)) * phase_rows + (sub >> (DIL_ROW_PHASES.bit_length() - 1))
        back = jnp.maximum(blk - 1, 0)
        if step == 1:
            own = [pl.ds(pl.multiple_of(base + BLOCK * blk, BLOCK), BLOCK)]
            prev = [pl.ds(pl.multiple_of(base + BLOCK * back, BLOCK), BLOCK)]
        else:
            own = [pl.ds(base + step * BLOCK * blk, BLOCK, stride=step)]
            prev = [pl.ds(base + step * BLOCK * back, BLOCK, stride=step)]
        return own, prev, blk == 0

    def gather(get, wins):
        parts = [get(w) for w in wins]
        return parts[0] if len(parts) == 1 else jnp.concatenate(parts, axis=0)

    def scatter(put, wins, value):
        rows = BLOCK // len(wins)
        for i, w in enumerate(wins):
            put(w, value[i * rows:(i + 1) * rows])

    def tiles(i, carry):
        slabs = []
        for j in range(DIL_TILES_PER_ITER):
            own, prev, first = windows(i * DIL_TILES_PER_ITER + j)
            key_wins = [w for pw, ow in zip(prev, own) for w in (pw, ow)]
            variant = first.astype(jnp.int32)
            slabs += [(pair, own, key_wins, variant) for pair in range(DIL_OUT_W // PAIR_W)]
        q = [gather(lambda w: q_ref[0, pair, w, :], own) for pair, own, _, _ in slabs]
        keys = [gather(lambda w: k_ref[0, pair, w, :], kw).astype(BF16) for pair, _, kw, _ in slabs]
        vals = [gather(lambda w: v_ref[0, pair, w, :], kw).astype(BF16) for pair, _, kw, _ in slabs]
        m_old = [gather(lambda w: m_ref[pair, w, :], own) for pair, own, _, _ in slabs]
        l_old = [gather(lambda w: l_ref[pair, w, :], own) for pair, own, _, _ in slabs]
        acc_old = [gather(lambda w: acc_ref[pair, w, :], own) for pair, own, _, _ in slabs]
        heads = [(s, h) for s in range(len(slabs)) for h in range(2)]
        logits = [lax.dot_general(jnp.where(head0 if h == 0 else ~head0, q[s], 0.0).astype(BF16),
                                  keys[s], _NT, preferred_element_type=F32) for s, h in heads]
        logits = [lg + bias_ref[slabs[s][3], 2 * slabs[s][0] + h] for lg, (s, h) in zip(logits, heads)]
        mx = [jnp.max(lg, axis=-1, keepdims=True) for lg in logits]
        p = [jnp.exp(lg - m) for lg, m in zip(logits, mx)]
        ls = [jnp.sum(ph, axis=-1, keepdims=True) for ph in p]
        pv = [_dot(ph.astype(BF16), vals[s]) for ph, (s, h) in zip(p, heads)]
        for s, (pair, own, _, _) in enumerate(slabs):
            m_t = jnp.where(head0, mx[2 * s], mx[2 * s + 1])
            l_t = jnp.where(head0, ls[2 * s], ls[2 * s + 1])
            acc_t = jnp.where(head0, pv[2 * s], pv[2 * s + 1])
            m_new = jnp.maximum(m_old[s], m_t)
            e_old = jnp.exp(m_old[s] - m_new)
            e_t = jnp.exp(m_t - m_new)

            def put(ref, value, pair=pair, own=own):
                def store(w, rows):
                    ref[pair, w, :] = rows
                scatter(store, own, value)

            put(m_ref, m_new)
            put(l_ref, l_old[s] * e_old + l_t * e_t)
            put(acc_ref, acc_old[s] * e_old + acc_t * e_t)
        return carry

    lax.fori_loop(0, seq_len // BLOCK // DIL_TILES_PER_ITER, tiles, 0)


def _dil_kernel(q_ref, k_ref, v_ref, bias_ref, o_ref, acc_ref, m_ref, l_ref):
    g = pl.program_id(1)

    @pl.when(g == 0)
    def _():
        m_ref[...] = jnp.full(m_ref.shape, NEG_INF, F32)
        l_ref[...] = jnp.zeros_like(l_ref)
        acc_ref[...] = jnp.zeros_like(acc_ref)

    for gi, (_, dil) in enumerate(DIL_GROUPS):
        pl.when(g == gi)(functools.partial(_dil_group, dil, q_ref, k_ref, v_ref, bias_ref, acc_ref, m_ref, l_ref))

    @pl.when(g == len(DIL_GROUPS) - 1)
    def _():
        phase_rows = o_ref.shape[2] // DIL_ROW_PHASES
        for pair in range(DIL_OUT_W // PAIR_W):
            for c in range(DIL_ROW_PHASES):
                rows = slice(c * phase_rows, (c + 1) * phase_rows)
                o_ref[0, pair, pl.ds(c, phase_rows, stride=DIL_ROW_PHASES), :] = (
                    acc_ref[pair, rows, :] / l_ref[pair, rows, :])


def _dilated_attention(q_slabs, kv_slabs, bias):
    b, _, s, _ = q_slabs.shape
    n_groups = len(DIL_GROUPS)
    pairs = DIL_OUT_W // PAIR_W
    blk = (1, pairs, s, PAIR_W)
    return pl.pallas_call(
        _dil_kernel,
        grid=(b, n_groups),
        in_specs=[
            pl.BlockSpec(blk, lambda i, g: (i, g, 0, 0)),
            pl.BlockSpec(blk, lambda i, g: (i, g, 0, 0)),
            pl.BlockSpec(blk, lambda i, g: (i, n_groups + g, 0, 0)),
            pl.BlockSpec((2, DIL_GROUP_HEADS, BLOCK, 2 * BLOCK), lambda i, g: (0, g, 0, 0)),
        ],
        out_specs=pl.BlockSpec(blk, lambda i, g: (i, 0, 0, 0)),
        out_shape=jax.ShapeDtypeStruct((b, pairs, s, PAIR_W), F32),
        scratch_shapes=[pltpu.VMEM((pairs, s, PAIR_W), F32)] * 3,
        compiler_params=_cparams(2, 56),
        name="dilated_attention",
    )(q_slabs, kv_slabs, kv_slabs, bias)


def kernel(x, mem, ffn_pre_norm, ffn_pre_w_in, ffn_pre_w_out, mix_norm, ffn_post_norm, ffn_post_w_in, ffn_post_w_out, mem_norm, mem_w_kv, mem_q_norm, mem_k_norm, a_w_in, a_shift_mu, a_w0, a_w_up, a_a0, a_a_up, a_g_up, a_kk_scale, a_k_a, a_r_k, a_lnx_g, a_lnx_b, a_w_out, b_w_q, b_q_norm, b_w_out, kv_norm, kv_w, kv_k_norm, rel_bias):
    b, s, d = x.shape
    depth = ffn_pre_w_in.shape[0]
    n_a = a_w_in.shape[0]
    n = b * s
    scale = 1.0 / math.sqrt(HEAD_DIM)
    ones_bd = _block_ones(SLAB_W, HEAD_DIM)
    row = lambda p: p.reshape(1, -1)

    k_mem, v_mem = _memkv(mem, mem_norm[:, None, :], mem_w_kv.astype(BF16),
                          jnp.tile(mem_k_norm, (1, MEM_HEADS))[:, None, :], ones_bd)
    mem_q_gain = jnp.tile(mem_q_norm, (1, MEM_HEADS)) * scale

    pre_w_in, pre_w_out = ffn_pre_w_in.astype(BF16), ffn_pre_w_out.astype(BF16)
    post_w_in, post_w_out = ffn_post_w_in.astype(BF16), ffn_post_w_out.astype(BF16)
    a_w_in_b = a_w_in.astype(BF16)
    xf = x.reshape(n, d)
    kv = None
    bias = None
    for layer in range(depth):
        xf = _ffn(xf, row(ffn_pre_norm[layer]), pre_w_in, pre_w_out, layer)
        if layer < n_a:
            i = layer
            ps, qm = _aproj(xf, row(mix_norm[layer]), a_w_in_b, i, row(a_shift_mu[i]),
                            row(mem_q_gain[layer]), ones_bd, s)
            zeros = jnp.zeros((DECAY_LORA, RWKV_W), F32)
            vecs = jnp.stack([a_w0[i], a_a0[i], a_kk_scale[i], a_k_a[i], a_r_k[i].reshape(-1),
                              a_lnx_g[i], a_lnx_b[i], jnp.zeros((RWKV_W,), F32)])
            y_main = _rwkv(ps.reshape(b, s, RWKV_SHIFT_W), vecs,
                           jnp.concatenate([a_w_up[i], zeros]).astype(BF16),
                           jnp.concatenate([zeros, a_a_up[i]]).astype(BF16),
                           a_g_up[i].astype(BF16), ones_bd)
            xf = _mixout(xf, y_main.reshape(n, RWKV_W), qm, k_mem[layer], v_mem[layer],
                         a_w_out[i].astype(BF16), s)
        else:
            j = layer - n_a
            q_gain = jnp.concatenate([jnp.tile(b_q_norm[j], DIL_W // HEAD_DIM) * scale, mem_q_gain[layer]])
            q_dil, qm = _qproj(xf, row(mix_norm[layer]), b_w_q[j].astype(BF16), row(q_gain), ones_bd, s, DIL_W)
            if bias is None:
                bias = _band_bias(rel_bias)
            y_dil = _dilated_attention(q_dil, kv, bias)
            xf = _mixout(xf, y_dil, qm, k_mem[layer], v_mem[layer],
                         b_w_out[j].astype(BF16), s)
        xf = _ffn(xf, row(ffn_post_norm[layer]), post_w_in, post_w_out, layer)
        if layer == n_a - 1:
            k_gain = jnp.tile(kv_k_norm, DIL_W // HEAD_DIM)
            kv, = _qproj(xf, row(kv_norm), kv_w.astype(BF16), row(k_gain), ones_bd, s, 2 * DIL_W)
    return xf.reshape(b, s, d)
```

```python
import functools
import math

import jax
import jax.numpy as jnp
import numpy as np
from jax import lax
from jax.experimental import pallas as pl
from jax.experimental.pallas import tpu as pltpu

F32 = jnp.float32
BF16 = jnp.bfloat16

HEAD_DIM = 64
MEM_HEADS = 4
MEM_W = MEM_HEADS * HEAD_DIM
RWKV_HEADS = 12
RWKV_W = RWKV_HEADS * HEAD_DIM
DECAY_LORA = 64
AAA_LORA = 64
GATE_LORA = 128
RWKV_SHIFT_W = 3 * RWKV_W + DECAY_LORA + AAA_LORA + GATE_LORA
LORA_LO = 3 * RWKV_W
DIL_GROUPS = ((128, 1), (512, 4), (2048, 16))
DIL_GROUP_HEADS = 4
DIL_W = len(DIL_GROUPS) * DIL_GROUP_HEADS * HEAD_DIM
DIL_OUT_W = DIL_GROUP_HEADS * HEAD_DIM
BLOCK = 128
NUM_BUCKETS = 32
MAX_DISTANCE = 2048
NORM_EPS = 1e-6
LNX_EPS = 64e-5
NEG_INF = -1e30

V7X_LANES = 128
V7X_SUBLANES = 8
V7X_VMEM_BYTES = 64 * 1024 * 1024

RWKV_CHUNK = 64
RWKV_SEQS_PER_STEP = 1
RWKV_PHASE_A_CHUNKS = 8
FFN_ROWS = 1024
FFN_CHUNK = 256
APROJ_ROWS = 1024
PROJ_SUBS_IN_FLIGHT = 4
QPROJ_ROWS = 1024
MIXOUT_ROWS = 1024
RWKV_ROWS = 512
RWKV_CUMSUM_ROWS = 256
DIL_TILES_PER_ITER = 2
DIL_ROW_PHASES = 4
PROJ_SUB_ROWS = 128
PAIR_W = 2 * HEAD_DIM
SLAB_W = 4 * HEAD_DIM

_NT = (((1,), (1,)), ((), ()))
_TN = (((0,), (0,)), ((), ()))


def _cparams(n_axes, vmem_mib):
    return pltpu.CompilerParams(
        dimension_semantics=("arbitrary",) * n_axes,
        vmem_limit_bytes=min(vmem_mib * 1024 * 1024, V7X_VMEM_BYTES - 4 * 1024 * 1024),
    )


def _const_spec(shape):
    zeros = (0,) * len(shape)
    return pl.BlockSpec(shape, lambda *_: zeros)


def _dot(a, b):
    return jnp.dot(a, b, preferred_element_type=F32)


def _rms(x, g, eps):
    return x * lax.rsqrt(jnp.mean(x * x, axis=-1, keepdims=True) + eps) * g


def _split2(x):
    hi = x.astype(BF16)
    lo = (x - hi.astype(F32)).astype(BF16)
    return hi, lo


def _seg_sum(x, ones_bd, pieces=2):
    outs = []
    for s in range(x.shape[-1] // SLAB_W):
        slab = x[:, s * SLAB_W:(s + 1) * SLAB_W]
        if pieces == 1:
            outs.append(_dot(slab.astype(BF16), ones_bd))
        else:
            hi, lo = _split2(slab)
            outs.append(_dot(hi, ones_bd) + _dot(lo, ones_bd))
    return outs[0] if len(outs) == 1 else jnp.concatenate(outs, axis=-1)


def _block_ones(n, blk, lower=False):
    i = np.arange(n)
    m = (i[:, None] // blk) == (i[None, :] // blk)
    if lower:
        m = m & (i[:, None] >= i[None, :])
    return jnp.asarray(m, dtype=BF16)


def _ffn_kernel(x_ref, g_ref, win_ref, wout_ref, o_ref):
    x = x_ref[...]
    d_ff = wout_ref.shape[1]
    xn = _rms(x, g_ref[...], NORM_EPS).astype(BF16)
    n_chunks = d_ff // FFN_CHUNK

    def first_stage(c):
        lo = c * FFN_CHUNK
        return (_dot(xn, win_ref[0, :, lo:lo + FFN_CHUNK]),
                _dot(xn, win_ref[0, :, d_ff + lo:d_ff + lo + FFN_CHUNK]))

    y = None
    pending = first_stage(0)
    for c in range(n_chunks):
        gate, up = pending
        if c + 1 < n_chunks:
            pending = first_stage(c + 1)
        act = (gate * jax.nn.sigmoid(gate) * up).astype(BF16)
        part = _dot(act, wout_ref[0, c * FFN_CHUNK:(c + 1) * FFN_CHUNK, :])
        y = part if y is None else y + part
    o_ref[...] = x + 0.5 * y


def _ffn(x, g, w_in, w_out, layer, tm=FFN_ROWS):
    n, d = x.shape
    assert w_out.shape[1] % FFN_CHUNK == 0 and n % tm == 0
    return pl.pallas_call(
        _ffn_kernel,
        grid=(n // tm,),
        in_specs=[
            pl.BlockSpec((tm, d), lambda i: (i, 0)),
            _const_spec((1, d)),
            pl.BlockSpec((1,) + w_in.shape[1:], lambda i: (layer, 0, 0), pipeline_mode=pl.Buffered(1)),
            pl.BlockSpec((1,) + w_out.shape[1:], lambda i: (layer, 0, 0), pipeline_mode=pl.Buffered(1)),
        ],
        out_specs=pl.BlockSpec((tm, d), lambda i: (i, 0)),
        out_shape=jax.ShapeDtypeStruct((n, d), F32),
        compiler_params=_cparams(1, 56),
        name="ffn",
    )(x, g, w_in, w_out)


def _memkv_kernel(mem_ref, g_ref, w_ref, kg_ref, ones_ref, k_ref, v_ref):
    m = _rms(mem_ref[0], g_ref[0], NORM_EPS).astype(BF16)
    kv = _dot(m, w_ref[0])
    k = kv[:, :MEM_W]
    ms = _seg_sum(k * k, ones_ref[...]) * (1.0 / HEAD_DIM)
    k_ref[0, 0] = (k * lax.rsqrt(ms + NORM_EPS) * kg_ref[0]).astype(BF16)
    v_ref[0, 0] = kv[:, MEM_W:].astype(BF16)


def _memkv(mem, mem_norm, w_kv, k_gain, ones_bd):
    b, m, d = mem.shape
    depth = w_kv.shape[0]
    out = jax.ShapeDtypeStruct((depth, b, m, MEM_W), BF16)
    return pl.pallas_call(
        _memkv_kernel,
        grid=(depth, b),
        in_specs=[
            pl.BlockSpec((1, m, d), lambda l, i: (i, 0, 0)),
            pl.BlockSpec((1, 1, d), lambda l, i: (l, 0, 0)),
            pl.BlockSpec((1, d, 2 * MEM_W), lambda l, i: (l, 0, 0)),
            pl.BlockSpec((1, 1, MEM_W), lambda l, i: (l, 0, 0)),
            _const_spec((SLAB_W, SLAB_W)),
        ],
        out_specs=[pl.BlockSpec((1, 1, m, MEM_W), lambda l, i: (l, i, 0, 0))] * 2,
        out_shape=[out, out],
        compiler_params=_cparams(2, 32),
        name="memkv",
    )(mem, mem_norm, w_kv, k_gain, ones_bd)


def _head_norm(q, gain, ones_bd):
    ms = _seg_sum(q * q, ones_bd, pieces=1) * (1.0 / HEAD_DIM)
    return q * lax.rsqrt(ms + NORM_EPS) * gain


def _aproj_kernel(x_ref, g_ref, w_ref, mu_ref, qg_ref, ones_ref, ps_ref, qm_ref, carry_ref, *, tiles_per_seq):
    i = pl.program_id(0)
    tm = x_ref.shape[0]
    u = _rms(x_ref[...], g_ref[...], NORM_EPS).astype(BF16)

    @pl.when(i % tiles_per_seq == 0)
    def _():
        carry_ref[...] = jnp.zeros_like(carry_ref)

    row = lax.broadcasted_iota(jnp.int32, (PROJ_SUB_ROWS, RWKV_SHIFT_W), 0)
    last = carry_ref[V7X_SUBLANES - 1:V7X_SUBLANES, :]
    group_rows = PROJ_SUB_ROWS * PROJ_SUBS_IN_FLIGHT
    for g0 in range(0, tm, group_rows):
        starts = range(g0, g0 + group_rows, PROJ_SUB_ROWS)
        projs = [_dot(u[r0:r0 + PROJ_SUB_ROWS], w_ref[0]) for r0 in starts]
        for r0, proj in zip(starts, projs):
            p = proj[:, :RWKV_SHIFT_W]
            prev = jnp.where(row == 0, last, pltpu.roll(p, 1, 0))
            ps_ref[r0:r0 + PROJ_SUB_ROWS, :] = p + mu_ref[...] * (prev - p)
            last = p[PROJ_SUB_ROWS - 1:, :]
            qm_ref[r0:r0 + PROJ_SUB_ROWS, :] = _head_norm(proj[:, RWKV_SHIFT_W:], qg_ref[...],
                                                          ones_ref[...]).astype(qm_ref.dtype)
    carry_ref[...] = projs[-1][PROJ_SUB_ROWS - V7X_SUBLANES:, :RWKV_SHIFT_W]


def _aproj(x, g, w_in, layer, mu, q_gain, ones_bd, seq_len, tm=APROJ_ROWS):
    n, d = x.shape
    return pl.pallas_call(
        functools.partial(_aproj_kernel, tiles_per_seq=seq_len // tm),
        grid=(n // tm,),
        in_specs=[
            pl.BlockSpec((tm, d), lambda i: (i, 0)),
            _const_spec((1, d)),
            pl.BlockSpec((1,) + w_in.shape[1:], lambda i: (layer, 0, 0), pipeline_mode=pl.Buffered(1)),
            _const_spec((1, RWKV_SHIFT_W)),
            _const_spec((1, MEM_W)),
            _const_spec((SLAB_W, SLAB_W)),
        ],
        out_specs=[pl.BlockSpec((tm, RWKV_SHIFT_W), lambda i: (i, 0)),
                   pl.BlockSpec((tm, MEM_W), lambda i: (i, 0))],
        out_shape=[jax.ShapeDtypeStruct((n, RWKV_SHIFT_W), F32),
                   jax.ShapeDtypeStruct((n, MEM_W), BF16)],
        scratch_shapes=[pltpu.VMEM((V7X_SUBLANES, RWKV_SHIFT_W), F32)],
        compiler_params=_cparams(1, 56),
        name="aproj",
    )(x, g, w_in, mu, q_gain, ones_bd)


def _qproj_kernel(x_ref, g_ref, w_ref, qg_ref, ones_ref, slab_ref, *rest, normed_w):
    stage_ref = rest[-1]
    flat_refs = rest[:-1]
    tm = x_ref.shape[0]
    u = _rms(x_ref[...], g_ref[...], NORM_EPS).astype(BF16)
    proj = _dot(u, w_ref[...])
    normed = _head_norm(proj[:, :normed_w], qg_ref[...], ones_ref[...])
    n_slabs = slab_ref.shape[1]
    for j in range(n_slabs):
        src = normed if (j + 1) * V7X_LANES <= normed_w else proj
        stage_ref[j] = src[:, j * V7X_LANES:(j + 1) * V7X_LANES]
    for j in range(n_slabs):
        for c in range(DIL_ROW_PHASES):
            slab_ref[0, j, c] = stage_ref[j, pl.ds(c, tm // DIL_ROW_PHASES, stride=DIL_ROW_PHASES), :]
    if flat_refs:
        flat_refs[0][...] = normed[:, n_slabs * V7X_LANES:].astype(flat_refs[0].dtype)


def _qproj(x, g, w, q_gain, ones_bd, seq_len, slab_w, tm=QPROJ_ROWS):
    n, d = x.shape
    wo = w.shape[1]
    normed_w = q_gain.shape[-1]
    n_slabs = slab_w // V7X_LANES
    tiles_per_seq = seq_len // tm
    phase_rows = seq_len // DIL_ROW_PHASES
    out_specs = [pl.BlockSpec((1, n_slabs, DIL_ROW_PHASES, tm // DIL_ROW_PHASES, V7X_LANES),
                              lambda i: (i // tiles_per_seq, 0, 0, i % tiles_per_seq, 0))]
    out_shape = [jax.ShapeDtypeStruct((n // seq_len, n_slabs, DIL_ROW_PHASES, phase_rows, V7X_LANES), F32)]
    if wo > slab_w:
        assert normed_w == wo
        out_specs.append(pl.BlockSpec((tm, wo - slab_w), lambda i: (i, 0)))
        out_shape.append(jax.ShapeDtypeStruct((n, wo - slab_w), BF16))
    outs = list(pl.pallas_call(
        functools.partial(_qproj_kernel, normed_w=normed_w),
        grid=(n // tm,),
        in_specs=[
            pl.BlockSpec((tm, d), lambda i: (i, 0)),
            _const_spec((1, d)),
            pl.BlockSpec(w.shape, lambda i: (0, 0), pipeline_mode=pl.Buffered(1)),
            _const_spec((1, normed_w)),
            _const_spec((SLAB_W, SLAB_W)),
        ],
        out_specs=out_specs,
        out_shape=out_shape,
        scratch_shapes=[pltpu.VMEM((n_slabs, tm, V7X_LANES), F32)],
        compiler_params=_cparams(1, 48),
        name="qproj",
    )(x, g, w, q_gain, ones_bd))
    outs[0] = outs[0].reshape(n // seq_len, n_slabs, seq_len, V7X_LANES)
    return outs


def _embed(x):
    head0 = lax.broadcasted_iota(jnp.int32, x.shape, 1) < HEAD_DIM
    zero = jnp.zeros_like(x)
    return jnp.concatenate([jnp.where(head0, x, zero), jnp.where(head0, zero, x)], axis=0)


def _tri_inverse(nmats, tpos, spos):
    eye = (tpos == spos).astype(F32)
    same = {s: (tpos >> s) == (spos >> s) for s in (3, 4, 5)}

    def mm(lhs, rhs):
        return [_dot(a, _embed(b)) for a, b in zip(lhs, rhs)]

    def bf(xs):
        return [x.astype(BF16) for x in xs]

    n8 = [jnp.where(same[3], n, 0.0) for n in nmats]
    n8b = bf(n8)
    n2 = mm(n8b, n8b)
    n2b = bf(n2)
    n4 = mm(n2b, n2b)
    t = mm(bf([eye + a for a in n8]), bf([eye + a for a in n2]))
    t = mm(bf(t), bf([eye + a for a in n4]))
    for lo, hi in ((3, 4), (4, 5), (5, None)):
        off = ~same[lo] if hi is None else (same[hi] & ~same[lo])
        tb = bf(t)
        z = mm(bf([jnp.where(off, n, 0.0) for n in nmats]), tb)
        t = [a + d for a, d in zip(t, mm(tb, bf(z)))]
    return t


def _rwkv_kernel(ps_ref, vec_ref, wup_ref, aup_ref, gup_ref, tri_ref, ones_ref, y_ref,
                 s_ref, rt_ref, kt_ref, bt_ref, at_ref, kh_ref, bh_ref, v_ref, gam_ref, yacc_ref,
                 bonus_ref, gate_ref, p_s, qt_s, rp_s, y0_s):
    n_seqs, seq_rows = ps_ref.shape[1], ps_ref.shape[2]
    tblk = n_seqs * seq_rows
    n_pairs = RWKV_W // PAIR_W

    @pl.when(pl.program_id(1) == 0)
    def _():
        s_ref[...] = jnp.zeros_like(s_ref)

    ps = jnp.concatenate([ps_ref[0, q] for q in range(n_seqs)], axis=0)
    r = ps[:, :RWKV_W]
    k = ps[:, RWKV_W:2 * RWKV_W]
    v = ps[:, 2 * RWKV_W:3 * RWKV_W]
    lora_in = ps[:, LORA_LO:LORA_LO + PAIR_W]
    g_lo = ps[:, LORA_LO + PAIR_W:]
    w0, a0, kk_scale, k_a = vec_ref[0:1, :], vec_ref[1:2, :], vec_ref[2:3, :], vec_ref[3:4, :]
    r_k, lnx_g, lnx_b = vec_ref[4:5, :], vec_ref[5:6, :], vec_ref[6:7, :]
    ones_bd = ones_ref[...]

    z = w0 + _dot(jnp.tanh(lora_in).astype(BF16), wup_ref[...])
    lw = -math.exp(-0.5) * jax.nn.sigmoid(z)
    a = jax.nn.sigmoid(a0 + _dot(lora_in.astype(BF16), aup_ref[...]))
    gate_ref[...] = _dot(jax.nn.sigmoid(g_lo).astype(BF16), gup_ref[...])
    kk = k * kk_scale
    kk = kk / jnp.maximum(jnp.sqrt(_seg_sum(kk * kk, ones_bd, pieces=1)), 1e-12)
    k2 = k * (1.0 + (a - 1.0) * k_a)
    kka = kk * a
    bonus_ref[...] = _seg_sum(r * k2 * r_k, ones_bd, pieces=1) * v

    hi, lo = _split2(lw)
    tri = tri_ref[...]
    gcum = jnp.concatenate(
        [_dot(tri, hi[s0:s0 + RWKV_CUMSUM_ROWS]) + _dot(tri, lo[s0:s0 + RWKV_CUMSUM_ROWS])
         for s0 in range(0, tblk, RWKV_CUMSUM_ROWS)], axis=0)
    n_chunks = tblk // RWKV_CHUNK
    gam_rows = [jnp.exp(gcum[(c + 1) * RWKV_CHUNK - 1:(c + 1) * RWKV_CHUNK, :]) for c in range(n_chunks)]
    gam = jnp.concatenate([jnp.broadcast_to(g, (RWKV_CHUNK, RWKV_W)) for g in gam_rows], axis=0)
    for c in range(n_chunks):
        gam_ref[c:c + 1, :] = gam_rows[c]
    e_neg = jnp.exp(-gcum)
    e_last = gam * e_neg
    rt_ref[...] = (r * jnp.exp(gcum)).astype(BF16)
    kt_ref[...] = (k2 * e_neg).astype(BF16)
    bt_ref[...] = (kka * e_neg).astype(BF16)
    at_ref[...] = (-kk * jnp.exp(gcum - lw)).astype(BF16)
    kh_ref[...] = (k2 * e_last).astype(BF16)
    bh_ref[...] = (kka * e_last).astype(BF16)
    v_ref[...] = v.astype(BF16)

    tpos = lax.broadcasted_iota(jnp.int32, (RWKV_CHUNK, PAIR_W), 0)
    spos = lax.broadcasted_iota(jnp.int32, (RWKV_CHUNK, PAIR_W), 1) & (HEAD_DIM - 1)
    strict = tpos > spos
    incl = tpos >= spos
    head0 = lax.broadcasted_iota(jnp.int32, (RWKV_CHUNK, PAIR_W), 1) < HEAD_DIM
    same_head = ((lax.broadcasted_iota(jnp.int32, (PAIR_W, PAIR_W), 0) < HEAD_DIM)
                 == (lax.broadcasted_iota(jnp.int32, (PAIR_W, PAIR_W), 1) < HEAD_DIM))

    pairs = range(n_pairs)
    lanes = [slice(p * PAIR_W, (p + 1) * PAIR_W) for p in pairs]

    def phase_a(i, carry):
        slots, at_v, rt_v, bt_v, kt_v, v_v, bh_v, kh_v = [], [], [], [], [], [], [], []
        for cc in range(RWKV_PHASE_A_CHUNKS):
            c = i * RWKV_PHASE_A_CHUNKS + cc
            rows = pl.ds(pl.multiple_of(c * RWKV_CHUNK, RWKV_CHUNK), RWKV_CHUNK)
            loaded = [ref[rows, :] for ref in (at_ref, rt_ref, bt_ref, kt_ref, v_ref, bh_ref, kh_ref)]
            for p in pairs:
                slots.append(c * n_pairs + p)
                for dst, x_c in zip((at_v, rt_v, bt_v, kt_v, v_v, bh_v, kh_v), loaded):
                    dst.append(x_c[:, lanes[p]])
        chains = range(len(slots))
        m4 = [lax.dot_general(jnp.concatenate([at_v[j], rt_v[j]], axis=0),
                              jnp.concatenate([_embed(bt_v[j]), _embed(kt_v[j])], axis=0), _NT,
                              preferred_element_type=F32) for j in chains]
        n_ab = [jnp.where(strict, m[:RWKV_CHUNK, :PAIR_W], 0.0) for m in m4]
        a_ak = [jnp.where(strict, m[:RWKV_CHUNK, PAIR_W:], 0.0).astype(BF16) for m in m4]
        a_rb = [jnp.where(incl, m[RWKV_CHUNK:, :PAIR_W], 0.0).astype(BF16) for m in m4]
        a_rk = [jnp.where(incl, m[RWKV_CHUNK:, PAIR_W:], 0.0).astype(BF16) for m in m4]
        v_m = [_embed(x) for x in v_v]
        akv = [_dot(a_ak[j], v_m[j]).astype(BF16) for j in chains]
        t_inv = [t.astype(BF16) for t in _tri_inverse(n_ab, tpos, spos)]
        wu0 = [_dot(t_inv[j], jnp.concatenate([_embed(at_v[j]), _embed(akv[j])], axis=1))
               for j in chains]
        w_b = [x[:, :PAIR_W].astype(BF16) for x in wu0]
        u0_b = [x[:, PAIR_W:].astype(BF16) for x in wu0]
        p_m = [jnp.where(same_head, lax.dot_general(bh_v[j], w_b[j], _TN, preferred_element_type=F32), 0.0)
               for j in chains]
        q_full = [lax.dot_general(jnp.concatenate([u0_b[j], v_v[j]], axis=0),
                                  jnp.concatenate([bh_v[j], kh_v[j]], axis=0), _TN,
                                  preferred_element_type=F32) for j in chains]
        q_t = [jnp.where(head0, q[:RWKV_CHUNK], q[RWKV_CHUNK:]) for q in q_full]
        r_p = [rt_v[j].astype(F32) + _dot(a_rb[j], _embed(w_b[j])) for j in chains]
        y_0 = [_dot(jnp.concatenate([a_rb[j], a_rk[j]], axis=1),
                    jnp.concatenate([_embed(u0_b[j]), v_m[j]], axis=0)) for j in chains]
        for j in chains:
            p_s[slots[j]] = p_m[j].astype(BF16)
            qt_s[slots[j]] = q_t[j]
            rp_s[slots[j]] = r_p[j].astype(BF16)
            y0_s[slots[j]] = y_0[j]
        return carry

    lax.fori_loop(0, n_chunks // RWKV_PHASE_A_CHUNKS, phase_a, 0)

    seq_chunks = seq_rows // RWKV_CHUNK
    heads = [(q, p) for q in range(n_seqs) for p in pairs]
    states = [s_ref[q * n_pairs + p] for q, p in heads]
    for c in range(seq_chunks):
        chunk = [q * seq_chunks + c for q, _ in heads]
        slots = [ch * n_pairs + p for ch, (_, p) in zip(chunk, heads)]
        gam_c = [gam_ref[ch:ch + 1, lanes[p]] for ch, (_, p) in zip(chunk, heads)]
        state_b = [s.astype(BF16) for s in states]
        upd = [lax.dot_general(sb, p_s[j], _NT, preferred_element_type=F32) for sb, j in zip(state_b, slots)]
        y_c = [lax.dot_general(rp_s[j], _embed(sb), _NT, preferred_element_type=F32) + y0_s[j]
               for sb, j in zip(state_b, slots)]
        states = [s * g + u + qt_s[j] for s, g, u, j in zip(states, gam_c, upd, slots)]
        for q in range(n_seqs):
            r0 = (q * seq_chunks + c) * RWKV_CHUNK
            yacc_ref[r0:r0 + RWKV_CHUNK, :] = jnp.concatenate(y_c[q * n_pairs:(q + 1) * n_pairs], axis=1)
    for i, s in enumerate(states):
        s_ref[i] = s

    y = yacc_ref[...]
    mean = _seg_sum(y, ones_bd) * (1.0 / HEAD_DIM)
    dev = y - mean
    var = _seg_sum(dev * dev, ones_bd, pieces=1) * (1.0 / HEAD_DIM)
    yn = dev * lax.rsqrt(var + LNX_EPS) * lnx_g + lnx_b
    out = ((yn + bonus_ref[...]) * gate_ref[...]).astype(y_ref.dtype)
    for q in range(n_seqs):
        y_ref[0, q] = out[q * seq_rows:(q + 1) * seq_rows]


def _rwkv(ps, vecs, w_up_p, a_up_p, g_up, ones_bd, seq_rows=RWKV_ROWS):
    b, s, _ = ps.shape
    n_seqs = RWKV_SEQS_PER_STEP
    tblk = n_seqs * seq_rows
    tri = _block_ones(RWKV_CUMSUM_ROWS, RWKV_CHUNK, lower=True)
    act = pltpu.VMEM((tblk, RWKV_W), F32)
    act_b = pltpu.VMEM((tblk, RWKV_W), BF16)
    n_pairs = RWKV_W // PAIR_W
    n_chunks = tblk // RWKV_CHUNK
    n_slots = n_chunks * n_pairs
    gam_rows = -(-n_chunks // V7X_SUBLANES) * V7X_SUBLANES
    scratch = [pltpu.VMEM((n_seqs * n_pairs, HEAD_DIM, PAIR_W), F32)]
    scratch += [act_b] * 7
    scratch += [pltpu.VMEM((gam_rows, RWKV_W), F32)] + [act] * 3
    scratch += [
        pltpu.VMEM((n_slots, PAIR_W, PAIR_W), BF16),
        pltpu.VMEM((n_slots, RWKV_CHUNK, PAIR_W), F32),
        pltpu.VMEM((n_slots, RWKV_CHUNK, PAIR_W), BF16),
        pltpu.VMEM((n_slots, RWKV_CHUNK, PAIR_W), F32),
    ]
    y = pl.pallas_call(
        _rwkv_kernel,
        grid=(b // n_seqs, s // seq_rows),
        in_specs=[
            pl.BlockSpec((1, n_seqs, seq_rows, RWKV_SHIFT_W), lambda i, t: (i, 0, t, 0)),
            _const_spec(vecs.shape),
            _const_spec(w_up_p.shape),
            _const_spec(a_up_p.shape),
            _const_spec(g_up.shape),
            _const_spec(tri.shape),
            _const_spec((SLAB_W, SLAB_W)),
        ],
        out_specs=pl.BlockSpec((1, n_seqs, seq_rows, RWKV_W), lambda i, t: (i, 0, t, 0)),
        out_shape=jax.ShapeDtypeStruct((b // n_seqs, n_seqs, s, RWKV_W), BF16),
        scratch_shapes=scratch,
        compiler_params=_cparams(2, 48),
        name="rwkv",
    )(ps.reshape(b // n_seqs, n_seqs, s, RWKV_SHIFT_W), vecs, w_up_p, a_up_p, g_up, tri, ones_bd)
    return y.reshape(b, s, RWKV_W)


def _mixout_kernel(x_ref, ya_ref, qm_ref, k_ref, v_ref, w_ref, o_ref):
    if len(ya_ref.shape) == 4:
        ya = jnp.concatenate([ya_ref[0, j] for j in range(ya_ref.shape[1])], axis=1)
    else:
        ya = ya_ref[...]
    wa = ya.shape[-1]
    qm = qm_ref[...]
    kmem = k_ref[0]
    vmem = v_ref[0]
    head = lax.broadcasted_iota(jnp.int32, qm.shape, 1) // HEAD_DIM
    heads = range(MEM_HEADS)
    logits = [lax.dot_general(jnp.where(head == h, qm, jnp.zeros_like(qm)), kmem, _NT,
                              preferred_element_type=F32) for h in heads]
    p = [jnp.exp(lg - jnp.max(lg, axis=-1, keepdims=True)) for lg in logits]
    inv_l = [1.0 / jnp.sum(ph, axis=-1, keepdims=True) for ph in p]
    pv = [_dot(ph.astype(BF16), vmem) for ph in p]
    y_mem = pv[0] * inv_l[0]
    for h in heads[1:]:
        y_mem = jnp.where(head == h, pv[h] * inv_l[h], y_mem)
    y = _dot(ya.astype(BF16), w_ref[:wa, :]) + _dot(y_mem.astype(BF16), w_ref[wa:, :])
    o_ref[...] = x_ref[...] + y


def _mixout(x, ya, qm, k_mem, v_mem, w_out, seq_len, tm=MIXOUT_ROWS):
    n, d = x.shape
    tiles_per_seq = seq_len // tm
    m = k_mem.shape[1]
    if ya.ndim == 4:
        ya_spec = pl.BlockSpec((1, ya.shape[1], tm, V7X_LANES),
                               lambda i: (i // tiles_per_seq, 0, i % tiles_per_seq, 0))
    else:
        ya_spec = pl.BlockSpec((tm, ya.shape[-1]), lambda i: (i, 0))
    return pl.pallas_call(
        _mixout_kernel,
        grid=(n // tm,),
        in_specs=[
            pl.BlockSpec((tm, d), lambda i: (i, 0)),
            ya_spec,
            pl.BlockSpec((tm, MEM_W), lambda i: (i, 0)),
            pl.BlockSpec((1, m, MEM_W), lambda i: (i // tiles_per_seq, 0, 0)),
            pl.BlockSpec((1, m, MEM_W), lambda i: (i // tiles_per_seq, 0, 0)),
            _const_spec(w_out.shape),
        ],
        out_specs=pl.BlockSpec((tm, d), lambda i: (i, 0)),
        out_shape=jax.ShapeDtypeStruct((n, d), F32),
        compiler_params=_cparams(1, 40),
        name="mixout",
    )(x, ya, qm, k_mem, v_mem, w_out)


def _t5_bucket(dist):
    max_exact = NUM_BUCKETS // 2
    d_f = jnp.maximum(dist, 1).astype(F32)
    large = max_exact + (jnp.log(d_f / max_exact) / math.log(MAX_DISTANCE / max_exact)
                         * (NUM_BUCKETS - max_exact)).astype(jnp.int32)
    large = jnp.minimum(large, NUM_BUCKETS - 1)
    return jnp.where(dist < max_exact, dist, large)


def _band_buckets():
    run = BLOCK // DIL_ROW_PHASES
    out = []
    for window, dil in DIL_GROUPS:
        rows_q = np.arange(BLOCK)
        rows_k = np.arange(2 * BLOCK)
        if dil < DIL_ROW_PHASES:
            u_q = DIL_ROW_PHASES * (rows_q % run) + rows_q // run
            u_k = ((rows_k % (2 * run)) // run) * BLOCK + DIL_ROW_PHASES * (rows_k % run) + rows_k // (2 * run)
        else:
            u_q, u_k = rows_q, rows_k
        dsub = jnp.asarray(BLOCK + u_q[:, None] - u_k[None, :])
        band = (dsub >= 0) & (dsub <= window // dil)
        idx = jnp.where(band, _t5_bucket(jnp.maximum(dsub, 0) * dil), -1)
        out.append(jnp.stack([idx, jnp.where(jnp.asarray(u_k < BLOCK)[None, :], -1, idx)]))
    return jnp.stack(out, axis=1).astype(jnp.int32)


def _bias_kernel(tab_ref, idx_ref, o_ref):
    head = pl.program_id(1)
    idx = idx_ref[0, 0]
    acc = jnp.full(idx.shape, NEG_INF, F32)
    for bucket in range(NUM_BUCKETS):
        acc = jnp.where(idx == bucket, tab_ref[bucket, head], acc)
    o_ref[0, 0] = acc


def _band_bias(rel_bias):
    n_heads = rel_bias.shape[1]
    return pl.pallas_call(
        _bias_kernel,
        grid=(2, n_heads),
        in_specs=[
            pl.BlockSpec(memory_space=pltpu.SMEM),
            pl.BlockSpec((1, 1, BLOCK, 2 * BLOCK), lambda v, h: (v, h // DIL_GROUP_HEADS, 0, 0)),
        ],
        out_specs=pl.BlockSpec((1, 1, BLOCK, 2 * BLOCK), lambda v, h: (v, h, 0, 0)),
        out_shape=jax.ShapeDtypeStruct((2, n_heads, BLOCK, 2 * BLOCK), F32),
        compiler_params=_cparams(2, 16),
        name="band_bias",
    )(rel_bias, _band_buckets())


def _dil_group(dil, q_ref, k_ref, v_ref, bias_ref, acc_ref, m_ref, l_ref):
    seq_len = q_ref.shape[2]
    phase_rows = seq_len // DIL_ROW_PHASES
    run = BLOCK // DIL_ROW_PHASES
    head0 = lax.broadcasted_iota(jnp.int32, (BLOCK, PAIR_W), 1) < HEAD_DIM

    def windows(t):
        if dil < DIL_ROW_PHASES:
            blk = t
            back = jnp.maximum(blk - 1, 0)
            own = [pl.ds(pl.multiple_of(c * phase_rows + run * blk, run), run) for c in range(DIL_ROW_PHASES)]
            prev = [pl.ds(pl.multiple_of(c * phase_rows + run * back, run), run) for c in range(DIL_ROW_PHASES)]
            return own, prev, blk == 0
        step = dil // DIL_ROW_PHASES
        n_blocks = phase_rows // (step * BLOCK)
        sub = t >> (n_blocks.bit_length() - 1)
        blk = t & (n_blocks - 1)
        base = (sub & (DIL_ROW_PHASES - 1)) * phase_rows + (sub >> (DIL_ROW_PHASES.bit_length() - 1))
        back = jnp.maximum(blk - 1, 0)
        if step == 1:
            own = [pl.ds(pl.multiple_of(base + BLOCK * blk, BLOCK), BLOCK)]
            prev = [pl.ds(pl.multiple_of(base + BLOCK * back, BLOCK), BLOCK)]
        else:
            own = [pl.ds(base + step * BLOCK * blk, BLOCK, stride=step)]
            prev = [pl.ds(base + step * BLOCK * back, BLOCK, stride=step)]
        return own, prev, blk == 0

    def gather(get, wins):
        parts = [get(w) for w in wins]
        return parts[0] if len(parts) == 1 else jnp.concatenate(parts, axis=0)

    def scatter(put, wins, value):
        rows = BLOCK // len(wins)
        for i, w in enumerate(wins):
            put(w, value[i * rows:(i + 1) * rows])

    def tiles(i, carry):
        slabs = []
        for j in range(DIL_TILES_PER_ITER):
            own, prev, first = windows(i * DIL_TILES_PER_ITER + j)
            key_wins = [w for pw, ow in zip(prev, own) for w in (pw, ow)]
            variant = first.astype(jnp.int32)
            slabs += [(pair, own, key_wins, variant) for pair in range(DIL_OUT_W // PAIR_W)]
        q = [gather(lambda w: q_ref[0, pair, w, :], own) for pair, own, _, _ in slabs]
        keys = [gather(lambda w: k_ref[0, pair, w, :], kw).astype(BF16) for pair, _, kw, _ in slabs]
        vals = [gather(lambda w: v_ref[0, pair, w, :], kw).astype(BF16) for pair, _, kw, _ in slabs]
        m_old = [gather(lambda w: m_ref[pair, w, :], own) for pair, own, _, _ in slabs]
        l_old = [gather(lambda w: l_ref[pair, w, :], own) for pair, own, _, _ in slabs]
        acc_old = [gather(lambda w: acc_ref[pair, w, :], own) for pair, own, _, _ in slabs]
        heads = [(s, h) for s in range(len(slabs)) for h in range(2)]
        logits = [lax.dot_general(jnp.where(head0 if h == 0 else ~head0, q[s], 0.0).astype(BF16),
                                  keys[s], _NT, preferred_element_type=F32) for s, h in heads]
        logits = [lg + bias_ref[slabs[s][3], 2 * slabs[s][0] + h] for lg, (s, h) in zip(logits, heads)]
        mx = [jnp.max(lg, axis=-1, keepdims=True) for lg in logits]
        p = [jnp.exp(lg - m) for lg, m in zip(logits, mx)]
        ls = [jnp.sum(ph, axis=-1, keepdims=True) for ph in p]
        pv = [_dot(ph.astype(BF16), vals[s]) for ph, (s, h) in zip(p, heads)]
        for s, (pair, own, _, _) in enumerate(slabs):
            m_t = jnp.where(head0, mx[2 * s], mx[2 * s + 1])
            l_t = jnp.where(head0, ls[2 * s], ls[2 * s + 1])
            acc_t = jnp.where(head0, pv[2 * s], pv[2 * s + 1])
            m_new = jnp.maximum(m_old[s], m_t)
            e_old = jnp.exp(m_old[s] - m_new)
            e_t = jnp.exp(m_t - m_new)

            def put(ref, value, pair=pair, own=own):
                def store(w, rows):
                    ref[pair, w, :] = rows
                scatter(store, own, value)

            put(m_ref, m_new)
            put(l_ref, l_old[s] * e_old + l_t * e_t)
            put(acc_ref, acc_old[s] * e_old + acc_t * e_t)
        return carry

    lax.fori_loop(0, seq_len // BLOCK // DIL_TILES_PER_ITER, tiles, 0)


def _dil_kernel(q_ref, k_ref, v_ref, bias_ref, o_ref, acc_ref, m_ref, l_ref):
    g = pl.program_id(1)

    @pl.when(g == 0)
    def _():
        m_ref[...] = jnp.full(m_ref.shape, NEG_INF, F32)
        l_ref[...] = jnp.zeros_like(l_ref)
        acc_ref[...] = jnp.zeros_like(acc_ref)

    for gi, (_, dil) in enumerate(DIL_GROUPS):
        pl.when(g == gi)(functools.partial(_dil_group, dil, q_ref, k_ref, v_ref, bias_ref, acc_ref, m_ref, l_ref))

    @pl.when(g == len(DIL_GROUPS) - 1)
    def _():
        phase_rows = o_ref.shape[2] // DIL_ROW_PHASES
        for pair in range(DIL_OUT_W // PAIR_W):
            for c in range(DIL_ROW_PHASES):
                rows = slice(c * phase_rows, (c + 1) * phase_rows)
                o_ref[0, pair, pl.ds(c, phase_rows, stride=DIL_ROW_PHASES), :] = (
                    acc_ref[pair, rows, :] / l_ref[pair, rows, :])


def _dilated_attention(q_slabs, kv_slabs, bias):
    b, _, s, _ = q_slabs.shape
    n_groups = len(DIL_GROUPS)
    pairs = DIL_OUT_W // PAIR_W
    blk = (1, pairs, s, PAIR_W)
    return pl.pallas_call(
        _dil_kernel,
        grid=(b, n_groups),
        in_specs=[
            pl.BlockSpec(blk, lambda i, g: (i, g, 0, 0)),
            pl.BlockSpec(blk, lambda i, g: (i, g, 0, 0)),
            pl.BlockSpec(blk, lambda i, g: (i, n_groups + g, 0, 0)),
            pl.BlockSpec((2, DIL_GROUP_HEADS, BLOCK, 2 * BLOCK), lambda i, g: (0, g, 0, 0)),
        ],
        out_specs=pl.BlockSpec(blk, lambda i, g: (i, 0, 0, 0)),
        out_shape=jax.ShapeDtypeStruct((b, pairs, s, PAIR_W), F32),
        scratch_shapes=[pltpu.VMEM((pairs, s, PAIR_W), F32)] * 3,
        compiler_params=_cparams(2, 56),
        name="dilated_attention",
    )(q_slabs, kv_slabs, kv_slabs, bias)


def kernel(x, mem, ffn_pre_norm, ffn_pre_w_in, ffn_pre_w_out, mix_norm, ffn_post_norm, ffn_post_w_in, ffn_post_w_out, mem_norm, mem_w_kv, mem_q_norm, mem_k_norm, a_w_in, a_shift_mu, a_w0, a_w_up, a_a0, a_a_up, a_g_up, a_kk_scale, a_k_a, a_r_k, a_lnx_g, a_lnx_b, a_w_out, b_w_q, b_q_norm, b_w_out, kv_norm, kv_w, kv_k_norm, rel_bias):
    b, s, d = x.shape
    depth = ffn_pre_w_in.shape[0]
    n_a = a_w_in.shape[0]
    n = b * s
    scale = 1.0 / math.sqrt(HEAD_DIM)
    ones_bd = _block_ones(SLAB_W, HEAD_DIM)
    row = lambda p: p.reshape(1, -1)

    k_mem, v_mem = _memkv(mem, mem_norm[:, None, :], mem_w_kv.astype(BF16),
                          jnp.tile(mem_k_norm, (1, MEM_HEADS))[:, None, :], ones_bd)
    mem_q_gain = jnp.tile(mem_q_norm, (1, MEM_HEADS)) * scale

    pre_w_in, pre_w_out = ffn_pre_w_in.astype(BF16), ffn_pre_w_out.astype(BF16)
    post_w_in, post_w_out = ffn_post_w_in.astype(BF16), ffn_post_w_out.astype(BF16)
    a_w_in_b = a_w_in.astype(BF16)
    xf = x.reshape(n, d)
    kv = None
    bias = None
    for layer in range(depth):
        xf = _ffn(xf, row(ffn_pre_norm[layer]), pre_w_in, pre_w_out, layer)
        if layer < n_a:
            i = layer
            ps, qm = _aproj(xf, row(mix_norm[layer]), a_w_in_b, i, row(a_shift_mu[i]),
                            row(mem_q_gain[layer]), ones_bd, s)
            zeros = jnp.zeros((DECAY_LORA, RWKV_W), F32)
            vecs = jnp.stack([a_w0[i], a_a0[i], a_kk_scale[i], a_k_a[i], a_r_k[i].reshape(-1),
                              a_lnx_g[i], a_lnx_b[i], jnp.zeros((RWKV_W,), F32)])
            y_main = _rwkv(ps.reshape(b, s, RWKV_SHIFT_W), vecs,
                           jnp.concatenate([a_w_up[i], zeros]).astype(BF16),
                           jnp.concatenate([zeros, a_a_up[i]]).astype(BF16),
                           a_g_up[i].astype(BF16), ones_bd)
            xf = _mixout(xf, y_main.reshape(n, RWKV_W), qm, k_mem[layer], v_mem[layer],
                         a_w_out[i].astype(BF16), s)
        else:
            j = layer - n_a
            q_gain = jnp.concatenate([jnp.tile(b_q_norm[j], DIL_W // HEAD_DIM) * scale, mem_q_gain[layer]])
            q_dil, qm = _qproj(xf, row(mix_norm[layer]), b_w_q[j].astype(BF16), row(q_gain), ones_bd, s, DIL_W)
            if bias is None:
                bias = _band_bias(rel_bias)
            y_dil = _dilated_attention(q_dil, kv, bias)
            xf = _mixout(xf, y_dil, qm, k_mem[layer], v_mem[layer],
                         b_w_out[j].astype(BF16), s)
        xf = _ffn(xf, row(ffn_post_norm[layer]), post_w_in, post_w_out, layer)
        if layer == n_a - 1:
            k_gain = jnp.tile(kv_k_norm, DIL_W // HEAD_DIM)
            kv, = _qproj(xf, row(kv_norm), kv_w.astype(BF16), row(k_gain), ones_bd, s, 2 * DIL_W)
    return xf.reshape(b, s, d)
```

```python
import functools
import math

import jax
import jax.numpy as jnp
import numpy as np
from jax import lax
from jax.experimental import pallas as pl
from jax.experimental.pallas import tpu as pltpu

F32 = jnp.float32
BF16 = jnp.bfloat16

HEAD_DIM = 64
MEM_HEADS = 4
MEM_W = MEM_HEADS * HEAD_DIM
RWKV_HEADS = 12
RWKV_W = RWKV_HEADS * HEAD_DIM
DECAY_LORA = 64
AAA_LORA = 64
GATE_LORA = 128
RWKV_SHIFT_W = 3 * RWKV_W + DECAY_LORA + AAA_LORA + GATE_LORA
LORA_LO = 3 * RWKV_W
DIL_GROUPS = ((128, 1), (512, 4), (2048, 16))
DIL_GROUP_HEADS = 4
DIL_W = len(DIL_GROUPS) * DIL_GROUP_HEADS * HEAD_DIM
DIL_OUT_W = DIL_GROUP_HEADS * HEAD_DIM
BLOCK = 128
NUM_BUCKETS = 32
MAX_DISTANCE = 2048
NORM_EPS = 1e-6
LNX_EPS = 64e-5
NEG_INF = -1e30

V7X_LANES = 128
V7X_SUBLANES = 8
V7X_VMEM_BYTES = 64 * 1024 * 1024

RWKV_CHUNK = 64
RWKV_SEQS_PER_STEP = 1
RWKV_PHASE_A_CHUNKS = 8
FFN_ROWS = 1024
FFN_CHUNK = 256
APROJ_ROWS = 1024
PROJ_SUBS_IN_FLIGHT = 4
QPROJ_ROWS = 1024
RWKV_ROWS = 512
RWKV_CUMSUM_ROWS = 256
DIL_TILES_PER_ITER = 2
DIL_ROW_PHASES = 4
PROJ_SUB_ROWS = 128
PAIR_W = 2 * HEAD_DIM
SLAB_W = 4 * HEAD_DIM

_NT = (((1,), (1,)), ((), ()))
_TN = (((0,), (0,)), ((), ()))


def _cparams(n_axes, vmem_mib):
    return pltpu.CompilerParams(
        dimension_semantics=("arbitrary",) * n_axes,
        vmem_limit_bytes=min(vmem_mib * 1024 * 1024, V7X_VMEM_BYTES - 4 * 1024 * 1024),
    )


def _const_spec(shape):
    zeros = (0,) * len(shape)
    return pl.BlockSpec(shape, lambda *_: zeros)


def _dot(a, b):
    return jnp.dot(a, b, preferred_element_type=F32)


def _rms(x, g, eps):
    return x * lax.rsqrt(jnp.mean(x * x, axis=-1, keepdims=True) + eps) * g


def _split2(x):
    hi = x.astype(BF16)
    lo = (x - hi.astype(F32)).astype(BF16)
    return hi, lo


def _seg_sum(x, ones_bd, pieces=2):
    outs = []
    for s in range(x.shape[-1] // SLAB_W):
        slab = x[:, s * SLAB_W:(s + 1) * SLAB_W]
        if pieces == 1:
            outs.append(_dot(slab.astype(BF16), ones_bd))
        else:
            hi, lo = _split2(slab)
            outs.append(_dot(hi, ones_bd) + _dot(lo, ones_bd))
    return outs[0] if len(outs) == 1 else jnp.concatenate(outs, axis=-1)


def _block_ones(n, blk, lower=False):
    i = np.arange(n)
    m = (i[:, None] // blk) == (i[None, :] // blk)
    if lower:
        m = m & (i[:, None] >= i[None, :])
    return jnp.asarray(m, dtype=BF16)


def _mix_delta(ya_ref, qm_ref, k_ref, v_ref, w_ref):
    if len(ya_ref.shape) == 4:
        ya = jnp.concatenate([ya_ref[0, j] for j in range(ya_ref.shape[1])], axis=1)
    else:
        ya = ya_ref[...]
    wa = ya.shape[-1]
    qm = qm_ref[...]
    kmem = k_ref[0]
    vmem = v_ref[0]
    head = lax.broadcasted_iota(jnp.int32, qm.shape, 1) // HEAD_DIM
    heads = range(MEM_HEADS)
    logits = [lax.dot_general(jnp.where(head == h, qm, jnp.zeros_like(qm)), kmem, _NT,
                              preferred_element_type=F32) for h in heads]
    p = [jnp.exp(lg - jnp.max(lg, axis=-1, keepdims=True)) for lg in logits]
    inv_l = [1.0 / jnp.sum(ph, axis=-1, keepdims=True) for ph in p]
    pv = [_dot(ph.astype(BF16), vmem) for ph in p]
    y_mem = pv[0] * inv_l[0]
    for h in heads[1:]:
        y_mem = jnp.where(head == h, pv[h] * inv_l[h], y_mem)
    return _dot(ya.astype(BF16), w_ref[:wa, :]) + _dot(y_mem.astype(BF16), w_ref[wa:, :])


def _ffn_kernel(*refs, has_mix):
    if has_mix:
        x_ref, ya_ref, qm_ref, k_ref, v_ref, wmix_ref, g_ref, win_ref, wout_ref, o_ref = refs
        x = x_ref[...] + _mix_delta(ya_ref, qm_ref, k_ref, v_ref, wmix_ref)
    else:
        x_ref, g_ref, win_ref, wout_ref, o_ref = refs
        x = x_ref[...]
    d_ff = wout_ref.shape[1]
    xn = _rms(x, g_ref[...], NORM_EPS).astype(BF16)
    n_chunks = d_ff // FFN_CHUNK

    def first_stage(c):
        lo = c * FFN_CHUNK
        return (_dot(xn, win_ref[0, :, lo:lo + FFN_CHUNK]),
                _dot(xn, win_ref[0, :, d_ff + lo:d_ff + lo + FFN_CHUNK]))

    y = None
    pending = first_stage(0)
    for c in range(n_chunks):
        gate, up = pending
        if c + 1 < n_chunks:
            pending = first_stage(c + 1)
        act = (gate * jax.nn.sigmoid(gate) * up).astype(BF16)
        part = _dot(act, wout_ref[0, c * FFN_CHUNK:(c + 1) * FFN_CHUNK, :])
        y = part if y is None else y + part
    o_ref[...] = x + 0.5 * y


def _ffn(x, g, w_in, w_out, layer, mix=None, tm=FFN_ROWS):
    n, d = x.shape
    assert w_out.shape[1] % FFN_CHUNK == 0 and n % tm == 0
    operands = [x]
    in_specs = [pl.BlockSpec((tm, d), lambda i: (i, 0))]
    if mix is not None:
        ya, qm, k_mem, v_mem, w_mix, seq_len = mix
        tiles_per_seq = seq_len // tm
        if ya.ndim == 4:
            ya_spec = pl.BlockSpec((1, ya.shape[1], tm, V7X_LANES),
                                   lambda i: (i // tiles_per_seq, 0, i % tiles_per_seq, 0))
        else:
            ya_spec = pl.BlockSpec((tm, ya.shape[-1]), lambda i: (i, 0))
        mem_spec = pl.BlockSpec((1,) + k_mem.shape[1:], lambda i: (i // tiles_per_seq, 0, 0))
        operands += [ya, qm, k_mem, v_mem, w_mix]
        in_specs += [ya_spec, pl.BlockSpec((tm, MEM_W), lambda i: (i, 0)), mem_spec, mem_spec,
                     _const_spec(w_mix.shape)]
    operands += [g, w_in, w_out]
    in_specs += [
        _const_spec((1, d)),
        pl.BlockSpec((1,) + w_in.shape[1:], lambda i: (layer, 0, 0), pipeline_mode=pl.Buffered(1)),
        pl.BlockSpec((1,) + w_out.shape[1:], lambda i: (layer, 0, 0), pipeline_mode=pl.Buffered(1)),
    ]
    return pl.pallas_call(
        functools.partial(_ffn_kernel, has_mix=mix is not None),
        grid=(n // tm,),
        in_specs=in_specs,
        out_specs=pl.BlockSpec((tm, d), lambda i: (i, 0)),
        out_shape=jax.ShapeDtypeStruct((n, d), F32),
        compiler_params=_cparams(1, 60),
        name="ffn_mix" if mix is not None else "ffn",
    )(*operands)


def _memkv_kernel(mem_ref, g_ref, w_ref, kg_ref, ones_ref, k_ref, v_ref):
    m = _rms(mem_ref[0], g_ref[0], NORM_EPS).astype(BF16)
    kv = _dot(m, w_ref[0])
    k = kv[:, :MEM_W]
    ms = _seg_sum(k * k, ones_ref[...]) * (1.0 / HEAD_DIM)
    k_ref[0, 0] = (k * lax.rsqrt(ms + NORM_EPS) * kg_ref[0]).astype(BF16)
    v_ref[0, 0] = kv[:, MEM_W:].astype(BF16)


def _memkv(mem, mem_norm, w_kv, k_gain, ones_bd):
    b, m, d = mem.shape
    depth = w_kv.shape[0]
    out = jax.ShapeDtypeStruct((depth, b, m, MEM_W), BF16)
    return pl.pallas_call(
        _memkv_kernel,
        grid=(depth, b),
        in_specs=[
            pl.BlockSpec((1, m, d), lambda l, i: (i, 0, 0)),
            pl.BlockSpec((1, 1, d), lambda l, i: (l, 0, 0)),
            pl.BlockSpec((1, d, 2 * MEM_W), lambda l, i: (l, 0, 0)),
            pl.BlockSpec((1, 1, MEM_W), lambda l, i: (l, 0, 0)),
            _const_spec((SLAB_W, SLAB_W)),
        ],
        out_specs=[pl.BlockSpec((1, 1, m, MEM_W), lambda l, i: (l, i, 0, 0))] * 2,
        out_shape=[out, out],
        compiler_params=_cparams(2, 32),
        name="memkv",
    )(mem, mem_norm, w_kv, k_gain, ones_bd)


def _head_norm(q, gain, ones_bd):
    ms = _seg_sum(q * q, ones_bd, pieces=1) * (1.0 / HEAD_DIM)
    return q * lax.rsqrt(ms + NORM_EPS) * gain


def _aproj_kernel(x_ref, g_ref, w_ref, mu_ref, qg_ref, ones_ref, ps_ref, qm_ref, carry_ref, *, tiles_per_seq):
    i = pl.program_id(0)
    tm = x_ref.shape[0]
    u = _rms(x_ref[...], g_ref[...], NORM_EPS).astype(BF16)

    @pl.when(i % tiles_per_seq == 0)
    def _():
        carry_ref[...] = jnp.zeros_like(carry_ref)

    row = lax.broadcasted_iota(jnp.int32, (PROJ_SUB_ROWS, RWKV_SHIFT_W), 0)
    last = carry_ref[V7X_SUBLANES - 1:V7X_SUBLANES, :]
    group_rows = PROJ_SUB_ROWS * PROJ_SUBS_IN_FLIGHT
    for g0 in range(0, tm, group_rows):
        starts = range(g0, g0 + group_rows, PROJ_SUB_ROWS)
        projs = [_dot(u[r0:r0 + PROJ_SUB_ROWS], w_ref[0]) for r0 in starts]
        for r0, proj in zip(starts, projs):
            p = proj[:, :RWKV_SHIFT_W]
            prev = jnp.where(row == 0, last, pltpu.roll(p, 1, 0))
            ps_ref[r0:r0 + PROJ_SUB_ROWS, :] = p + mu_ref[...] * (prev - p)
            last = p[PROJ_SUB_ROWS - 1:, :]
            qm_ref[r0:r0 + PROJ_SUB_ROWS, :] = _head_norm(proj[:, RWKV_SHIFT_W:], qg_ref[...],
                                                          ones_ref[...]).astype(qm_ref.dtype)
    carry_ref[...] = projs[-1][PROJ_SUB_ROWS - V7X_SUBLANES:, :RWKV_SHIFT_W]


def _aproj(x, g, w_in, layer, mu, q_gain, ones_bd, seq_len, tm=APROJ_ROWS):
    n, d = x.shape
    return pl.pallas_call(
        functools.partial(_aproj_kernel, tiles_per_seq=seq_len // tm),
        grid=(n // tm,),
        in_specs=[
            pl.BlockSpec((tm, d), lambda i: (i, 0)),
            _const_spec((1, d)),
            pl.BlockSpec((1,) + w_in.shape[1:], lambda i: (layer, 0, 0), pipeline_mode=pl.Buffered(1)),
            _const_spec((1, RWKV_SHIFT_W)),
            _const_spec((1, MEM_W)),
            _const_spec((SLAB_W, SLAB_W)),
        ],
        out_specs=[pl.BlockSpec((tm, RWKV_SHIFT_W), lambda i: (i, 0)),
                   pl.BlockSpec((tm, MEM_W), lambda i: (i, 0))],
        out_shape=[jax.ShapeDtypeStruct((n, RWKV_SHIFT_W), F32),
                   jax.ShapeDtypeStruct((n, MEM_W), BF16)],
        scratch_shapes=[pltpu.VMEM((V7X_SUBLANES, RWKV_SHIFT_W), F32)],
        compiler_params=_cparams(1, 56),
        name="aproj",
    )(x, g, w_in, mu, q_gain, ones_bd)


def _qproj_kernel(x_ref, g_ref, w_ref, qg_ref, ones_ref, slab_ref, *rest, normed_w):
    stage_ref = rest[-1]
    flat_refs = rest[:-1]
    tm = x_ref.shape[0]
    u = _rms(x_ref[...], g_ref[...], NORM_EPS).astype(BF16)
    proj = _dot(u, w_ref[...])
    normed = _head_norm(proj[:, :normed_w], qg_ref[...], ones_ref[...])
    n_slabs = slab_ref.shape[1]
    for j in range(n_slabs):
        src = normed if (j + 1) * V7X_LANES <= normed_w else proj
        stage_ref[j] = src[:, j * V7X_LANES:(j + 1) * V7X_LANES]
    for j in range(n_slabs):
        for c in range(DIL_ROW_PHASES):
            slab_ref[0, j, c] = stage_ref[j, pl.ds(c, tm // DIL_ROW_PHASES, stride=DIL_ROW_PHASES), :]
    if flat_refs:
        flat_refs[0][...] = normed[:, n_slabs * V7X_LANES:].astype(flat_refs[0].dtype)


def _qproj(x, g, w, q_gain, ones_bd, seq_len, slab_w, tm=QPROJ_ROWS):
    n, d = x.shape
    wo = w.shape[1]
    normed_w = q_gain.shape[-1]
    n_slabs = slab_w // V7X_LANES
    tiles_per_seq = seq_len // tm
    phase_rows = seq_len // DIL_ROW_PHASES
    out_specs = [pl.BlockSpec((1, n_slabs, DIL_ROW_PHASES, tm // DIL_ROW_PHASES, V7X_LANES),
                              lambda i: (i // tiles_per_seq, 0, 0, i % tiles_per_seq, 0))]
    out_shape = [jax.ShapeDtypeStruct((n // seq_len, n_slabs, DIL_ROW_PHASES, phase_rows, V7X_LANES), F32)]
    if wo > slab_w:
        assert normed_w == wo
        out_specs.append(pl.BlockSpec((tm, wo - slab_w), lambda i: (i, 0)))
        out_shape.append(jax.ShapeDtypeStruct((n, wo - slab_w), BF16))
    outs = list(pl.pallas_call(
        functools.partial(_qproj_kernel, normed_w=normed_w),
        grid=(n // tm,),
        in_specs=[
            pl.BlockSpec((tm, d), lambda i: (i, 0)),
            _const_spec((1, d)),
            pl.BlockSpec(w.shape, lambda i: (0, 0), pipeline_mode=pl.Buffered(1)),
            _const_spec((1, normed_w)),
            _const_spec((SLAB_W, SLAB_W)),
        ],
        out_specs=out_specs,
        out_shape=out_shape,
        scratch_shapes=[pltpu.VMEM((n_slabs, tm, V7X_LANES), F32)],
        compiler_params=_cparams(1, 48),
        name="qproj",
    )(x, g, w, q_gain, ones_bd))
    outs[0] = outs[0].reshape(n // seq_len, n_slabs, seq_len, V7X_LANES)
    return outs


def _embed(x):
    head0 = lax.broadcasted_iota(jnp.int32, x.shape, 1) < HEAD_DIM
    zero = jnp.zeros_like(x)
    return jnp.concatenate([jnp.where(head0, x, zero), jnp.where(head0, zero, x)], axis=0)


def _tri_inverse(nmats, tpos, spos):
    eye = (tpos == spos).astype(F32)
    same = {s: (tpos >> s) == (spos >> s) for s in (3, 4, 5)}

    def mm(lhs, rhs):
        return [_dot(a, _embed(b)) for a, b in zip(lhs, rhs)]

    def bf(xs):
        return [x.astype(BF16) for x in xs]

    n8 = [jnp.where(same[3], n, 0.0) for n in nmats]
    n8b = bf(n8)
    n2 = mm(n8b, n8b)
    n2b = bf(n2)
    n4 = mm(n2b, n2b)
    t = mm(bf([eye + a for a in n8]), bf([eye + a for a in n2]))
    t = mm(bf(t), bf([eye + a for a in n4]))
    for lo, hi in ((3, 4), (4, 5), (5, None)):
        off = ~same[lo] if hi is None else (same[hi] & ~same[lo])
        tb = bf(t)
        z = mm(bf([jnp.where(off, n, 0.0) for n in nmats]), tb)
        t = [a + d for a, d in zip(t, mm(tb, bf(z)))]
    return t


def _rwkv_kernel(ps_ref, vec_ref, wup_ref, aup_ref, gup_ref, tri_ref, ones_ref, y_ref,
                 s_ref, rt_ref, kt_ref, bt_ref, at_ref, kh_ref, bh_ref, v_ref, gam_ref, yacc_ref,
                 bonus_ref, gate_ref, p_s, qt_s, rp_s, y0_s):
    n_seqs, seq_rows = ps_ref.shape[1], ps_ref.shape[2]
    tblk = n_seqs * seq_rows
    n_pairs = RWKV_W // PAIR_W

    @pl.when(pl.program_id(1) == 0)
    def _():
        s_ref[...] = jnp.zeros_like(s_ref)

    ps = jnp.concatenate([ps_ref[0, q] for q in range(n_seqs)], axis=0)
    r = ps[:, :RWKV_W]
    k = ps[:, RWKV_W:2 * RWKV_W]
    v = ps[:, 2 * RWKV_W:3 * RWKV_W]
    lora_in = ps[:, LORA_LO:LORA_LO + PAIR_W]
    g_lo = ps[:, LORA_LO + PAIR_W:]
    w0, a0, kk_scale, k_a = vec_ref[0:1, :], vec_ref[1:2, :], vec_ref[2:3, :], vec_ref[3:4, :]
    r_k, lnx_g, lnx_b = vec_ref[4:5, :], vec_ref[5:6, :], vec_ref[6:7, :]
    ones_bd = ones_ref[...]

    z = w0 + _dot(jnp.tanh(lora_in).astype(BF16), wup_ref[...])
    lw = -math.exp(-0.5) * jax.nn.sigmoid(z)
    a = jax.nn.sigmoid(a0 + _dot(lora_in.astype(BF16), aup_ref[...]))
    gate_ref[...] = _dot(jax.nn.sigmoid(g_lo).astype(BF16), gup_ref[...])
    kk = k * kk_scale
    kk = kk / jnp.maximum(jnp.sqrt(_seg_sum(kk * kk, ones_bd, pieces=1)), 1e-12)
    k2 = k * (1.0 + (a - 1.0) * k_a)
    kka = kk * a
    bonus_ref[...] = _seg_sum(r * k2 * r_k, ones_bd, pieces=1) * v

    hi, lo = _split2(lw)
    tri = tri_ref[...]
    gcum = jnp.concatenate(
        [_dot(tri, hi[s0:s0 + RWKV_CUMSUM_ROWS]) + _dot(tri, lo[s0:s0 + RWKV_CUMSUM_ROWS])
         for s0 in range(0, tblk, RWKV_CUMSUM_ROWS)], axis=0)
    n_chunks = tblk // RWKV_CHUNK
    gam_rows = [jnp.exp(gcum[(c + 1) * RWKV_CHUNK - 1:(c + 1) * RWKV_CHUNK, :]) for c in range(n_chunks)]
    gam = jnp.concatenate([jnp.broadcast_to(g, (RWKV_CHUNK, RWKV_W)) for g in gam_rows], axis=0)
    for c in range(n_chunks):
        gam_ref[c:c + 1, :] = gam_rows[c]
    e_neg = jnp.exp(-gcum)
    e_last = gam * e_neg
    rt_ref[...] = (r * jnp.exp(gcum)).astype(BF16)
    kt_ref[...] = (k2 * e_neg).astype(BF16)
    bt_ref[...] = (kka * e_neg).astype(BF16)
    at_ref[...] = (-kk * jnp.exp(gcum - lw)).astype(BF16)
    kh_ref[...] = (k2 * e_last).astype(BF16)
    bh_ref[...] = (kka * e_last).astype(BF16)
    v_ref[...] = v.astype(BF16)

    tpos = lax.broadcasted_iota(jnp.int32, (RWKV_CHUNK, PAIR_W), 0)
    spos = lax.broadcasted_iota(jnp.int32, (RWKV_CHUNK, PAIR_W), 1) & (HEAD_DIM - 1)
    strict = tpos > spos
    incl = tpos >= spos
    head0 = lax.broadcasted_iota(jnp.int32, (RWKV_CHUNK, PAIR_W), 1) < HEAD_DIM
    same_head = ((lax.broadcasted_iota(jnp.int32, (PAIR_W, PAIR_W), 0) < HEAD_DIM)
                 == (lax.broadcasted_iota(jnp.int32, (PAIR_W, PAIR_W), 1) < HEAD_DIM))

    pairs = range(n_pairs)
    lanes = [slice(p * PAIR_W, (p + 1) * PAIR_W) for p in pairs]

    def phase_a(i, carry):
        slots, at_v, rt_v, bt_v, kt_v, v_v, bh_v, kh_v = [], [], [], [], [], [], [], []
        for cc in range(RWKV_PHASE_A_CHUNKS):
            c = i * RWKV_PHASE_A_CHUNKS + cc
            rows = pl.ds(pl.multiple_of(c * RWKV_CHUNK, RWKV_CHUNK), RWKV_CHUNK)
            loaded = [ref[rows, :] for ref in (at_ref, rt_ref, bt_ref, kt_ref, v_ref, bh_ref, kh_ref)]
            for p in pairs:
                slots.append(c * n_pairs + p)
                for dst, x_c in zip((at_v, rt_v, bt_v, kt_v, v_v, bh_v, kh_v), loaded):
                    dst.append(x_c[:, lanes[p]])
        chains = range(len(slots))
        m4 = [lax.dot_general(jnp.concatenate([at_v[j], rt_v[j]], axis=0),
                              jnp.concatenate([_embed(bt_v[j]), _embed(kt_v[j])], axis=0), _NT,
                              preferred_element_type=F32) for j in chains]
        n_ab = [jnp.where(strict, m[:RWKV_CHUNK, :PAIR_W], 0.0) for m in m4]
        a_ak = [jnp.where(strict, m[:RWKV_CHUNK, PAIR_W:], 0.0).astype(BF16) for m in m4]
        a_rb = [jnp.where(incl, m[RWKV_CHUNK:, :PAIR_W], 0.0).astype(BF16) for m in m4]
        a_rk = [jnp.where(incl, m[RWKV_CHUNK:, PAIR_W:], 0.0).astype(BF16) for m in m4]
        v_m = [_embed(x) for x in v_v]
        akv = [_dot(a_ak[j], v_m[j]).astype(BF16) for j in chains]
        t_inv = [t.astype(BF16) for t in _tri_inverse(n_ab, tpos, spos)]
        wu0 = [_dot(t_inv[j], jnp.concatenate([_embed(at_v[j]), _embed(akv[j])], axis=1))
               for j in chains]
        w_b = [x[:, :PAIR_W].astype(BF16) for x in wu0]
        u0_b = [x[:, PAIR_W:].astype(BF16) for x in wu0]
        p_m = [jnp.where(same_head, lax.dot_general(bh_v[j], w_b[j], _TN, preferred_element_type=F32), 0.0)
               for j in chains]
        q_full = [lax.dot_general(jnp.concatenate([u0_b[j], v_v[j]], axis=0),
                                  jnp.concatenate([bh_v[j], kh_v[j]], axis=0), _TN,
                                  preferred_element_type=F32) for j in chains]
        q_t = [jnp.where(head0, q[:RWKV_CHUNK], q[RWKV_CHUNK:]) for q in q_full]
        r_p = [rt_v[j].astype(F32) + _dot(a_rb[j], _embed(w_b[j])) for j in chains]
        y_0 = [_dot(jnp.concatenate([a_rb[j], a_rk[j]], axis=1),
                    jnp.concatenate([_embed(u0_b[j]), v_m[j]], axis=0)) for j in chains]
        for j in chains:
            p_s[slots[j]] = p_m[j].astype(BF16)
            qt_s[slots[j]] = q_t[j]
            rp_s[slots[j]] = r_p[j].astype(BF16)
            y0_s[slots[j]] = y_0[j]
        return carry

    lax.fori_loop(0, n_chunks // RWKV_PHASE_A_CHUNKS, phase_a, 0)

    seq_chunks = seq_rows // RWKV_CHUNK
    heads = [(q, p) for q in range(n_seqs) for p in pairs]
    states = [s_ref[q * n_pairs + p] for q, p in heads]
    for c in range(seq_chunks):
        chunk = [q * seq_chunks + c for q, _ in heads]
        slots = [ch * n_pairs + p for ch, (_, p) in zip(chunk, heads)]
        gam_c = [gam_ref[ch:ch + 1, lanes[p]] for ch, (_, p) in zip(chunk, heads)]
        state_b = [s.astype(BF16) for s in states]
        upd = [lax.dot_general(sb, p_s[j], _NT, preferred_element_type=F32) for sb, j in zip(state_b, slots)]
        y_c = [lax.dot_general(rp_s[j], _embed(sb), _NT, preferred_element_type=F32) + y0_s[j]
               for sb, j in zip(state_b, slots)]
        states = [s * g + u + qt_s[j] for s, g, u, j in zip(states, gam_c, upd, slots)]
        for q in range(n_seqs):
            r0 = (q * seq_chunks + c) * RWKV_CHUNK
            yacc_ref[r0:r0 + RWKV_CHUNK, :] = jnp.concatenate(y_c[q * n_pairs:(q + 1) * n_pairs], axis=1)
    for i, s in enumerate(states):
        s_ref[i] = s

    y = yacc_ref[...]
    mean = _seg_sum(y, ones_bd) * (1.0 / HEAD_DIM)
    dev = y - mean
    var = _seg_sum(dev * dev, ones_bd, pieces=1) * (1.0 / HEAD_DIM)
    yn = dev * lax.rsqrt(var + LNX_EPS) * lnx_g + lnx_b
    out = ((yn + bonus_ref[...]) * gate_ref[...]).astype(y_ref.dtype)
    for q in range(n_seqs):
        y_ref[0, q] = out[q * seq_rows:(q + 1) * seq_rows]


def _rwkv(ps, vecs, w_up_p, a_up_p, g_up, ones_bd, seq_rows=RWKV_ROWS):
    b, s, _ = ps.shape
    n_seqs = RWKV_SEQS_PER_STEP
    tblk = n_seqs * seq_rows
    tri = _block_ones(RWKV_CUMSUM_ROWS, RWKV_CHUNK, lower=True)
    act = pltpu.VMEM((tblk, RWKV_W), F32)
    act_b = pltpu.VMEM((tblk, RWKV_W), BF16)
    n_pairs = RWKV_W // PAIR_W
    n_chunks = tblk // RWKV_CHUNK
    n_slots = n_chunks * n_pairs
    gam_rows = -(-n_chunks // V7X_SUBLANES) * V7X_SUBLANES
    scratch = [pltpu.VMEM((n_seqs * n_pairs, HEAD_DIM, PAIR_W), F32)]
    scratch += [act_b] * 7
    scratch += [pltpu.VMEM((gam_rows, RWKV_W), F32)] + [act] * 3
    scratch += [
        pltpu.VMEM((n_slots, PAIR_W, PAIR_W), BF16),
        pltpu.VMEM((n_slots, RWKV_CHUNK, PAIR_W), F32),
        pltpu.VMEM((n_slots, RWKV_CHUNK, PAIR_W), BF16),
        pltpu.VMEM((n_slots, RWKV_CHUNK, PAIR_W), F32),
    ]
    y = pl.pallas_call(
        _rwkv_kernel,
        grid=(b // n_seqs, s // seq_rows),
        in_specs=[
            pl.BlockSpec((1, n_seqs, seq_rows, RWKV_SHIFT_W), lambda i, t: (i, 0, t, 0)),
            _const_spec(vecs.shape),
            _const_spec(w_up_p.shape),
            _const_spec(a_up_p.shape),
            _const_spec(g_up.shape),
            _const_spec(tri.shape),
            _const_spec((SLAB_W, SLAB_W)),
        ],
        out_specs=pl.BlockSpec((1, n_seqs, seq_rows, RWKV_W), lambda i, t: (i, 0, t, 0)),
        out_shape=jax.ShapeDtypeStruct((b // n_seqs, n_seqs, s, RWKV_W), BF16),
        scratch_shapes=scratch,
        compiler_params=_cparams(2, 48),
        name="rwkv",
    )(ps.reshape(b // n_seqs, n_seqs, s, RWKV_SHIFT_W), vecs, w_up_p, a_up_p, g_up, tri, ones_bd)
    return y.reshape(b, s, RWKV_W)


def _t5_bucket(dist):
    max_exact = NUM_BUCKETS // 2
    d_f = jnp.maximum(dist, 1).astype(F32)
    large = max_exact + (jnp.log(d_f / max_exact) / math.log(MAX_DISTANCE / max_exact)
                         * (NUM_BUCKETS - max_exact)).astype(jnp.int32)
    large = jnp.minimum(large, NUM_BUCKETS - 1)
    return jnp.where(dist < max_exact, dist, large)


def _band_buckets():
    run = BLOCK // DIL_ROW_PHASES
    out = []
    for window, dil in DIL_GROUPS:
        rows_q = np.arange(BLOCK)
        rows_k = np.arange(2 * BLOCK)
        if dil < DIL_ROW_PHASES:
            u_q = DIL_ROW_PHASES * (rows_q % run) + rows_q // run
            u_k = ((rows_k % (2 * run)) // run) * BLOCK + DIL_ROW_PHASES * (rows_k % run) + rows_k // (2 * run)
        else:
            u_q, u_k = rows_q, rows_k
        dsub = jnp.asarray(BLOCK + u_q[:, None] - u_k[None, :])
        band = (dsub >= 0) & (dsub <= window // dil)
        idx = jnp.where(band, _t5_bucket(jnp.maximum(dsub, 0) * dil), -1)
        out.append(jnp.stack([idx, jnp.where(jnp.asarray(u_k < BLOCK)[None, :], -1, idx)]))
    return jnp.stack(out, axis=1).astype(jnp.int32)


def _bias_kernel(tab_ref, idx_ref, o_ref):
    head = pl.program_id(1)
    idx = idx_ref[0, 0]
    acc = jnp.full(idx.shape, NEG_INF, F32)
    for bucket in range(NUM_BUCKETS):
        acc = jnp.where(idx == bucket, tab_ref[bucket, head], acc)
    o_ref[0, 0] = acc


def _band_bias(rel_bias):
    n_heads = rel_bias.shape[1]
    return pl.pallas_call(
        _bias_kernel,
        grid=(2, n_heads),
        in_specs=[
            pl.BlockSpec(memory_space=pltpu.SMEM),
            pl.BlockSpec((1, 1, BLOCK, 2 * BLOCK), lambda v, h: (v, h // DIL_GROUP_HEADS, 0, 0)),
        ],
        out_specs=pl.BlockSpec((1, 1, BLOCK, 2 * BLOCK), lambda v, h: (v, h, 0, 0)),
        out_shape=jax.ShapeDtypeStruct((2, n_heads, BLOCK, 2 * BLOCK), F32),
        compiler_params=_cparams(2, 16),
        name="band_bias",
    )(rel_bias, _band_buckets())


def _dil_group(dil, q_ref, k_ref, v_ref, bias_ref, acc_ref, m_ref, l_ref):
    seq_len = q_ref.shape[2]
    phase_rows = seq_len // DIL_ROW_PHASES
    run = BLOCK // DIL_ROW_PHASES
    head0 = lax.broadcasted_iota(jnp.int32, (BLOCK, PAIR_W), 1) < HEAD_DIM

    def windows(t):
        if dil < DIL_ROW_PHASES:
            blk = t
            back = jnp.maximum(blk - 1, 0)
            own = [pl.ds(pl.multiple_of(c * phase_rows + run * blk, run), run) for c in range(DIL_ROW_PHASES)]
            prev = [pl.ds(pl.multiple_of(c * phase_rows + run * back, run), run) for c in range(DIL_ROW_PHASES)]
            return own, prev, blk == 0
        step = dil // DIL_ROW_PHASES
        n_blocks = phase_rows // (step * BLOCK)
        sub = t >> (n_blocks.bit_length() - 1)
        blk = t & (n_blocks - 1)
        base = (sub & (DIL_ROW_PHASES - 1)) * phase_rows + (sub >> (DIL_ROW_PHASES.bit_length() - 1))
        back = jnp.maximum(blk - 1, 0)
        if step == 1:
            own = [pl.ds(pl.multiple_of(base + BLOCK * blk, BLOCK), BLOCK)]
            prev = [pl.ds(pl.multiple_of(base + BLOCK * back, BLOCK), BLOCK)]
        else:
            own = [pl.ds(base + step * BLOCK * blk, BLOCK, stride=step)]
            prev = [pl.ds(base + step * BLOCK * back, BLOCK, stride=step)]
        return own, prev, blk == 0

    def gather(get, wins):
        parts = [get(w) for w in wins]
        return parts[0] if len(parts) == 1 else jnp.concatenate(parts, axis=0)

    def scatter(put, wins, value):
        rows = BLOCK // len(wins)
        for i, w in enumerate(wins):
            put(w, value[i * rows:(i + 1) * rows])

    def tiles(i, carry):
        slabs = []
        for j in range(DIL_TILES_PER_ITER):
            own, prev, first = windows(i * DIL_TILES_PER_ITER + j)
            key_wins = [w for pw, ow in zip(prev, own) for w in (pw, ow)]
            variant = first.astype(jnp.int32)
            slabs += [(pair, own, key_wins, variant) for pair in range(DIL_OUT_W // PAIR_W)]
        q = [gather(lambda w: q_ref[0, pair, w, :], own) for pair, own, _, _ in slabs]
        keys = [gather(lambda w: k_ref[0, pair, w, :], kw).astype(BF16) for pair, _, kw, _ in slabs]
        vals = [gather(lambda w: v_ref[0, pair, w, :], kw).astype(BF16) for pair, _, kw, _ in slabs]
        m_old = [gather(lambda w: m_ref[pair, w, :], own) for pair, own, _, _ in slabs]
        l_old = [gather(lambda w: l_ref[pair, w, :], own) for pair, own, _, _ in slabs]
        acc_old = [gather(lambda w: acc_ref[pair, w, :], own) for pair, own, _, _ in slabs]
        heads = [(s, h) for s in range(len(slabs)) for h in range(2)]
        logits = [lax.dot_general(jnp.where(head0 if h == 0 else ~head0, q[s], 0.0).astype(BF16),
                                  keys[s], _NT, preferred_element_type=F32) for s, h in heads]
        logits = [lg + bias_ref[slabs[s][3], 2 * slabs[s][0] + h] for lg, (s, h) in zip(logits, heads)]
        mx = [jnp.max(lg, axis=-1, keepdims=True) for lg in logits]
        p = [jnp.exp(lg - m) for lg, m in zip(logits, mx)]
        ls = [jnp.sum(ph, axis=-1, keepdims=True) for ph in p]
        pv = [_dot(ph.astype(BF16), vals[s]) for ph, (s, h) in zip(p, heads)]
        for s, (pair, own, _, _) in enumerate(slabs):
            m_t = jnp.where(head0, mx[2 * s], mx[2 * s + 1])
            l_t = jnp.where(head0, ls[2 * s], ls[2 * s + 1])
            acc_t = jnp.where(head0, pv[2 * s], pv[2 * s + 1])
            m_new = jnp.maximum(m_old[s], m_t)
            e_old = jnp.exp(m_old[s] - m_new)
            e_t = jnp.exp(m_t - m_new)

            def put(ref, value, pair=pair, own=own):
                def store(w, rows):
                    ref[pair, w, :] = rows
                scatter(store, own, value)

            put(m_ref, m_new)
            put(l_ref, l_old[s] * e_old + l_t * e_t)
            put(acc_ref, acc_old[s] * e_old + acc_t * e_t)
        return carry

    lax.fori_loop(0, seq_len // BLOCK // DIL_TILES_PER_ITER, tiles, 0)


def _dil_kernel(q_ref, k_ref, v_ref, bias_ref, o_ref, acc_ref, m_ref, l_ref):
    g = pl.program_id(1)

    @pl.when(g == 0)
    def _():
        m_ref[...] = jnp.full(m_ref.shape, NEG_INF, F32)
        l_ref[...] = jnp.zeros_like(l_ref)
        acc_ref[...] = jnp.zeros_like(acc_ref)

    for gi, (_, dil) in enumerate(DIL_GROUPS):
        pl.when(g == gi)(functools.partial(_dil_group, dil, q_ref, k_ref, v_ref, bias_ref, acc_ref, m_ref, l_ref))

    @pl.when(g == len(DIL_GROUPS) - 1)
    def _():
        phase_rows = o_ref.shape[2] // DIL_ROW_PHASES
        for pair in range(DIL_OUT_W // PAIR_W):
            for c in range(DIL_ROW_PHASES):
                rows = slice(c * phase_rows, (c + 1) * phase_rows)
                o_ref[0, pair, pl.ds(c, phase_rows, stride=DIL_ROW_PHASES), :] = (
                    acc_ref[pair, rows, :] / l_ref[pair, rows, :])


def _dilated_attention(q_slabs, kv_slabs, bias):
    b, _, s, _ = q_slabs.shape
    n_groups = len(DIL_GROUPS)
    pairs = DIL_OUT_W // PAIR_W
    blk = (1, pairs, s, PAIR_W)
    return pl.pallas_call(
        _dil_kernel,
        grid=(b, n_groups),
        in_specs=[
            pl.BlockSpec(blk, lambda i, g: (i, g, 0, 0)),
            pl.BlockSpec(blk, lambda i, g: (i, g, 0, 0)),
            pl.BlockSpec(blk, lambda i, g: (i, n_groups + g, 0, 0)),
            pl.BlockSpec((2, DIL_GROUP_HEADS, BLOCK, 2 * BLOCK), lambda i, g: (0, g, 0, 0)),
        ],
        out_specs=pl.BlockSpec(blk, lambda i, g: (i, 0, 0, 0)),
        out_shape=jax.ShapeDtypeStruct((b, pairs, s, PAIR_W), F32),
        scratch_shapes=[pltpu.VMEM((pairs, s, PAIR_W), F32)] * 3,
        compiler_params=_cparams(2, 56),
        name="dilated_attention",
    )(q_slabs, kv_slabs, kv_slabs, bias)


def kernel(x, mem, ffn_pre_norm, ffn_pre_w_in, ffn_pre_w_out, mix_norm, ffn_post_norm, ffn_post_w_in, ffn_post_w_out, mem_norm, mem_w_kv, mem_q_norm, mem_k_norm, a_w_in, a_shift_mu, a_w0, a_w_up, a_a0, a_a_up, a_g_up, a_kk_scale, a_k_a, a_r_k, a_lnx_g, a_lnx_b, a_w_out, b_w_q, b_q_norm, b_w_out, kv_norm, kv_w, kv_k_norm, rel_bias):
    b, s, d = x.shape
    depth = ffn_pre_w_in.shape[0]
    n_a = a_w_in.shape[0]
    n = b * s
    scale = 1.0 / math.sqrt(HEAD_DIM)
    ones_bd = _block_ones(SLAB_W, HEAD_DIM)
    row = lambda p: p.reshape(1, -1)

    k_mem, v_mem = _memkv(mem, mem_norm[:, None, :], mem_w_kv.astype(BF16),
                          jnp.tile(mem_k_norm, (1, MEM_HEADS))[:, None, :], ones_bd)
    mem_q_gain = jnp.tile(mem_q_norm, (1, MEM_HEADS)) * scale

    pre_w_in, pre_w_out = ffn_pre_w_in.astype(BF16), ffn_pre_w_out.astype(BF16)
    post_w_in, post_w_out = ffn_post_w_in.astype(BF16), ffn_post_w_out.astype(BF16)
    a_w_in_b = a_w_in.astype(BF16)
    xf = x.reshape(n, d)
    kv = None
    bias = None
    for layer in range(depth):
        xf = _ffn(xf, row(ffn_pre_norm[layer]), pre_w_in, pre_w_out, layer)
        if layer < n_a:
            i = layer
            ps, qm = _aproj(xf, row(mix_norm[layer]), a_w_in_b, i, row(a_shift_mu[i]),
                            row(mem_q_gain[layer]), ones_bd, s)
            zeros = jnp.zeros((DECAY_LORA, RWKV_W), F32)
            vecs = jnp.stack([a_w0[i], a_a0[i], a_kk_scale[i], a_k_a[i], a_r_k[i].reshape(-1),
                              a_lnx_g[i], a_lnx_b[i], jnp.zeros((RWKV_W,), F32)])
            y_main = _rwkv(ps.reshape(b, s, RWKV_SHIFT_W), vecs,
                           jnp.concatenate([a_w_up[i], zeros]).astype(BF16),
                           jnp.concatenate([zeros, a_a_up[i]]).astype(BF16),
                           a_g_up[i].astype(BF16), ones_bd)
            mix = (y_main.reshape(n, RWKV_W), qm, k_mem[layer], v_mem[layer], a_w_out[i].astype(BF16), s)
        else:
            j = layer - n_a
            q_gain = jnp.concatenate([jnp.tile(b_q_norm[j], DIL_W // HEAD_DIM) * scale, mem_q_gain[layer]])
            q_dil, qm = _qproj(xf, row(mix_norm[layer]), b_w_q[j].astype(BF16), row(q_gain), ones_bd, s, DIL_W)
            if bias is None:
                bias = _band_bias(rel_bias)
            y_dil = _dilated_attention(q_dil, kv, bias)
            mix = (y_dil, qm, k_mem[layer], v_mem[layer], b_w_out[j].astype(BF16), s)
        xf = _ffn(xf, row(ffn_post_norm[layer]), post_w_in, post_w_out, layer, mix=mix)
        if layer == n_a - 1:
            k_gain = jnp.tile(kv_k_norm, DIL_W // HEAD_DIM)
            kv, = _qproj(xf, row(kv_norm), kv_w.astype(BF16), row(k_gain), ones_bd, s, 2 * DIL_W)
    return xf.reshape(b, s, d)
```

```python
import functools
import math

import jax
import jax.numpy as jnp
import numpy as np
from jax import lax
from jax.experimental import pallas as pl
from jax.experimental.pallas import tpu as pltpu

F32 = jnp.float32
BF16 = jnp.bfloat16

HEAD_DIM = 64
MEM_HEADS = 4
MEM_W = MEM_HEADS * HEAD_DIM
RWKV_HEADS = 12
RWKV_W = RWKV_HEADS * HEAD_DIM
DECAY_LORA = 64
AAA_LORA = 64
GATE_LORA = 128
RWKV_SHIFT_W = 3 * RWKV_W + DECAY_LORA + AAA_LORA + GATE_LORA
LORA_LO = 3 * RWKV_W
DIL_GROUPS = ((128, 1), (512, 4), (2048, 16))
DIL_GROUP_HEADS = 4
DIL_W = len(DIL_GROUPS) * DIL_GROUP_HEADS * HEAD_DIM
DIL_OUT_W = DIL_GROUP_HEADS * HEAD_DIM
BLOCK = 128
NUM_BUCKETS = 32
MAX_DISTANCE = 2048
NORM_EPS = 1e-6
LNX_EPS = 64e-5
NEG_INF = -1e30

V7X_LANES = 128
V7X_SUBLANES = 8
V7X_VMEM_BYTES = 64 * 1024 * 1024
V7X_VMEM_LIMIT_CAP_BYTES = V7X_VMEM_BYTES * 15 // 16

RWKV_CHUNK = 64
RWKV_PHASE_A_CHUNKS = 8
FFN_ROWS = 1024
FFN_CHUNK = 256
APROJ_ROWS = 1024
PROJ_SUBS_IN_FLIGHT = 4
QPROJ_ROWS = 1024
RWKV_ROWS = 512
RWKV_CUMSUM_ROWS = 256
DIL_TILES_PER_ITER = 4
DIL_ROW_PHASES = 4
PROJ_SUB_ROWS = 128
PAIR_W = 2 * HEAD_DIM
SLAB_W = 4 * HEAD_DIM

_NT = (((1,), (1,)), ((), ()))
_TN = (((0,), (0,)), ((), ()))


def _cparams(n_axes, vmem_mib):
    assert vmem_mib * 1024 * 1024 <= V7X_VMEM_LIMIT_CAP_BYTES
    return pltpu.CompilerParams(
        dimension_semantics=("arbitrary",) * n_axes,
        vmem_limit_bytes=vmem_mib * 1024 * 1024,
    )


def _const_spec(shape):
    zeros = (0,) * len(shape)
    return pl.BlockSpec(shape, lambda *_: zeros)


def _dot(a, b):
    return jnp.dot(a, b, preferred_element_type=F32)


def _rms(x, g, eps):
    return x * lax.rsqrt(jnp.mean(x * x, axis=-1, keepdims=True) + eps) * g


def _split2(x):
    hi = x.astype(BF16)
    lo = (x - hi.astype(F32)).astype(BF16)
    return hi, lo


def _seg_sum(x, ones_bd, pieces=2):
    outs = []
    for s in range(x.shape[-1] // SLAB_W):
        slab = x[:, s * SLAB_W:(s + 1) * SLAB_W]
        if pieces == 1:
            outs.append(_dot(slab.astype(BF16), ones_bd))
        else:
            hi, lo = _split2(slab)
            outs.append(_dot(hi, ones_bd) + _dot(lo, ones_bd))
    return outs[0] if len(outs) == 1 else jnp.concatenate(outs, axis=-1)


def _block_ones(n, blk, lower=False):
    i = np.arange(n)
    m = (i[:, None] // blk) == (i[None, :] // blk)
    if lower:
        m = m & (i[:, None] >= i[None, :])
    return jnp.asarray(m, dtype=BF16)


def _mix_delta(ya_ref, qm_ref, k_ref, v_ref, w_ref):
    if len(ya_ref.shape) == 4:
        ya = jnp.concatenate([ya_ref[0, j] for j in range(ya_ref.shape[1])], axis=1)
    else:
        ya = ya_ref[...]
    wa = ya.shape[-1]
    qm = qm_ref[...]
    kmem = k_ref[0]
    vmem = v_ref[0]
    head = lax.broadcasted_iota(jnp.int32, qm.shape, 1) // HEAD_DIM
    heads = range(MEM_HEADS)
    logits = [lax.dot_general(jnp.where(head == h, qm, jnp.zeros_like(qm)), kmem, _NT,
                              preferred_element_type=F32) for h in heads]
    p = [jnp.exp(lg - jnp.max(lg, axis=-1, keepdims=True)) for lg in logits]
    inv_l = [1.0 / jnp.sum(ph, axis=-1, keepdims=True) for ph in p]
    pv = [_dot(ph.astype(BF16), vmem) for ph in p]
    y_mem = pv[0] * inv_l[0]
    for h in heads[1:]:
        y_mem = jnp.where(head == h, pv[h] * inv_l[h], y_mem)
    return _dot(ya.astype(BF16), w_ref[:wa, :]) + _dot(y_mem.astype(BF16), w_ref[wa:, :])


def _ffn_kernel(*refs, has_mix):
    if has_mix:
        x_ref, ya_ref, qm_ref, k_ref, v_ref, wmix_ref, g_ref, win_ref, wout_ref, o_ref = refs
        x = x_ref[...] + _mix_delta(ya_ref, qm_ref, k_ref, v_ref, wmix_ref)
    else:
        x_ref, g_ref, win_ref, wout_ref, o_ref = refs
        x = x_ref[...]
    d_ff = wout_ref.shape[1]
    xn = _rms(x, g_ref[...], NORM_EPS).astype(BF16)
    n_chunks = d_ff // FFN_CHUNK

    def first_stage(c):
        lo = c * FFN_CHUNK
        return (_dot(xn, win_ref[0, :, lo:lo + FFN_CHUNK]),
                _dot(xn, win_ref[0, :, d_ff + lo:d_ff + lo + FFN_CHUNK]))

    y = None
    pending = first_stage(0)
    for c in range(n_chunks):
        gate, up = pending
        if c + 1 < n_chunks:
            pending = first_stage(c + 1)
        act = (gate * jax.nn.sigmoid(gate) * up).astype(BF16)
        part = _dot(act, wout_ref[0, c * FFN_CHUNK:(c + 1) * FFN_CHUNK, :])
        y = part if y is None else y + part
    o_ref[...] = x + 0.5 * y


def _ffn(x, g, w_in, w_out, layer, mix=None, tm=FFN_ROWS):
    n, d = x.shape
    assert w_out.shape[1] % FFN_CHUNK == 0 and n % tm == 0
    operands = [x]
    in_specs = [pl.BlockSpec((tm, d), lambda i: (i, 0))]
    if mix is not None:
        ya, qm, k_mem, v_mem, w_mix, seq_len = mix
        tiles_per_seq = seq_len // tm
        if ya.ndim == 4:
            ya_spec = pl.BlockSpec((1, ya.shape[1], tm, V7X_LANES),
                                   lambda i: (i // tiles_per_seq, 0, i % tiles_per_seq, 0))
        else:
            ya_spec = pl.BlockSpec((tm, ya.shape[-1]), lambda i: (i, 0))
        mem_spec = pl.BlockSpec((1,) + k_mem.shape[1:], lambda i: (i // tiles_per_seq, 0, 0))
        operands += [ya, qm, k_mem, v_mem, w_mix]
        in_specs += [ya_spec, pl.BlockSpec((tm, MEM_W), lambda i: (i, 0)), mem_spec, mem_spec,
                     _const_spec(w_mix.shape)]
    operands += [g, w_in, w_out]
    in_specs += [
        _const_spec((1, d)),
        pl.BlockSpec((1,) + w_in.shape[1:], lambda i: (layer, 0, 0), pipeline_mode=pl.Buffered(1)),
        pl.BlockSpec((1,) + w_out.shape[1:], lambda i: (layer, 0, 0), pipeline_mode=pl.Buffered(1)),
    ]
    return pl.pallas_call(
        functools.partial(_ffn_kernel, has_mix=mix is not None),
        grid=(n // tm,),
        in_specs=in_specs,
        out_specs=pl.BlockSpec((tm, d), lambda i: (i, 0)),
        out_shape=jax.ShapeDtypeStruct((n, d), F32),
        compiler_params=_cparams(1, 60),
        name="ffn_mix" if mix is not None else "ffn",
    )(*operands)


def _memkv_kernel(mem_ref, g_ref, w_ref, kg_ref, ones_ref, k_ref, v_ref):
    m = _rms(mem_ref[0], g_ref[0], NORM_EPS).astype(BF16)
    kv = _dot(m, w_ref[0])
    k = kv[:, :MEM_W]
    ms = _seg_sum(k * k, ones_ref[...]) * (1.0 / HEAD_DIM)
    k_ref[0, 0] = (k * lax.rsqrt(ms + NORM_EPS) * kg_ref[0]).astype(BF16)
    v_ref[0, 0] = kv[:, MEM_W:].astype(BF16)


def _memkv(mem, mem_norm, w_kv, k_gain, ones_bd):
    b, m, d = mem.shape
    depth = w_kv.shape[0]
    out = jax.ShapeDtypeStruct((depth, b, m, MEM_W), BF16)
    return pl.pallas_call(
        _memkv_kernel,
        grid=(depth, b),
        in_specs=[
            pl.BlockSpec((1, m, d), lambda l, i: (i, 0, 0)),
            pl.BlockSpec((1, 1, d), lambda l, i: (l, 0, 0)),
            pl.BlockSpec((1, d, 2 * MEM_W), lambda l, i: (l, 0, 0)),
            pl.BlockSpec((1, 1, MEM_W), lambda l, i: (l, 0, 0)),
            _const_spec((SLAB_W, SLAB_W)),
        ],
        out_specs=[pl.BlockSpec((1, 1, m, MEM_W), lambda l, i: (l, i, 0, 0))] * 2,
        out_shape=[out, out],
        compiler_params=_cparams(2, 32),
        name="memkv",
    )(mem, mem_norm, w_kv, k_gain, ones_bd)


def _head_norm(q, gain, ones_bd):
    ms = _seg_sum(q * q, ones_bd, pieces=1) * (1.0 / HEAD_DIM)
    return q * lax.rsqrt(ms + NORM_EPS) * gain


def _aproj_kernel(x_ref, g_ref, w_ref, mu_ref, qg_ref, ones_ref, ps_ref, qm_ref, carry_ref, *, tiles_per_seq):
    i = pl.program_id(0)
    tm = x_ref.shape[0]
    u = _rms(x_ref[...], g_ref[...], NORM_EPS).astype(BF16)

    @pl.when(i % tiles_per_seq == 0)
    def _():
        carry_ref[...] = jnp.zeros_like(carry_ref)

    row = lax.broadcasted_iota(jnp.int32, (PROJ_SUB_ROWS, RWKV_SHIFT_W), 0)
    last = carry_ref[V7X_SUBLANES - 1:V7X_SUBLANES, :]
    group_rows = PROJ_SUB_ROWS * PROJ_SUBS_IN_FLIGHT
    for g0 in range(0, tm, group_rows):
        starts = range(g0, g0 + group_rows, PROJ_SUB_ROWS)
        projs = [_dot(u[r0:r0 + PROJ_SUB_ROWS], w_ref[0]) for r0 in starts]
        for r0, proj in zip(starts, projs):
            p = proj[:, :RWKV_SHIFT_W]
            prev = jnp.where(row == 0, last, pltpu.roll(p, 1, 0))
            ps_ref[r0:r0 + PROJ_SUB_ROWS, :] = p + mu_ref[...] * (prev - p)
            last = p[PROJ_SUB_ROWS - 1:, :]
            qm_ref[r0:r0 + PROJ_SUB_ROWS, :] = _head_norm(proj[:, RWKV_SHIFT_W:], qg_ref[...],
                                                          ones_ref[...]).astype(qm_ref.dtype)
    carry_ref[...] = projs[-1][PROJ_SUB_ROWS - V7X_SUBLANES:, :RWKV_SHIFT_W]


def _aproj(x, g, w_in, layer, mu, q_gain, ones_bd, seq_len, tm=APROJ_ROWS):
    n, d = x.shape
    return pl.pallas_call(
        functools.partial(_aproj_kernel, tiles_per_seq=seq_len // tm),
        grid=(n // tm,),
        in_specs=[
            pl.BlockSpec((tm, d), lambda i: (i, 0)),
            _const_spec((1, d)),
            pl.BlockSpec((1,) + w_in.shape[1:], lambda i: (layer, 0, 0), pipeline_mode=pl.Buffered(1)),
            _const_spec((1, RWKV_SHIFT_W)),
            _const_spec((1, MEM_W)),
            _const_spec((SLAB_W, SLAB_W)),
        ],
        out_specs=[pl.BlockSpec((tm, RWKV_SHIFT_W), lambda i: (i, 0)),
                   pl.BlockSpec((tm, MEM_W), lambda i: (i, 0))],
        out_shape=[jax.ShapeDtypeStruct((n, RWKV_SHIFT_W), F32),
                   jax.ShapeDtypeStruct((n, MEM_W), BF16)],
        scratch_shapes=[pltpu.VMEM((V7X_SUBLANES, RWKV_SHIFT_W), F32)],
        compiler_params=_cparams(1, 56),
        name="aproj",
    )(x, g, w_in, mu, q_gain, ones_bd)


def _qproj_kernel(x_ref, g_ref, w_ref, qg_ref, ones_ref, slab_ref, *rest, normed_w):
    stage_ref = rest[-1]
    flat_refs = rest[:-1]
    tm = x_ref.shape[0]
    u = _rms(x_ref[...], g_ref[...], NORM_EPS).astype(BF16)
    proj = _dot(u, w_ref[...])
    normed = _head_norm(proj[:, :normed_w], qg_ref[...], ones_ref[...])
    n_slabs = slab_ref.shape[1]
    for j in range(n_slabs):
        src = normed if (j + 1) * V7X_LANES <= normed_w else proj
        stage_ref[j] = src[:, j * V7X_LANES:(j + 1) * V7X_LANES]
    for j in range(n_slabs):
        for c in range(DIL_ROW_PHASES):
            slab_ref[0, j, c] = stage_ref[j, pl.ds(c, tm // DIL_ROW_PHASES, stride=DIL_ROW_PHASES), :]
    if flat_refs:
        flat_refs[0][...] = normed[:, n_slabs * V7X_LANES:].astype(flat_refs[0].dtype)


def _qproj(x, g, w, q_gain, ones_bd, seq_len, slab_w, tm=QPROJ_ROWS):
    n, d = x.shape
    wo = w.shape[1]
    normed_w = q_gain.shape[-1]
    n_slabs = slab_w // V7X_LANES
    tiles_per_seq = seq_len // tm
    phase_rows = seq_len // DIL_ROW_PHASES
    out_specs = [pl.BlockSpec((1, n_slabs, DIL_ROW_PHASES, tm // DIL_ROW_PHASES, V7X_LANES),
                              lambda i: (i // tiles_per_seq, 0, 0, i % tiles_per_seq, 0))]
    out_shape = [jax.ShapeDtypeStruct((n // seq_len, n_slabs, DIL_ROW_PHASES, phase_rows, V7X_LANES), F32)]
    if wo > slab_w:
        assert normed_w == wo
        out_specs.append(pl.BlockSpec((tm, wo - slab_w), lambda i: (i, 0)))
        out_shape.append(jax.ShapeDtypeStruct((n, wo - slab_w), BF16))
    outs = list(pl.pallas_call(
        functools.partial(_qproj_kernel, normed_w=normed_w),
        grid=(n // tm,),
        in_specs=[
            pl.BlockSpec((tm, d), lambda i: (i, 0)),
            _const_spec((1, d)),
            pl.BlockSpec(w.shape, lambda i: (0, 0), pipeline_mode=pl.Buffered(1)),
            _const_spec((1, normed_w)),
            _const_spec((SLAB_W, SLAB_W)),
        ],
        out_specs=out_specs,
        out_shape=out_shape,
        scratch_shapes=[pltpu.VMEM((n_slabs, tm, V7X_LANES), F32)],
        compiler_params=_cparams(1, 48),
        name="qproj",
    )(x, g, w, q_gain, ones_bd))
    outs[0] = outs[0].reshape(n // seq_len, n_slabs, seq_len, V7X_LANES)
    return outs


def _embed(x):
    head0 = lax.broadcasted_iota(jnp.int32, x.shape, 1) < HEAD_DIM
    zero = jnp.zeros_like(x)
    return jnp.concatenate([jnp.where(head0, x, zero), jnp.where(head0, zero, x)], axis=0)


def _tri_inverse(nmats, tpos, spos):
    eye = (tpos == spos).astype(F32)
    same = {s: (tpos >> s) == (spos >> s) for s in (3, 4, 5)}

    def mm(lhs, rhs):
        return [_dot(a, _embed(b)) for a, b in zip(lhs, rhs)]

    def bf(xs):
        return [x.astype(BF16) for x in xs]

    n8 = [jnp.where(same[3], n, 0.0) for n in nmats]
    n8b = bf(n8)
    n2 = mm(n8b, n8b)
    n2b = bf(n2)
    n4 = mm(n2b, n2b)
    t = mm(bf([eye + a for a in n8]), bf([eye + a for a in n2]))
    t = mm(bf(t), bf([eye + a for a in n4]))
    for lo, hi in ((3, 4), (4, 5), (5, None)):
        off = ~same[lo] if hi is None else (same[hi] & ~same[lo])
        tb = bf(t)
        z = mm(bf([jnp.where(off, n, 0.0) for n in nmats]), tb)
        t = [a + d for a, d in zip(t, mm(tb, bf(z)))]
    return t


def _rwkv_kernel(ps_ref, vec_ref, wup_ref, aup_ref, gup_ref, tri_ref, ones_ref, y_ref,
                 s_ref, rt_ref, kt_ref, bt_ref, at_ref, kh_ref, bh_ref, v_ref, gam_ref, yacc_ref,
                 bonus_ref, gate_ref, p_s, qt_s, rp_s, y0_s):
    tblk = ps_ref.shape[1]
    n_pairs = RWKV_W // PAIR_W

    @pl.when(pl.program_id(1) == 0)
    def _():
        s_ref[...] = jnp.zeros_like(s_ref)

    ps = ps_ref[0]
    r = ps[:, :RWKV_W]
    k = ps[:, RWKV_W:2 * RWKV_W]
    v = ps[:, 2 * RWKV_W:3 * RWKV_W]
    lora_in = ps[:, LORA_LO:LORA_LO + PAIR_W]
    g_lo = ps[:, LORA_LO + PAIR_W:]
    w0, a0, kk_scale, k_a = vec_ref[0:1, :], vec_ref[1:2, :], vec_ref[2:3, :], vec_ref[3:4, :]
    r_k, lnx_g, lnx_b = vec_ref[4:5, :], vec_ref[5:6, :], vec_ref[6:7, :]
    ones_bd = ones_ref[...]

    z = w0 + _dot(jnp.tanh(lora_in).astype(BF16), wup_ref[...])
    lw = -math.exp(-0.5) * jax.nn.sigmoid(z)
    a = jax.nn.sigmoid(a0 + _dot(lora_in.astype(BF16), aup_ref[...]))
    gate_ref[...] = _dot(jax.nn.sigmoid(g_lo).astype(BF16), gup_ref[...])
    kk = k * kk_scale
    kk = kk / jnp.maximum(jnp.sqrt(_seg_sum(kk * kk, ones_bd, pieces=1)), 1e-12)
    k2 = k * (1.0 + (a - 1.0) * k_a)
    kka = kk * a
    bonus_ref[...] = _seg_sum(r * k2 * r_k, ones_bd, pieces=1) * v

    hi, lo = _split2(lw)
    tri = tri_ref[...]
    gcum = jnp.concatenate(
        [_dot(tri, hi[s0:s0 + RWKV_CUMSUM_ROWS]) + _dot(tri, lo[s0:s0 + RWKV_CUMSUM_ROWS])
         for s0 in range(0, tblk, RWKV_CUMSUM_ROWS)], axis=0)
    n_chunks = tblk // RWKV_CHUNK
    gam_rows = [jnp.exp(gcum[(c + 1) * RWKV_CHUNK - 1:(c + 1) * RWKV_CHUNK, :]) for c in range(n_chunks)]
    gam = jnp.concatenate([jnp.broadcast_to(g, (RWKV_CHUNK, RWKV_W)) for g in gam_rows], axis=0)
    for c in range(n_chunks):
        gam_ref[c:c + 1, :] = gam_rows[c]
    e_neg = jnp.exp(-gcum)
    k_t = k2 * e_neg
    b_t = kka * e_neg
    rt_ref[...] = (r * jnp.exp(gcum)).astype(BF16)
    kt_ref[...] = k_t.astype(BF16)
    bt_ref[...] = b_t.astype(BF16)
    at_ref[...] = (-kk * jnp.exp(gcum - lw)).astype(BF16)
    kh_ref[...] = (k_t * gam).astype(BF16)
    bh_ref[...] = (b_t * gam).astype(BF16)
    v_ref[...] = v.astype(BF16)

    tpos = lax.broadcasted_iota(jnp.int32, (RWKV_CHUNK, PAIR_W), 0)
    spos = lax.broadcasted_iota(jnp.int32, (RWKV_CHUNK, PAIR_W), 1) & (HEAD_DIM - 1)
    strict = tpos > spos
    incl = tpos >= spos
    head0 = lax.broadcasted_iota(jnp.int32, (RWKV_CHUNK, PAIR_W), 1) < HEAD_DIM
    same_head = ((lax.broadcasted_iota(jnp.int32, (PAIR_W, PAIR_W), 0) < HEAD_DIM)
                 == (lax.broadcasted_iota(jnp.int32, (PAIR_W, PAIR_W), 1) < HEAD_DIM))

    pairs = range(n_pairs)
    lanes = [slice(p * PAIR_W, (p + 1) * PAIR_W) for p in pairs]

    def phase_a(i, carry):
        slots, at_v, rt_v, bt_v, kt_v, v_v, bh_v, kh_v = [], [], [], [], [], [], [], []
        for cc in range(RWKV_PHASE_A_CHUNKS):
            c = i * RWKV_PHASE_A_CHUNKS + cc
            rows = pl.ds(pl.multiple_of(c * RWKV_CHUNK, RWKV_CHUNK), RWKV_CHUNK)
            loaded = [ref[rows, :] for ref in (at_ref, rt_ref, bt_ref, kt_ref, v_ref, bh_ref, kh_ref)]
            for p in pairs:
                slots.append(c * n_pairs + p)
                for dst, x_c in zip((at_v, rt_v, bt_v, kt_v, v_v, bh_v, kh_v), loaded):
                    dst.append(x_c[:, lanes[p]])
        chains = range(len(slots))
        m4 = [lax.dot_general(jnp.concatenate([at_v[j], rt_v[j]], axis=0),
                              jnp.concatenate([_embed(bt_v[j]), _embed(kt_v[j])], axis=0), _NT,
                              preferred_element_type=F32) for j in chains]
        n_ab = [jnp.where(strict, m[:RWKV_CHUNK, :PAIR_W], 0.0) for m in m4]
        a_ak = [jnp.where(strict, m[:RWKV_CHUNK, PAIR_W:], 0.0).astype(BF16) for m in m4]
        a_rb = [jnp.where(incl, m[RWKV_CHUNK:, :PAIR_W], 0.0).astype(BF16) for m in m4]
        a_rk = [jnp.where(incl, m[RWKV_CHUNK:, PAIR_W:], 0.0).astype(BF16) for m in m4]
        v_m = [_embed(x) for x in v_v]
        akv = [_dot(a_ak[j], v_m[j]).astype(BF16) for j in chains]
        t_inv = [t.astype(BF16) for t in _tri_inverse(n_ab, tpos, spos)]
        wu0 = [_dot(t_inv[j], jnp.concatenate([_embed(at_v[j]), _embed(akv[j])], axis=1))
               for j in chains]
        w_b = [x[:, :PAIR_W].astype(BF16) for x in wu0]
        u0_b = [x[:, PAIR_W:].astype(BF16) for x in wu0]
        p_m = [jnp.where(same_head, lax.dot_general(bh_v[j], w_b[j], _TN, preferred_element_type=F32), 0.0)
               for j in chains]
        q_full = [lax.dot_general(jnp.concatenate([u0_b[j], v_v[j]], axis=0),
                                  jnp.concatenate([bh_v[j], kh_v[j]], axis=0), _TN,
                                  preferred_element_type=F32) for j in chains]
        q_t = [jnp.where(head0, q[:RWKV_CHUNK], q[RWKV_CHUNK:]) for q in q_full]
        r_p = [rt_v[j].astype(F32) + _dot(a_rb[j], _embed(w_b[j])) for j in chains]
        y_0 = [_dot(jnp.concatenate([a_rb[j], a_rk[j]], axis=1),
                    jnp.concatenate([_embed(u0_b[j]), v_m[j]], axis=0)) for j in chains]
        for j in chains:
            p_s[slots[j]] = p_m[j].astype(BF16)
            qt_s[slots[j]] = q_t[j]
            rp_s[slots[j]] = r_p[j].astype(BF16)
            y0_s[slots[j]] = y_0[j]
        return carry

    lax.fori_loop(0, n_chunks // RWKV_PHASE_A_CHUNKS, phase_a, 0)

    states = [s_ref[p] for p in pairs]
    for c in range(n_chunks):
        r0 = c * RWKV_CHUNK
        gam_c = gam_ref[c:c + 1, :]
        slots = [c * n_pairs + p for p in pairs]
        state_b = [s.astype(BF16) for s in states]
        upd = [lax.dot_general(state_b[p], p_s[slots[p]], _NT, preferred_element_type=F32) for p in pairs]
        y_c = [lax.dot_general(rp_s[slots[p]], _embed(state_b[p]), _NT, preferred_element_type=F32)
               + y0_s[slots[p]] for p in pairs]
        states = [states[p] * gam_c[:, lanes[p]] + upd[p] + qt_s[slots[p]] for p in pairs]
        yacc_ref[r0:r0 + RWKV_CHUNK, :] = jnp.concatenate(y_c, axis=1)
    for p in pairs:
        s_ref[p] = states[p]

    y = yacc_ref[...]
    mean = _seg_sum(y, ones_bd) * (1.0 / HEAD_DIM)
    dev = y - mean
    var = _seg_sum(dev * dev, ones_bd, pieces=1) * (1.0 / HEAD_DIM)
    yn = dev * lax.rsqrt(var + LNX_EPS) * lnx_g + lnx_b
    y_ref[0] = ((yn + bonus_ref[...]) * gate_ref[...]).astype(y_ref.dtype)


def _rwkv(ps, vecs, w_up_p, a_up_p, g_up, ones_bd, tblk=RWKV_ROWS):
    b, s, _ = ps.shape
    assert s % tblk == 0 and tblk % (RWKV_PHASE_A_CHUNKS * RWKV_CHUNK) == 0 and tblk % RWKV_CUMSUM_ROWS == 0
    tri = _block_ones(RWKV_CUMSUM_ROWS, RWKV_CHUNK, lower=True)
    act = pltpu.VMEM((tblk, RWKV_W), F32)
    act_b = pltpu.VMEM((tblk, RWKV_W), BF16)
    n_pairs = RWKV_W // PAIR_W
    n_chunks = tblk // RWKV_CHUNK
    n_slots = n_chunks * n_pairs
    gam_rows = -(-n_chunks // V7X_SUBLANES) * V7X_SUBLANES
    scratch = [pltpu.VMEM((n_pairs, HEAD_DIM, PAIR_W), F32)]
    scratch += [act_b] * 7
    scratch += [pltpu.VMEM((gam_rows, RWKV_W), F32)] + [act] * 3
    scratch += [
        pltpu.VMEM((n_slots, PAIR_W, PAIR_W), BF16),
        pltpu.VMEM((n_slots, RWKV_CHUNK, PAIR_W), F32),
        pltpu.VMEM((n_slots, RWKV_CHUNK, PAIR_W), BF16),
        pltpu.VMEM((n_slots, RWKV_CHUNK, PAIR_W), F32),
    ]
    return pl.pallas_call(
        _rwkv_kernel,
        grid=(b, s // tblk),
        in_specs=[
            pl.BlockSpec((1, tblk, RWKV_SHIFT_W), lambda i, t: (i, t, 0)),
            _const_spec(vecs.shape),
            _const_spec(w_up_p.shape),
            _const_spec(a_up_p.shape),
            _const_spec(g_up.shape),
            _const_spec(tri.shape),
            _const_spec((SLAB_W, SLAB_W)),
        ],
        out_specs=pl.BlockSpec((1, tblk, RWKV_W), lambda i, t: (i, t, 0)),
        out_shape=jax.ShapeDtypeStruct((b, s, RWKV_W), BF16),
        scratch_shapes=scratch,
        compiler_params=_cparams(2, 48),
        name="rwkv",
    )(ps, vecs, w_up_p, a_up_p, g_up, tri, ones_bd)


def _t5_bucket(dist):
    max_exact = NUM_BUCKETS // 2
    d_f = jnp.maximum(dist, 1).astype(F32)
    large = max_exact + (jnp.log(d_f / max_exact) / math.log(MAX_DISTANCE / max_exact)
                         * (NUM_BUCKETS - max_exact)).astype(jnp.int32)
    large = jnp.minimum(large, NUM_BUCKETS - 1)
    return jnp.where(dist < max_exact, dist, large)


def _band_buckets():
    run = BLOCK // DIL_ROW_PHASES
    out = []
    for window, dil in DIL_GROUPS:
        rows_q = np.arange(BLOCK)
        rows_k = np.arange(2 * BLOCK)
        if dil < DIL_ROW_PHASES:
            u_q = DIL_ROW_PHASES * (rows_q % run) + rows_q // run
            u_k = ((rows_k % (2 * run)) // run) * BLOCK + DIL_ROW_PHASES * (rows_k % run) + rows_k // (2 * run)
        else:
            u_q, u_k = rows_q, rows_k
        dsub = jnp.asarray(BLOCK + u_q[:, None] - u_k[None, :])
        band = (dsub >= 0) & (dsub <= window // dil)
        idx = jnp.where(band, _t5_bucket(jnp.maximum(dsub, 0) * dil), -1)
        out.append(jnp.stack([idx, jnp.where(jnp.asarray(u_k < BLOCK)[None, :], -1, idx)]))
    return jnp.stack(out, axis=1).astype(jnp.int32)


def _bias_kernel(tab_ref, idx_ref, o_ref):
    head = pl.program_id(1)
    idx = idx_ref[0, 0]
    acc = jnp.full(idx.shape, NEG_INF, F32)
    for bucket in range(NUM_BUCKETS):
        acc = jnp.where(idx == bucket, tab_ref[bucket, head], acc)
    o_ref[0, 0] = acc


def _band_bias(rel_bias):
    n_heads = rel_bias.shape[1]
    return pl.pallas_call(
        _bias_kernel,
        grid=(2, n_heads),
        in_specs=[
            pl.BlockSpec(memory_space=pltpu.SMEM),
            pl.BlockSpec((1, 1, BLOCK, 2 * BLOCK), lambda v, h: (v, h // DIL_GROUP_HEADS, 0, 0)),
        ],
        out_specs=pl.BlockSpec((1, 1, BLOCK, 2 * BLOCK), lambda v, h: (v, h, 0, 0)),
        out_shape=jax.ShapeDtypeStruct((2, n_heads, BLOCK, 2 * BLOCK), F32),
        compiler_params=_cparams(2, 16),
        name="band_bias",
    )(rel_bias, _band_buckets())


def _dil_group(dil, q_ref, k_ref, v_ref, bias_ref, acc_ref, m_ref, l_ref):
    seq_len = q_ref.shape[2]
    phase_rows = seq_len // DIL_ROW_PHASES
    run = BLOCK // DIL_ROW_PHASES
    head0 = lax.broadcasted_iota(jnp.int32, (BLOCK, PAIR_W), 1) < HEAD_DIM

    def windows(t):
        if dil < DIL_ROW_PHASES:
            blk = t
            back = jnp.maximum(blk - 1, 0)
            own = [pl.ds(pl.multiple_of(c * phase_rows + run * blk, run), run) for c in range(DIL_ROW_PHASES)]
            prev = [pl.ds(pl.multiple_of(c * phase_rows + run * back, run), run) for c in range(DIL_ROW_PHASES)]
            return own, prev, blk == 0
        step = dil // DIL_ROW_PHASES
        n_blocks = phase_rows // (step * BLOCK)
        sub = t >> (n_blocks.bit_length() - 1)
        blk = t & (n_blocks - 1)
        base = (sub & (DIL_ROW_PHASES - 1)) * phase_rows + (sub >> (DIL_ROW_PHASES.bit_length() - 1))
        back = jnp.maximum(blk - 1, 0)
        if step == 1:
            own = [pl.ds(pl.multiple_of(base + BLOCK * blk, BLOCK), BLOCK)]
            prev = [pl.ds(pl.multiple_of(base + BLOCK * back, BLOCK), BLOCK)]
        else:
            own = [pl.ds(base + step * BLOCK * blk, BLOCK, stride=step)]
            prev = [pl.ds(base + step * BLOCK * back, BLOCK, stride=step)]
        return own, prev, blk == 0

    def gather(get, wins):
        parts = [get(w) for w in wins]
        return parts[0] if len(parts) == 1 else jnp.concatenate(parts, axis=0)

    def scatter(put, wins, value):
        rows = BLOCK // len(wins)
        for i, w in enumerate(wins):
            put(w, value[i * rows:(i + 1) * rows])

    def tiles(i, carry):
        slabs = []
        for j in range(DIL_TILES_PER_ITER):
            own, prev, first = windows(i * DIL_TILES_PER_ITER + j)
            key_wins = [w for pw, ow in zip(prev, own) for w in (pw, ow)]
            variant = first.astype(jnp.int32)
            slabs += [(pair, own, key_wins, variant) for pair in range(DIL_OUT_W // PAIR_W)]
        q = [gather(lambda w: q_ref[0, pair, w, :], own) for pair, own, _, _ in slabs]
        keys = [gather(lambda w: k_ref[0, pair, w, :], kw).astype(BF16) for pair, _, kw, _ in slabs]
        vals = [gather(lambda w: v_ref[0, pair, w, :], kw).astype(BF16) for pair, _, kw, _ in slabs]
        m_old = [gather(lambda w: m_ref[pair, w, :], own) for pair, own, _, _ in slabs]
        l_old = [gather(lambda w: l_ref[pair, w, :], own) for pair, own, _, _ in slabs]
        acc_old = [gather(lambda w: acc_ref[pair, w, :], own) for pair, own, _, _ in slabs]
        heads = [(s, h) for s in range(len(slabs)) for h in range(2)]
        logits = [lax.dot_general(jnp.where(head0 if h == 0 else ~head0, q[s], 0.0).astype(BF16),
                                  keys[s], _NT, preferred_element_type=F32) for s, h in heads]
        logits = [lg + bias_ref[slabs[s][3], 2 * slabs[s][0] + h] for lg, (s, h) in zip(logits, heads)]
        mx = [jnp.max(lg, axis=-1, keepdims=True) for lg in logits]
        p = [jnp.exp(lg - m) for lg, m in zip(logits, mx)]
        ls = [jnp.sum(ph, axis=-1, keepdims=True) for ph in p]
        pv = [_dot(ph.astype(BF16), vals[s]) for ph, (s, h) in zip(p, heads)]
        for s, (pair, own, _, _) in enumerate(slabs):
            m_t = jnp.where(head0, mx[2 * s], mx[2 * s + 1])
            l_t = jnp.where(head0, ls[2 * s], ls[2 * s + 1])
            acc_t = jnp.where(head0, pv[2 * s], pv[2 * s + 1])
            m_new = jnp.maximum(m_old[s], m_t)
            e_old = jnp.exp(m_old[s] - m_new)
            e_t = jnp.exp(m_t - m_new)

            def put(ref, value, pair=pair, own=own):
                def store(w, rows):
                    ref[pair, w, :] = rows
                scatter(store, own, value)

            put(m_ref, m_new)
            put(l_ref, l_old[s] * e_old + l_t * e_t)
            put(acc_ref, acc_old[s] * e_old + acc_t * e_t)
        return carry

    lax.fori_loop(0, seq_len // BLOCK // DIL_TILES_PER_ITER, tiles, 0)


def _dil_kernel(q_ref, k_ref, v_ref, bias_ref, o_ref, acc_ref, m_ref, l_ref):
    g = pl.program_id(1)

    @pl.when(g == 0)
    def _():
        m_ref[...] = jnp.full(m_ref.shape, NEG_INF, F32)
        l_ref[...] = jnp.zeros_like(l_ref)
        acc_ref[...] = jnp.zeros_like(acc_ref)

    for gi, (_, dil) in enumerate(DIL_GROUPS):
        pl.when(g == gi)(functools.partial(_dil_group, dil, q_ref, k_ref, v_ref, bias_ref, acc_ref, m_ref, l_ref))

    @pl.when(g == len(DIL_GROUPS) - 1)
    def _():
        phase_rows = o_ref.shape[2] // DIL_ROW_PHASES
        for pair in range(DIL_OUT_W // PAIR_W):
            for c in range(DIL_ROW_PHASES):
                rows = slice(c * phase_rows, (c + 1) * phase_rows)
                o_ref[0, pair, pl.ds(c, phase_rows, stride=DIL_ROW_PHASES), :] = (
                    acc_ref[pair, rows, :] / l_ref[pair, rows, :])


def _dilated_attention(q_slabs, kv_slabs, bias):
    b, _, s, _ = q_slabs.shape
    n_groups = len(DIL_GROUPS)
    pairs = DIL_OUT_W // PAIR_W
    blk = (1, pairs, s, PAIR_W)
    return pl.pallas_call(
        _dil_kernel,
        grid=(b, n_groups),
        in_specs=[
            pl.BlockSpec(blk, lambda i, g: (i, g, 0, 0)),
            pl.BlockSpec(blk, lambda i, g: (i, g, 0, 0)),
            pl.BlockSpec(blk, lambda i, g: (i, n_groups + g, 0, 0)),
            pl.BlockSpec((2, DIL_GROUP_HEADS, BLOCK, 2 * BLOCK), lambda i, g: (0, g, 0, 0)),
        ],
        out_specs=pl.BlockSpec(blk, lambda i, g: (i, 0, 0, 0)),
        out_shape=jax.ShapeDtypeStruct((b, pairs, s, PAIR_W), F32),
        scratch_shapes=[pltpu.VMEM((pairs, s, PAIR_W), F32)] * 3,
        compiler_params=_cparams(2, 56),
        name="dilated_attention",
    )(q_slabs, kv_slabs, kv_slabs, bias)


def kernel(x, mem, ffn_pre_norm, ffn_pre_w_in, ffn_pre_w_out, mix_norm, ffn_post_norm, ffn_post_w_in, ffn_post_w_out, mem_norm, mem_w_kv, mem_q_norm, mem_k_norm, a_w_in, a_shift_mu, a_w0, a_w_up, a_a0, a_a_up, a_g_up, a_kk_scale, a_k_a, a_r_k, a_lnx_g, a_lnx_b, a_w_out, b_w_q, b_q_norm, b_w_out, kv_norm, kv_w, kv_k_norm, rel_bias):
    b, s, d = x.shape
    depth = ffn_pre_w_in.shape[0]
    n_a = a_w_in.shape[0]
    n = b * s
    scale = 1.0 / math.sqrt(HEAD_DIM)
    ones_bd = _block_ones(SLAB_W, HEAD_DIM)
    row = lambda p: p.reshape(1, -1)

    k_mem, v_mem = _memkv(mem, mem_norm[:, None, :], mem_w_kv.astype(BF16),
                          jnp.tile(mem_k_norm, (1, MEM_HEADS))[:, None, :], ones_bd)
    mem_q_gain = jnp.tile(mem_q_norm, (1, MEM_HEADS)) * scale

    pre_w_in, pre_w_out = ffn_pre_w_in.astype(BF16), ffn_pre_w_out.astype(BF16)
    post_w_in, post_w_out = ffn_post_w_in.astype(BF16), ffn_post_w_out.astype(BF16)
    a_w_in_b = a_w_in.astype(BF16)
    xf = x.reshape(n, d)
    kv = None
    bias = None
    for layer in range(depth):
        xf = _ffn(xf, row(ffn_pre_norm[layer]), pre_w_in, pre_w_out, layer)
        if layer < n_a:
            i = layer
            ps, qm = _aproj(xf, row(mix_norm[layer]), a_w_in_b, i, row(a_shift_mu[i]),
                            row(mem_q_gain[layer]), ones_bd, s)
            zeros = jnp.zeros((DECAY_LORA, RWKV_W), F32)
            vecs = jnp.stack([a_w0[i], a_a0[i], a_kk_scale[i], a_k_a[i], a_r_k[i].reshape(-1),
                              a_lnx_g[i], a_lnx_b[i], jnp.zeros((RWKV_W,), F32)])
            y_main = _rwkv(ps.reshape(b, s, RWKV_SHIFT_W), vecs,
                           jnp.concatenate([a_w_up[i], zeros]).astype(BF16),
                           jnp.concatenate([zeros, a_a_up[i]]).astype(BF16),
                           a_g_up[i].astype(BF16), ones_bd)
            mix = (y_main.reshape(n, RWKV_W), qm, k_mem[layer], v_mem[layer], a_w_out[i].astype(BF16), s)
        else:
            j = layer - n_a
            q_gain = jnp.concatenate([jnp.tile(b_q_norm[j], DIL_W // HEAD_DIM) * scale, mem_q_gain[layer]])
            q_dil, qm = _qproj(xf, row(mix_norm[layer]), b_w_q[j].astype(BF16), row(q_gain), ones_bd, s, DIL_W)
            if bias is None:
                bias = _band_bias(rel_bias)
            y_dil = _dilated_attention(q_dil, kv, bias)
            mix = (y_dil, qm, k_mem[layer], v_mem[layer], b_w_out[j].astype(BF16), s)
        xf = _ffn(xf, row(ffn_post_norm[layer]), post_w_in, post_w_out, layer, mix=mix)
        if layer == n_a - 1:
            k_gain = jnp.tile(kv_k_norm, DIL_W // HEAD_DIM)
            kv, = _qproj(xf, row(kv_norm), kv_w.astype(BF16), row(k_gain), ones_bd, s, 2 * DIL_W)
    return xf.reshape(b, s, d)
```

```python
import functools
import math

import jax
import jax.numpy as jnp
import numpy as np
from jax import lax
from jax.experimental import pallas as pl
from jax.experimental.pallas import tpu as pltpu

F32 = jnp.float32
BF16 = jnp.bfloat16

HEAD_DIM = 64
MEM_HEADS = 4
MEM_W = MEM_HEADS * HEAD_DIM
RWKV_HEADS = 12
RWKV_W = RWKV_HEADS * HEAD_DIM
DECAY_LORA = 64
AAA_LORA = 64
GATE_LORA = 128
RWKV_SHIFT_W = 3 * RWKV_W + DECAY_LORA + AAA_LORA + GATE_LORA
LORA_LO = 3 * RWKV_W
DIL_GROUPS = ((128, 1), (512, 4), (2048, 16))
DIL_GROUP_HEADS = 4
DIL_W = len(DIL_GROUPS) * DIL_GROUP_HEADS * HEAD_DIM
DIL_OUT_W = DIL_GROUP_HEADS * HEAD_DIM
BLOCK = 128
NUM_BUCKETS = 32
MAX_DISTANCE = 2048
NORM_EPS = 1e-6
LNX_EPS = 64e-5
NEG_INF = -1e30
LOG2_E = math.log2(math.e)

V7X_LANES = 128
V7X_SUBLANES = 8
V7X_VMEM_BYTES = 64 * 1024 * 1024
V7X_VMEM_LIMIT_CAP_BYTES = V7X_VMEM_BYTES * 15 // 16

RWKV_CHUNK = 64
RWKV_PHASE_A_CHUNKS = 8
FFN_ROWS = 1024
FFN_CHUNK = 256
APROJ_ROWS = 1024
PROJ_SUBS_IN_FLIGHT = 4
QPROJ_ROWS = 1024
RWKV_ROWS = 512
RWKV_CUMSUM_ROWS = 256
DIL_TILES_PER_ITER = 4
DIL_ROW_PHASES = 4
PROJ_SUB_ROWS = 128
PAIR_W = 2 * HEAD_DIM
SLAB_W = 4 * HEAD_DIM

_NT = (((1,), (1,)), ((), ()))
_TN = (((0,), (0,)), ((), ()))


def _cparams(n_axes, vmem_mib):
    assert vmem_mib * 1024 * 1024 <= V7X_VMEM_LIMIT_CAP_BYTES
    return pltpu.CompilerParams(
        dimension_semantics=("arbitrary",) * n_axes,
        vmem_limit_bytes=vmem_mib * 1024 * 1024,
    )


def _const_spec(shape):
    zeros = (0,) * len(shape)
    return pl.BlockSpec(shape, lambda *_: zeros)


def _dot(a, b):
    return jnp.dot(a, b, preferred_element_type=F32)


def _rms(x, g, eps):
    return x * lax.rsqrt(jnp.mean(x * x, axis=-1, keepdims=True) + eps) * g


def _split2(x):
    hi = x.astype(BF16)
    lo = (x - hi.astype(F32)).astype(BF16)
    return hi, lo


def _seg_sum(x, ones_bd, pieces=2):
    outs = []
    for s in range(x.shape[-1] // SLAB_W):
        slab = x[:, s * SLAB_W:(s + 1) * SLAB_W]
        if pieces == 1:
            outs.append(_dot(slab.astype(BF16), ones_bd))
        else:
            hi, lo = _split2(slab)
            outs.append(_dot(hi, ones_bd) + _dot(lo, ones_bd))
    return outs[0] if len(outs) == 1 else jnp.concatenate(outs, axis=-1)


def _block_ones(n, blk, lower=False):
    i = np.arange(n)
    m = (i[:, None] // blk) == (i[None, :] // blk)
    if lower:
        m = m & (i[:, None] >= i[None, :])
    return jnp.asarray(m, dtype=BF16)


def _mix_delta(ya_ref, qm_ref, k_ref, v_ref, w_ref):
    if len(ya_ref.shape) == 4:
        ya = jnp.concatenate([ya_ref[0, j] for j in range(ya_ref.shape[1])], axis=1)
    else:
        ya = ya_ref[...]
    wa = ya.shape[-1]
    qm = qm_ref[...]
    kmem = k_ref[0]
    vmem = v_ref[0]
    head = lax.broadcasted_iota(jnp.int32, qm.shape, 1) // HEAD_DIM
    heads = range(MEM_HEADS)
    logits = [lax.dot_general(jnp.where(head == h, qm, jnp.zeros_like(qm)), kmem, _NT,
                              preferred_element_type=F32) for h in heads]
    p = [jnp.exp2(lg - jnp.max(lg, axis=-1, keepdims=True)) for lg in logits]
    inv_l = [1.0 / jnp.sum(ph, axis=-1, keepdims=True) for ph in p]
    pv = [_dot(ph.astype(BF16), vmem) for ph in p]
    y_mem = pv[0] * inv_l[0]
    for h in heads[1:]:
        y_mem = jnp.where(head == h, pv[h] * inv_l[h], y_mem)
    return _dot(ya.astype(BF16), w_ref[:wa, :]) + _dot(y_mem.astype(BF16), w_ref[wa:, :])


def _ffn_kernel(*refs, has_mix):
    if has_mix:
        x_ref, ya_ref, qm_ref, k_ref, v_ref, wmix_ref, g_ref, win_ref, wout_ref, o_ref = refs
        x = x_ref[...] + _mix_delta(ya_ref, qm_ref, k_ref, v_ref, wmix_ref)
    else:
        x_ref, g_ref, win_ref, wout_ref, o_ref = refs
        x = x_ref[...]
    d_ff = wout_ref.shape[1]
    xn = _rms(x, g_ref[...], NORM_EPS).astype(BF16)
    n_chunks = d_ff // FFN_CHUNK

    def first_stage(c):
        lo = c * FFN_CHUNK
        return (_dot(xn, win_ref[0, :, lo:lo + FFN_CHUNK]),
                _dot(xn, win_ref[0, :, d_ff + lo:d_ff + lo + FFN_CHUNK]))

    y = None
    pending = first_stage(0)
    for c in range(n_chunks):
        gate, up = pending
        if c + 1 < n_chunks:
            pending = first_stage(c + 1)
        act = (gate * jax.nn.sigmoid(gate) * up).astype(BF16)
        part = _dot(act, wout_ref[0, c * FFN_CHUNK:(c + 1) * FFN_CHUNK, :])
        y = part if y is None else y + part
    o_ref[...] = x + 0.5 * y


def _ffn(x, g, w_in, w_out, layer, mix=None, tm=FFN_ROWS):
    n, d = x.shape
    assert w_out.shape[1] % FFN_CHUNK == 0 and n % tm == 0
    operands = [x]
    in_specs = [pl.BlockSpec((tm, d), lambda i: (i, 0))]
    if mix is not None:
        ya, qm, k_mem, v_mem, w_mix, seq_len = mix
        tiles_per_seq = seq_len // tm
        if ya.ndim == 4:
            ya_spec = pl.BlockSpec((1, ya.shape[1], tm, V7X_LANES),
                                   lambda i: (i // tiles_per_seq, 0, i % tiles_per_seq, 0))
        else:
            ya_spec = pl.BlockSpec((tm, ya.shape[-1]), lambda i: (i, 0))
        mem_spec = pl.BlockSpec((1,) + k_mem.shape[1:], lambda i: (i // tiles_per_seq, 0, 0))
        operands += [ya, qm, k_mem, v_mem, w_mix]
        in_specs += [ya_spec, pl.BlockSpec((tm, MEM_W), lambda i: (i, 0)), mem_spec, mem_spec,
                     _const_spec(w_mix.shape)]
    operands += [g, w_in, w_out]
    in_specs += [
        _const_spec((1, d)),
        pl.BlockSpec((1,) + w_in.shape[1:], lambda i: (layer, 0, 0), pipeline_mode=pl.Buffered(1)),
        pl.BlockSpec((1,) + w_out.shape[1:], lambda i: (layer, 0, 0), pipeline_mode=pl.Buffered(1)),
    ]
    return pl.pallas_call(
        functools.partial(_ffn_kernel, has_mix=mix is not None),
        grid=(n // tm,),
        in_specs=in_specs,
        out_specs=pl.BlockSpec((tm, d), lambda i: (i, 0)),
        out_shape=jax.ShapeDtypeStruct((n, d), F32),
        compiler_params=_cparams(1, 60),
        name="ffn_mix" if mix is not None else "ffn",
    )(*operands)


def _memkv_kernel(mem_ref, g_ref, w_ref, kg_ref, ones_ref, k_ref, v_ref):
    m = _rms(mem_ref[0], g_ref[0], NORM_EPS).astype(BF16)
    kv = _dot(m, w_ref[0])
    k = kv[:, :MEM_W]
    ms = _seg_sum(k * k, ones_ref[...]) * (1.0 / HEAD_DIM)
    k_ref[0, 0] = (k * lax.rsqrt(ms + NORM_EPS) * kg_ref[0]).astype(BF16)
    v_ref[0, 0] = kv[:, MEM_W:].astype(BF16)


def _memkv(mem, mem_norm, w_kv, k_gain, ones_bd):
    b, m, d = mem.shape
    depth = w_kv.shape[0]
    out = jax.ShapeDtypeStruct((depth, b, m, MEM_W), BF16)
    return pl.pallas_call(
        _memkv_kernel,
        grid=(depth, b),
        in_specs=[
            pl.BlockSpec((1, m, d), lambda l, i: (i, 0, 0)),
            pl.BlockSpec((1, 1, d), lambda l, i: (l, 0, 0)),
            pl.BlockSpec((1, d, 2 * MEM_W), lambda l, i: (l, 0, 0)),
            pl.BlockSpec((1, 1, MEM_W), lambda l, i: (l, 0, 0)),
            _const_spec((SLAB_W, SLAB_W)),
        ],
        out_specs=[pl.BlockSpec((1, 1, m, MEM_W), lambda l, i: (l, i, 0, 0))] * 2,
        out_shape=[out, out],
        compiler_params=_cparams(2, 32),
        name="memkv",
    )(mem, mem_norm, w_kv, k_gain, ones_bd)


def _head_norm(q, gain, ones_bd):
    ms = _seg_sum(q * q, ones_bd, pieces=1) * (1.0 / HEAD_DIM)
    return q * lax.rsqrt(ms + NORM_EPS) * gain


def _aproj_kernel(x_ref, g_ref, w_ref, mu_ref, qg_ref, ones_ref, ps_ref, qm_ref, carry_ref, *, tiles_per_seq):
    i = pl.program_id(0)
    tm = x_ref.shape[0]
    u = _rms(x_ref[...], g_ref[...], NORM_EPS).astype(BF16)

    @pl.when(i % tiles_per_seq == 0)
    def _():
        carry_ref[...] = jnp.zeros_like(carry_ref)

    row = lax.broadcasted_iota(jnp.int32, (PROJ_SUB_ROWS, RWKV_SHIFT_W), 0)
    last = carry_ref[V7X_SUBLANES - 1:V7X_SUBLANES, :]
    group_rows = PROJ_SUB_ROWS * PROJ_SUBS_IN_FLIGHT
    for g0 in range(0, tm, group_rows):
        starts = range(g0, g0 + group_rows, PROJ_SUB_ROWS)
        projs = [_dot(u[r0:r0 + PROJ_SUB_ROWS], w_ref[0]) for r0 in starts]
        for r0, proj in zip(starts, projs):
            p = proj[:, :RWKV_SHIFT_W]
            prev = jnp.where(row == 0, last, pltpu.roll(p, 1, 0))
            ps_ref[r0:r0 + PROJ_SUB_ROWS, :] = p + mu_ref[...] * (prev - p)
            last = p[PROJ_SUB_ROWS - 1:, :]
            qm_ref[r0:r0 + PROJ_SUB_ROWS, :] = _head_norm(proj[:, RWKV_SHIFT_W:], qg_ref[...],
                                                          ones_ref[...]).astype(qm_ref.dtype)
    carry_ref[...] = projs[-1][PROJ_SUB_ROWS - V7X_SUBLANES:, :RWKV_SHIFT_W]


def _aproj(x, g, w_in, layer, mu, q_gain, ones_bd, seq_len, tm=APROJ_ROWS):
    n, d = x.shape
    return pl.pallas_call(
        functools.partial(_aproj_kernel, tiles_per_seq=seq_len // tm),
        grid=(n // tm,),
        in_specs=[
            pl.BlockSpec((tm, d), lambda i: (i, 0)),
            _const_spec((1, d)),
            pl.BlockSpec((1,) + w_in.shape[1:], lambda i: (layer, 0, 0), pipeline_mode=pl.Buffered(1)),
            _const_spec((1, RWKV_SHIFT_W)),
            _const_spec((1, MEM_W)),
            _const_spec((SLAB_W, SLAB_W)),
        ],
        out_specs=[pl.BlockSpec((tm, RWKV_SHIFT_W), lambda i: (i, 0)),
                   pl.BlockSpec((tm, MEM_W), lambda i: (i, 0))],
        out_shape=[jax.ShapeDtypeStruct((n, RWKV_SHIFT_W), F32),
                   jax.ShapeDtypeStruct((n, MEM_W), BF16)],
        scratch_shapes=[pltpu.VMEM((V7X_SUBLANES, RWKV_SHIFT_W), F32)],
        compiler_params=_cparams(1, 56),
        name="aproj",
    )(x, g, w_in, mu, q_gain, ones_bd)


def _qproj_kernel(x_ref, g_ref, w_ref, qg_ref, ones_ref, slab_ref, *rest, normed_w):
    stage_ref = rest[-1]
    flat_refs = rest[:-1]
    tm = x_ref.shape[0]
    u = _rms(x_ref[...], g_ref[...], NORM_EPS).astype(BF16)
    proj = _dot(u, w_ref[...])
    normed = _head_norm(proj[:, :normed_w], qg_ref[...], ones_ref[...])
    n_slabs = slab_ref.shape[1]
    for j in range(n_slabs):
        src = normed if (j + 1) * V7X_LANES <= normed_w else proj
        stage_ref[j] = src[:, j * V7X_LANES:(j + 1) * V7X_LANES]
    for j in range(n_slabs):
        for c in range(DIL_ROW_PHASES):
            slab_ref[0, j, c] = stage_ref[j, pl.ds(c, tm // DIL_ROW_PHASES, stride=DIL_ROW_PHASES), :]
    if flat_refs:
        flat_refs[0][...] = normed[:, n_slabs * V7X_LANES:].astype(flat_refs[0].dtype)


def _qproj(x, g, w, q_gain, ones_bd, seq_len, slab_w, tm=QPROJ_ROWS):
    n, d = x.shape
    wo = w.shape[1]
    normed_w = q_gain.shape[-1]
    n_slabs = slab_w // V7X_LANES
    tiles_per_seq = seq_len // tm
    phase_rows = seq_len // DIL_ROW_PHASES
    out_specs = [pl.BlockSpec((1, n_slabs, DIL_ROW_PHASES, tm // DIL_ROW_PHASES, V7X_LANES),
                              lambda i: (i // tiles_per_seq, 0, 0, i % tiles_per_seq, 0))]
    out_shape = [jax.ShapeDtypeStruct((n // seq_len, n_slabs, DIL_ROW_PHASES, phase_rows, V7X_LANES), F32)]
    if wo > slab_w:
        assert normed_w == wo
        out_specs.append(pl.BlockSpec((tm, wo - slab_w), lambda i: (i, 0)))
        out_shape.append(jax.ShapeDtypeStruct((n, wo - slab_w), BF16))
    outs = list(pl.pallas_call(
        functools.partial(_qproj_kernel, normed_w=normed_w),
        grid=(n // tm,),
        in_specs=[
            pl.BlockSpec((tm, d), lambda i: (i, 0)),
            _const_spec((1, d)),
            pl.BlockSpec(w.shape, lambda i: (0, 0), pipeline_mode=pl.Buffered(1)),
            _const_spec((1, normed_w)),
            _const_spec((SLAB_W, SLAB_W)),
        ],
        out_specs=out_specs,
        out_shape=out_shape,
        scratch_shapes=[pltpu.VMEM((n_slabs, tm, V7X_LANES), F32)],
        compiler_params=_cparams(1, 48),
        name="qproj",
    )(x, g, w, q_gain, ones_bd))
    outs[0] = outs[0].reshape(n // seq_len, n_slabs, seq_len, V7X_LANES)
    return outs


def _embed(x):
    head0 = lax.broadcasted_iota(jnp.int32, x.shape, 1) < HEAD_DIM
    zero = jnp.zeros_like(x)
    return jnp.concatenate([jnp.where(head0, x, zero), jnp.where(head0, zero, x)], axis=0)


def _tri_inverse(nmats, tpos, spos):
    eye = (tpos == spos).astype(F32)
    same = {s: (tpos >> s) == (spos >> s) for s in (3, 4, 5)}

    def mm(lhs, rhs):
        return [_dot(a, _embed(b)) for a, b in zip(lhs, rhs)]

    def bf(xs):
        return [x.astype(BF16) for x in xs]

    n8 = [jnp.where(same[3], n, 0.0) for n in nmats]
    n8b = bf(n8)
    n2 = mm(n8b, n8b)
    n2b = bf(n2)
    n4 = mm(n2b, n2b)
    t = mm(bf([eye + a for a in n8]), bf([eye + a for a in n2]))
    t = mm(bf(t), bf([eye + a for a in n4]))
    for lo, hi in ((3, 4), (4, 5), (5, None)):
        off = ~same[lo] if hi is None else (same[hi] & ~same[lo])
        tb = bf(t)
        z = mm(bf([jnp.where(off, n, 0.0) for n in nmats]), tb)
        t = [a + d for a, d in zip(t, mm(tb, bf(z)))]
    return t


def _rwkv_kernel(ps_ref, vec_ref, wup_ref, aup_ref, gup_ref, tri_ref, ones_ref, y_ref,
                 s_ref, rt_ref, kt_ref, bt_ref, at_ref, kh_ref, bh_ref, v_ref, gam_ref, yacc_ref,
                 bonus_ref, gate_ref, p_s, qt_s, rp_s, y0_s):
    tblk = ps_ref.shape[1]
    n_pairs = RWKV_W // PAIR_W

    @pl.when(pl.program_id(1) == 0)
    def _():
        s_ref[...] = jnp.zeros_like(s_ref)

    ps = ps_ref[0]
    r = ps[:, :RWKV_W]
    k = ps[:, RWKV_W:2 * RWKV_W]
    v = ps[:, 2 * RWKV_W:3 * RWKV_W]
    lora_in = ps[:, LORA_LO:LORA_LO + PAIR_W]
    g_lo = ps[:, LORA_LO + PAIR_W:]
    w0, a0, kk_scale, k_a = vec_ref[0:1, :], vec_ref[1:2, :], vec_ref[2:3, :], vec_ref[3:4, :]
    r_k, lnx_g, lnx_b = vec_ref[4:5, :], vec_ref[5:6, :], vec_ref[6:7, :]
    ones_bd = ones_ref[...]

    z = w0 + _dot(jnp.tanh(lora_in).astype(BF16), wup_ref[...])
    lw = -(math.exp(-0.5) * LOG2_E) * jax.nn.sigmoid(z)
    a = jax.nn.sigmoid(a0 + _dot(lora_in.astype(BF16), aup_ref[...]))
    gate_ref[...] = _dot(jax.nn.sigmoid(g_lo).astype(BF16), gup_ref[...])
    kk = k * kk_scale
    kk = kk / jnp.maximum(jnp.sqrt(_seg_sum(kk * kk, ones_bd, pieces=1)), 1e-12)
    k2 = k * (1.0 + (a - 1.0) * k_a)
    kka = kk * a
    bonus_ref[...] = _seg_sum(r * k2 * r_k, ones_bd, pieces=1) * v

    hi, lo = _split2(lw)
    tri = tri_ref[...]
    gcum = jnp.concatenate(
        [_dot(tri, hi[s0:s0 + RWKV_CUMSUM_ROWS]) + _dot(tri, lo[s0:s0 + RWKV_CUMSUM_ROWS])
         for s0 in range(0, tblk, RWKV_CUMSUM_ROWS)], axis=0)
    n_chunks = tblk // RWKV_CHUNK
    gam_rows = [jnp.exp2(gcum[(c + 1) * RWKV_CHUNK - 1:(c + 1) * RWKV_CHUNK, :]) for c in range(n_chunks)]
    gam = jnp.concatenate([jnp.broadcast_to(g, (RWKV_CHUNK, RWKV_W)) for g in gam_rows], axis=0)
    for c in range(n_chunks):
        gam_ref[c:c + 1, :] = gam_rows[c]
    e_neg = jnp.exp2(-gcum)
    k_t = k2 * e_neg
    b_t = kka * e_neg
    rt_ref[...] = (r * jnp.exp2(gcum)).astype(BF16)
    kt_ref[...] = k_t.astype(BF16)
    bt_ref[...] = b_t.astype(BF16)
    at_ref[...] = (-kk * jnp.exp2(gcum - lw)).astype(BF16)
    kh_ref[...] = (k_t * gam).astype(BF16)
    bh_ref[...] = (b_t * gam).astype(BF16)
    v_ref[...] = v.astype(BF16)

    tpos = lax.broadcasted_iota(jnp.int32, (RWKV_CHUNK, PAIR_W), 0)
    spos = lax.broadcasted_iota(jnp.int32, (RWKV_CHUNK, PAIR_W), 1) & (HEAD_DIM - 1)
    strict = tpos > spos
    incl = tpos >= spos
    head0 = lax.broadcasted_iota(jnp.int32, (RWKV_CHUNK, PAIR_W), 1) < HEAD_DIM
    same_head = ((lax.broadcasted_iota(jnp.int32, (PAIR_W, PAIR_W), 0) < HEAD_DIM)
                 == (lax.broadcasted_iota(jnp.int32, (PAIR_W, PAIR_W), 1) < HEAD_DIM))

    pairs = range(n_pairs)
    lanes = [slice(p * PAIR_W, (p + 1) * PAIR_W) for p in pairs]

    def phase_a(i, carry):
        slots, at_v, rt_v, bt_v, kt_v, v_v, bh_v, kh_v = [], [], [], [], [], [], [], []
        for cc in range(RWKV_PHASE_A_CHUNKS):
            c = i * RWKV_PHASE_A_CHUNKS + cc
            rows = pl.ds(pl.multiple_of(c * RWKV_CHUNK, RWKV_CHUNK), RWKV_CHUNK)
            loaded = [ref[rows, :] for ref in (at_ref, rt_ref, bt_ref, kt_ref, v_ref, bh_ref, kh_ref)]
            for p in pairs:
                slots.append(c * n_pairs + p)
                for dst, x_c in zip((at_v, rt_v, bt_v, kt_v, v_v, bh_v, kh_v), loaded):
                    dst.append(x_c[:, lanes[p]])
        chains = range(len(slots))
        m4 = [lax.dot_general(jnp.concatenate([at_v[j], rt_v[j]], axis=0),
                              jnp.concatenate([_embed(bt_v[j]), _embed(kt_v[j])], axis=0), _NT,
                              preferred_element_type=F32) for j in chains]
        n_ab = [jnp.where(strict, m[:RWKV_CHUNK, :PAIR_W], 0.0) for m in m4]
        a_ak = [jnp.where(strict, m[:RWKV_CHUNK, PAIR_W:], 0.0).astype(BF16) for m in m4]
        a_rb = [jnp.where(incl, m[RWKV_CHUNK:, :PAIR_W], 0.0).astype(BF16) for m in m4]
        a_rk = [jnp.where(incl, m[RWKV_CHUNK:, PAIR_W:], 0.0).astype(BF16) for m in m4]
        v_m = [_embed(x) for x in v_v]
        akv = [_dot(a_ak[j], v_m[j]).astype(BF16) for j in chains]
        t_inv = [t.astype(BF16) for t in _tri_inverse(n_ab, tpos, spos)]
        wu0 = [_dot(t_inv[j], jnp.concatenate([_embed(at_v[j]), _embed(akv[j])], axis=1))
               for j in chains]
        w_b = [x[:, :PAIR_W].astype(BF16) for x in wu0]
        u0_b = [x[:, PAIR_W:].astype(BF16) for x in wu0]
        p_m = [jnp.where(same_head, lax.dot_general(bh_v[j], w_b[j], _TN, preferred_element_type=F32), 0.0)
               for j in chains]
        q_full = [lax.dot_general(jnp.concatenate([u0_b[j], v_v[j]], axis=0),
                                  jnp.concatenate([bh_v[j], kh_v[j]], axis=0), _TN,
                                  preferred_element_type=F32) for j in chains]
        q_t = [jnp.where(head0, q[:RWKV_CHUNK], q[RWKV_CHUNK:]) for q in q_full]
        r_p = [rt_v[j].astype(F32) + _dot(a_rb[j], _embed(w_b[j])) for j in chains]
        y_0 = [_dot(jnp.concatenate([a_rb[j], a_rk[j]], axis=1),
                    jnp.concatenate([_embed(u0_b[j]), v_m[j]], axis=0)) for j in chains]
        for j in chains:
            p_s[slots[j]] = p_m[j].astype(BF16)
            qt_s[slots[j]] = q_t[j]
            rp_s[slots[j]] = r_p[j].astype(BF16)
            y0_s[slots[j]] = y_0[j]
        return carry

    lax.fori_loop(0, n_chunks // RWKV_PHASE_A_CHUNKS, phase_a, 0)

    states = [s_ref[p] for p in pairs]
    for c in range(n_chunks):
        r0 = c * RWKV_CHUNK
        gam_c = gam_ref[c:c + 1, :]
        slots = [c * n_pairs + p for p in pairs]
        state_b = [s.astype(BF16) for s in states]
        upd = [lax.dot_general(state_b[p], p_s[slots[p]], _NT, preferred_element_type=F32) for p in pairs]
        y_c = [lax.dot_general(rp_s[slots[p]], _embed(state_b[p]), _NT, preferred_element_type=F32)
               + y0_s[slots[p]] for p in pairs]
        states = [states[p] * gam_c[:, lanes[p]] + upd[p] + qt_s[slots[p]] for p in pairs]
        yacc_ref[r0:r0 + RWKV_CHUNK, :] = jnp.concatenate(y_c, axis=1)
    for p in pairs:
        s_ref[p] = states[p]

    y = yacc_ref[...]
    mean = _seg_sum(y, ones_bd) * (1.0 / HEAD_DIM)
    dev = y - mean
    var = _seg_sum(dev * dev, ones_bd, pieces=1) * (1.0 / HEAD_DIM)
    yn = dev * lax.rsqrt(var + LNX_EPS) * lnx_g + lnx_b
    y_ref[0] = ((yn + bonus_ref[...]) * gate_ref[...]).astype(y_ref.dtype)


def _rwkv(ps, vecs, w_up_p, a_up_p, g_up, ones_bd, tblk=RWKV_ROWS):
    b, s, _ = ps.shape
    assert s % tblk == 0 and tblk % (RWKV_PHASE_A_CHUNKS * RWKV_CHUNK) == 0 and tblk % RWKV_CUMSUM_ROWS == 0
    tri = _block_ones(RWKV_CUMSUM_ROWS, RWKV_CHUNK, lower=True)
    act = pltpu.VMEM((tblk, RWKV_W), F32)
    act_b = pltpu.VMEM((tblk, RWKV_W), BF16)
    n_pairs = RWKV_W // PAIR_W
    n_chunks = tblk // RWKV_CHUNK
    n_slots = n_chunks * n_pairs
    gam_rows = -(-n_chunks // V7X_SUBLANES) * V7X_SUBLANES
    scratch = [pltpu.VMEM((n_pairs, HEAD_DIM, PAIR_W), F32)]
    scratch += [act_b] * 7
    scratch += [pltpu.VMEM((gam_rows, RWKV_W), F32)] + [act] * 3
    scratch += [
        pltpu.VMEM((n_slots, PAIR_W, PAIR_W), BF16),
        pltpu.VMEM((n_slots, RWKV_CHUNK, PAIR_W), F32),
        pltpu.VMEM((n_slots, RWKV_CHUNK, PAIR_W), BF16),
        pltpu.VMEM((n_slots, RWKV_CHUNK, PAIR_W), F32),
    ]
    return pl.pallas_call(
        _rwkv_kernel,
        grid=(b, s // tblk),
        in_specs=[
            pl.BlockSpec((1, tblk, RWKV_SHIFT_W), lambda i, t: (i, t, 0)),
            _const_spec(vecs.shape),
            _const_spec(w_up_p.shape),
            _const_spec(a_up_p.shape),
            _const_spec(g_up.shape),
            _const_spec(tri.shape),
            _const_spec((SLAB_W, SLAB_W)),
        ],
        out_specs=pl.BlockSpec((1, tblk, RWKV_W), lambda i, t: (i, t, 0)),
        out_shape=jax.ShapeDtypeStruct((b, s, RWKV_W), BF16),
        scratch_shapes=scratch,
        compiler_params=_cparams(2, 48),
        name="rwkv",
    )(ps, vecs, w_up_p, a_up_p, g_up, tri, ones_bd)


def _t5_bucket(dist):
    max_exact = NUM_BUCKETS // 2
    d_f = jnp.maximum(dist, 1).astype(F32)
    large = max_exact + (jnp.log(d_f / max_exact) / math.log(MAX_DISTANCE / max_exact)
                         * (NUM_BUCKETS - max_exact)).astype(jnp.int32)
    large = jnp.minimum(large, NUM_BUCKETS - 1)
    return jnp.where(dist < max_exact, dist, large)


def _band_buckets():
    run = BLOCK // DIL_ROW_PHASES
    out = []
    for window, dil in DIL_GROUPS:
        rows_q = np.arange(BLOCK)
        rows_k = np.arange(2 * BLOCK)
        if dil < DIL_ROW_PHASES:
            u_q = DIL_ROW_PHASES * (rows_q % run) + rows_q // run
            u_k = ((rows_k % (2 * run)) // run) * BLOCK + DIL_ROW_PHASES * (rows_k % run) + rows_k // (2 * run)
        else:
            u_q, u_k = rows_q, rows_k
        dsub = jnp.asarray(BLOCK + u_q[:, None] - u_k[None, :])
        band = (dsub >= 0) & (dsub <= window // dil)
        idx = jnp.where(band, _t5_bucket(jnp.maximum(dsub, 0) * dil), -1)
        out.append(jnp.stack([idx, jnp.where(jnp.asarray(u_k < BLOCK)[None, :], -1, idx)]))
    return jnp.stack(out, axis=1).astype(jnp.int32)


def _bias_kernel(tab_ref, idx_ref, o_ref):
    head = pl.program_id(1)
    idx = idx_ref[0, 0]
    acc = jnp.full(idx.shape, NEG_INF, F32)
    for bucket in range(NUM_BUCKETS):
        acc = jnp.where(idx == bucket, tab_ref[bucket, head] * LOG2_E, acc)
    o_ref[0, 0] = acc


def _band_bias(rel_bias):
    n_heads = rel_bias.shape[1]
    return pl.pallas_call(
        _bias_kernel,
        grid=(2, n_heads),
        in_specs=[
            pl.BlockSpec(memory_space=pltpu.SMEM),
            pl.BlockSpec((1, 1, BLOCK, 2 * BLOCK), lambda v, h: (v, h // DIL_GROUP_HEADS, 0, 0)),
        ],
        out_specs=pl.BlockSpec((1, 1, BLOCK, 2 * BLOCK), lambda v, h: (v, h, 0, 0)),
        out_shape=jax.ShapeDtypeStruct((2, n_heads, BLOCK, 2 * BLOCK), F32),
        compiler_params=_cparams(2, 16),
        name="band_bias",
    )(rel_bias, _band_buckets())


def _dil_group(dil, q_ref, k_ref, v_ref, bias_ref, acc_ref, m_ref, l_ref):
    seq_len = q_ref.shape[2]
    phase_rows = seq_len // DIL_ROW_PHASES
    run = BLOCK // DIL_ROW_PHASES
    head0 = lax.broadcasted_iota(jnp.int32, (BLOCK, PAIR_W), 1) < HEAD_DIM

    def windows(t):
        if dil < DIL_ROW_PHASES:
            blk = t
            back = jnp.maximum(blk - 1, 0)
            own = [pl.ds(pl.multiple_of(c * phase_rows + run * blk, run), run) for c in range(DIL_ROW_PHASES)]
            prev = [pl.ds(pl.multiple_of(c * phase_rows + run * back, run), run) for c in range(DIL_ROW_PHASES)]
            return own, prev, blk == 0
        step = dil // DIL_ROW_PHASES
        n_blocks = phase_rows // (step * BLOCK)
        sub = t >> (n_blocks.bit_length() - 1)
        blk = t & (n_blocks - 1)
        base = (sub & (DIL_ROW_PHASES - 1)) * phase_rows + (sub >> (DIL_ROW_PHASES.bit_length() - 1))
        back = jnp.maximum(blk - 1, 0)
        if step == 1:
            own = [pl.ds(pl.multiple_of(base + BLOCK * blk, BLOCK), BLOCK)]
            prev = [pl.ds(pl.multiple_of(base + BLOCK * back, BLOCK), BLOCK)]
        else:
            own = [pl.ds(base + step * BLOCK * blk, BLOCK, stride=step)]
            prev = [pl.ds(base + step * BLOCK * back, BLOCK, stride=step)]
        return own, prev, blk == 0

    def gather(get, wins):
        parts = [get(w) for w in wins]
        return parts[0] if len(parts) == 1 else jnp.concatenate(parts, axis=0)

    def scatter(put, wins, value):
        rows = BLOCK // len(wins)
        for i, w in enumerate(wins):
            put(w, value[i * rows:(i + 1) * rows])

    def tiles(i, carry):
        slabs = []
        for j in range(DIL_TILES_PER_ITER):
            own, prev, first = windows(i * DIL_TILES_PER_ITER + j)
            key_wins = [w for pw, ow in zip(prev, own) for w in (pw, ow)]
            variant = first.astype(jnp.int32)
            slabs += [(pair, own, key_wins, variant) for pair in range(DIL_OUT_W // PAIR_W)]
        q = [gather(lambda w: q_ref[0, pair, w, :], own) for pair, own, _, _ in slabs]
        keys = [gather(lambda w: k_ref[0, pair, w, :], kw).astype(BF16) for pair, _, kw, _ in slabs]
        vals = [gather(lambda w: v_ref[0, pair, w, :], kw).astype(BF16) for pair, _, kw, _ in slabs]
        m_old = [gather(lambda w: m_ref[pair, w, :], own) for pair, own, _, _ in slabs]
        l_old = [gather(lambda w: l_ref[pair, w, :], own) for pair, own, _, _ in slabs]
        acc_old = [gather(lambda w: acc_ref[pair, w, :], own) for pair, own, _, _ in slabs]
        heads = [(s, h) for s in range(len(slabs)) for h in range(2)]
        logits = [lax.dot_general(jnp.where(head0 if h == 0 else ~head0, q[s], 0.0).astype(BF16),
                                  keys[s], _NT, preferred_element_type=F32) for s, h in heads]
        logits = [lg + bias_ref[slabs[s][3], 2 * slabs[s][0] + h] for lg, (s, h) in zip(logits, heads)]
        mx = [jnp.max(lg, axis=-1, keepdims=True) for lg in logits]
        p = [jnp.exp2(lg - m) for lg, m in zip(logits, mx)]
        ls = [jnp.sum(ph, axis=-1, keepdims=True) for ph in p]
        pv = [_dot(ph.astype(BF16), vals[s]) for ph, (s, h) in zip(p, heads)]
        for s, (pair, own, _, _) in enumerate(slabs):
            m_t = jnp.where(head0, mx[2 * s], mx[2 * s + 1])
            l_t = jnp.where(head0, ls[2 * s], ls[2 * s + 1])
            acc_t = jnp.where(head0, pv[2 * s], pv[2 * s + 1])
            m_new = jnp.maximum(m_old[s], m_t)
            e_old = jnp.exp2(m_old[s] - m_new)
            e_t = jnp.exp2(m_t - m_new)

            def put(ref, value, pair=pair, own=own):
                def store(w, rows):
                    ref[pair, w, :] = rows
                scatter(store, own, value)

            put(m_ref, m_new)
            put(l_ref, l_old[s] * e_old + l_t * e_t)
            put(acc_ref, acc_old[s] * e_old + acc_t * e_t)
        return carry

    lax.fori_loop(0, seq_len // BLOCK // DIL_TILES_PER_ITER, tiles, 0)


def _dil_kernel(q_ref, k_ref, v_ref, bias_ref, o_ref, acc_ref, m_ref, l_ref):
    g = pl.program_id(1)

    @pl.when(g == 0)
    def _():
        m_ref[...] = jnp.full(m_ref.shape, NEG_INF, F32)
        l_ref[...] = jnp.zeros_like(l_ref)
        acc_ref[...] = jnp.zeros_like(acc_ref)

    for gi, (_, dil) in enumerate(DIL_GROUPS):
        pl.when(g == gi)(functools.partial(_dil_group, dil, q_ref, k_ref, v_ref, bias_ref, acc_ref, m_ref, l_ref))

    @pl.when(g == len(DIL_GROUPS) - 1)
    def _():
        phase_rows = o_ref.shape[2] // DIL_ROW_PHASES
        for pair in range(DIL_OUT_W // PAIR_W):
            for c in range(DIL_ROW_PHASES):
                rows = slice(c * phase_rows, (c + 1) * phase_rows)
                o_ref[0, pair, pl.ds(c, phase_rows, stride=DIL_ROW_PHASES), :] = (
                    acc_ref[pair, rows, :] / l_ref[pair, rows, :])


def _dilated_attention(q_slabs, kv_slabs, bias):
    b, _, s, _ = q_slabs.shape
    n_groups = len(DIL_GROUPS)
    pairs = DIL_OUT_W // PAIR_W
    blk = (1, pairs, s, PAIR_W)
    return pl.pallas_call(
        _dil_kernel,
        grid=(b, n_groups),
        in_specs=[
            pl.BlockSpec(blk, lambda i, g: (i, g, 0, 0)),
            pl.BlockSpec(blk, lambda i, g: (i, g, 0, 0)),
            pl.BlockSpec(blk, lambda i, g: (i, n_groups + g, 0, 0)),
            pl.BlockSpec((2, DIL_GROUP_HEADS, BLOCK, 2 * BLOCK), lambda i, g: (0, g, 0, 0)),
        ],
        out_specs=pl.BlockSpec(blk, lambda i, g: (i, 0, 0, 0)),
        out_shape=jax.ShapeDtypeStruct((b, pairs, s, PAIR_W), F32),
        scratch_shapes=[pltpu.VMEM((pairs, s, PAIR_W), F32)] * 3,
        compiler_params=_cparams(2, 56),
        name="dilated_attention",
    )(q_slabs, kv_slabs, kv_slabs, bias)


def kernel(x, mem, ffn_pre_norm, ffn_pre_w_in, ffn_pre_w_out, mix_norm, ffn_post_norm, ffn_post_w_in, ffn_post_w_out, mem_norm, mem_w_kv, mem_q_norm, mem_k_norm, a_w_in, a_shift_mu, a_w0, a_w_up, a_a0, a_a_up, a_g_up, a_kk_scale, a_k_a, a_r_k, a_lnx_g, a_lnx_b, a_w_out, b_w_q, b_q_norm, b_w_out, kv_norm, kv_w, kv_k_norm, rel_bias):
    b, s, d = x.shape
    depth = ffn_pre_w_in.shape[0]
    n_a = a_w_in.shape[0]
    n = b * s
    scale = LOG2_E / math.sqrt(HEAD_DIM)
    ones_bd = _block_ones(SLAB_W, HEAD_DIM)
    row = lambda p: p.reshape(1, -1)

    k_mem, v_mem = _memkv(mem, mem_norm[:, None, :], mem_w_kv.astype(BF16),
                          jnp.tile(mem_k_norm, (1, MEM_HEADS))[:, None, :], ones_bd)
    mem_q_gain = jnp.tile(mem_q_norm, (1, MEM_HEADS)) * scale

    pre_w_in, pre_w_out = ffn_pre_w_in.astype(BF16), ffn_pre_w_out.astype(BF16)
    post_w_in, post_w_out = ffn_post_w_in.astype(BF16), ffn_post_w_out.astype(BF16)
    a_w_in_b = a_w_in.astype(BF16)
    xf = x.reshape(n, d)
    kv = None
    bias = None
    for layer in range(depth):
        xf = _ffn(xf, row(ffn_pre_norm[layer]), pre_w_in, pre_w_out, layer)
        if layer < n_a:
            i = layer
            ps, qm = _aproj(xf, row(mix_norm[layer]), a_w_in_b, i, row(a_shift_mu[i]),
                            row(mem_q_gain[layer]), ones_bd, s)
            zeros = jnp.zeros((DECAY_LORA, RWKV_W), F32)
            vecs = jnp.stack([a_w0[i], a_a0[i], a_kk_scale[i], a_k_a[i], a_r_k[i].reshape(-1),
                              a_lnx_g[i], a_lnx_b[i], jnp.zeros((RWKV_W,), F32)])
            y_main = _rwkv(ps.reshape(b, s, RWKV_SHIFT_W), vecs,
                           jnp.concatenate([a_w_up[i], zeros]).astype(BF16),
                           jnp.concatenate([zeros, a_a_up[i]]).astype(BF16),
                           a_g_up[i].astype(BF16), ones_bd)
            mix = (y_main.reshape(n, RWKV_W), qm, k_mem[layer], v_mem[layer], a_w_out[i].astype(BF16), s)
        else:
            j = layer - n_a
            q_gain = jnp.concatenate([jnp.tile(b_q_norm[j], DIL_W // HEAD_DIM) * scale, mem_q_gain[layer]])
            q_dil, qm = _qproj(xf, row(mix_norm[layer]), b_w_q[j].astype(BF16), row(q_gain), ones_bd, s, DIL_W)
            if bias is None:
                bias = _band_bias(rel_bias)
            y_dil = _dilated_attention(q_dil, kv, bias)
            mix = (y_dil, qm, k_mem[layer], v_mem[layer], b_w_out[j].astype(BF16), s)
        xf = _ffn(xf, row(ffn_post_norm[layer]), post_w_in, post_w_out, layer, mix=mix)
        if layer == n_a - 1:
            k_gain = jnp.tile(kv_k_norm, DIL_W // HEAD_DIM)
            kv, = _qproj(xf, row(kv_norm), kv_w.astype(BF16), row(k_gain), ones_bd, s, 2 * DIL_W)
    return xf.reshape(b, s, d)
```

```python
import functools
import math

import jax
import jax.numpy as jnp
import numpy as np
from jax import lax
from jax.experimental import pallas as pl
from jax.experimental.pallas import tpu as pltpu

F32 = jnp.float32
BF16 = jnp.bfloat16

HEAD_DIM = 64
MEM_HEADS = 4
MEM_W = MEM_HEADS * HEAD_DIM
RWKV_HEADS = 12
RWKV_W = RWKV_HEADS * HEAD_DIM
DECAY_LORA = 64
AAA_LORA = 64
GATE_LORA = 128
RWKV_SHIFT_W = 3 * RWKV_W + DECAY_LORA + AAA_LORA + GATE_LORA
LORA_LO = 3 * RWKV_W
DIL_GROUPS = ((128, 1), (512, 4), (2048, 16))
DIL_GROUP_HEADS = 4
DIL_W = len(DIL_GROUPS) * DIL_GROUP_HEADS * HEAD_DIM
DIL_OUT_W = DIL_GROUP_HEADS * HEAD_DIM
BLOCK = 128
NUM_BUCKETS = 32
MAX_DISTANCE = 2048
NORM_EPS = 1e-6
LNX_EPS = 64e-5
NEG_INF = -1e30
LOG2_E = math.log2(math.e)

V7X_LANES = 128
V7X_SUBLANES = 8
V7X_VMEM_BYTES = 64 * 1024 * 1024
V7X_VMEM_LIMIT_CAP_BYTES = V7X_VMEM_BYTES * 15 // 16

RWKV_CHUNK = 64
RWKV_PHASE_A_CHUNKS = 8
FFN_ROWS = 1024
FFN_CHUNK = 256
APROJ_ROWS = 1024
PROJ_SUBS_IN_FLIGHT = 4
QPROJ_ROWS = 1024
RWKV_ROWS = 512
RWKV_CUMSUM_ROWS = 256
DIL_TILES_PER_ITER = 4
DIL_ROW_PHASES = 4
PROJ_SUB_ROWS = 128
PAIR_W = 2 * HEAD_DIM
SLAB_W = 4 * HEAD_DIM

_NT = (((1,), (1,)), ((), ()))
_TN = (((0,), (0,)), ((), ()))


def _cparams(n_axes, vmem_mib):
    assert vmem_mib * 1024 * 1024 <= V7X_VMEM_LIMIT_CAP_BYTES
    return pltpu.CompilerParams(
        dimension_semantics=("arbitrary",) * n_axes,
        vmem_limit_bytes=vmem_mib * 1024 * 1024,
    )


def _const_spec(shape):
    zeros = (0,) * len(shape)
    return pl.BlockSpec(shape, lambda *_: zeros)


def _dot(a, b):
    return jnp.dot(a, b, preferred_element_type=F32)


def _rms(x, g, eps):
    return x * lax.rsqrt(jnp.mean(x * x, axis=-1, keepdims=True) + eps) * g


def _split2(x):
    hi = x.astype(BF16)
    lo = (x - hi.astype(F32)).astype(BF16)
    return hi, lo


def _seg_sum(x, ones_bd, pieces=2):
    outs = []
    for s in range(x.shape[-1] // SLAB_W):
        slab = x[:, s * SLAB_W:(s + 1) * SLAB_W]
        if pieces == 1:
            outs.append(_dot(slab.astype(BF16), ones_bd))
        else:
            hi, lo = _split2(slab)
            outs.append(_dot(hi, ones_bd) + _dot(lo, ones_bd))
    return outs[0] if len(outs) == 1 else jnp.concatenate(outs, axis=-1)


def _block_ones(n, blk, lower=False):
    i = np.arange(n)
    m = (i[:, None] // blk) == (i[None, :] // blk)
    if lower:
        m = m & (i[:, None] >= i[None, :])
    return jnp.asarray(m, dtype=BF16)


def _mix_delta(ya_ref, qm_ref, k_ref, v_ref, w_ref):
    if len(ya_ref.shape) == 4:
        ya = jnp.concatenate([ya_ref[0, j] for j in range(ya_ref.shape[1])], axis=1)
    else:
        ya = ya_ref[...]
    wa = ya.shape[-1]
    qm = qm_ref[...]
    kmem = k_ref[0]
    vmem = v_ref[0]
    head = lax.broadcasted_iota(jnp.int32, qm.shape, 1) // HEAD_DIM
    heads = range(MEM_HEADS)
    logits = [lax.dot_general(jnp.where(head == h, qm, jnp.zeros_like(qm)), kmem, _NT,
                              preferred_element_type=F32) for h in heads]
    p = [jnp.exp2(lg - jnp.max(lg, axis=-1, keepdims=True)) for lg in logits]
    inv_l = [1.0 / jnp.sum(ph, axis=-1, keepdims=True) for ph in p]
    pv = [_dot(ph.astype(BF16), vmem) for ph in p]
    y_mem = pv[0] * inv_l[0]
    for h in heads[1:]:
        y_mem = jnp.where(head == h, pv[h] * inv_l[h], y_mem)
    return _dot(ya.astype(BF16), w_ref[:wa, :]) + _dot(y_mem.astype(BF16), w_ref[wa:, :])


def _ffn_kernel(*refs, has_mix):
    if has_mix:
        x_ref, ya_ref, qm_ref, k_ref, v_ref, wmix_ref, g_ref, win_ref, wout_ref, o_ref = refs
        x = x_ref[...] + _mix_delta(ya_ref, qm_ref, k_ref, v_ref, wmix_ref)
    else:
        x_ref, g_ref, win_ref, wout_ref, o_ref = refs
        x = x_ref[...]
    d_ff = wout_ref.shape[1]
    xn = _rms(x, g_ref[...], NORM_EPS).astype(BF16)
    n_chunks = d_ff // FFN_CHUNK

    def first_stage(c):
        lo = c * FFN_CHUNK
        return (_dot(xn, win_ref[0, :, lo:lo + FFN_CHUNK]),
                _dot(xn, win_ref[0, :, d_ff + lo:d_ff + lo + FFN_CHUNK]))

    y = None
    pending = first_stage(0)
    for c in range(n_chunks):
        gate, up = pending
        if c + 1 < n_chunks:
            pending = first_stage(c + 1)
        act = (gate * jax.nn.sigmoid(gate) * up).astype(BF16)
        part = _dot(act, wout_ref[0, c * FFN_CHUNK:(c + 1) * FFN_CHUNK, :])
        y = part if y is None else y + part
    o_ref[...] = x + 0.5 * y


def _ffn(x, g, w_in, w_out, layer, mix=None, tm=FFN_ROWS):
    n, d = x.shape
    assert w_out.shape[1] % FFN_CHUNK == 0 and n % tm == 0
    operands = [x]
    in_specs = [pl.BlockSpec((tm, d), lambda i: (i, 0))]
    if mix is not None:
        ya, qm, k_mem, v_mem, w_mix, seq_len = mix
        tiles_per_seq = seq_len // tm
        if ya.ndim == 4:
            ya_spec = pl.BlockSpec((1, ya.shape[1], tm, V7X_LANES),
                                   lambda i: (i // tiles_per_seq, 0, i % tiles_per_seq, 0))
        else:
            ya_spec = pl.BlockSpec((tm, ya.shape[-1]), lambda i: (i, 0))
        mem_spec = pl.BlockSpec((1,) + k_mem.shape[1:], lambda i: (i // tiles_per_seq, 0, 0))
        operands += [ya, qm, k_mem, v_mem, w_mix]
        in_specs += [ya_spec, pl.BlockSpec((tm, MEM_W), lambda i: (i, 0)), mem_spec, mem_spec,
                     _const_spec(w_mix.shape)]
    operands += [g, w_in, w_out]
    in_specs += [
        _const_spec((1, d)),
        pl.BlockSpec((1,) + w_in.shape[1:], lambda i: (layer, 0, 0), pipeline_mode=pl.Buffered(1)),
        pl.BlockSpec((1,) + w_out.shape[1:], lambda i: (layer, 0, 0), pipeline_mode=pl.Buffered(1)),
    ]
    return pl.pallas_call(
        functools.partial(_ffn_kernel, has_mix=mix is not None),
        grid=(n // tm,),
        in_specs=in_specs,
        out_specs=pl.BlockSpec((tm, d), lambda i: (i, 0)),
        out_shape=jax.ShapeDtypeStruct((n, d), F32),
        compiler_params=_cparams(1, 60),
        name="ffn_mix" if mix is not None else "ffn",
    )(*operands)


def _memkv_kernel(mem_ref, g_ref, w_ref, kg_ref, ones_ref, k_ref, v_ref):
    m = _rms(mem_ref[0], g_ref[0], NORM_EPS).astype(BF16)
    kv = _dot(m, w_ref[0])
    k = kv[:, :MEM_W]
    ms = _seg_sum(k * k, ones_ref[...]) * (1.0 / HEAD_DIM)
    k_ref[0, 0] = (k * lax.rsqrt(ms + NORM_EPS) * kg_ref[0]).astype(BF16)
    v_ref[0, 0] = kv[:, MEM_W:].astype(BF16)


def _memkv(mem, mem_norm, w_kv, k_gain, ones_bd):
    b, m, d = mem.shape
    depth = w_kv.shape[0]
    out = jax.ShapeDtypeStruct((depth, b, m, MEM_W), BF16)
    return pl.pallas_call(
        _memkv_kernel,
        grid=(depth, b),
        in_specs=[
            pl.BlockSpec((1, m, d), lambda l, i: (i, 0, 0)),
            pl.BlockSpec((1, 1, d), lambda l, i: (l, 0, 0)),
            pl.BlockSpec((1, d, 2 * MEM_W), lambda l, i: (l, 0, 0)),
            pl.BlockSpec((1, 1, MEM_W), lambda l, i: (l, 0, 0)),
            _const_spec((SLAB_W, SLAB_W)),
        ],
        out_specs=[pl.BlockSpec((1, 1, m, MEM_W), lambda l, i: (l, i, 0, 0))] * 2,
        out_shape=[out, out],
        compiler_params=_cparams(2, 32),
        name="memkv",
    )(mem, mem_norm, w_kv, k_gain, ones_bd)


def _head_norm(q, gain, ones_bd):
    ms = _seg_sum(q * q, ones_bd, pieces=1) * (1.0 / HEAD_DIM)
    return q * lax.rsqrt(ms + NORM_EPS) * gain


def _aproj_kernel(x_ref, g_ref, w_ref, mu_ref, qg_ref, ones_ref, ps_ref, qm_ref, carry_ref, *, tiles_per_seq):
    i = pl.program_id(0)
    tm = x_ref.shape[0]
    u = _rms(x_ref[...], g_ref[...], NORM_EPS).astype(BF16)

    @pl.when(i % tiles_per_seq == 0)
    def _():
        carry_ref[...] = jnp.zeros_like(carry_ref)

    row = lax.broadcasted_iota(jnp.int32, (PROJ_SUB_ROWS, RWKV_SHIFT_W), 0)
    last = carry_ref[V7X_SUBLANES - 1:V7X_SUBLANES, :]
    group_rows = PROJ_SUB_ROWS * PROJ_SUBS_IN_FLIGHT
    for g0 in range(0, tm, group_rows):
        starts = range(g0, g0 + group_rows, PROJ_SUB_ROWS)
        projs = [_dot(u[r0:r0 + PROJ_SUB_ROWS], w_ref[0]) for r0 in starts]
        for r0, proj in zip(starts, projs):
            p = proj[:, :RWKV_SHIFT_W]
            prev = jnp.where(row == 0, last, pltpu.roll(p, 1, 0))
            ps_ref[r0:r0 + PROJ_SUB_ROWS, :] = p + mu_ref[...] * (prev - p)
            last = p[PROJ_SUB_ROWS - 1:, :]
            qm_ref[r0:r0 + PROJ_SUB_ROWS, :] = _head_norm(proj[:, RWKV_SHIFT_W:], qg_ref[...],
                                                          ones_ref[...]).astype(qm_ref.dtype)
    carry_ref[...] = projs[-1][PROJ_SUB_ROWS - V7X_SUBLANES:, :RWKV_SHIFT_W]


def _aproj(x, g, w_in, layer, mu, q_gain, ones_bd, seq_len, tm=APROJ_ROWS):
    n, d = x.shape
    return pl.pallas_call(
        functools.partial(_aproj_kernel, tiles_per_seq=seq_len // tm),
        grid=(n // tm,),
        in_specs=[
            pl.BlockSpec((tm, d), lambda i: (i, 0)),
            _const_spec((1, d)),
            pl.BlockSpec((1,) + w_in.shape[1:], lambda i: (layer, 0, 0), pipeline_mode=pl.Buffered(1)),
            _const_spec((1, RWKV_SHIFT_W)),
            _const_spec((1, MEM_W)),
            _const_spec((SLAB_W, SLAB_W)),
        ],
        out_specs=[pl.BlockSpec((tm, RWKV_SHIFT_W), lambda i: (i, 0)),
                   pl.BlockSpec((tm, MEM_W), lambda i: (i, 0))],
        out_shape=[jax.ShapeDtypeStruct((n, RWKV_SHIFT_W), F32),
                   jax.ShapeDtypeStruct((n, MEM_W), BF16)],
        scratch_shapes=[pltpu.VMEM((V7X_SUBLANES, RWKV_SHIFT_W), F32)],
        compiler_params=_cparams(1, 56),
        name="aproj",
    )(x, g, w_in, mu, q_gain, ones_bd)


def _qproj_kernel(x_ref, g_ref, w_ref, qg_ref, ones_ref, slab_ref, *rest, normed_w):
    stage_ref = rest[-1]
    flat_refs = rest[:-1]
    tm = x_ref.shape[0]
    u = _rms(x_ref[...], g_ref[...], NORM_EPS).astype(BF16)
    proj = _dot(u, w_ref[...])
    normed = _head_norm(proj[:, :normed_w], qg_ref[...], ones_ref[...])
    n_slabs = slab_ref.shape[1]
    for j in range(n_slabs):
        src = normed if (j + 1) * V7X_LANES <= normed_w else proj
        stage_ref[j] = src[:, j * V7X_LANES:(j + 1) * V7X_LANES]
    for j in range(n_slabs):
        for c in range(DIL_ROW_PHASES):
            slab_ref[0, j, c] = stage_ref[j, pl.ds(c, tm // DIL_ROW_PHASES, stride=DIL_ROW_PHASES), :]
    if flat_refs:
        flat_refs[0][...] = normed[:, n_slabs * V7X_LANES:].astype(flat_refs[0].dtype)


def _qproj(x, g, w, q_gain, ones_bd, seq_len, slab_w, tm=QPROJ_ROWS):
    n, d = x.shape
    wo = w.shape[1]
    normed_w = q_gain.shape[-1]
    n_slabs = slab_w // V7X_LANES
    tiles_per_seq = seq_len // tm
    phase_rows = seq_len // DIL_ROW_PHASES
    out_specs = [pl.BlockSpec((1, n_slabs, DIL_ROW_PHASES, tm // DIL_ROW_PHASES, V7X_LANES),
                              lambda i: (i // tiles_per_seq, 0, 0, i % tiles_per_seq, 0))]
    out_shape = [jax.ShapeDtypeStruct((n // seq_len, n_slabs, DIL_ROW_PHASES, phase_rows, V7X_LANES), F32)]
    if wo > slab_w:
        assert normed_w == wo
        out_specs.append(pl.BlockSpec((tm, wo - slab_w), lambda i: (i, 0)))
        out_shape.append(jax.ShapeDtypeStruct((n, wo - slab_w), BF16))
    outs = list(pl.pallas_call(
        functools.partial(_qproj_kernel, normed_w=normed_w),
        grid=(n // tm,),
        in_specs=[
            pl.BlockSpec((tm, d), lambda i: (i, 0)),
            _const_spec((1, d)),
            pl.BlockSpec(w.shape, lambda i: (0, 0), pipeline_mode=pl.Buffered(1)),
            _const_spec((1, normed_w)),
            _const_spec((SLAB_W, SLAB_W)),
        ],
        out_specs=out_specs,
        out_shape=out_shape,
        scratch_shapes=[pltpu.VMEM((n_slabs, tm, V7X_LANES), F32)],
        compiler_params=_cparams(1, 48),
        name="qproj",
    )(x, g, w, q_gain, ones_bd))
    outs[0] = outs[0].reshape(n // seq_len, n_slabs, seq_len, V7X_LANES)
    return outs


def _embed(x):
    head0 = lax.broadcasted_iota(jnp.int32, x.shape, 1) < HEAD_DIM
    zero = jnp.zeros_like(x)
    return jnp.concatenate([jnp.where(head0, x, zero), jnp.where(head0, zero, x)], axis=0)


def _tri_inverse(nmats, tpos, spos):
    eye = (tpos == spos).astype(F32)
    same = {s: (tpos >> s) == (spos >> s) for s in (3, 4, 5)}

    def mm(lhs, rhs):
        return [_dot(a, _embed(b)) for a, b in zip(lhs, rhs)]

    def bf(xs):
        return [x.astype(BF16) for x in xs]

    n8 = [jnp.where(same[3], n, 0.0) for n in nmats]
    n8b = bf(n8)
    n2 = mm(n8b, n8b)
    n2b = bf(n2)
    n4 = mm(n2b, n2b)
    t = mm(bf([eye + a for a in n8]), bf([eye + a for a in n2]))
    t = mm(bf(t), bf([eye + a for a in n4]))
    for lo, hi in ((3, 4), (4, 5), (5, None)):
        off = ~same[lo] if hi is None else (same[hi] & ~same[lo])
        tb = bf(t)
        z = mm(bf([jnp.where(off, n, 0.0) for n in nmats]), tb)
        t = [a + d for a, d in zip(t, mm(tb, bf(z)))]
    return t


def _rwkv_kernel(ps_ref, vec_ref, wup_ref, aup_ref, gup_ref, tri_ref, ones_ref, y_ref,
                 s_ref, rt_ref, kt_ref, bt_ref, at_ref, kh_ref, bh_ref, v_ref, gam_ref, yacc_ref,
                 bonus_ref, gate_ref, p_s, qt_s, rp_s, y0_s):
    tblk = ps_ref.shape[1]
    n_pairs = RWKV_W // PAIR_W

    @pl.when(pl.program_id(1) == 0)
    def _():
        s_ref[...] = jnp.zeros_like(s_ref)

    ps = ps_ref[0]
    r = ps[:, :RWKV_W]
    k = ps[:, RWKV_W:2 * RWKV_W]
    v = ps[:, 2 * RWKV_W:3 * RWKV_W]
    lora_in = ps[:, LORA_LO:LORA_LO + PAIR_W]
    g_lo = ps[:, LORA_LO + PAIR_W:]
    w0, a0, kk_scale, k_a = vec_ref[0:1, :], vec_ref[1:2, :], vec_ref[2:3, :], vec_ref[3:4, :]
    r_k, lnx_g, lnx_b = vec_ref[4:5, :], vec_ref[5:6, :], vec_ref[6:7, :]
    ones_bd = ones_ref[...]

    z = w0 + _dot(jnp.tanh(lora_in).astype(BF16), wup_ref[...])
    lw = -(math.exp(-0.5) * LOG2_E) * jax.nn.sigmoid(z)
    a = jax.nn.sigmoid(a0 + _dot(lora_in.astype(BF16), aup_ref[...]))
    gate_ref[...] = _dot(jax.nn.sigmoid(g_lo).astype(BF16), gup_ref[...])
    kk = k * kk_scale
    kk = kk / jnp.maximum(jnp.sqrt(_seg_sum(kk * kk, ones_bd, pieces=1)), 1e-12)
    k2 = k * (1.0 + (a - 1.0) * k_a)
    kka = kk * a
    bonus_ref[...] = _seg_sum(r * k2 * r_k, ones_bd, pieces=1) * v

    hi, lo = _split2(lw)
    tri = tri_ref[...]
    gcum = jnp.concatenate(
        [_dot(tri, hi[s0:s0 + RWKV_CUMSUM_ROWS]) + _dot(tri, lo[s0:s0 + RWKV_CUMSUM_ROWS])
         for s0 in range(0, tblk, RWKV_CUMSUM_ROWS)], axis=0)
    n_chunks = tblk // RWKV_CHUNK
    gam_rows = [jnp.exp2(gcum[(c + 1) * RWKV_CHUNK - 1:(c + 1) * RWKV_CHUNK, :]) for c in range(n_chunks)]
    gam = jnp.concatenate([jnp.broadcast_to(g, (RWKV_CHUNK, RWKV_W)) for g in gam_rows], axis=0)
    for c in range(n_chunks):
        gam_ref[c:c + 1, :] = gam_rows[c]
    e_neg = jnp.exp2(-gcum)
    k_t = k2 * e_neg
    b_t = kka * e_neg
    rt_ref[...] = (r * jnp.exp2(gcum)).astype(BF16)
    kt_ref[...] = k_t.astype(BF16)
    bt_ref[...] = b_t.astype(BF16)
    at_ref[...] = (-kk * jnp.exp2(gcum - lw)).astype(BF16)
    kh_ref[...] = (k_t * gam).astype(BF16)
    bh_ref[...] = (b_t * gam).astype(BF16)
    v_ref[...] = v.astype(BF16)

    tpos = lax.broadcasted_iota(jnp.int32, (RWKV_CHUNK, PAIR_W), 0)
    spos = lax.broadcasted_iota(jnp.int32, (RWKV_CHUNK, PAIR_W), 1) & (HEAD_DIM - 1)
    strict = tpos > spos
    incl = tpos >= spos
    head0 = lax.broadcasted_iota(jnp.int32, (RWKV_CHUNK, PAIR_W), 1) < HEAD_DIM
    same_head = ((lax.broadcasted_iota(jnp.int32, (PAIR_W, PAIR_W), 0) < HEAD_DIM)
                 == (lax.broadcasted_iota(jnp.int32, (PAIR_W, PAIR_W), 1) < HEAD_DIM))

    pairs = range(n_pairs)
    lanes = [slice(p * PAIR_W, (p + 1) * PAIR_W) for p in pairs]

    def phase_a(i, carry):
        slots, at_v, rt_v, bt_v, kt_v, v_v, bh_v, kh_v = [], [], [], [], [], [], [], []
        for cc in range(RWKV_PHASE_A_CHUNKS):
            c = i * RWKV_PHASE_A_CHUNKS + cc
            rows = pl.ds(pl.multiple_of(c * RWKV_CHUNK, RWKV_CHUNK), RWKV_CHUNK)
            loaded = [ref[rows, :] for ref in (at_ref, rt_ref, bt_ref, kt_ref, v_ref, bh_ref, kh_ref)]
            for p in pairs:
                slots.append(c * n_pairs + p)
                for dst, x_c in zip((at_v, rt_v, bt_v, kt_v, v_v, bh_v, kh_v), loaded):
                    dst.append(x_c[:, lanes[p]])
        chains = range(len(slots))
        m4 = [lax.dot_general(jnp.concatenate([at_v[j], rt_v[j]], axis=0),
                              jnp.concatenate([_embed(bt_v[j]), _embed(kt_v[j])], axis=0), _NT,
                              preferred_element_type=F32) for j in chains]
        n_ab = [jnp.where(strict, m[:RWKV_CHUNK, :PAIR_W], 0.0) for m in m4]
        a_ak = [jnp.where(strict, m[:RWKV_CHUNK, PAIR_W:], 0.0).astype(BF16) for m in m4]
        a_rb = [jnp.where(incl, m[RWKV_CHUNK:, :PAIR_W], 0.0).astype(BF16) for m in m4]
        a_rk = [jnp.where(incl, m[RWKV_CHUNK:, PAIR_W:], 0.0).astype(BF16) for m in m4]
        v_m = [_embed(x) for x in v_v]
        akv = [_dot(a_ak[j], v_m[j]).astype(BF16) for j in chains]
        t_inv = [t.astype(BF16) for t in _tri_inverse(n_ab, tpos, spos)]
        wu0 = [_dot(t_inv[j], jnp.concatenate([_embed(at_v[j]), _embed(akv[j])], axis=1))
               for j in chains]
        w_b = [x[:, :PAIR_W].astype(BF16) for x in wu0]
        u0_b = [x[:, PAIR_W:].astype(BF16) for x in wu0]
        p_m = [jnp.where(same_head, lax.dot_general(bh_v[j], w_b[j], _TN, preferred_element_type=F32), 0.0)
               for j in chains]
        q_full = [lax.dot_general(jnp.concatenate([u0_b[j], v_v[j]], axis=0),
                                  jnp.concatenate([bh_v[j], kh_v[j]], axis=0), _TN,
                                  preferred_element_type=F32) for j in chains]
        q_t = [jnp.where(head0, q[:RWKV_CHUNK], q[RWKV_CHUNK:]) for q in q_full]
        r_p = [rt_v[j].astype(F32) + _dot(a_rb[j], _embed(w_b[j])) for j in chains]
        y_0 = [_dot(jnp.concatenate([a_rb[j], a_rk[j]], axis=1),
                    jnp.concatenate([_embed(u0_b[j]), v_m[j]], axis=0)) for j in chains]
        for j in chains:
            p_s[slots[j]] = p_m[j].astype(BF16)
            qt_s[slots[j]] = q_t[j]
            rp_s[slots[j]] = r_p[j].astype(BF16)
            y0_s[slots[j]] = y_0[j]
        return carry

    lax.fori_loop(0, n_chunks // RWKV_PHASE_A_CHUNKS, phase_a, 0)

    states = [s_ref[p] for p in pairs]
    for c in range(n_chunks):
        r0 = c * RWKV_CHUNK
        gam_c = gam_ref[c:c + 1, :]
        slots = [c * n_pairs + p for p in pairs]
        state_b = [s.astype(BF16) for s in states]
        upd = [lax.dot_general(state_b[p], p_s[slots[p]], _NT, preferred_element_type=F32) for p in pairs]
        y_c = [lax.dot_general(rp_s[slots[p]], _embed(state_b[p]), _NT, preferred_element_type=F32)
               + y0_s[slots[p]] for p in pairs]
        states = [states[p] * gam_c[:, lanes[p]] + upd[p] + qt_s[slots[p]] for p in pairs]
        yacc_ref[r0:r0 + RWKV_CHUNK, :] = jnp.concatenate(y_c, axis=1)
    for p in pairs:
        s_ref[p] = states[p]

    y = yacc_ref[...]
    mean = _seg_sum(y, ones_bd) * (1.0 / HEAD_DIM)
    dev = y - mean
    var = _seg_sum(dev * dev, ones_bd, pieces=1) * (1.0 / HEAD_DIM)
    yn = dev * lax.rsqrt(var + LNX_EPS) * lnx_g + lnx_b
    y_ref[0] = ((yn + bonus_ref[...]) * gate_ref[...]).astype(y_ref.dtype)


def _rwkv(ps, vecs, w_up_p, a_up_p, g_up, ones_bd, tblk=RWKV_ROWS):
    b, s, _ = ps.shape
    assert s % tblk == 0 and tblk % (RWKV_PHASE_A_CHUNKS * RWKV_CHUNK) == 0 and tblk % RWKV_CUMSUM_ROWS == 0
    tri = _block_ones(RWKV_CUMSUM_ROWS, RWKV_CHUNK, lower=True)
    act = pltpu.VMEM((tblk, RWKV_W), F32)
    act_b = pltpu.VMEM((tblk, RWKV_W), BF16)
    n_pairs = RWKV_W // PAIR_W
    n_chunks = tblk // RWKV_CHUNK
    n_slots = n_chunks * n_pairs
    gam_rows = -(-n_chunks // V7X_SUBLANES) * V7X_SUBLANES
    scratch = [pltpu.VMEM((n_pairs, HEAD_DIM, PAIR_W), F32)]
    scratch += [act_b] * 7
    scratch += [pltpu.VMEM((gam_rows, RWKV_W), F32)] + [act] * 3
    scratch += [
        pltpu.VMEM((n_slots, PAIR_W, PAIR_W), BF16),
        pltpu.VMEM((n_slots, RWKV_CHUNK, PAIR_W), F32),
        pltpu.VMEM((n_slots, RWKV_CHUNK, PAIR_W), BF16),
        pltpu.VMEM((n_slots, RWKV_CHUNK, PAIR_W), F32),
    ]
    return pl.pallas_call(
        _rwkv_kernel,
        grid=(b, s // tblk),
        in_specs=[
            pl.BlockSpec((1, tblk, RWKV_SHIFT_W), lambda i, t: (i, t, 0)),
            _const_spec(vecs.shape),
            _const_spec(w_up_p.shape),
            _const_spec(a_up_p.shape),
            _const_spec(g_up.shape),
            _const_spec(tri.shape),
            _const_spec((SLAB_W, SLAB_W)),
        ],
        out_specs=pl.BlockSpec((1, tblk, RWKV_W), lambda i, t: (i, t, 0)),
        out_shape=jax.ShapeDtypeStruct((b, s, RWKV_W), BF16),
        scratch_shapes=scratch,
        compiler_params=_cparams(2, 48),
        name="rwkv",
    )(ps, vecs, w_up_p, a_up_p, g_up, tri, ones_bd)


def _t5_bucket(dist):
    max_exact = NUM_BUCKETS // 2
    d_f = jnp.maximum(dist, 1).astype(F32)
    large = max_exact + (jnp.log(d_f / max_exact) / math.log(MAX_DISTANCE / max_exact)
                         * (NUM_BUCKETS - max_exact)).astype(jnp.int32)
    large = jnp.minimum(large, NUM_BUCKETS - 1)
    return jnp.where(dist < max_exact, dist, large)


def _band_buckets():
    run = BLOCK // DIL_ROW_PHASES
    out = []
    for window, dil in DIL_GROUPS:
        rows_q = np.arange(BLOCK)
        rows_k = np.arange(2 * BLOCK)
        if dil < DIL_ROW_PHASES:
            u_q = DIL_ROW_PHASES * (rows_q % run) + rows_q // run
            u_k = ((rows_k % (2 * run)) // run) * BLOCK + DIL_ROW_PHASES * (rows_k % run) + rows_k // (2 * run)
        else:
            u_q, u_k = rows_q, rows_k
        dsub = jnp.asarray(BLOCK + u_q[:, None] - u_k[None, :])
        band = (dsub >= 0) & (dsub <= window // dil)
        idx = jnp.where(band, _t5_bucket(jnp.maximum(dsub, 0) * dil), -1)
        out.append(jnp.stack([idx, jnp.where(jnp.asarray(u_k < BLOCK)[None, :], -1, idx)]))
    return jnp.stack(out, axis=1).astype(jnp.int32)


def _bias_kernel(tab_ref, idx_ref, o_ref):
    head = pl.program_id(1)
    idx = idx_ref[0, 0]
    acc = jnp.full(idx.shape, NEG_INF, F32)
    for bucket in range(NUM_BUCKETS):
        acc = jnp.where(idx == bucket, tab_ref[bucket, head] * LOG2_E, acc)
    o_ref[0, 0] = acc


def _band_bias(rel_bias):
    n_heads = rel_bias.shape[1]
    return pl.pallas_call(
        _bias_kernel,
        grid=(2, n_heads),
        in_specs=[
            pl.BlockSpec(memory_space=pltpu.SMEM),
            pl.BlockSpec((1, 1, BLOCK, 2 * BLOCK), lambda v, h: (v, h // DIL_GROUP_HEADS, 0, 0)),
        ],
        out_specs=pl.BlockSpec((1, 1, BLOCK, 2 * BLOCK), lambda v, h: (v, h, 0, 0)),
        out_shape=jax.ShapeDtypeStruct((2, n_heads, BLOCK, 2 * BLOCK), F32),
        compiler_params=_cparams(2, 16),
        name="band_bias",
    )(rel_bias, _band_buckets())


def _dil_group(dil, first_group, q_ref, k_ref, v_ref, bias_ref, acc_ref, m_ref, l_ref):
    seq_len = q_ref.shape[2]
    phase_rows = seq_len // DIL_ROW_PHASES
    run = BLOCK // DIL_ROW_PHASES
    head0 = lax.broadcasted_iota(jnp.int32, (BLOCK, PAIR_W), 1) < HEAD_DIM

    def windows(t):
        if dil < DIL_ROW_PHASES:
            blk = t
            back = jnp.maximum(blk - 1, 0)
            own = [pl.ds(pl.multiple_of(c * phase_rows + run * blk, run), run) for c in range(DIL_ROW_PHASES)]
            prev = [pl.ds(pl.multiple_of(c * phase_rows + run * back, run), run) for c in range(DIL_ROW_PHASES)]
            return own, prev, blk == 0
        step = dil // DIL_ROW_PHASES
        n_blocks = phase_rows // (step * BLOCK)
        sub = t >> (n_blocks.bit_length() - 1)
        blk = t & (n_blocks - 1)
        base = (sub & (DIL_ROW_PHASES - 1)) * phase_rows + (sub >> (DIL_ROW_PHASES.bit_length() - 1))
        back = jnp.maximum(blk - 1, 0)
        if step == 1:
            own = [pl.ds(pl.multiple_of(base + BLOCK * blk, BLOCK), BLOCK)]
            prev = [pl.ds(pl.multiple_of(base + BLOCK * back, BLOCK), BLOCK)]
        else:
            own = [pl.ds(base + step * BLOCK * blk, BLOCK, stride=step)]
            prev = [pl.ds(base + step * BLOCK * back, BLOCK, stride=step)]
        return own, prev, blk == 0

    def gather(get, wins):
        parts = [get(w) for w in wins]
        return parts[0] if len(parts) == 1 else jnp.concatenate(parts, axis=0)

    def scatter(put, wins, value):
        rows = BLOCK // len(wins)
        for i, w in enumerate(wins):
            put(w, value[i * rows:(i + 1) * rows])

    def tiles(i, carry):
        slabs = []
        for j in range(DIL_TILES_PER_ITER):
            own, prev, first = windows(i * DIL_TILES_PER_ITER + j)
            key_wins = [w for pw, ow in zip(prev, own) for w in (pw, ow)]
            variant = first.astype(jnp.int32)
            slabs += [(pair, own, key_wins, variant) for pair in range(DIL_OUT_W // PAIR_W)]
        q = [gather(lambda w: q_ref[0, pair, w, :], own) for pair, own, _, _ in slabs]
        keys = [gather(lambda w: k_ref[0, pair, w, :], kw).astype(BF16) for pair, _, kw, _ in slabs]
        vals = [gather(lambda w: v_ref[0, pair, w, :], kw).astype(BF16) for pair, _, kw, _ in slabs]
        if not first_group:
            m_old = [gather(lambda w: m_ref[pair, w, :], own) for pair, own, _, _ in slabs]
            l_old = [gather(lambda w: l_ref[pair, w, :], own) for pair, own, _, _ in slabs]
            acc_old = [gather(lambda w: acc_ref[pair, w, :], own) for pair, own, _, _ in slabs]
        heads = [(s, h) for s in range(len(slabs)) for h in range(2)]
        logits = [lax.dot_general(jnp.where(head0 if h == 0 else ~head0, q[s], 0.0).astype(BF16),
                                  keys[s], _NT, preferred_element_type=F32) for s, h in heads]
        logits = [lg + bias_ref[slabs[s][3], 2 * slabs[s][0] + h] for lg, (s, h) in zip(logits, heads)]
        mx = [jnp.max(lg, axis=-1, keepdims=True) for lg in logits]
        p = [jnp.exp2(lg - m) for lg, m in zip(logits, mx)]
        ls = [jnp.sum(ph, axis=-1, keepdims=True) for ph in p]
        pv = [_dot(ph.astype(BF16), vals[s]) for ph, (s, h) in zip(p, heads)]
        for s, (pair, own, _, _) in enumerate(slabs):
            m_t = jnp.where(head0, mx[2 * s], mx[2 * s + 1])
            l_t = jnp.where(head0, ls[2 * s], ls[2 * s + 1])
            acc_t = jnp.where(head0, pv[2 * s], pv[2 * s + 1])

            def put(ref, value, pair=pair, own=own):
                def store(w, rows):
                    ref[pair, w, :] = rows
                scatter(store, own, value)

            if first_group:
                put(m_ref, m_t)
                put(l_ref, l_t)
                put(acc_ref, acc_t)
            else:
                m_new = jnp.maximum(m_old[s], m_t)
                e_old = jnp.exp2(m_old[s] - m_new)
                e_t = jnp.exp2(m_t - m_new)
                put(m_ref, m_new)
                put(l_ref, l_old[s] * e_old + l_t * e_t)
                put(acc_ref, acc_old[s] * e_old + acc_t * e_t)
        return carry

    lax.fori_loop(0, seq_len // BLOCK // DIL_TILES_PER_ITER, tiles, 0)


def _dil_kernel(q_ref, k_ref, v_ref, bias_ref, o_ref, acc_ref, m_ref, l_ref):
    g = pl.program_id(1)

    for gi, (_, dil) in enumerate(DIL_GROUPS):
        pl.when(g == gi)(functools.partial(_dil_group, dil, gi == 0, q_ref, k_ref, v_ref, bias_ref,
                                           acc_ref, m_ref, l_ref))

    @pl.when(g == len(DIL_GROUPS) - 1)
    def _():
        phase_rows = o_ref.shape[2] // DIL_ROW_PHASES
        for pair in range(DIL_OUT_W // PAIR_W):
            for c in range(DIL_ROW_PHASES):
                rows = slice(c * phase_rows, (c + 1) * phase_rows)
                o_ref[0, pair, pl.ds(c, phase_rows, stride=DIL_ROW_PHASES), :] = (
                    acc_ref[pair, rows, :] / l_ref[pair, rows, :])


def _dilated_attention(q_slabs, kv_slabs, bias):
    b, _, s, _ = q_slabs.shape
    n_groups = len(DIL_GROUPS)
    pairs = DIL_OUT_W // PAIR_W
    blk = (1, pairs, s, PAIR_W)
    return pl.pallas_call(
        _dil_kernel,
        grid=(b, n_groups),
        in_specs=[
            pl.BlockSpec(blk, lambda i, g: (i, g, 0, 0)),
            pl.BlockSpec(blk, lambda i, g: (i, g, 0, 0)),
            pl.BlockSpec(blk, lambda i, g: (i, n_groups + g, 0, 0)),
            pl.BlockSpec((2, DIL_GROUP_HEADS, BLOCK, 2 * BLOCK), lambda i, g: (0, g, 0, 0)),
        ],
        out_specs=pl.BlockSpec(blk, lambda i, g: (i, 0, 0, 0)),
        out_shape=jax.ShapeDtypeStruct((b, pairs, s, PAIR_W), F32),
        scratch_shapes=[pltpu.VMEM((pairs, s, PAIR_W), F32)] * 3,
        compiler_params=_cparams(2, 56),
        name="dilated_attention",
    )(q_slabs, kv_slabs, kv_slabs, bias)


def kernel(x, mem, ffn_pre_norm, ffn_pre_w_in, ffn_pre_w_out, mix_norm, ffn_post_norm, ffn_post_w_in, ffn_post_w_out, mem_norm, mem_w_kv, mem_q_norm, mem_k_norm, a_w_in, a_shift_mu, a_w0, a_w_up, a_a0, a_a_up, a_g_up, a_kk_scale, a_k_a, a_r_k, a_lnx_g, a_lnx_b, a_w_out, b_w_q, b_q_norm, b_w_out, kv_norm, kv_w, kv_k_norm, rel_bias):
    b, s, d = x.shape
    depth = ffn_pre_w_in.shape[0]
    n_a = a_w_in.shape[0]
    n = b * s
    scale = LOG2_E / math.sqrt(HEAD_DIM)
    ones_bd = _block_ones(SLAB_W, HEAD_DIM)
    row = lambda p: p.reshape(1, -1)

    k_mem, v_mem = _memkv(mem, mem_norm[:, None, :], mem_w_kv.astype(BF16),
                          jnp.tile(mem_k_norm, (1, MEM_HEADS))[:, None, :], ones_bd)
    mem_q_gain = jnp.tile(mem_q_norm, (1, MEM_HEADS)) * scale

    pre_w_in, pre_w_out = ffn_pre_w_in.astype(BF16), ffn_pre_w_out.astype(BF16)
    post_w_in, post_w_out = ffn_post_w_in.astype(BF16), ffn_post_w_out.astype(BF16)
    a_w_in_b = a_w_in.astype(BF16)
    xf = x.reshape(n, d)
    kv = None
    bias = None
    for layer in range(depth):
        xf = _ffn(xf, row(ffn_pre_norm[layer]), pre_w_in, pre_w_out, layer)
        if layer < n_a:
            i = layer
            ps, qm = _aproj(xf, row(mix_norm[layer]), a_w_in_b, i, row(a_shift_mu[i]),
                            row(mem_q_gain[layer]), ones_bd, s)
            zeros = jnp.zeros((DECAY_LORA, RWKV_W), F32)
            vecs = jnp.stack([a_w0[i], a_a0[i], a_kk_scale[i], a_k_a[i], a_r_k[i].reshape(-1),
                              a_lnx_g[i], a_lnx_b[i], jnp.zeros((RWKV_W,), F32)])
            y_main = _rwkv(ps.reshape(b, s, RWKV_SHIFT_W), vecs,
                           jnp.concatenate([a_w_up[i], zeros]).astype(BF16),
                           jnp.concatenate([zeros, a_a_up[i]]).astype(BF16),
                           a_g_up[i].astype(BF16), ones_bd)
            mix = (y_main.reshape(n, RWKV_W), qm, k_mem[layer], v_mem[layer], a_w_out[i].astype(BF16), s)
        else:
            j = layer - n_a
            q_gain = jnp.concatenate([jnp.tile(b_q_norm[j], DIL_W // HEAD_DIM) * scale, mem_q_gain[layer]])
            q_dil, qm = _qproj(xf, row(mix_norm[layer]), b_w_q[j].astype(BF16), row(q_gain), ones_bd, s, DIL_W)
            if bias is None:
                bias = _band_bias(rel_bias)
            y_dil = _dilated_attention(q_dil, kv, bias)
            mix = (y_dil, qm, k_mem[layer], v_mem[layer], b_w_out[j].astype(BF16), s)
        xf = _ffn(xf, row(ffn_post_norm[layer]), post_w_in, post_w_out, layer, mix=mix)
        if layer == n_a - 1:
            k_gain = jnp.tile(kv_k_norm, DIL_W // HEAD_DIM)
            kv, = _qproj(xf, row(kv_norm), kv_w.astype(BF16), row(k_gain), ones_bd, s, 2 * DIL_W)
    return xf.reshape(b, s, d)
```

```python
import functools
import math

import jax
import jax.numpy as jnp
import numpy as np
from jax import lax
from jax.experimental import pallas as pl
from jax.experimental.pallas import tpu as pltpu

F32 = jnp.float32
BF16 = jnp.bfloat16

HEAD_DIM = 64
MEM_HEADS = 4
MEM_W = MEM_HEADS * HEAD_DIM
RWKV_HEADS = 12
RWKV_W = RWKV_HEADS * HEAD_DIM
DECAY_LORA = 64
AAA_LORA = 64
GATE_LORA = 128
RWKV_SHIFT_W = 3 * RWKV_W + DECAY_LORA + AAA_LORA + GATE_LORA
LORA_LO = 3 * RWKV_W
DIL_GROUPS = ((128, 1), (512, 4), (2048, 16))
DIL_GROUP_HEADS = 4
DIL_W = len(DIL_GROUPS) * DIL_GROUP_HEADS * HEAD_DIM
DIL_OUT_W = DIL_GROUP_HEADS * HEAD_DIM
BLOCK = 128
NUM_BUCKETS = 32
MAX_DISTANCE = 2048
NORM_EPS = 1e-6
LNX_EPS = 64e-5
NEG_INF = -1e30
LOG2_E = math.log2(math.e)

V7X_LANES = 128
V7X_SUBLANES = 8
V7X_VMEM_BYTES = 64 * 1024 * 1024
V7X_VMEM_LIMIT_CAP_BYTES = V7X_VMEM_BYTES * 15 // 16

RWKV_CHUNK = 64
RWKV_PHASE_A_CHUNKS = 8
FFN_ROWS = 1024
FFN_CHUNK = 256
FFN_HEAD_ROWS = 256
APROJ_ROWS = 1024
PROJ_SUBS_IN_FLIGHT = 4
QPROJ_ROWS = 1024
QPROJ_SUB_ROWS = 256
RWKV_ROWS = 512
RWKV_CUMSUM_ROWS = 256
DIL_TILES_PER_ITER = 4
DIL_ROW_PHASES = 4
PROJ_SUB_ROWS = 128
PAIR_W = 2 * HEAD_DIM
SLAB_W = 4 * HEAD_DIM

_NT = (((1,), (1,)), ((), ()))
_TN = (((0,), (0,)), ((), ()))


def _cparams(n_axes, vmem_mib):
    assert vmem_mib * 1024 * 1024 <= V7X_VMEM_LIMIT_CAP_BYTES
    return pltpu.CompilerParams(
        dimension_semantics=("arbitrary",) * n_axes,
        vmem_limit_bytes=vmem_mib * 1024 * 1024,
    )


def _const_spec(shape):
    zeros = (0,) * len(shape)
    return pl.BlockSpec(shape, lambda *_: zeros)


def _dot(a, b):
    return jnp.dot(a, b, preferred_element_type=F32)


def _rms(x, g, eps):
    return x * lax.rsqrt(jnp.mean(x * x, axis=-1, keepdims=True) + eps) * g


def _split2(x):
    hi = x.astype(BF16)
    lo = (x - hi.astype(F32)).astype(BF16)
    return hi, lo


def _seg_sum(x, ones_bd, pieces=2):
    outs = []
    for s in range(x.shape[-1] // SLAB_W):
        slab = x[:, s * SLAB_W:(s + 1) * SLAB_W]
        if pieces == 1:
            outs.append(_dot(slab.astype(BF16), ones_bd))
        else:
            hi, lo = _split2(slab)
            outs.append(_dot(hi, ones_bd) + _dot(lo, ones_bd))
    return outs[0] if len(outs) == 1 else jnp.concatenate(outs, axis=-1)


def _block_ones(n, blk, lower=False):
    i = np.arange(n)
    m = (i[:, None] // blk) == (i[None, :] // blk)
    if lower:
        m = m & (i[:, None] >= i[None, :])
    return jnp.asarray(m, dtype=BF16)


def _mix_delta(ya_ref, qm_ref, k_ref, v_ref, w_ref):
    if len(ya_ref.shape) == 4:
        ya = jnp.concatenate([ya_ref[0, j] for j in range(ya_ref.shape[1])], axis=1)
    else:
        ya = ya_ref[...]
    wa = ya.shape[-1]
    qm = qm_ref[...]
    kmem = k_ref[0]
    vmem = v_ref[0]
    head = lax.broadcasted_iota(jnp.int32, qm.shape, 1) // HEAD_DIM
    heads = range(MEM_HEADS)
    logits = [lax.dot_general(jnp.where(head == h, qm, jnp.zeros_like(qm)), kmem, _NT,
                              preferred_element_type=F32) for h in heads]
    p = [jnp.exp2(lg - jnp.max(lg, axis=-1, keepdims=True)) for lg in logits]
    inv_l = [1.0 / jnp.sum(ph, axis=-1, keepdims=True) for ph in p]
    pv = [_dot(ph.astype(BF16), vmem) for ph in p]
    y_mem = pv[0] * inv_l[0]
    for h in heads[1:]:
        y_mem = jnp.where(head == h, pv[h] * inv_l[h], y_mem)
    return _dot(ya.astype(BF16), w_ref[:wa, :]) + _dot(y_mem.astype(BF16), w_ref[wa:, :])


def _ffn_kernel(*refs, has_mix):
    if has_mix:
        x_ref, ya_ref, qm_ref, k_ref, v_ref, wmix_ref, g_ref, win_ref, wout_ref, o_ref = refs
        x = x_ref[...] + _mix_delta(ya_ref, qm_ref, k_ref, v_ref, wmix_ref)
    else:
        x_ref, g_ref, win_ref, wout_ref, o_ref = refs
        x = x_ref[...]
    d_ff = wout_ref.shape[1]
    n_chunks = d_ff // FFN_CHUNK

    def first_stage(xn_rows, c):
        lo = c * FFN_CHUNK
        return (_dot(xn_rows, win_ref[0, :, lo:lo + FFN_CHUNK]),
                _dot(xn_rows, win_ref[0, :, d_ff + lo:d_ff + lo + FFN_CHUNK]))

    head_rows = x.shape[0] if has_mix else FFN_HEAD_ROWS
    xn_parts, gate_parts, up_parts = [], [], []
    for r0 in range(0, x.shape[0], head_rows):
        xn_parts.append(_rms(x[r0:r0 + head_rows], g_ref[...], NORM_EPS).astype(BF16))
        gate_rows, up_rows = first_stage(xn_parts[-1], 0)
        gate_parts.append(gate_rows)
        up_parts.append(up_rows)
    xn = xn_parts[0] if len(xn_parts) == 1 else jnp.concatenate(xn_parts, axis=0)

    y = None
    pending = ((gate_parts[0], up_parts[0]) if len(xn_parts) == 1 else
               (jnp.concatenate(gate_parts, axis=0), jnp.concatenate(up_parts, axis=0)))
    for c in range(n_chunks):
        gate, up = pending
        if c + 1 < n_chunks:
            pending = first_stage(xn, c + 1)
        act = (gate * jax.nn.sigmoid(gate) * up).astype(BF16)
        part = _dot(act, wout_ref[0, c * FFN_CHUNK:(c + 1) * FFN_CHUNK, :])
        y = part if y is None else y + part
    o_ref[...] = x + 0.5 * y


def _ffn(x, g, w_in, w_out, layer, mix=None, tm=FFN_ROWS):
    n, d = x.shape
    assert w_out.shape[1] % FFN_CHUNK == 0 and n % tm == 0
    operands = [x]
    in_specs = [pl.BlockSpec((tm, d), lambda i: (i, 0))]
    if mix is not None:
        ya, qm, k_mem, v_mem, w_mix, seq_len = mix
        tiles_per_seq = seq_len // tm
        if ya.ndim == 4:
            ya_spec = pl.BlockSpec((1, ya.shape[1], tm, V7X_LANES),
                                   lambda i: (i // tiles_per_seq, 0, i % tiles_per_seq, 0))
        else:
            ya_spec = pl.BlockSpec((tm, ya.shape[-1]), lambda i: (i, 0))
        mem_spec = pl.BlockSpec((1,) + k_mem.shape[1:], lambda i: (i // tiles_per_seq, 0, 0))
        operands += [ya, qm, k_mem, v_mem, w_mix]
        in_specs += [ya_spec, pl.BlockSpec((tm, MEM_W), lambda i: (i, 0)), mem_spec, mem_spec,
                     _const_spec(w_mix.shape)]
    operands += [g, w_in, w_out]
    in_specs += [
        _const_spec((1, d)),
        pl.BlockSpec((1,) + w_in.shape[1:], lambda i: (layer, 0, 0), pipeline_mode=pl.Buffered(1)),
        pl.BlockSpec((1,) + w_out.shape[1:], lambda i: (layer, 0, 0), pipeline_mode=pl.Buffered(1)),
    ]
    return pl.pallas_call(
        functools.partial(_ffn_kernel, has_mix=mix is not None),
        grid=(n // tm,),
        in_specs=in_specs,
        out_specs=pl.BlockSpec((tm, d), lambda i: (i, 0)),
        out_shape=jax.ShapeDtypeStruct((n, d), F32),
        compiler_params=_cparams(1, 60),
        name="ffn_mix" if mix is not None else "ffn",
    )(*operands)


def _memkv_kernel(mem_ref, g_ref, w_ref, kg_ref, ones_ref, k_ref, v_ref):
    m = _rms(mem_ref[0], g_ref[0], NORM_EPS).astype(BF16)
    kv = _dot(m, w_ref[0])
    k = kv[:, :MEM_W]
    ms = _seg_sum(k * k, ones_ref[...]) * (1.0 / HEAD_DIM)
    k_ref[0, 0] = (k * lax.rsqrt(ms + NORM_EPS) * kg_ref[0]).astype(BF16)
    v_ref[0, 0] = kv[:, MEM_W:].astype(BF16)


def _memkv(mem, mem_norm, w_kv, k_gain, ones_bd):
    b, m, d = mem.shape
    depth = w_kv.shape[0]
    out = jax.ShapeDtypeStruct((depth, b, m, MEM_W), BF16)
    return pl.pallas_call(
        _memkv_kernel,
        grid=(depth, b),
        in_specs=[
            pl.BlockSpec((1, m, d), lambda l, i: (i, 0, 0)),
            pl.BlockSpec((1, 1, d), lambda l, i: (l, 0, 0)),
            pl.BlockSpec((1, d, 2 * MEM_W), lambda l, i: (l, 0, 0)),
            pl.BlockSpec((1, 1, MEM_W), lambda l, i: (l, 0, 0)),
            _const_spec((SLAB_W, SLAB_W)),
        ],
        out_specs=[pl.BlockSpec((1, 1, m, MEM_W), lambda l, i: (l, i, 0, 0))] * 2,
        out_shape=[out, out],
        compiler_params=_cparams(2, 32),
        name="memkv",
    )(mem, mem_norm, w_kv, k_gain, ones_bd)


def _head_norm(q, gain, ones_bd):
    ms = _seg_sum(q * q, ones_bd, pieces=1) * (1.0 / HEAD_DIM)
    return q * lax.rsqrt(ms + NORM_EPS) * gain


def _aproj_kernel(x_ref, g_ref, w_ref, mu_ref, qg_ref, ones_ref, ps_ref, qm_ref, carry_ref, *, tiles_per_seq):
    i = pl.program_id(0)
    tm = x_ref.shape[0]

    @pl.when(i % tiles_per_seq == 0)
    def _():
        carry_ref[...] = jnp.zeros_like(carry_ref)

    row = lax.broadcasted_iota(jnp.int32, (PROJ_SUB_ROWS, RWKV_SHIFT_W), 0)
    last = carry_ref[V7X_SUBLANES - 1:V7X_SUBLANES, :]
    group_rows = PROJ_SUB_ROWS * PROJ_SUBS_IN_FLIGHT
    for g0 in range(0, tm, group_rows):
        starts = range(g0, g0 + group_rows, PROJ_SUB_ROWS)
        projs = [_dot(_rms(x_ref[r0:r0 + PROJ_SUB_ROWS, :], g_ref[...], NORM_EPS).astype(BF16), w_ref[0])
                 for r0 in starts]
        for r0, proj in zip(starts, projs):
            p = proj[:, :RWKV_SHIFT_W]
            prev = jnp.where(row == 0, last, pltpu.roll(p, 1, 0))
            ps_ref[r0:r0 + PROJ_SUB_ROWS, :] = p + mu_ref[...] * (prev - p)
            last = p[PROJ_SUB_ROWS - 1:, :]
            qm_ref[r0:r0 + PROJ_SUB_ROWS, :] = _head_norm(proj[:, RWKV_SHIFT_W:], qg_ref[...],
                                                          ones_ref[...]).astype(qm_ref.dtype)
    carry_ref[...] = projs[-1][PROJ_SUB_ROWS - V7X_SUBLANES:, :RWKV_SHIFT_W]


def _aproj(x, g, w_in, layer, mu, q_gain, ones_bd, seq_len, tm=APROJ_ROWS):
    n, d = x.shape
    return pl.pallas_call(
        functools.partial(_aproj_kernel, tiles_per_seq=seq_len // tm),
        grid=(n // tm,),
        in_specs=[
            pl.BlockSpec((tm, d), lambda i: (i, 0)),
            _const_spec((1, d)),
            pl.BlockSpec((1,) + w_in.shape[1:], lambda i: (layer, 0, 0), pipeline_mode=pl.Buffered(1)),
            _const_spec((1, RWKV_SHIFT_W)),
            _const_spec((1, MEM_W)),
            _const_spec((SLAB_W, SLAB_W)),
        ],
        out_specs=[pl.BlockSpec((tm, RWKV_SHIFT_W), lambda i: (i, 0)),
                   pl.BlockSpec((tm, MEM_W), lambda i: (i, 0))],
        out_shape=[jax.ShapeDtypeStruct((n, RWKV_SHIFT_W), F32),
                   jax.ShapeDtypeStruct((n, MEM_W), BF16)],
        scratch_shapes=[pltpu.VMEM((V7X_SUBLANES, RWKV_SHIFT_W), F32)],
        compiler_params=_cparams(1, 56),
        name="aproj",
    )(x, g, w_in, mu, q_gain, ones_bd)


def _qproj_kernel(x_ref, g_ref, w_ref, qg_ref, ones_ref, slab_ref, *rest, normed_w):
    stage_ref = rest[-1]
    flat_refs = rest[:-1]
    tm = x_ref.shape[0]
    n_slabs = slab_ref.shape[1]
    starts = range(0, tm, QPROJ_SUB_ROWS)
    projs = [_dot(_rms(x_ref[r0:r0 + QPROJ_SUB_ROWS, :], g_ref[...], NORM_EPS).astype(BF16), w_ref[...])
             for r0 in starts]
    for r0, proj in zip(starts, projs):
        rows = slice(r0, r0 + QPROJ_SUB_ROWS)
        normed = _head_norm(proj[:, :normed_w], qg_ref[...], ones_ref[...])
        for j in range(n_slabs):
            src = normed if (j + 1) * V7X_LANES <= normed_w else proj
            stage_ref[j, rows, :] = src[:, j * V7X_LANES:(j + 1) * V7X_LANES]
        if flat_refs:
            flat_refs[0][rows, :] = normed[:, n_slabs * V7X_LANES:].astype(flat_refs[0].dtype)
    for j in range(n_slabs):
        for c in range(DIL_ROW_PHASES):
            slab_ref[0, j, c] = stage_ref[j, pl.ds(c, tm // DIL_ROW_PHASES, stride=DIL_ROW_PHASES), :]


def _qproj(x, g, w, q_gain, ones_bd, seq_len, slab_w, tm=QPROJ_ROWS):
    n, d = x.shape
    wo = w.shape[1]
    normed_w = q_gain.shape[-1]
    n_slabs = slab_w // V7X_LANES
    tiles_per_seq = seq_len // tm
    phase_rows = seq_len // DIL_ROW_PHASES
    out_specs = [pl.BlockSpec((1, n_slabs, DIL_ROW_PHASES, tm // DIL_ROW_PHASES, V7X_LANES),
                              lambda i: (i // tiles_per_seq, 0, 0, i % tiles_per_seq, 0))]
    out_shape = [jax.ShapeDtypeStruct((n // seq_len, n_slabs, DIL_ROW_PHASES, phase_rows, V7X_LANES), F32)]
    if wo > slab_w:
        assert normed_w == wo
        out_specs.append(pl.BlockSpec((tm, wo - slab_w), lambda i: (i, 0)))
        out_shape.append(jax.ShapeDtypeStruct((n, wo - slab_w), BF16))
    outs = list(pl.pallas_call(
        functools.partial(_qproj_kernel, normed_w=normed_w),
        grid=(n // tm,),
        in_specs=[
            pl.BlockSpec((tm, d), lambda i: (i, 0)),
            _const_spec((1, d)),
            pl.BlockSpec(w.shape, lambda i: (0, 0), pipeline_mode=pl.Buffered(1)),
            _const_spec((1, normed_w)),
            _const_spec((SLAB_W, SLAB_W)),
        ],
        out_specs=out_specs,
        out_shape=out_shape,
        scratch_shapes=[pltpu.VMEM((n_slabs, tm, V7X_LANES), F32)],
        compiler_params=_cparams(1, 48),
        name="qproj",
    )(x, g, w, q_gain, ones_bd))
    outs[0] = outs[0].reshape(n // seq_len, n_slabs, seq_len, V7X_LANES)
    return outs


def _embed(x):
    head0 = lax.broadcasted_iota(jnp.int32, x.shape, 1) < HEAD_DIM
    zero = jnp.zeros_like(x)
    return jnp.concatenate([jnp.where(head0, x, zero), jnp.where(head0, zero, x)], axis=0)


def _tri_inverse(nmats, tpos, spos):
    eye = (tpos == spos).astype(F32)
    same = {s: (tpos >> s) == (spos >> s) for s in (3, 4, 5)}

    def mm(lhs, rhs):
        return [_dot(a, _embed(b)) for a, b in zip(lhs, rhs)]

    def bf(xs):
        return [x.astype(BF16) for x in xs]

    n8 = [jnp.where(same[3], n, 0.0) for n in nmats]
    n8b = bf(n8)
    n2 = mm(n8b, n8b)
    n2b = bf(n2)
    n4 = mm(n2b, n2b)
    t = mm(bf([eye + a for a in n8]), bf([eye + a for a in n2]))
    t = mm(bf(t), bf([eye + a for a in n4]))
    for lo, hi in ((3, 4), (4, 5), (5, None)):
        off = ~same[lo] if hi is None else (same[hi] & ~same[lo])
        tb = bf(t)
        z = mm(bf([jnp.where(off, n, 0.0) for n in nmats]), tb)
        t = [a + d for a, d in zip(t, mm(tb, bf(z)))]
    return t


def _rwkv_kernel(ps_ref, vec_ref, wup_ref, aup_ref, gup_ref, tri_ref, ones_ref, y_ref,
                 s_ref, rt_ref, kt_ref, bt_ref, at_ref, kh_ref, bh_ref, v_ref, gam_ref, yacc_ref,
                 bonus_ref, gate_ref, p_s, qt_s, rp_s, y0_s):
    tblk = ps_ref.shape[1]
    n_pairs = RWKV_W // PAIR_W

    @pl.when(pl.program_id(1) == 0)
    def _():
        s_ref[...] = jnp.zeros_like(s_ref)

    ps = ps_ref[0]
    r = ps[:, :RWKV_W]
    k = ps[:, RWKV_W:2 * RWKV_W]
    v = ps[:, 2 * RWKV_W:3 * RWKV_W]
    lora_in = ps[:, LORA_LO:LORA_LO + PAIR_W]
    g_lo = ps[:, LORA_LO + PAIR_W:]
    w0, a0, kk_scale, k_a = vec_ref[0:1, :], vec_ref[1:2, :], vec_ref[2:3, :], vec_ref[3:4, :]
    r_k, lnx_g, lnx_b = vec_ref[4:5, :], vec_ref[5:6, :], vec_ref[6:7, :]
    ones_bd = ones_ref[...]

    z = w0 + _dot(jnp.tanh(lora_in).astype(BF16), wup_ref[...])
    lw = -(math.exp(-0.5) * LOG2_E) * jax.nn.sigmoid(z)
    a = jax.nn.sigmoid(a0 + _dot(lora_in.astype(BF16), aup_ref[...]))
    gate_ref[...] = _dot(jax.nn.sigmoid(g_lo).astype(BF16), gup_ref[...])
    kk = k * kk_scale
    kk = kk / jnp.maximum(jnp.sqrt(_seg_sum(kk * kk, ones_bd, pieces=1)), 1e-12)
    k2 = k * (1.0 + (a - 1.0) * k_a)
    kka = kk * a
    bonus_ref[...] = _seg_sum(r * k2 * r_k, ones_bd, pieces=1) * v

    hi, lo = _split2(lw)
    tri = tri_ref[...]
    gcum = jnp.concatenate(
        [_dot(tri, hi[s0:s0 + RWKV_CUMSUM_ROWS]) + _dot(tri, lo[s0:s0 + RWKV_CUMSUM_ROWS])
         for s0 in range(0, tblk, RWKV_CUMSUM_ROWS)], axis=0)
    n_chunks = tblk // RWKV_CHUNK
    gam_rows = [jnp.exp2(gcum[(c + 1) * RWKV_CHUNK - 1:(c + 1) * RWKV_CHUNK, :]) for c in range(n_chunks)]
    gam = jnp.concatenate([jnp.broadcast_to(g, (RWKV_CHUNK, RWKV_W)) for g in gam_rows], axis=0)
    for c in range(n_chunks):
        gam_ref[c:c + 1, :] = gam_rows[c]
    e_neg = jnp.exp2(-gcum)
    k_t = k2 * e_neg
    b_t = kka * e_neg
    rt_ref[...] = (r * jnp.exp2(gcum)).astype(BF16)
    kt_ref[...] = k_t.astype(BF16)
    bt_ref[...] = b_t.astype(BF16)
    at_ref[...] = (-kk * jnp.exp2(gcum - lw)).astype(BF16)
    kh_ref[...] = (k_t * gam).astype(BF16)
    bh_ref[...] = (b_t * gam).astype(BF16)
    v_ref[...] = v.astype(BF16)

    tpos = lax.broadcasted_iota(jnp.int32, (RWKV_CHUNK, PAIR_W), 0)
    spos = lax.broadcasted_iota(jnp.int32, (RWKV_CHUNK, PAIR_W), 1) & (HEAD_DIM - 1)
    strict = tpos > spos
    incl = tpos >= spos
    head0 = lax.broadcasted_iota(jnp.int32, (RWKV_CHUNK, PAIR_W), 1) < HEAD_DIM
    same_head = ((lax.broadcasted_iota(jnp.int32, (PAIR_W, PAIR_W), 0) < HEAD_DIM)
                 == (lax.broadcasted_iota(jnp.int32, (PAIR_W, PAIR_W), 1) < HEAD_DIM))

    pairs = range(n_pairs)
    lanes = [slice(p * PAIR_W, (p + 1) * PAIR_W) for p in pairs]

    def phase_a(i, carry):
        slots, at_v, rt_v, bt_v, kt_v, v_v, bh_v, kh_v = [], [], [], [], [], [], [], []
        for cc in range(RWKV_PHASE_A_CHUNKS):
            c = i * RWKV_PHASE_A_CHUNKS + cc
            rows = pl.ds(pl.multiple_of(c * RWKV_CHUNK, RWKV_CHUNK), RWKV_CHUNK)
            loaded = [ref[rows, :] for ref in (at_ref, rt_ref, bt_ref, kt_ref, v_ref, bh_ref, kh_ref)]
            for p in pairs:
                slots.append(c * n_pairs + p)
                for dst, x_c in zip((at_v, rt_v, bt_v, kt_v, v_v, bh_v, kh_v), loaded):
                    dst.append(x_c[:, lanes[p]])
        chains = range(len(slots))
        m4 = [lax.dot_general(jnp.concatenate([at_v[j], rt_v[j]], axis=0),
                              jnp.concatenate([_embed(bt_v[j]), _embed(kt_v[j])], axis=0), _NT,
                              preferred_element_type=F32) for j in chains]
        n_ab = [jnp.where(strict, m[:RWKV_CHUNK, :PAIR_W], 0.0) for m in m4]
        a_ak = [jnp.where(strict, m[:RWKV_CHUNK, PAIR_W:], 0.0).astype(BF16) for m in m4]
        a_rb = [jnp.where(incl, m[RWKV_CHUNK:, :PAIR_W], 0.0).astype(BF16) for m in m4]
        a_rk = [jnp.where(incl, m[RWKV_CHUNK:, PAIR_W:], 0.0).astype(BF16) for m in m4]
        v_m = [_embed(x) for x in v_v]
        akv = [_dot(a_ak[j], v_m[j]).astype(BF16) for j in chains]
        t_inv = [t.astype(BF16) for t in _tri_inverse(n_ab, tpos, spos)]
        wu0 = [_dot(t_inv[j], jnp.concatenate([_embed(at_v[j]), _embed(akv[j])], axis=1))
               for j in chains]
        w_b = [x[:, :PAIR_W].astype(BF16) for x in wu0]
        u0_b = [x[:, PAIR_W:].astype(BF16) for x in wu0]
        p_m = [jnp.where(same_head, lax.dot_general(bh_v[j], w_b[j], _TN, preferred_element_type=F32), 0.0)
               for j in chains]
        q_full = [lax.dot_general(jnp.concatenate([u0_b[j], v_v[j]], axis=0),
                                  jnp.concatenate([bh_v[j], kh_v[j]], axis=0), _TN,
                                  preferred_element_type=F32) for j in chains]
        q_t = [jnp.where(head0, q[:RWKV_CHUNK], q[RWKV_CHUNK:]) for q in q_full]
        r_p = [rt_v[j].astype(F32) + _dot(a_rb[j], _embed(w_b[j])) for j in chains]
        y_0 = [_dot(jnp.concatenate([a_rb[j], a_rk[j]], axis=1),
                    jnp.concatenate([_embed(u0_b[j]), v_m[j]], axis=0)) for j in chains]
        for j in chains:
            p_s[slots[j]] = p_m[j].astype(BF16)
            qt_s[slots[j]] = q_t[j]
            rp_s[slots[j]] = r_p[j].astype(BF16)
            y0_s[slots[j]] = y_0[j]
        return carry

    lax.fori_loop(0, n_chunks // RWKV_PHASE_A_CHUNKS, phase_a, 0)

    states = [s_ref[p] for p in pairs]
    for c in range(n_chunks):
        r0 = c * RWKV_CHUNK
        gam_c = gam_ref[c:c + 1, :]
        slots = [c * n_pairs + p for p in pairs]
        state_b = [s.astype(BF16) for s in states]
        upd = [lax.dot_general(state_b[p], p_s[slots[p]], _NT, preferred_element_type=F32) for p in pairs]
        y_c = [lax.dot_general(rp_s[slots[p]], _embed(state_b[p]), _NT, preferred_element_type=F32)
               + y0_s[slots[p]] for p in pairs]
        states = [states[p] * gam_c[:, lanes[p]] + upd[p] + qt_s[slots[p]] for p in pairs]
        yacc_ref[r0:r0 + RWKV_CHUNK, :] = jnp.concatenate(y_c, axis=1)
    for p in pairs:
        s_ref[p] = states[p]

    y = yacc_ref[...]
    mean = _seg_sum(y, ones_bd) * (1.0 / HEAD_DIM)
    dev = y - mean
    var = _seg_sum(dev * dev, ones_bd, pieces=1) * (1.0 / HEAD_DIM)
    yn = dev * lax.rsqrt(var + LNX_EPS) * lnx_g + lnx_b
    y_ref[0] = ((yn + bonus_ref[...]) * gate_ref[...]).astype(y_ref.dtype)


def _rwkv(ps, vecs, w_up_p, a_up_p, g_up, ones_bd, tblk=RWKV_ROWS):
    b, s, _ = ps.shape
    assert s % tblk == 0 and tblk % (RWKV_PHASE_A_CHUNKS * RWKV_CHUNK) == 0 and tblk % RWKV_CUMSUM_ROWS == 0
    tri = _block_ones(RWKV_CUMSUM_ROWS, RWKV_CHUNK, lower=True)
    act = pltpu.VMEM((tblk, RWKV_W), F32)
    act_b = pltpu.VMEM((tblk, RWKV_W), BF16)
    n_pairs = RWKV_W // PAIR_W
    n_chunks = tblk // RWKV_CHUNK
    n_slots = n_chunks * n_pairs
    gam_rows = -(-n_chunks // V7X_SUBLANES) * V7X_SUBLANES
    scratch = [pltpu.VMEM((n_pairs, HEAD_DIM, PAIR_W), F32)]
    scratch += [act_b] * 7
    scratch += [pltpu.VMEM((gam_rows, RWKV_W), F32)] + [act] * 3
    scratch += [
        pltpu.VMEM((n_slots, PAIR_W, PAIR_W), BF16),
        pltpu.VMEM((n_slots, RWKV_CHUNK, PAIR_W), F32),
        pltpu.VMEM((n_slots, RWKV_CHUNK, PAIR_W), BF16),
        pltpu.VMEM((n_slots, RWKV_CHUNK, PAIR_W), F32),
    ]
    return pl.pallas_call(
        _rwkv_kernel,
        grid=(b, s // tblk),
        in_specs=[
            pl.BlockSpec((1, tblk, RWKV_SHIFT_W), lambda i, t: (i, t, 0)),
            _const_spec(vecs.shape),
            _const_spec(w_up_p.shape),
            _const_spec(a_up_p.shape),
            _const_spec(g_up.shape),
            _const_spec(tri.shape),
            _const_spec((SLAB_W, SLAB_W)),
        ],
        out_specs=pl.BlockSpec((1, tblk, RWKV_W), lambda i, t: (i, t, 0)),
        out_shape=jax.ShapeDtypeStruct((b, s, RWKV_W), BF16),
        scratch_shapes=scratch,
        compiler_params=_cparams(2, 48),
        name="rwkv",
    )(ps, vecs, w_up_p, a_up_p, g_up, tri, ones_bd)


def _t5_bucket(dist):
    max_exact = NUM_BUCKETS // 2
    d_f = jnp.maximum(dist, 1).astype(F32)
    large = max_exact + (jnp.log(d_f / max_exact) / math.log(MAX_DISTANCE / max_exact)
                         * (NUM_BUCKETS - max_exact)).astype(jnp.int32)
    large = jnp.minimum(large, NUM_BUCKETS - 1)
    return jnp.where(dist < max_exact, dist, large)


def _band_buckets():
    run = BLOCK // DIL_ROW_PHASES
    out = []
    for window, dil in DIL_GROUPS:
        rows_q = np.arange(BLOCK)
        rows_k = np.arange(2 * BLOCK)
        if dil < DIL_ROW_PHASES:
            u_q = DIL_ROW_PHASES * (rows_q % run) + rows_q // run
            u_k = ((rows_k % (2 * run)) // run) * BLOCK + DIL_ROW_PHASES * (rows_k % run) + rows_k // (2 * run)
        else:
            u_q, u_k = rows_q, rows_k
        dsub = jnp.asarray(BLOCK + u_q[:, None] - u_k[None, :])
        band = (dsub >= 0) & (dsub <= window // dil)
        idx = jnp.where(band, _t5_bucket(jnp.maximum(dsub, 0) * dil), -1)
        out.append(jnp.stack([idx, jnp.where(jnp.asarray(u_k < BLOCK)[None, :], -1, idx)]))
    return jnp.stack(out, axis=1).astype(jnp.int32)


def _bias_kernel(tab_ref, idx_ref, o_ref):
    head = pl.program_id(1)
    idx = idx_ref[0, 0]
    acc = jnp.full(idx.shape, NEG_INF, F32)
    for bucket in range(NUM_BUCKETS):
        acc = jnp.where(idx == bucket, tab_ref[bucket, head] * LOG2_E, acc)
    o_ref[0, 0] = acc


def _band_bias(rel_bias):
    n_heads = rel_bias.shape[1]
    return pl.pallas_call(
        _bias_kernel,
        grid=(2, n_heads),
        in_specs=[
            pl.BlockSpec(memory_space=pltpu.SMEM),
            pl.BlockSpec((1, 1, BLOCK, 2 * BLOCK), lambda v, h: (v, h // DIL_GROUP_HEADS, 0, 0)),
        ],
        out_specs=pl.BlockSpec((1, 1, BLOCK, 2 * BLOCK), lambda v, h: (v, h, 0, 0)),
        out_shape=jax.ShapeDtypeStruct((2, n_heads, BLOCK, 2 * BLOCK), F32),
        compiler_params=_cparams(2, 16),
        name="band_bias",
    )(rel_bias, _band_buckets())


def _dil_group(dil, first_group, q_ref, k_ref, v_ref, bias_ref, acc_ref, m_ref, l_ref):
    seq_len = q_ref.shape[2]
    phase_rows = seq_len // DIL_ROW_PHASES
    run = BLOCK // DIL_ROW_PHASES
    head0 = lax.broadcasted_iota(jnp.int32, (BLOCK, PAIR_W), 1) < HEAD_DIM

    def windows(t):
        if dil < DIL_ROW_PHASES:
            blk = t
            back = jnp.maximum(blk - 1, 0)
            own = [pl.ds(pl.multiple_of(c * phase_rows + run * blk, run), run) for c in range(DIL_ROW_PHASES)]
            prev = [pl.ds(pl.multiple_of(c * phase_rows + run * back, run), run) for c in range(DIL_ROW_PHASES)]
            return own, prev, blk == 0
        step = dil // DIL_ROW_PHASES
        n_blocks = phase_rows // (step * BLOCK)
        sub = t >> (n_blocks.bit_length() - 1)
        blk = t & (n_blocks - 1)
        base = (sub & (DIL_ROW_PHASES - 1)) * phase_rows + (sub >> (DIL_ROW_PHASES.bit_length() - 1))
        back = jnp.maximum(blk - 1, 0)
        if step == 1:
            own = [pl.ds(pl.multiple_of(base + BLOCK * blk, BLOCK), BLOCK)]
            prev = [pl.ds(pl.multiple_of(base + BLOCK * back, BLOCK), BLOCK)]
        else:
            own = [pl.ds(base + step * BLOCK * blk, BLOCK, stride=step)]
            prev = [pl.ds(base + step * BLOCK * back, BLOCK, stride=step)]
        return own, prev, blk == 0

    def gather(get, wins):
        parts = [get(w) for w in wins]
        return parts[0] if len(parts) == 1 else jnp.concatenate(parts, axis=0)

    def scatter(put, wins, value):
        rows = BLOCK // len(wins)
        for i, w in enumerate(wins):
            put(w, value[i * rows:(i + 1) * rows])

    def tiles(i, carry):
        slabs = []
        for j in range(DIL_TILES_PER_ITER):
            own, prev, first = windows(i * DIL_TILES_PER_ITER + j)
            key_wins = [w for pw, ow in zip(prev, own) for w in (pw, ow)]
            variant = first.astype(jnp.int32)
            slabs += [(pair, own, key_wins, variant) for pair in range(DIL_OUT_W // PAIR_W)]
        q = [gather(lambda w: q_ref[0, pair, w, :], own) for pair, own, _, _ in slabs]
        keys = [gather(lambda w: k_ref[0, pair, w, :], kw).astype(BF16) for pair, _, kw, _ in slabs]
        vals = [gather(lambda w: v_ref[0, pair, w, :], kw).astype(BF16) for pair, _, kw, _ in slabs]
        if not first_group:
            m_old = [gather(lambda w: m_ref[pair, w, :], own) for pair, own, _, _ in slabs]
            l_old = [gather(lambda w: l_ref[pair, w, :], own) for pair, own, _, _ in slabs]
            acc_old = [gather(lambda w: acc_ref[pair, w, :], own) for pair, own, _, _ in slabs]
        heads = [(s, h) for s in range(len(slabs)) for h in range(2)]
        logits = [lax.dot_general(jnp.where(head0 if h == 0 else ~head0, q[s], 0.0).astype(BF16),
                                  keys[s], _NT, preferred_element_type=F32) for s, h in heads]
        logits = [lg + bias_ref[slabs[s][3], 2 * slabs[s][0] + h] for lg, (s, h) in zip(logits, heads)]
        mx = [jnp.max(lg, axis=-1, keepdims=True) for lg in logits]
        p = [jnp.exp2(lg - m) for lg, m in zip(logits, mx)]
        ls = [jnp.sum(ph, axis=-1, keepdims=True) for ph in p]
        pv = [_dot(ph.astype(BF16), vals[s]) for ph, (s, h) in zip(p, heads)]
        for s, (pair, own, _, _) in enumerate(slabs):
            m_t = jnp.where(head0, mx[2 * s], mx[2 * s + 1])
            l_t = jnp.where(head0, ls[2 * s], ls[2 * s + 1])
            acc_t = jnp.where(head0, pv[2 * s], pv[2 * s + 1])

            def put(ref, value, pair=pair, own=own):
                def store(w, rows):
                    ref[pair, w, :] = rows
                scatter(store, own, value)

            if first_group:
                put(m_ref, m_t)
                put(l_ref, l_t)
                put(acc_ref, acc_t)
            else:
                m_new = jnp.maximum(m_old[s], m_t)
                e_old = jnp.exp2(m_old[s] - m_new)
                e_t = jnp.exp2(m_t - m_new)
                put(m_ref, m_new)
                put(l_ref, l_old[s] * e_old + l_t * e_t)
                put(acc_ref, acc_old[s] * e_old + acc_t * e_t)
        return carry

    lax.fori_loop(0, seq_len // BLOCK // DIL_TILES_PER_ITER, tiles, 0)


def _dil_kernel(q_ref, k_ref, v_ref, bias_ref, o_ref, acc_ref, m_ref, l_ref):
    g = pl.program_id(1)

    for gi, (_, dil) in enumerate(DIL_GROUPS):
        pl.when(g == gi)(functools.partial(_dil_group, dil, gi == 0, q_ref, k_ref, v_ref, bias_ref,
                                           acc_ref, m_ref, l_ref))

    @pl.when(g == len(DIL_GROUPS) - 1)
    def _():
        phase_rows = o_ref.shape[2] // DIL_ROW_PHASES
        for pair in range(DIL_OUT_W // PAIR_W):
            for c in range(DIL_ROW_PHASES):
                rows = slice(c * phase_rows, (c + 1) * phase_rows)
                o_ref[0, pair, pl.ds(c, phase_rows, stride=DIL_ROW_PHASES), :] = (
                    acc_ref[pair, rows, :] / l_ref[pair, rows, :])


def _dilated_attention(q_slabs, kv_slabs, bias):
    b, _, s, _ = q_slabs.shape
    n_groups = len(DIL_GROUPS)
    pairs = DIL_OUT_W // PAIR_W
    blk = (1, pairs, s, PAIR_W)
    return pl.pallas_call(
        _dil_kernel,
        grid=(b, n_groups),
        in_specs=[
            pl.BlockSpec(blk, lambda i, g: (i, g, 0, 0)),
            pl.BlockSpec(blk, lambda i, g: (i, g, 0, 0)),
            pl.BlockSpec(blk, lambda i, g: (i, n_groups + g, 0, 0)),
            pl.BlockSpec((2, DIL_GROUP_HEADS, BLOCK, 2 * BLOCK), lambda i, g: (0, g, 0, 0)),
        ],
        out_specs=pl.BlockSpec(blk, lambda i, g: (i, 0, 0, 0)),
        out_shape=jax.ShapeDtypeStruct((b, pairs, s, PAIR_W), F32),
        scratch_shapes=[pltpu.VMEM((pairs, s, PAIR_W), F32)] * 3,
        compiler_params=_cparams(2, 56),
        name="dilated_attention",
    )(q_slabs, kv_slabs, kv_slabs, bias)


def kernel(x, mem, ffn_pre_norm, ffn_pre_w_in, ffn_pre_w_out, mix_norm, ffn_post_norm, ffn_post_w_in, ffn_post_w_out, mem_norm, mem_w_kv, mem_q_norm, mem_k_norm, a_w_in, a_shift_mu, a_w0, a_w_up, a_a0, a_a_up, a_g_up, a_kk_scale, a_k_a, a_r_k, a_lnx_g, a_lnx_b, a_w_out, b_w_q, b_q_norm, b_w_out, kv_norm, kv_w, kv_k_norm, rel_bias):
    b, s, d = x.shape
    depth = ffn_pre_w_in.shape[0]
    n_a = a_w_in.shape[0]
    n = b * s
    scale = LOG2_E / math.sqrt(HEAD_DIM)
    ones_bd = _block_ones(SLAB_W, HEAD_DIM)
    row = lambda p: p.reshape(1, -1)

    k_mem, v_mem = _memkv(mem, mem_norm[:, None, :], mem_w_kv.astype(BF16),
                          jnp.tile(mem_k_norm, (1, MEM_HEADS))[:, None, :], ones_bd)
    mem_q_gain = jnp.tile(mem_q_norm, (1, MEM_HEADS)) * scale

    pre_w_in, pre_w_out = ffn_pre_w_in.astype(BF16), ffn_pre_w_out.astype(BF16)
    post_w_in, post_w_out = ffn_post_w_in.astype(BF16), ffn_post_w_out.astype(BF16)
    a_w_in_b = a_w_in.astype(BF16)
    xf = x.reshape(n, d)
    kv = None
    bias = None
    for layer in range(depth):
        xf = _ffn(xf, row(ffn_pre_norm[layer]), pre_w_in, pre_w_out, layer)
        if layer < n_a:
            i = layer
            ps, qm = _aproj(xf, row(mix_norm[layer]), a_w_in_b, i, row(a_shift_mu[i]),
                            row(mem_q_gain[layer]), ones_bd, s)
            zeros = jnp.zeros((DECAY_LORA, RWKV_W), F32)
            vecs = jnp.stack([a_w0[i], a_a0[i], a_kk_scale[i], a_k_a[i], a_r_k[i].reshape(-1),
                              a_lnx_g[i], a_lnx_b[i], jnp.zeros((RWKV_W,), F32)])
            y_main = _rwkv(ps.reshape(b, s, RWKV_SHIFT_W), vecs,
                           jnp.concatenate([a_w_up[i], zeros]).astype(BF16),
                           jnp.concatenate([zeros, a_a_up[i]]).astype(BF16),
                           a_g_up[i].astype(BF16), ones_bd)
            mix = (y_main.reshape(n, RWKV_W), qm, k_mem[layer], v_mem[layer], a_w_out[i].astype(BF16), s)
        else:
            j = layer - n_a
            q_gain = jnp.concatenate([jnp.tile(b_q_norm[j], DIL_W // HEAD_DIM) * scale, mem_q_gain[layer]])
            q_dil, qm = _qproj(xf, row(mix_norm[layer]), b_w_q[j].astype(BF16), row(q_gain), ones_bd, s, DIL_W)
            if bias is None:
                bias = _band_bias(rel_bias)
            y_dil = _dilated_attention(q_dil, kv, bias)
            mix = (y_dil, qm, k_mem[layer], v_mem[layer], b_w_out[j].astype(BF16), s)
        xf = _ffn(xf, row(ffn_post_norm[layer]), post_w_in, post_w_out, layer, mix=mix)
        if layer == n_a - 1:
            k_gain = jnp.tile(kv_k_norm, DIL_W // HEAD_DIM)
            kv, = _qproj(xf, row(kv_norm), kv_w.astype(BF16), row(k_gain), ones_bd, s, 2 * DIL_W)
    return xf.reshape(b, s, d)
```

```python
import functools
import math

import jax
import jax.numpy as jnp
import numpy as np
from jax import lax
from jax.experimental import pallas as pl
from jax.experimental.pallas import tpu as pltpu

F32 = jnp.float32
BF16 = jnp.bfloat16

HEAD_DIM = 64
MEM_HEADS = 4
MEM_W = MEM_HEADS * HEAD_DIM
RWKV_HEADS = 12
RWKV_W = RWKV_HEADS * HEAD_DIM
DECAY_LORA = 64
AAA_LORA = 64
GATE_LORA = 128
RWKV_SHIFT_W = 3 * RWKV_W + DECAY_LORA + AAA_LORA + GATE_LORA
LORA_LO = 3 * RWKV_W
DIL_GROUPS = ((128, 1), (512, 4), (2048, 16))
DIL_GROUP_HEADS = 4
DIL_W = len(DIL_GROUPS) * DIL_GROUP_HEADS * HEAD_DIM
DIL_OUT_W = DIL_GROUP_HEADS * HEAD_DIM
BLOCK = 128
NUM_BUCKETS = 32
MAX_DISTANCE = 2048
NORM_EPS = 1e-6
LNX_EPS = 64e-5
NEG_INF = -1e30
LOG2_E = math.log2(math.e)

V7X_LANES = 128
V7X_SUBLANES = 8
V7X_VMEM_BYTES = 64 * 1024 * 1024
V7X_VMEM_LIMIT_CAP_BYTES = V7X_VMEM_BYTES * 15 // 16

RWKV_CHUNK = 64
RWKV_PHASE_A_CHUNKS = 8
FFN_ROWS = 1024
FFN_CHUNK = 256
FFN_HEAD_ROWS = 256
APROJ_ROWS = 1024
PROJ_SUBS_IN_FLIGHT = 4
QPROJ_ROWS = 1024
QPROJ_SUB_ROWS = 256
RWKV_ROWS = 512
RWKV_CUMSUM_ROWS = 128
DIL_TILES_PER_ITER = 4
DIL_ROW_PHASES = 4
PROJ_SUB_ROWS = 128
PAIR_W = 2 * HEAD_DIM
SLAB_W = 4 * HEAD_DIM

_NT = (((1,), (1,)), ((), ()))
_TN = (((0,), (0,)), ((), ()))


def _cparams(n_axes, vmem_mib):
    assert vmem_mib * 1024 * 1024 <= V7X_VMEM_LIMIT_CAP_BYTES
    return pltpu.CompilerParams(
        dimension_semantics=("arbitrary",) * n_axes,
        vmem_limit_bytes=vmem_mib * 1024 * 1024,
    )


def _const_spec(shape):
    zeros = (0,) * len(shape)
    return pl.BlockSpec(shape, lambda *_: zeros)


def _dot(a, b):
    return jnp.dot(a, b, preferred_element_type=F32)


def _rms(x, g, eps):
    return x * lax.rsqrt(jnp.mean(x * x, axis=-1, keepdims=True) + eps) * g


def _split2(x):
    hi = x.astype(BF16)
    lo = (x - hi.astype(F32)).astype(BF16)
    return hi, lo


def _seg_sum(x, ones_bd, pieces=2):
    outs = []
    for s in range(x.shape[-1] // SLAB_W):
        slab = x[:, s * SLAB_W:(s + 1) * SLAB_W]
        if pieces == 1:
            outs.append(_dot(slab.astype(BF16), ones_bd))
        else:
            hi, lo = _split2(slab)
            outs.append(_dot(hi, ones_bd) + _dot(lo, ones_bd))
    return outs[0] if len(outs) == 1 else jnp.concatenate(outs, axis=-1)


def _block_ones(n, blk, lower=False):
    i = np.arange(n)
    m = (i[:, None] // blk) == (i[None, :] // blk)
    if lower:
        m = m & (i[:, None] >= i[None, :])
    return jnp.asarray(m, dtype=BF16)


def _mix_delta(ya_ref, qm_ref, k_ref, v_ref, w_ref):
    if len(ya_ref.shape) == 4:
        ya = jnp.concatenate([ya_ref[0, j] for j in range(ya_ref.shape[1])], axis=1)
    else:
        ya = ya_ref[...]
    wa = ya.shape[-1]
    qm = qm_ref[...]
    kmem = k_ref[0]
    vmem = v_ref[0]
    head = lax.broadcasted_iota(jnp.int32, qm.shape, 1) // HEAD_DIM
    heads = range(MEM_HEADS)
    logits = [lax.dot_general(jnp.where(head == h, qm, jnp.zeros_like(qm)), kmem, _NT,
                              preferred_element_type=F32) for h in heads]
    p = [jnp.exp2(lg - jnp.max(lg, axis=-1, keepdims=True)) for lg in logits]
    inv_l = [1.0 / jnp.sum(ph, axis=-1, keepdims=True) for ph in p]
    pv = [_dot(ph.astype(BF16), vmem) for ph in p]
    y_mem = pv[0] * inv_l[0]
    for h in heads[1:]:
        y_mem = jnp.where(head == h, pv[h] * inv_l[h], y_mem)
    return _dot(ya.astype(BF16), w_ref[:wa, :]) + _dot(y_mem.astype(BF16), w_ref[wa:, :])


def _ffn_kernel(*refs, has_mix):
    if has_mix:
        x_ref, ya_ref, qm_ref, k_ref, v_ref, wmix_ref, g_ref, win_ref, wout_ref, o_ref = refs
        x = x_ref[...] + _mix_delta(ya_ref, qm_ref, k_ref, v_ref, wmix_ref)
    else:
        x_ref, g_ref, win_ref, wout_ref, o_ref = refs
        x = x_ref[...]
    d_ff = wout_ref.shape[1]
    n_chunks = d_ff // FFN_CHUNK

    def first_stage(xn_rows, c):
        lo = c * FFN_CHUNK
        return (_dot(xn_rows, win_ref[0, :, lo:lo + FFN_CHUNK]),
                _dot(xn_rows, win_ref[0, :, d_ff + lo:d_ff + lo + FFN_CHUNK]))

    head_rows = x.shape[0] if has_mix else FFN_HEAD_ROWS
    xn_parts, gate_parts, up_parts = [], [], []
    for r0 in range(0, x.shape[0], head_rows):
        xn_parts.append(_rms(x[r0:r0 + head_rows], g_ref[...], NORM_EPS).astype(BF16))
        gate_rows, up_rows = first_stage(xn_parts[-1], 0)
        gate_parts.append(gate_rows)
        up_parts.append(up_rows)
    xn = xn_parts[0] if len(xn_parts) == 1 else jnp.concatenate(xn_parts, axis=0)

    y = None
    pending = ((gate_parts[0], up_parts[0]) if len(xn_parts) == 1 else
               (jnp.concatenate(gate_parts, axis=0), jnp.concatenate(up_parts, axis=0)))
    for c in range(n_chunks):
        gate, up = pending
        if c + 1 < n_chunks:
            pending = first_stage(xn, c + 1)
        act = (gate * jax.nn.sigmoid(gate) * up).astype(BF16)
        part = _dot(act, wout_ref[0, c * FFN_CHUNK:(c + 1) * FFN_CHUNK, :])
        y = part if y is None else y + part
    o_ref[...] = x + 0.5 * y


def _ffn(x, g, w_in, w_out, layer, mix=None, tm=FFN_ROWS):
    n, d = x.shape
    assert w_out.shape[1] % FFN_CHUNK == 0 and n % tm == 0
    operands = [x]
    in_specs = [pl.BlockSpec((tm, d), lambda i: (i, 0))]
    if mix is not None:
        ya, qm, k_mem, v_mem, w_mix, seq_len = mix
        tiles_per_seq = seq_len // tm
        if ya.ndim == 4:
            ya_spec = pl.BlockSpec((1, ya.shape[1], tm, V7X_LANES),
                                   lambda i: (i // tiles_per_seq, 0, i % tiles_per_seq, 0))
        else:
            ya_spec = pl.BlockSpec((tm, ya.shape[-1]), lambda i: (i, 0))
        mem_spec = pl.BlockSpec((1,) + k_mem.shape[1:], lambda i: (i // tiles_per_seq, 0, 0))
        operands += [ya, qm, k_mem, v_mem, w_mix]
        in_specs += [ya_spec, pl.BlockSpec((tm, MEM_W), lambda i: (i, 0)), mem_spec, mem_spec,
                     _const_spec(w_mix.shape)]
    operands += [g, w_in, w_out]
    in_specs += [
        _const_spec((1, d)),
        pl.BlockSpec((1,) + w_in.shape[1:], lambda i: (layer, 0, 0), pipeline_mode=pl.Buffered(1)),
        pl.BlockSpec((1,) + w_out.shape[1:], lambda i: (layer, 0, 0), pipeline_mode=pl.Buffered(1)),
    ]
    return pl.pallas_call(
        functools.partial(_ffn_kernel, has_mix=mix is not None),
        grid=(n // tm,),
        in_specs=in_specs,
        out_specs=pl.BlockSpec((tm, d), lambda i: (i, 0)),
        out_shape=jax.ShapeDtypeStruct((n, d), F32),
        compiler_params=_cparams(1, 60),
        name="ffn_mix" if mix is not None else "ffn",
    )(*operands)


def _memkv_kernel(mem_ref, g_ref, w_ref, kg_ref, ones_ref, k_ref, v_ref):
    m = _rms(mem_ref[0], g_ref[0], NORM_EPS).astype(BF16)
    kv = _dot(m, w_ref[0])
    k = kv[:, :MEM_W]
    ms = _seg_sum(k * k, ones_ref[...]) * (1.0 / HEAD_DIM)
    k_ref[0, 0] = (k * lax.rsqrt(ms + NORM_EPS) * kg_ref[0]).astype(BF16)
    v_ref[0, 0] = kv[:, MEM_W:].astype(BF16)


def _memkv(mem, mem_norm, w_kv, k_gain, ones_bd):
    b, m, d = mem.shape
    depth = w_kv.shape[0]
    out = jax.ShapeDtypeStruct((depth, b, m, MEM_W), BF16)
    return pl.pallas_call(
        _memkv_kernel,
        grid=(depth, b),
        in_specs=[
            pl.BlockSpec((1, m, d), lambda l, i: (i, 0, 0)),
            pl.BlockSpec((1, 1, d), lambda l, i: (l, 0, 0)),
            pl.BlockSpec((1, d, 2 * MEM_W), lambda l, i: (l, 0, 0)),
            pl.BlockSpec((1, 1, MEM_W), lambda l, i: (l, 0, 0)),
            _const_spec((SLAB_W, SLAB_W)),
        ],
        out_specs=[pl.BlockSpec((1, 1, m, MEM_W), lambda l, i: (l, i, 0, 0))] * 2,
        out_shape=[out, out],
        compiler_params=_cparams(2, 32),
        name="memkv",
    )(mem, mem_norm, w_kv, k_gain, ones_bd)


def _head_norm(q, gain, ones_bd):
    ms = _seg_sum(q * q, ones_bd, pieces=1) * (1.0 / HEAD_DIM)
    return q * lax.rsqrt(ms + NORM_EPS) * gain


def _aproj_kernel(x_ref, g_ref, w_ref, mu_ref, qg_ref, ones_ref, ps_ref, qm_ref, carry_ref, *, tiles_per_seq):
    i = pl.program_id(0)
    tm = x_ref.shape[0]

    @pl.when(i % tiles_per_seq == 0)
    def _():
        carry_ref[...] = jnp.zeros_like(carry_ref)

    row = lax.broadcasted_iota(jnp.int32, (PROJ_SUB_ROWS, RWKV_SHIFT_W), 0)
    last = carry_ref[V7X_SUBLANES - 1:V7X_SUBLANES, :]
    group_rows = PROJ_SUB_ROWS * PROJ_SUBS_IN_FLIGHT
    for g0 in range(0, tm, group_rows):
        starts = range(g0, g0 + group_rows, PROJ_SUB_ROWS)
        projs = [_dot(_rms(x_ref[r0:r0 + PROJ_SUB_ROWS, :], g_ref[...], NORM_EPS).astype(BF16), w_ref[0])
                 for r0 in starts]
        for r0, proj in zip(starts, projs):
            p = proj[:, :RWKV_SHIFT_W]
            prev = jnp.where(row == 0, last, pltpu.roll(p, 1, 0))
            ps_ref[r0:r0 + PROJ_SUB_ROWS, :] = p + mu_ref[...] * (prev - p)
            last = p[PROJ_SUB_ROWS - 1:, :]
            qm_ref[r0:r0 + PROJ_SUB_ROWS, :] = _head_norm(proj[:, RWKV_SHIFT_W:], qg_ref[...],
                                                          ones_ref[...]).astype(qm_ref.dtype)
    carry_ref[...] = projs[-1][PROJ_SUB_ROWS - V7X_SUBLANES:, :RWKV_SHIFT_W]


def _aproj(x, g, w_in, layer, mu, q_gain, ones_bd, seq_len, tm=APROJ_ROWS):
    n, d = x.shape
    return pl.pallas_call(
        functools.partial(_aproj_kernel, tiles_per_seq=seq_len // tm),
        grid=(n // tm,),
        in_specs=[
            pl.BlockSpec((tm, d), lambda i: (i, 0)),
            _const_spec((1, d)),
            pl.BlockSpec((1,) + w_in.shape[1:], lambda i: (layer, 0, 0), pipeline_mode=pl.Buffered(1)),
            _const_spec((1, RWKV_SHIFT_W)),
            _const_spec((1, MEM_W)),
            _const_spec((SLAB_W, SLAB_W)),
        ],
        out_specs=[pl.BlockSpec((tm, RWKV_SHIFT_W), lambda i: (i, 0)),
                   pl.BlockSpec((tm, MEM_W), lambda i: (i, 0))],
        out_shape=[jax.ShapeDtypeStruct((n, RWKV_SHIFT_W), F32),
                   jax.ShapeDtypeStruct((n, MEM_W), BF16)],
        scratch_shapes=[pltpu.VMEM((V7X_SUBLANES, RWKV_SHIFT_W), F32)],
        compiler_params=_cparams(1, 56),
        name="aproj",
    )(x, g, w_in, mu, q_gain, ones_bd)


def _qproj_kernel(x_ref, g_ref, w_ref, qg_ref, ones_ref, slab_ref, *rest, normed_w):
    stage_ref = rest[-1]
    flat_refs = rest[:-1]
    tm = x_ref.shape[0]
    n_slabs = slab_ref.shape[1]
    starts = range(0, tm, QPROJ_SUB_ROWS)
    projs = [_dot(_rms(x_ref[r0:r0 + QPROJ_SUB_ROWS, :], g_ref[...], NORM_EPS).astype(BF16), w_ref[...])
             for r0 in starts]
    for r0, proj in zip(starts, projs):
        rows = slice(r0, r0 + QPROJ_SUB_ROWS)
        normed = _head_norm(proj[:, :normed_w], qg_ref[...], ones_ref[...])
        for j in range(n_slabs):
            src = normed if (j + 1) * V7X_LANES <= normed_w else proj
            stage_ref[j, rows, :] = src[:, j * V7X_LANES:(j + 1) * V7X_LANES]
        if flat_refs:
            flat_refs[0][rows, :] = normed[:, n_slabs * V7X_LANES:].astype(flat_refs[0].dtype)
    for j in range(n_slabs):
        for c in range(DIL_ROW_PHASES):
            slab_ref[0, j, c] = stage_ref[j, pl.ds(c, tm // DIL_ROW_PHASES, stride=DIL_ROW_PHASES), :]


def _qproj(x, g, w, q_gain, ones_bd, seq_len, slab_w, tm=QPROJ_ROWS):
    n, d = x.shape
    wo = w.shape[1]
    normed_w = q_gain.shape[-1]
    n_slabs = slab_w // V7X_LANES
    tiles_per_seq = seq_len // tm
    phase_rows = seq_len // DIL_ROW_PHASES
    out_specs = [pl.BlockSpec((1, n_slabs, DIL_ROW_PHASES, tm // DIL_ROW_PHASES, V7X_LANES),
                              lambda i: (i // tiles_per_seq, 0, 0, i % tiles_per_seq, 0))]
    out_shape = [jax.ShapeDtypeStruct((n // seq_len, n_slabs, DIL_ROW_PHASES, phase_rows, V7X_LANES), F32)]
    if wo > slab_w:
        assert normed_w == wo
        out_specs.append(pl.BlockSpec((tm, wo - slab_w), lambda i: (i, 0)))
        out_shape.append(jax.ShapeDtypeStruct((n, wo - slab_w), BF16))
    outs = list(pl.pallas_call(
        functools.partial(_qproj_kernel, normed_w=normed_w),
        grid=(n // tm,),
        in_specs=[
            pl.BlockSpec((tm, d), lambda i: (i, 0)),
            _const_spec((1, d)),
            pl.BlockSpec(w.shape, lambda i: (0, 0), pipeline_mode=pl.Buffered(1)),
            _const_spec((1, normed_w)),
            _const_spec((SLAB_W, SLAB_W)),
        ],
        out_specs=out_specs,
        out_shape=out_shape,
        scratch_shapes=[pltpu.VMEM((n_slabs, tm, V7X_LANES), F32)],
        compiler_params=_cparams(1, 48),
        name="qproj",
    )(x, g, w, q_gain, ones_bd))
    outs[0] = outs[0].reshape(n // seq_len, n_slabs, seq_len, V7X_LANES)
    return outs


def _embed(x):
    head0 = lax.broadcasted_iota(jnp.int32, x.shape, 1) < HEAD_DIM
    zero = jnp.zeros_like(x)
    return jnp.concatenate([jnp.where(head0, x, zero), jnp.where(head0, zero, x)], axis=0)


def _tri_inverse(nmats, tpos, spos):
    eye = (tpos == spos).astype(F32)
    same = {s: (tpos >> s) == (spos >> s) for s in (3, 4, 5)}

    def mm(lhs, rhs):
        return [_dot(a, _embed(b)) for a, b in zip(lhs, rhs)]

    def bf(xs):
        return [x.astype(BF16) for x in xs]

    n8 = [jnp.where(same[3], n, 0.0) for n in nmats]
    n8b = bf(n8)
    n2 = mm(n8b, n8b)
    n2b = bf(n2)
    n4 = mm(n2b, n2b)
    t = mm(bf([eye + a for a in n8]), bf([eye + a for a in n2]))
    t = mm(bf(t), bf([eye + a for a in n4]))
    for lo, hi in ((3, 4), (4, 5), (5, None)):
        off = ~same[lo] if hi is None else (same[hi] & ~same[lo])
        tb = bf(t)
        z = mm(bf([jnp.where(off, n, 0.0) for n in nmats]), tb)
        t = [a + d for a, d in zip(t, mm(tb, bf(z)))]
    return t


def _rwkv_kernel(ps_ref, vec_ref, wup_ref, aup_ref, gup_ref, tri_ref, ones_ref, y_ref,
                 s_ref, rt_ref, kt_ref, bt_ref, at_ref, kh_ref, bh_ref, v_ref, gam_ref, yacc_ref,
                 bonus_ref, gate_ref, p_s, qt_s, rp_s, y0_s):
    tblk = ps_ref.shape[1]
    n_pairs = RWKV_W // PAIR_W

    @pl.when(pl.program_id(1) == 0)
    def _():
        s_ref[...] = jnp.zeros_like(s_ref)

    w0, a0, kk_scale, k_a = vec_ref[0:1, :], vec_ref[1:2, :], vec_ref[2:3, :], vec_ref[3:4, :]
    r_k, lnx_g, lnx_b = vec_ref[4:5, :], vec_ref[5:6, :], vec_ref[6:7, :]
    ones_bd = ones_ref[...]
    n_chunks = tblk // RWKV_CHUNK
    sub_chunks = RWKV_CUMSUM_ROWS // RWKV_CHUNK

    def prepare(i, carry):
        rows = pl.ds(pl.multiple_of(i * RWKV_CUMSUM_ROWS, RWKV_CUMSUM_ROWS), RWKV_CUMSUM_ROWS)
        ps = ps_ref[0, rows, :]
        r = ps[:, :RWKV_W]
        k = ps[:, RWKV_W:2 * RWKV_W]
        v = ps[:, 2 * RWKV_W:3 * RWKV_W]
        lora_in = ps[:, LORA_LO:LORA_LO + PAIR_W]
        g_lo = ps[:, LORA_LO + PAIR_W:]
        z = w0 + _dot(jnp.tanh(lora_in).astype(BF16), wup_ref[...])
        lw = -(math.exp(-0.5) * LOG2_E) * jax.nn.sigmoid(z)
        a = jax.nn.sigmoid(a0 + _dot(lora_in.astype(BF16), aup_ref[...]))
        gate_ref[rows, :] = _dot(jax.nn.sigmoid(g_lo).astype(BF16), gup_ref[...])
        kk = k * kk_scale
        kk = kk / jnp.maximum(jnp.sqrt(_seg_sum(kk * kk, ones_bd, pieces=1)), 1e-12)
        k2 = k * (1.0 + (a - 1.0) * k_a)
        kka = kk * a
        bonus_ref[rows, :] = _seg_sum(r * k2 * r_k, ones_bd, pieces=1) * v
        hi, lo = _split2(lw)
        tri = tri_ref[...]
        gcum = _dot(tri, hi) + _dot(tri, lo)
        gam_rows = [jnp.exp2(gcum[(c + 1) * RWKV_CHUNK - 1:(c + 1) * RWKV_CHUNK, :]) for c in range(sub_chunks)]
        gam = jnp.concatenate([jnp.broadcast_to(g, (RWKV_CHUNK, RWKV_W)) for g in gam_rows], axis=0)
        for c in range(sub_chunks):
            gam_ref[i * sub_chunks + c] = jnp.broadcast_to(gam_rows[c], (V7X_SUBLANES, RWKV_W))
        e_neg = jnp.exp2(-gcum)
        k_t = k2 * e_neg
        b_t = kka * e_neg
        rt_ref[rows, :] = (r * jnp.exp2(gcum)).astype(BF16)
        kt_ref[rows, :] = k_t.astype(BF16)
        bt_ref[rows, :] = b_t.astype(BF16)
        at_ref[rows, :] = (-kk * jnp.exp2(gcum - lw)).astype(BF16)
        kh_ref[rows, :] = (k_t * gam).astype(BF16)
        bh_ref[rows, :] = (b_t * gam).astype(BF16)
        v_ref[rows, :] = v.astype(BF16)
        return carry

    lax.fori_loop(0, tblk // RWKV_CUMSUM_ROWS, prepare, 0)

    tpos = lax.broadcasted_iota(jnp.int32, (RWKV_CHUNK, PAIR_W), 0)
    spos = lax.broadcasted_iota(jnp.int32, (RWKV_CHUNK, PAIR_W), 1) & (HEAD_DIM - 1)
    strict = tpos > spos
    incl = tpos >= spos
    head0 = lax.broadcasted_iota(jnp.int32, (RWKV_CHUNK, PAIR_W), 1) < HEAD_DIM
    same_head = ((lax.broadcasted_iota(jnp.int32, (PAIR_W, PAIR_W), 0) < HEAD_DIM)
                 == (lax.broadcasted_iota(jnp.int32, (PAIR_W, PAIR_W), 1) < HEAD_DIM))

    pairs = range(n_pairs)
    lanes = [slice(p * PAIR_W, (p + 1) * PAIR_W) for p in pairs]

    def phase_a(i, carry):
        slots, at_v, rt_v, bt_v, kt_v, v_v, bh_v, kh_v = [], [], [], [], [], [], [], []
        for cc in range(RWKV_PHASE_A_CHUNKS):
            c = i * RWKV_PHASE_A_CHUNKS + cc
            rows = pl.ds(pl.multiple_of(c * RWKV_CHUNK, RWKV_CHUNK), RWKV_CHUNK)
            loaded = [ref[rows, :] for ref in (at_ref, rt_ref, bt_ref, kt_ref, v_ref, bh_ref, kh_ref)]
            for p in pairs:
                slots.append(c * n_pairs + p)
                for dst, x_c in zip((at_v, rt_v, bt_v, kt_v, v_v, bh_v, kh_v), loaded):
                    dst.append(x_c[:, lanes[p]])
        chains = range(len(slots))
        m4 = [lax.dot_general(jnp.concatenate([at_v[j], rt_v[j]], axis=0),
                              jnp.concatenate([_embed(bt_v[j]), _embed(kt_v[j])], axis=0), _NT,
                              preferred_element_type=F32) for j in chains]
        n_ab = [jnp.where(strict, m[:RWKV_CHUNK, :PAIR_W], 0.0) for m in m4]
        a_ak = [jnp.where(strict, m[:RWKV_CHUNK, PAIR_W:], 0.0).astype(BF16) for m in m4]
        a_rb = [jnp.where(incl, m[RWKV_CHUNK:, :PAIR_W], 0.0).astype(BF16) for m in m4]
        a_rk = [jnp.where(incl, m[RWKV_CHUNK:, PAIR_W:], 0.0).astype(BF16) for m in m4]
        v_m = [_embed(x) for x in v_v]
        akv = [_dot(a_ak[j], v_m[j]).astype(BF16) for j in chains]
        t_inv = [t.astype(BF16) for t in _tri_inverse(n_ab, tpos, spos)]
        wu0 = [_dot(t_inv[j], jnp.concatenate([_embed(at_v[j]), _embed(akv[j])], axis=1))
               for j in chains]
        w_b = [x[:, :PAIR_W].astype(BF16) for x in wu0]
        u0_b = [x[:, PAIR_W:].astype(BF16) for x in wu0]
        p_m = [jnp.where(same_head, lax.dot_general(bh_v[j], w_b[j], _TN, preferred_element_type=F32), 0.0)
               for j in chains]
        q_full = [lax.dot_general(jnp.concatenate([u0_b[j], v_v[j]], axis=0),
                                  jnp.concatenate([bh_v[j], kh_v[j]], axis=0), _TN,
                                  preferred_element_type=F32) for j in chains]
        q_t = [jnp.where(head0, q[:RWKV_CHUNK], q[RWKV_CHUNK:]) for q in q_full]
        r_p = [rt_v[j].astype(F32) + _dot(a_rb[j], _embed(w_b[j])) for j in chains]
        y_0 = [_dot(jnp.concatenate([a_rb[j], a_rk[j]], axis=1),
                    jnp.concatenate([_embed(u0_b[j]), v_m[j]], axis=0)) for j in chains]
        for j in chains:
            p_s[slots[j]] = p_m[j].astype(BF16)
            qt_s[slots[j]] = q_t[j]
            rp_s[slots[j]] = r_p[j].astype(BF16)
            y0_s[slots[j]] = y_0[j]
        return carry

    lax.fori_loop(0, n_chunks // RWKV_PHASE_A_CHUNKS, phase_a, 0)

    states = [s_ref[p] for p in pairs]
    for c in range(n_chunks):
        r0 = c * RWKV_CHUNK
        gam_c = gam_ref[c, 0:1, :]
        slots = [c * n_pairs + p for p in pairs]
        state_b = [s.astype(BF16) for s in states]
        upd = [lax.dot_general(state_b[p], p_s[slots[p]], _NT, preferred_element_type=F32) for p in pairs]
        y_c = [lax.dot_general(rp_s[slots[p]], _embed(state_b[p]), _NT, preferred_element_type=F32)
               + y0_s[slots[p]] for p in pairs]
        states = [states[p] * gam_c[:, lanes[p]] + upd[p] + qt_s[slots[p]] for p in pairs]
        yacc_ref[r0:r0 + RWKV_CHUNK, :] = jnp.concatenate(y_c, axis=1)
    for p in pairs:
        s_ref[p] = states[p]

    y = yacc_ref[...]
    mean = _seg_sum(y, ones_bd) * (1.0 / HEAD_DIM)
    dev = y - mean
    var = _seg_sum(dev * dev, ones_bd, pieces=1) * (1.0 / HEAD_DIM)
    yn = dev * lax.rsqrt(var + LNX_EPS) * lnx_g + lnx_b
    y_ref[0] = ((yn + bonus_ref[...]) * gate_ref[...]).astype(y_ref.dtype)


def _rwkv(ps, vecs, w_up_p, a_up_p, g_up, ones_bd, tblk=RWKV_ROWS):
    b, s, _ = ps.shape
    assert s % tblk == 0 and tblk % (RWKV_PHASE_A_CHUNKS * RWKV_CHUNK) == 0 and tblk % RWKV_CUMSUM_ROWS == 0
    tri = _block_ones(RWKV_CUMSUM_ROWS, RWKV_CHUNK, lower=True)
    act = pltpu.VMEM((tblk, RWKV_W), F32)
    act_b = pltpu.VMEM((tblk, RWKV_W), BF16)
    n_pairs = RWKV_W // PAIR_W
    n_chunks = tblk // RWKV_CHUNK
    n_slots = n_chunks * n_pairs
    scratch = [pltpu.VMEM((n_pairs, HEAD_DIM, PAIR_W), F32)]
    scratch += [act_b] * 7
    scratch += [pltpu.VMEM((n_chunks, V7X_SUBLANES, RWKV_W), F32)] + [act] * 3
    scratch += [
        pltpu.VMEM((n_slots, PAIR_W, PAIR_W), BF16),
        pltpu.VMEM((n_slots, RWKV_CHUNK, PAIR_W), F32),
        pltpu.VMEM((n_slots, RWKV_CHUNK, PAIR_W), BF16),
        pltpu.VMEM((n_slots, RWKV_CHUNK, PAIR_W), F32),
    ]
    return pl.pallas_call(
        _rwkv_kernel,
        grid=(b, s // tblk),
        in_specs=[
            pl.BlockSpec((1, tblk, RWKV_SHIFT_W), lambda i, t: (i, t, 0)),
            _const_spec(vecs.shape),
            _const_spec(w_up_p.shape),
            _const_spec(a_up_p.shape),
            _const_spec(g_up.shape),
            _const_spec(tri.shape),
            _const_spec((SLAB_W, SLAB_W)),
        ],
        out_specs=pl.BlockSpec((1, tblk, RWKV_W), lambda i, t: (i, t, 0)),
        out_shape=jax.ShapeDtypeStruct((b, s, RWKV_W), BF16),
        scratch_shapes=scratch,
        compiler_params=_cparams(2, 48),
        name="rwkv",
    )(ps, vecs, w_up_p, a_up_p, g_up, tri, ones_bd)


def _t5_bucket(dist):
    max_exact = NUM_BUCKETS // 2
    d_f = jnp.maximum(dist, 1).astype(F32)
    large = max_exact + (jnp.log(d_f / max_exact) / math.log(MAX_DISTANCE / max_exact)
                         * (NUM_BUCKETS - max_exact)).astype(jnp.int32)
    large = jnp.minimum(large, NUM_BUCKETS - 1)
    return jnp.where(dist < max_exact, dist, large)


def _band_buckets():
    run = BLOCK // DIL_ROW_PHASES
    out = []
    for window, dil in DIL_GROUPS:
        rows_q = np.arange(BLOCK)
        rows_k = np.arange(2 * BLOCK)
        if dil < DIL_ROW_PHASES:
            u_q = DIL_ROW_PHASES * (rows_q % run) + rows_q // run
            u_k = ((rows_k % (2 * run)) // run) * BLOCK + DIL_ROW_PHASES * (rows_k % run) + rows_k // (2 * run)
        else:
            u_q, u_k = rows_q, rows_k
        dsub = jnp.asarray(BLOCK + u_q[:, None] - u_k[None, :])
        band = (dsub >= 0) & (dsub <= window // dil)
        idx = jnp.where(band, _t5_bucket(jnp.maximum(dsub, 0) * dil), -1)
        out.append(jnp.stack([idx, jnp.where(jnp.asarray(u_k < BLOCK)[None, :], -1, idx)]))
    return jnp.stack(out, axis=1).astype(jnp.int32)


def _bias_kernel(tab_ref, idx_ref, o_ref):
    head = pl.program_id(1)
    idx = idx_ref[0, 0]
    acc = jnp.full(idx.shape, NEG_INF, F32)
    for bucket in range(NUM_BUCKETS):
        acc = jnp.where(idx == bucket, tab_ref[bucket, head] * LOG2_E, acc)
    o_ref[0, 0] = acc


def _band_bias(rel_bias):
    n_heads = rel_bias.shape[1]
    return pl.pallas_call(
        _bias_kernel,
        grid=(2, n_heads),
        in_specs=[
            pl.BlockSpec(memory_space=pltpu.SMEM),
            pl.BlockSpec((1, 1, BLOCK, 2 * BLOCK), lambda v, h: (v, h // DIL_GROUP_HEADS, 0, 0)),
        ],
        out_specs=pl.BlockSpec((1, 1, BLOCK, 2 * BLOCK), lambda v, h: (v, h, 0, 0)),
        out_shape=jax.ShapeDtypeStruct((2, n_heads, BLOCK, 2 * BLOCK), F32),
        compiler_params=_cparams(2, 16),
        name="band_bias",
    )(rel_bias, _band_buckets())


def _dil_group(dil, first_group, q_ref, k_ref, v_ref, bias_ref, acc_ref, m_ref, l_ref):
    seq_len = q_ref.shape[2]
    phase_rows = seq_len // DIL_ROW_PHASES
    run = BLOCK // DIL_ROW_PHASES
    head0 = lax.broadcasted_iota(jnp.int32, (BLOCK, PAIR_W), 1) < HEAD_DIM

    def windows(t):
        if dil < DIL_ROW_PHASES:
            blk = t
            back = jnp.maximum(blk - 1, 0)
            own = [pl.ds(pl.multiple_of(c * phase_rows + run * blk, run), run) for c in range(DIL_ROW_PHASES)]
            prev = [pl.ds(pl.multiple_of(c * phase_rows + run * back, run), run) for c in range(DIL_ROW_PHASES)]
            return own, prev, blk == 0
        step = dil // DIL_ROW_PHASES
        n_blocks = phase_rows // (step * BLOCK)
        sub = t >> (n_blocks.bit_length() - 1)
        blk = t & (n_blocks - 1)
        base = (sub & (DIL_ROW_PHASES - 1)) * phase_rows + (sub >> (DIL_ROW_PHASES.bit_length() - 1))
        back = jnp.maximum(blk - 1, 0)
        if step == 1:
            own = [pl.ds(pl.multiple_of(base + BLOCK * blk, BLOCK), BLOCK)]
            prev = [pl.ds(pl.multiple_of(base + BLOCK * back, BLOCK), BLOCK)]
        else:
            own = [pl.ds(base + step * BLOCK * blk, BLOCK, stride=step)]
            prev = [pl.ds(base + step * BLOCK * back, BLOCK, stride=step)]
        return own, prev, blk == 0

    def gather(get, wins):
        parts = [get(w) for w in wins]
        return parts[0] if len(parts) == 1 else jnp.concatenate(parts, axis=0)

    def scatter(put, wins, value):
        rows = BLOCK // len(wins)
        for i, w in enumerate(wins):
            put(w, value[i * rows:(i + 1) * rows])

    def tiles(i, carry):
        slabs = []
        for j in range(DIL_TILES_PER_ITER):
            own, prev, first = windows(i * DIL_TILES_PER_ITER + j)
            key_wins = [w for pw, ow in zip(prev, own) for w in (pw, ow)]
            variant = first.astype(jnp.int32)
            slabs += [(pair, own, key_wins, variant) for pair in range(DIL_OUT_W // PAIR_W)]
        q = [gather(lambda w: q_ref[0, pair, w, :], own) for pair, own, _, _ in slabs]
        keys = [gather(lambda w: k_ref[0, pair, w, :], kw).astype(BF16) for pair, _, kw, _ in slabs]
        vals = [gather(lambda w: v_ref[0, pair, w, :], kw).astype(BF16) for pair, _, kw, _ in slabs]
        if not first_group:
            m_old = [gather(lambda w: m_ref[pair, w, :], own) for pair, own, _, _ in slabs]
            l_old = [gather(lambda w: l_ref[pair, w, :], own) for pair, own, _, _ in slabs]
            acc_old = [gather(lambda w: acc_ref[pair, w, :], own) for pair, own, _, _ in slabs]
        heads = [(s, h) for s in range(len(slabs)) for h in range(2)]
        logits = [lax.dot_general(jnp.where(head0 if h == 0 else ~head0, q[s], 0.0).astype(BF16),
                                  keys[s], _NT, preferred_element_type=F32) for s, h in heads]
        logits = [lg + bias_ref[slabs[s][3], 2 * slabs[s][0] + h] for lg, (s, h) in zip(logits, heads)]
        mx = [jnp.max(lg, axis=-1, keepdims=True) for lg in logits]
        p = [jnp.exp2(lg - m) for lg, m in zip(logits, mx)]
        ls = [jnp.sum(ph, axis=-1, keepdims=True) for ph in p]
        pv = [_dot(ph.astype(BF16), vals[s]) for ph, (s, h) in zip(p, heads)]
        for s, (pair, own, _, _) in enumerate(slabs):
            m_t = jnp.where(head0, mx[2 * s], mx[2 * s + 1])
            l_t = jnp.where(head0, ls[2 * s], ls[2 * s + 1])
            acc_t = jnp.where(head0, pv[2 * s], pv[2 * s + 1])

            def put(ref, value, pair=pair, own=own):
                def store(w, rows):
                    ref[pair, w, :] = rows
                scatter(store, own, value)

            if first_group:
                put(m_ref, m_t)
                put(l_ref, l_t)
                put(acc_ref, acc_t)
            else:
                m_new = jnp.maximum(m_old[s], m_t)
                e_old = jnp.exp2(m_old[s] - m_new)
                e_t = jnp.exp2(m_t - m_new)
                put(m_ref, m_new)
                put(l_ref, l_old[s] * e_old + l_t * e_t)
                put(acc_ref, acc_old[s] * e_old + acc_t * e_t)
        return carry

    lax.fori_loop(0, seq_len // BLOCK // DIL_TILES_PER_ITER, tiles, 0)


def _dil_kernel(q_ref, k_ref, v_ref, bias_ref, o_ref, acc_ref, m_ref, l_ref):
    g = pl.program_id(1)

    for gi, (_, dil) in enumerate(DIL_GROUPS):
        pl.when(g == gi)(functools.partial(_dil_group, dil, gi == 0, q_ref, k_ref, v_ref, bias_ref,
                                           acc_ref, m_ref, l_ref))

    @pl.when(g == len(DIL_GROUPS) - 1)
    def _():
        phase_rows = o_ref.shape[2] // DIL_ROW_PHASES
        for pair in range(DIL_OUT_W // PAIR_W):
            for c in range(DIL_ROW_PHASES):
                rows = slice(c * phase_rows, (c + 1) * phase_rows)
                o_ref[0, pair, pl.ds(c, phase_rows, stride=DIL_ROW_PHASES), :] = (
                    acc_ref[pair, rows, :] / l_ref[pair, rows, :])


def _dilated_attention(q_slabs, kv_slabs, bias):
    b, _, s, _ = q_slabs.shape
    n_groups = len(DIL_GROUPS)
    pairs = DIL_OUT_W // PAIR_W
    blk = (1, pairs, s, PAIR_W)
    return pl.pallas_call(
        _dil_kernel,
        grid=(b, n_groups),
        in_specs=[
            pl.BlockSpec(blk, lambda i, g: (i, g, 0, 0)),
            pl.BlockSpec(blk, lambda i, g: (i, g, 0, 0)),
            pl.BlockSpec(blk, lambda i, g: (i, n_groups + g, 0, 0)),
            pl.BlockSpec((2, DIL_GROUP_HEADS, BLOCK, 2 * BLOCK), lambda i, g: (0, g, 0, 0)),
        ],
        out_specs=pl.BlockSpec(blk, lambda i, g: (i, 0, 0, 0)),
        out_shape=jax.ShapeDtypeStruct((b, pairs, s, PAIR_W), F32),
        scratch_shapes=[pltpu.VMEM((pairs, s, PAIR_W), F32)] * 3,
        compiler_params=_cparams(2, 56),
        name="dilated_attention",
    )(q_slabs, kv_slabs, kv_slabs, bias)


def kernel(x, mem, ffn_pre_norm, ffn_pre_w_in, ffn_pre_w_out, mix_norm, ffn_post_norm, ffn_post_w_in, ffn_post_w_out, mem_norm, mem_w_kv, mem_q_norm, mem_k_norm, a_w_in, a_shift_mu, a_w0, a_w_up, a_a0, a_a_up, a_g_up, a_kk_scale, a_k_a, a_r_k, a_lnx_g, a_lnx_b, a_w_out, b_w_q, b_q_norm, b_w_out, kv_norm, kv_w, kv_k_norm, rel_bias):
    b, s, d = x.shape
    depth = ffn_pre_w_in.shape[0]
    n_a = a_w_in.shape[0]
    n = b * s
    scale = LOG2_E / math.sqrt(HEAD_DIM)
    ones_bd = _block_ones(SLAB_W, HEAD_DIM)
    row = lambda p: p.reshape(1, -1)

    k_mem, v_mem = _memkv(mem, mem_norm[:, None, :], mem_w_kv.astype(BF16),
                          jnp.tile(mem_k_norm, (1, MEM_HEADS))[:, None, :], ones_bd)
    mem_q_gain = jnp.tile(mem_q_norm, (1, MEM_HEADS)) * scale

    pre_w_in, pre_w_out = ffn_pre_w_in.astype(BF16), ffn_pre_w_out.astype(BF16)
    post_w_in, post_w_out = ffn_post_w_in.astype(BF16), ffn_post_w_out.astype(BF16)
    a_w_in_b = a_w_in.astype(BF16)
    xf = x.reshape(n, d)
    kv = None
    bias = None
    for layer in range(depth):
        xf = _ffn(xf, row(ffn_pre_norm[layer]), pre_w_in, pre_w_out, layer)
        if layer < n_a:
            i = layer
            ps, qm = _aproj(xf, row(mix_norm[layer]), a_w_in_b, i, row(a_shift_mu[i]),
                            row(mem_q_gain[layer]), ones_bd, s)
            zeros = jnp.zeros((DECAY_LORA, RWKV_W), F32)
            vecs = jnp.stack([a_w0[i], a_a0[i], a_kk_scale[i], a_k_a[i], a_r_k[i].reshape(-1),
                              a_lnx_g[i], a_lnx_b[i], jnp.zeros((RWKV_W,), F32)])
            y_main = _rwkv(ps.reshape(b, s, RWKV_SHIFT_W), vecs,
                           jnp.concatenate([a_w_up[i], zeros]).astype(BF16),
                           jnp.concatenate([zeros, a_a_up[i]]).astype(BF16),
                           a_g_up[i].astype(BF16), ones_bd)
            mix = (y_main.reshape(n, RWKV_W), qm, k_mem[layer], v_mem[layer], a_w_out[i].astype(BF16), s)
        else:
            j = layer - n_a
            q_gain = jnp.concatenate([jnp.tile(b_q_norm[j], DIL_W // HEAD_DIM) * scale, mem_q_gain[layer]])
            q_dil, qm = _qproj(xf, row(mix_norm[layer]), b_w_q[j].astype(BF16), row(q_gain), ones_bd, s, DIL_W)
            if bias is None:
                bias = _band_bias(rel_bias)
            y_dil = _dilated_attention(q_dil, kv, bias)
            mix = (y_dil, qm, k_mem[layer], v_mem[layer], b_w_out[j].astype(BF16), s)
        xf = _ffn(xf, row(ffn_post_norm[layer]), post_w_in, post_w_out, layer, mix=mix)
        if layer == n_a - 1:
            k_gain = jnp.tile(kv_k_norm, DIL_W // HEAD_DIM)
            kv, = _qproj(xf, row(kv_norm), kv_w.astype(BF16), row(k_gain), ones_bd, s, 2 * DIL_W)
    return xf.reshape(b, s, d)
```
